```python
import jax, jax.numpy as jnp
from jax import lax
import numpy as np

D_MODEL = 2048
BATCH = 8
SEQ = 8192
DEPTH = 2

HEAD_DIM = 128
N_SB_HEADS = 12
N_MEM_HEADS = 4
N_MLA_HEADS = 12
MEM_LEN = 256
Q_LORA_RANK = 512
KV_LORA_RANK = 512
QK_NOPE_DIM = 128
QK_ROPE_DIM = 64
V_HEAD_DIM = 128
ROPE_THETA = 10000.0
BLOCK_Q = 128
EPS = 1e-6
N_A_LAYERS = DEPTH // 2
N_B_LAYERS = DEPTH - N_A_LAYERS
SB_W = N_SB_HEADS * HEAD_DIM
MEM_W = N_MEM_HEADS * HEAD_DIM
MLA_W = N_MLA_HEADS * V_HEAD_DIM
A_IN_SIZES = (SB_W, SB_W, SB_W, SB_W, MEM_W, MEM_W)
B_IN_SIZES = (Q_LORA_RANK, MLA_W, MEM_W, MEM_W)
A_IN_W = sum(A_IN_SIZES)
B_IN_W = sum(B_IN_SIZES)
MIX_A_W = SB_W + MEM_W
MIX_B_W = MLA_W + MEM_W

kernel_name = "yoco_stickbreak_mla_memory_hybrid"


def _split(x, sizes):
    idx = [int(i) for i in np.cumsum(sizes)[:-1]]
    return jnp.split(x, idx, axis=-1)


def rmsnorm(x, g):
    xf = x.astype(jnp.float32)
    y = xf * lax.rsqrt(jnp.mean(xf * xf, axis=-1, keepdims=True) + EPS)
    return (y * g.astype(jnp.float32)).astype(x.dtype)


def rope_tables(positions):
    inv_freq = jnp.power(ROPE_THETA, -jnp.arange(0, QK_ROPE_DIM, 2, dtype=jnp.float32) / QK_ROPE_DIM)
    ang = positions.astype(jnp.float32)[..., None] * inv_freq
    return jnp.cos(ang), jnp.sin(ang)


def apply_rope(x, cos, sin):
    half = x.shape[-1] // 2
    x1, x2 = x[..., :half], x[..., half:]
    return jnp.concatenate([x1 * cos - x2 * sin, x2 * cos + x1 * sin], axis=-1).astype(x.dtype)


def _to_blocks(t):
    b, s = t.shape[0], t.shape[1]
    return t.reshape(b, s // BLOCK_Q, BLOCK_Q, *t.shape[2:]).swapaxes(0, 1)


def _from_blocks(t):
    t = t.swapaxes(0, 1)
    return t.reshape(t.shape[0], t.shape[1] * t.shape[2], *t.shape[3:])


def stick_breaking_attention(q, k, v):
    s_len, d = q.shape[1], q.shape[-1]
    scale = d ** -0.5
    key_pos = jnp.arange(s_len)

    def block(args):
        qi, bi = args
        z = jnp.einsum('bqhd,bkhd->bhqk', qi, k).astype(jnp.float32) * scale
        q_pos = bi * BLOCK_Q + jnp.arange(BLOCK_Q)
        causal = key_pos[None, :] < q_pos[:, None]
        log_1m_beta = jnp.where(causal, jax.nn.log_sigmoid(-z), 0.0)
        between = lax.cumsum(log_1m_beta, axis=3, reverse=True) - log_1m_beta
        a = jnp.where(causal, jnp.exp(jax.nn.log_sigmoid(z) + between), 0.0)
        return jnp.einsum('bhqk,bkhd->bqhd', a.astype(v.dtype), v)

    nblk = s_len // BLOCK_Q
    out = lax.map(block, (_to_blocks(q), jnp.arange(nblk)))
    return _from_blocks(out)


def mla_causal_attention(q_nope, q_rope, k_nope, k_rope, v):
    s_len = q_nope.shape[1]
    scale = (QK_NOPE_DIM + QK_ROPE_DIM) ** -0.5
    key_pos = jnp.arange(s_len)

    def block(args):
        qn, qr, bi = args
        s = (jnp.einsum('bqhd,bkhd->bhqk', qn, k_nope)
             + jnp.einsum('bqhr,bkr->bhqk', qr, k_rope)).astype(jnp.float32) * scale
        q_pos = bi * BLOCK_Q + jnp.arange(BLOCK_Q)
        mask = key_pos[None, :] <= q_pos[:, None]
        p = jax.nn.softmax(jnp.where(mask, s, -jnp.inf), axis=-1)
        return jnp.einsum('bhqk,bkhd->bqhd', p.astype(v.dtype), v)

    nblk = s_len // BLOCK_Q
    out = lax.map(block, (_to_blocks(q_nope), _to_blocks(q_rope), jnp.arange(nblk)))
    return _from_blocks(out)


def memory_attention(q_m, mem, g_norm, w_kv, g_q, g_k):
    b, s_len, _ = q_m.shape
    mk, mv = _split(rmsnorm(mem, g_norm) @ w_kv, (MEM_W, MEM_W))
    mk = rmsnorm(mk.reshape(b, -1, N_MEM_HEADS, HEAD_DIM), g_k)
    mv = mv.reshape(b, -1, N_MEM_HEADS, HEAD_DIM)
    q = rmsnorm(q_m.reshape(b, s_len, N_MEM_HEADS, HEAD_DIM), g_q)
    s = jnp.einsum('bqhd,bmhd->bhqm', q, mk).astype(jnp.float32) * HEAD_DIM ** -0.5
    p = jax.nn.softmax(s, axis=-1)
    return jnp.einsum('bhqm,bmhd->bqhd', p.astype(mv.dtype), mv).reshape(b, s_len, MEM_W)


def _fwd_setup_inputs(seed: int = 0) -> dict:
    key = jax.random.key(seed)
    ks = iter(jax.random.split(key, 32))
    f32 = jnp.float32

    def w(shape, fan_in):
        return jax.random.normal(next(ks), shape, f32) * fan_in ** -0.5

    def gain(shape):
        return 1.0 + 0.02 * jax.random.normal(next(ks), shape, f32)

    x = jax.random.normal(next(ks), (BATCH, SEQ, D_MODEL), f32)
    mem = jax.random.normal(next(ks), (BATCH, MEM_LEN, D_MODEL), f32)
    positions = jnp.broadcast_to(jnp.arange(SEQ, dtype=jnp.int32)[None, :], (BATCH, SEQ))
    return {
        "x": x,
        "mem": mem,
        "positions": positions,
        "a_norm": gain((N_A_LAYERS, D_MODEL)),
        "a_w_in": w((N_A_LAYERS, D_MODEL, A_IN_W), D_MODEL),
        "a_w_out": w((N_A_LAYERS, MIX_A_W, D_MODEL), MIX_A_W),
        "kv_norm": gain((D_MODEL,)),
        "w_dkv": w((D_MODEL, KV_LORA_RANK + QK_ROPE_DIM), D_MODEL),
        "g_ckv": gain((KV_LORA_RANK,)),
        "w_ukv": w((KV_LORA_RANK, N_MLA_HEADS * (QK_NOPE_DIM + V_HEAD_DIM)), KV_LORA_RANK),
        "g_k_nope": gain((QK_NOPE_DIM,)),
        "g_k_rope": gain((QK_ROPE_DIM,)),
        "b_norm": gain((N_B_LAYERS, D_MODEL)),
        "b_w_in": w((N_B_LAYERS, D_MODEL, B_IN_W), D_MODEL),
        "b_g_q_lat": gain((N_B_LAYERS, Q_LORA_RANK)),
        "b_w_uq": w((N_B_LAYERS, Q_LORA_RANK, N_MLA_HEADS * (QK_NOPE_DIM + QK_ROPE_DIM)), Q_LORA_RANK),
        "b_g_q_nope": gain((N_B_LAYERS, QK_NOPE_DIM)),
        "b_g_q_rope": gain((N_B_LAYERS, QK_ROPE_DIM)),
        "b_w_out": w((N_B_LAYERS, MIX_B_W, D_MODEL), MIX_B_W),
        "mem_norm": gain((DEPTH, D_MODEL)),
        "w_mem_kv": w((DEPTH, D_MODEL, 2 * MEM_W), D_MODEL),
        "g_mem_q": gain((DEPTH, HEAD_DIM)),
        "g_mem_k": gain((DEPTH, HEAD_DIM)),
    }


def _fwd_reference(x, mem, positions, a_norm, a_w_in, a_w_out, kv_norm, w_dkv, g_ckv, w_ukv,
              g_k_nope, g_k_rope, b_norm, b_w_in, b_g_q_lat, b_w_uq, b_g_q_nope, b_g_q_rope,
              b_w_out, mem_norm, w_mem_kv, g_mem_q, g_mem_k):
    b, s_len, _ = x.shape
    cos, sin = rope_tables(positions)
    cos_h, sin_h = cos[:, :, None, :], sin[:, :, None, :]
    shared = None
    for layer in range(DEPTH):
        if layer < N_A_LAYERS:
            i = layer
            h = rmsnorm(x, a_norm[i])
            q, k, v, g_sb, q_m, g_m = _split(h @ a_w_in[i], A_IN_SIZES)
            heads = lambda t: t.reshape(b, s_len, N_SB_HEADS, HEAD_DIM)
            sb = stick_breaking_attention(heads(q), heads(k), heads(v)).reshape(b, s_len, SB_W)
            mo = memory_attention(q_m, mem, mem_norm[layer], w_mem_kv[layer], g_mem_q[layer], g_mem_k[layer])
            mixed = jnp.concatenate([sb * jax.nn.silu(g_sb), mo * jax.nn.silu(g_m)], axis=-1)
            x = x + mixed @ a_w_out[i]
        else:
            j = layer - N_A_LAYERS
            if shared is None:
                c_kv, k_r = _split(rmsnorm(x, kv_norm) @ w_dkv, (KV_LORA_RANK, QK_ROPE_DIM))
                kv = (rmsnorm(c_kv, g_ckv) @ w_ukv).reshape(b, s_len, N_MLA_HEADS, QK_NOPE_DIM + V_HEAD_DIM)
                k_nope = rmsnorm(kv[..., :QK_NOPE_DIM], g_k_nope)
                v_mla = kv[..., QK_NOPE_DIM:]
                k_rope = apply_rope(rmsnorm(k_r, g_k_rope), cos, sin)
                shared = (k_nope, k_rope, v_mla)
            k_nope, k_rope, v_mla = shared
            h = rmsnorm(x, b_norm[j])
            q_lat, g_mla, q_m, g_m = _split(h @ b_w_in[j], B_IN_SIZES)
            q = (rmsnorm(q_lat, b_g_q_lat[j]) @ b_w_uq[j]).reshape(
                b, s_len, N_MLA_HEADS, QK_NOPE_DIM + QK_ROPE_DIM)
            q_nope = rmsnorm(q[..., :QK_NOPE_DIM], b_g_q_nope[j])
            q_rope = apply_rope(rmsnorm(q[..., QK_NOPE_DIM:], b_g_q_rope[j]), cos_h, sin_h)
            att = mla_causal_attention(q_nope, q_rope, k_nope, k_rope, v_mla).reshape(b, s_len, MLA_W)
            mo = memory_attention(q_m, mem, mem_norm[layer], w_mem_kv[layer], g_mem_q[layer], g_mem_k[layer])
            mixed = jnp.concatenate([att * jax.nn.silu(g_mla), mo * jax.nn.silu(g_m)], axis=-1)
            x = x + mixed @ b_w_out[j]
    return x


import jax as _jax
import jax.numpy as _jnp

TWIN_FORMAT = 'train_step'
FWD_PARAMS = ['x', 'mem', 'positions', 'a_norm', 'a_w_in', 'a_w_out', 'kv_norm', 'w_dkv', 'g_ckv', 'w_ukv', 'g_k_nope', 'g_k_rope', 'b_norm', 'b_w_in', 'b_g_q_lat', 'b_w_uq', 'b_g_q_nope', 'b_g_q_rope', 'b_w_out', 'mem_norm', 'w_mem_kv', 'g_mem_q', 'g_mem_k']
TWIN_WEIGHTS = ['a_norm', 'a_w_in', 'a_w_out', 'kv_norm', 'w_dkv', 'g_ckv', 'w_ukv', 'g_k_nope', 'g_k_rope', 'b_norm', 'b_w_in', 'b_g_q_lat', 'b_w_uq', 'b_g_q_nope', 'b_g_q_rope', 'b_w_out', 'mem_norm', 'w_mem_kv', 'g_mem_q', 'g_mem_k']
TWIN_DIFF_INPUT = 'x'
TWIN_INPUTS = ['x', 'mem', 'positions', 'a_norm', 'a_w_in', 'a_w_out', 'kv_norm', 'w_dkv', 'g_ckv', 'w_ukv', 'g_k_nope', 'g_k_rope', 'b_norm', 'b_w_in', 'b_g_q_lat', 'b_w_uq', 'b_g_q_nope', 'b_g_q_rope', 'b_w_out', 'mem_norm', 'w_mem_kv', 'g_mem_q', 'g_mem_k', 'loss_target', 'm_a_norm', 'm_a_w_in', 'm_a_w_out', 'm_kv_norm', 'm_w_dkv', 'm_g_ckv', 'm_w_ukv', 'm_g_k_nope', 'm_g_k_rope', 'm_b_norm', 'm_b_w_in', 'm_b_g_q_lat', 'm_b_w_uq', 'm_b_g_q_nope', 'm_b_g_q_rope', 'm_b_w_out', 'm_mem_norm', 'm_w_mem_kv', 'm_g_mem_q', 'm_g_mem_k', 'v_a_norm', 'v_a_w_in', 'v_a_w_out', 'v_kv_norm', 'v_w_dkv', 'v_g_ckv', 'v_w_ukv', 'v_g_k_nope', 'v_g_k_rope', 'v_b_norm', 'v_b_w_in', 'v_b_g_q_lat', 'v_b_w_uq', 'v_b_g_q_nope', 'v_b_g_q_rope', 'v_b_w_out', 'v_mem_norm', 'v_w_mem_kv', 'v_g_mem_q', 'v_g_mem_k']
TWIN_OUTPUTS = ['loss', 'grad_x', 'grad_a_norm', 'grad_a_w_in', 'grad_a_w_out', 'grad_kv_norm', 'grad_w_dkv', 'grad_g_ckv', 'grad_w_ukv', 'grad_g_k_nope', 'grad_g_k_rope', 'grad_b_norm', 'grad_b_w_in', 'grad_b_g_q_lat', 'grad_b_w_uq', 'grad_b_g_q_nope', 'grad_b_g_q_rope', 'grad_b_w_out', 'grad_mem_norm', 'grad_w_mem_kv', 'grad_g_mem_q', 'grad_g_mem_k', 'delta_a_norm', 'delta_a_w_in', 'delta_a_w_out', 'delta_kv_norm', 'delta_w_dkv', 'delta_g_ckv', 'delta_w_ukv', 'delta_g_k_nope', 'delta_g_k_rope', 'delta_b_norm', 'delta_b_w_in', 'delta_b_g_q_lat', 'delta_b_w_uq', 'delta_b_g_q_nope', 'delta_b_g_q_rope', 'delta_b_w_out', 'delta_mem_norm', 'delta_w_mem_kv', 'delta_g_mem_q', 'delta_g_mem_k', 'new_m_a_norm', 'new_m_a_w_in', 'new_m_a_w_out', 'new_m_kv_norm', 'new_m_w_dkv', 'new_m_g_ckv', 'new_m_w_ukv', 'new_m_g_k_nope', 'new_m_g_k_rope', 'new_m_b_norm', 'new_m_b_w_in', 'new_m_b_g_q_lat', 'new_m_b_w_uq', 'new_m_b_g_q_nope', 'new_m_b_g_q_rope', 'new_m_b_w_out', 'new_m_mem_norm', 'new_m_w_mem_kv', 'new_m_g_mem_q', 'new_m_g_mem_k', 'new_v_a_norm', 'new_v_a_w_in', 'new_v_a_w_out', 'new_v_kv_norm', 'new_v_w_dkv', 'new_v_g_ckv', 'new_v_w_ukv', 'new_v_g_k_nope', 'new_v_g_k_rope', 'new_v_b_norm', 'new_v_b_w_in', 'new_v_b_g_q_lat', 'new_v_b_w_uq', 'new_v_b_g_q_nope', 'new_v_b_g_q_rope', 'new_v_b_w_out', 'new_v_mem_norm', 'new_v_w_mem_kv', 'new_v_g_mem_q', 'new_v_g_mem_k']
TWIN_LEAF_KINDS = {'loss': 'loss', 'grad_x': 'grad_x', 'grad_a_norm': 'grad_w', 'grad_a_w_in': 'grad_w', 'grad_a_w_out': 'grad_w', 'grad_kv_norm': 'grad_w', 'grad_w_dkv': 'grad_w', 'grad_g_ckv': 'grad_w', 'grad_w_ukv': 'grad_w', 'grad_g_k_nope': 'grad_w', 'grad_g_k_rope': 'grad_w', 'grad_b_norm': 'grad_w', 'grad_b_w_in': 'grad_w', 'grad_b_g_q_lat': 'grad_w', 'grad_b_w_uq': 'grad_w', 'grad_b_g_q_nope': 'grad_w', 'grad_b_g_q_rope': 'grad_w', 'grad_b_w_out': 'grad_w', 'grad_mem_norm': 'grad_w', 'grad_w_mem_kv': 'grad_w', 'grad_g_mem_q': 'grad_w', 'grad_g_mem_k': 'grad_w', 'delta_a_norm': 'delta_w', 'delta_a_w_in': 'delta_w', 'delta_a_w_out': 'delta_w', 'delta_kv_norm': 'delta_w', 'delta_w_dkv': 'delta_w', 'delta_g_ckv': 'delta_w', 'delta_w_ukv': 'delta_w', 'delta_g_k_nope': 'delta_w', 'delta_g_k_rope': 'delta_w', 'delta_b_norm': 'delta_w', 'delta_b_w_in': 'delta_w', 'delta_b_g_q_lat': 'delta_w', 'delta_b_w_uq': 'delta_w', 'delta_b_g_q_nope': 'delta_w', 'delta_b_g_q_rope': 'delta_w', 'delta_b_w_out': 'delta_w', 'delta_mem_norm': 'delta_w', 'delta_w_mem_kv': 'delta_w', 'delta_g_mem_q': 'delta_w', 'delta_g_mem_k': 'delta_w', 'new_m_a_norm': 'new_m', 'new_m_a_w_in': 'new_m', 'new_m_a_w_out': 'new_m', 'new_m_kv_norm': 'new_m', 'new_m_w_dkv': 'new_m', 'new_m_g_ckv': 'new_m', 'new_m_w_ukv': 'new_m', 'new_m_g_k_nope': 'new_m', 'new_m_g_k_rope': 'new_m', 'new_m_b_norm': 'new_m', 'new_m_b_w_in': 'new_m', 'new_m_b_g_q_lat': 'new_m', 'new_m_b_w_uq': 'new_m', 'new_m_b_g_q_nope': 'new_m', 'new_m_b_g_q_rope': 'new_m', 'new_m_b_w_out': 'new_m', 'new_m_mem_norm': 'new_m', 'new_m_w_mem_kv': 'new_m', 'new_m_g_mem_q': 'new_m', 'new_m_g_mem_k': 'new_m', 'new_v_a_norm': 'new_v', 'new_v_a_w_in': 'new_v', 'new_v_a_w_out': 'new_v', 'new_v_kv_norm': 'new_v', 'new_v_w_dkv': 'new_v', 'new_v_g_ckv': 'new_v', 'new_v_w_ukv': 'new_v', 'new_v_g_k_nope': 'new_v', 'new_v_g_k_rope': 'new_v', 'new_v_b_norm': 'new_v', 'new_v_b_w_in': 'new_v', 'new_v_b_g_q_lat': 'new_v', 'new_v_b_w_uq': 'new_v', 'new_v_b_g_q_nope': 'new_v', 'new_v_b_g_q_rope': 'new_v', 'new_v_b_w_out': 'new_v', 'new_v_mem_norm': 'new_v', 'new_v_w_mem_kv': 'new_v', 'new_v_g_mem_q': 'new_v', 'new_v_g_mem_k': 'new_v'}


def _forward(args):
    return _fwd_reference(*[args[k] for k in FWD_PARAMS])


def _output_shape():
    def fwd():
        inp = _fwd_setup_inputs(0)
        return _fwd_reference(*[inp[k] for k in FWD_PARAMS])
    out = _jax.eval_shape(fwd)
    return out.shape, out.dtype

N_MICROBATCH = 1
ADAM_LR = 0.001
ADAM_B1 = 0.9
ADAM_B2 = 0.999
ADAM_EPS = 1e-08
ADAM_WD = 0.01
ADAM_STEP = 10
PER_EXAMPLE_BATCH_AXIS = {'x': 0, 'mem': 0, 'positions': 0, 'loss_target': 0}
SHARED_INPUTS = []
_WEIGHT_DTYPES = {'a_norm': _jnp.float32, 'a_w_in': _jnp.float32, 'a_w_out': _jnp.float32, 'kv_norm': _jnp.float32, 'w_dkv': _jnp.float32, 'g_ckv': _jnp.float32, 'w_ukv': _jnp.float32, 'g_k_nope': _jnp.float32, 'g_k_rope': _jnp.float32, 'b_norm': _jnp.float32, 'b_w_in': _jnp.float32, 'b_g_q_lat': _jnp.float32, 'b_w_uq': _jnp.float32, 'b_g_q_nope': _jnp.float32, 'b_g_q_rope': _jnp.float32, 'b_w_out': _jnp.float32, 'mem_norm': _jnp.float32, 'w_mem_kv': _jnp.float32, 'g_mem_q': _jnp.float32, 'g_mem_k': _jnp.float32}
MOMENT_SCALE = {'a_norm': 8.814301e+00, 'a_w_in': 8.139470e-02, 'a_w_out': 7.766638e-02, 'kv_norm': 3.427590e-02, 'w_dkv': 6.305355e-02, 'g_ckv': 2.936619e-01, 'w_ukv': 2.472277e-02, 'g_k_nope': 4.339775e-01, 'g_k_rope': 4.359353e-01, 'b_norm': 9.635387e-02, 'b_w_in': 2.678396e-02, 'b_g_q_lat': 4.336735e-02, 'b_w_uq': 1.906747e-02, 'b_g_q_nope': 4.333215e-01, 'b_g_q_rope': 4.351080e-01, 'b_w_out': 2.165412e-02, 'mem_norm': 2.267596e-02, 'w_mem_kv': 1.331284e-02, 'g_mem_q': 4.050264e-01, 'g_mem_k': 4.059331e-01}


def _to_microbatches(a, axis):
    t = _jnp.moveaxis(a, axis, 0)
    t = t.reshape((N_MICROBATCH, t.shape[0] // N_MICROBATCH) + t.shape[1:])
    return _jnp.moveaxis(t, 1, axis + 1)


def setup_inputs(seed: int = 0) -> dict:
    inp = _fwd_setup_inputs(seed)
    key = _jax.random.fold_in(_jax.random.key(seed), 7919)
    shape, _ = _output_shape()
    out = dict(inp)
    out["loss_target"] = _jax.random.normal(_jax.random.fold_in(key, 0), shape, _jnp.float32)
    for i, name in enumerate(TWIN_WEIGHTS):
        w = inp[name].astype(_jnp.float32)
        if MOMENT_SCALE is None:
            s = _jnp.sqrt(_jnp.mean(_jnp.square(w)) + 1e-30)
        else:
            s = MOMENT_SCALE[name]
        km, kv = _jax.random.split(_jax.random.fold_in(key, i + 1))
        out[name] = w
        out["m_" + name] = s * _jax.random.normal(km, w.shape, _jnp.float32)
        out["v_" + name] = (s * s) * _jax.random.uniform(kv, w.shape, _jnp.float32, 0.5, 1.5)
    if N_MICROBATCH > 1:
        for name, axis in PER_EXAMPLE_BATCH_AXIS.items():
            out[name] = _to_microbatches(out[name], axis)
    return {'x': out['x'], 'mem': out['mem'], 'positions': out['positions'], 'a_norm': out['a_norm'], 'a_w_in': out['a_w_in'], 'a_w_out': out['a_w_out'], 'kv_norm': out['kv_norm'], 'w_dkv': out['w_dkv'], 'g_ckv': out['g_ckv'], 'w_ukv': out['w_ukv'], 'g_k_nope': out['g_k_nope'], 'g_k_rope': out['g_k_rope'], 'b_norm': out['b_norm'], 'b_w_in': out['b_w_in'], 'b_g_q_lat': out['b_g_q_lat'], 'b_w_uq': out['b_w_uq'], 'b_g_q_nope': out['b_g_q_nope'], 'b_g_q_rope': out['b_g_q_rope'], 'b_w_out': out['b_w_out'], 'mem_norm': out['mem_norm'], 'w_mem_kv': out['w_mem_kv'], 'g_mem_q': out['g_mem_q'], 'g_mem_k': out['g_mem_k'], 'loss_target': out['loss_target'], 'm_a_norm': out['m_a_norm'], 'm_a_w_in': out['m_a_w_in'], 'm_a_w_out': out['m_a_w_out'], 'm_kv_norm': out['m_kv_norm'], 'm_w_dkv': out['m_w_dkv'], 'm_g_ckv': out['m_g_ckv'], 'm_w_ukv': out['m_w_ukv'], 'm_g_k_nope': out['m_g_k_nope'], 'm_g_k_rope': out['m_g_k_rope'], 'm_b_norm': out['m_b_norm'], 'm_b_w_in': out['m_b_w_in'], 'm_b_g_q_lat': out['m_b_g_q_lat'], 'm_b_w_uq': out['m_b_w_uq'], 'm_b_g_q_nope': out['m_b_g_q_nope'], 'm_b_g_q_rope': out['m_b_g_q_rope'], 'm_b_w_out': out['m_b_w_out'], 'm_mem_norm': out['m_mem_norm'], 'm_w_mem_kv': out['m_w_mem_kv'], 'm_g_mem_q': out['m_g_mem_q'], 'm_g_mem_k': out['m_g_mem_k'], 'v_a_norm': out['v_a_norm'], 'v_a_w_in': out['v_a_w_in'], 'v_a_w_out': out['v_a_w_out'], 'v_kv_norm': out['v_kv_norm'], 'v_w_dkv': out['v_w_dkv'], 'v_g_ckv': out['v_g_ckv'], 'v_w_ukv': out['v_w_ukv'], 'v_g_k_nope': out['v_g_k_nope'], 'v_g_k_rope': out['v_g_k_rope'], 'v_b_norm': out['v_b_norm'], 'v_b_w_in': out['v_b_w_in'], 'v_b_g_q_lat': out['v_b_g_q_lat'], 'v_b_w_uq': out['v_b_w_uq'], 'v_b_g_q_nope': out['v_b_g_q_nope'], 'v_b_g_q_rope': out['v_b_g_q_rope'], 'v_b_w_out': out['v_b_w_out'], 'v_mem_norm': out['v_mem_norm'], 'v_w_mem_kv': out['v_w_mem_kv'], 'v_g_mem_q': out['v_g_mem_q'], 'v_g_mem_k': out['v_g_mem_k']}


def _loss(weights, diff, rest, loss_target):
    with _jax.named_scope("forward"):
        args = {**rest, TWIN_DIFF_INPUT: diff, **{k: w.astype(_WEIGHT_DTYPES[k]) for k, w in weights.items()}}
        y = _forward(args)
    with _jax.named_scope("loss_head"):
        err = _jnp.square(y.astype(_jnp.float32) - loss_target)
        return 0.5 * _jnp.sum(_jnp.mean(err, axis=-1)) if err.ndim else 0.5 * err


def _adamw(w, g, m, v):
    m = ADAM_B1 * m + (1.0 - ADAM_B1) * g
    v = ADAM_B2 * v + (1.0 - ADAM_B2) * _jnp.square(g)
    m_hat = m / (1.0 - ADAM_B1 ** ADAM_STEP)
    v_hat = v / (1.0 - ADAM_B2 ** ADAM_STEP)
    delta = -ADAM_LR * (m_hat / (_jnp.sqrt(v_hat) + ADAM_EPS) + ADAM_WD * w)
    return delta, m, v


def reference(x, mem, positions, a_norm, a_w_in, a_w_out, kv_norm, w_dkv, g_ckv, w_ukv, g_k_nope, g_k_rope, b_norm, b_w_in, b_g_q_lat, b_w_uq, b_g_q_nope, b_g_q_rope, b_w_out, mem_norm, w_mem_kv, g_mem_q, g_mem_k, loss_target, m_a_norm, m_a_w_in, m_a_w_out, m_kv_norm, m_w_dkv, m_g_ckv, m_w_ukv, m_g_k_nope, m_g_k_rope, m_b_norm, m_b_w_in, m_b_g_q_lat, m_b_w_uq, m_b_g_q_nope, m_b_g_q_rope, m_b_w_out, m_mem_norm, m_w_mem_kv, m_g_mem_q, m_g_mem_k, v_a_norm, v_a_w_in, v_a_w_out, v_kv_norm, v_w_dkv, v_g_ckv, v_w_ukv, v_g_k_nope, v_g_k_rope, v_b_norm, v_b_w_in, v_b_g_q_lat, v_b_w_uq, v_b_g_q_nope, v_b_g_q_rope, v_b_w_out, v_mem_norm, v_w_mem_kv, v_g_mem_q, v_g_mem_k):
    given = dict(x=x, mem=mem, positions=positions, a_norm=a_norm, a_w_in=a_w_in, a_w_out=a_w_out, kv_norm=kv_norm, w_dkv=w_dkv, g_ckv=g_ckv, w_ukv=w_ukv, g_k_nope=g_k_nope, g_k_rope=g_k_rope, b_norm=b_norm, b_w_in=b_w_in, b_g_q_lat=b_g_q_lat, b_w_uq=b_w_uq, b_g_q_nope=b_g_q_nope, b_g_q_rope=b_g_q_rope, b_w_out=b_w_out, mem_norm=mem_norm, w_mem_kv=w_mem_kv, g_mem_q=g_mem_q, g_mem_k=g_mem_k, loss_target=loss_target, m_a_norm=m_a_norm, m_a_w_in=m_a_w_in, m_a_w_out=m_a_w_out, m_kv_norm=m_kv_norm, m_w_dkv=m_w_dkv, m_g_ckv=m_g_ckv, m_w_ukv=m_w_ukv, m_g_k_nope=m_g_k_nope, m_g_k_rope=m_g_k_rope, m_b_norm=m_b_norm, m_b_w_in=m_b_w_in, m_b_g_q_lat=m_b_g_q_lat, m_b_w_uq=m_b_w_uq, m_b_g_q_nope=m_b_g_q_nope, m_b_g_q_rope=m_b_g_q_rope, m_b_w_out=m_b_w_out, m_mem_norm=m_mem_norm, m_w_mem_kv=m_w_mem_kv, m_g_mem_q=m_g_mem_q, m_g_mem_k=m_g_mem_k, v_a_norm=v_a_norm, v_a_w_in=v_a_w_in, v_a_w_out=v_a_w_out, v_kv_norm=v_kv_norm, v_w_dkv=v_w_dkv, v_g_ckv=v_g_ckv, v_w_ukv=v_w_ukv, v_g_k_nope=v_g_k_nope, v_g_k_rope=v_g_k_rope, v_b_norm=v_b_norm, v_b_w_in=v_b_w_in, v_b_g_q_lat=v_b_g_q_lat, v_b_w_uq=v_b_w_uq, v_b_g_q_nope=v_b_g_q_nope, v_b_g_q_rope=v_b_g_q_rope, v_b_w_out=v_b_w_out, v_mem_norm=v_mem_norm, v_w_mem_kv=v_w_mem_kv, v_g_mem_q=v_g_mem_q, v_g_mem_k=v_g_mem_k)
    weights = {n: given[n] for n in TWIN_WEIGHTS}
    shared = {n: given[n] for n in SHARED_INPUTS}
    per_example = {n: given[n] for n in ['x', 'mem', 'positions']}
    grad_fn = _jax.value_and_grad(_loss, argnums=(0, 1))

    def one_microbatch(ex, loss_target):
        ex = dict(ex)
        diff = ex.pop(TWIN_DIFF_INPUT)
        return grad_fn(weights, diff, {**shared, **ex}, loss_target)

    if N_MICROBATCH == 1:
        loss, (grad_w, grad_x) = one_microbatch(per_example, given["loss_target"])
    else:
        def body(carry, xs):
            loss_sum, grad_sum = carry
            l_k, (gw_k, gx_k) = one_microbatch(xs[0], xs[1])
            with _jax.named_scope("update"):
                return (loss_sum + l_k, _jax.tree.map(_jnp.add, grad_sum, gw_k)), gx_k

        init = (_jnp.zeros((), _jnp.float32), _jax.tree.map(_jnp.zeros_like, weights))
        (loss, grad_w), grad_x = _jax.lax.scan(body, init, (per_example, given["loss_target"]))
    with _jax.named_scope("update"):
        delta_w, new_m, new_v = {}, {}, {}
        for n in TWIN_WEIGHTS:
            delta_w[n], new_m[n], new_v[n] = _adamw(weights[n], grad_w[n], given["m_" + n], given["v_" + n])
    return (loss, grad_x, *[grad_w[n] for n in TWIN_WEIGHTS], *[delta_w[n] for n in TWIN_WEIGHTS],
            *[new_m[n] for n in TWIN_WEIGHTS], *[new_v[n] for n in TWIN_WEIGHTS])
```

```python
import functools

import jax
import jax.numpy as jnp
from jax import lax
from jax.experimental import pallas as pl
from jax.experimental.pallas import tpu as pltpu

F32, BF16 = jnp.float32, jnp.bfloat16

N_DEV = 8
DH = 128
H_SB, H_MEM, H_MLA = 12, 4, 12
ROPE = 64
MLA_QK = DH + ROPE
EPS = 1e-6
ROPE_THETA = 10000.0
ADAM_LR, ADAM_B1, ADAM_B2, ADAM_EPS, ADAM_WD, ADAM_STEP = 0.001, 0.9, 0.999, 1e-08, 0.01, 10

LANES = 1024
VMEM_LIMIT = 48 * 1024 * 1024
VMEM_LIMIT_BIG = 56 * 1024 * 1024

NN = (((1,), (0,)), ((), ()))
NT = (((1,), (1,)), ((), ()))
TN = (((0,), (0,)), ((), ()))
_DIMS = {"nn": NN, "nt": NT, "tn": TN}


def _dot16(a, b, dims):
    return lax.dot_general(a.astype(BF16), b.astype(BF16), _DIMS[dims], preferred_element_type=F32)


@functools.partial(jax.custom_vjp, nondiff_argnums=(2,))
def _bdot(a, b, dims):
    return _dot16(a, b, dims)


def _bdot_fwd(a, b, dims):
    return _dot16(a, b, dims), (a, b)


def _bdot_bwd(dims, res, g):
    a, b = res
    if dims == "nn":
        return _dot16(g, b, "nt"), _dot16(a, g, "tn")
    return _dot16(g, b, "nn"), _dot16(g, a, "tn")


_bdot.defvjp(_bdot_fwd, _bdot_bwd)


def _tile(n, pref):
    if n <= pref:
        return n
    t = (pref // 128) * 128
    while n % t:
        t -= 128
    return t


def _mm(a, b, dims, *, name, out_dtype=F32, add=None, tm=1024, tn=1024, tk=1024):
    if dims == "tn":
        (K, M), (_, N) = a.shape, b.shape
    elif dims == "nt":
        (M, K), (N, _) = a.shape, b.shape
    else:
        (M, K), (_, N) = a.shape, b.shape
    tm, tn, tk = _tile(M, tm), _tile(N, tn), _tile(K, tk)
    nk = K // tk

    def body(*refs):
        if add is None:
            a_ref, b_ref, o_ref, acc_ref = refs
        else:
            a_ref, b_ref, r_ref, o_ref, acc_ref = refs
        k = pl.program_id(2)

        @pl.when(k == 0)
        def _():
            acc_ref[...] = jnp.zeros_like(acc_ref)

        acc_ref[...] += _dot16(a_ref[...], b_ref[...], dims)

        @pl.when(k == nk - 1)
        def _():
            r = acc_ref[...]
            if add is not None:
                r = r + r_ref[...]
            o_ref[...] = r.astype(o_ref.dtype)

    a_spec = pl.BlockSpec((tk, tm), lambda i, j, k: (k, i)) if dims == "tn" else pl.BlockSpec((tm, tk), lambda i, j, k: (i, k))
    b_spec = pl.BlockSpec((tn, tk), lambda i, j, k: (j, k)) if dims == "nt" else pl.BlockSpec((tk, tn), lambda i, j, k: (k, j))
    o_spec = pl.BlockSpec((tm, tn), lambda i, j, k: (i, j))
    in_specs, args = [a_spec, b_spec], [a, b]
    if add is not None:
        in_specs.append(o_spec)
        args.append(add)
    return pl.pallas_call(
        body, name=name, grid=(M // tm, N // tn, nk),
        in_specs=in_specs, out_specs=o_spec,
        out_shape=jax.ShapeDtypeStruct((M, N), out_dtype),
        scratch_shapes=[pltpu.VMEM((tm, tn), F32)],
        compiler_params=pltpu.CompilerParams(
            dimension_semantics=("parallel", "parallel", "arbitrary"), vmem_limit_bytes=VMEM_LIMIT),
    )(*args)


def _rowwise(fn, rows, consts, outs, accs=(), *, name, tm=256):
    S = rows[0][0].shape[0]
    tm = min(tm, S)
    nr, nc, no = len(rows), len(consts), len(outs)

    def body(*refs):
        res = fn(*[r[...] for r in refs[:nr + nc]])
        res = tuple(res) if isinstance(res, (tuple, list)) else (res,)
        orefs, arefs = refs[nr + nc:nr + nc + no], refs[nr + nc + no:]
        for r, v in zip(orefs, res[:no]):
            r[...] = v.astype(r.dtype)
        if arefs:
            @pl.when(pl.program_id(0) == 0)
            def _():
                for r in arefs:
                    r[...] = jnp.zeros_like(r)

            for r, v in zip(arefs, res[no:]):
                r[...] += v

    in_specs = [pl.BlockSpec((tm, w), lambda i, cb=cb: (i, cb)) for (_, w, cb) in rows]
    in_specs += [pl.BlockSpec(c.shape, lambda i: (0, 0)) for c in consts]
    out_specs = [pl.BlockSpec((tm, w), lambda i: (i, 0)) for (w, _) in outs]
    out_specs += [pl.BlockSpec(s, lambda i: (0, 0)) for s in accs]
    out_shape = [jax.ShapeDtypeStruct((S, w), dt) for (w, dt) in outs]
    out_shape += [jax.ShapeDtypeStruct(s, F32) for s in accs]
    res = pl.pallas_call(
        body, name=name, grid=(S // tm,), in_specs=in_specs, out_specs=out_specs, out_shape=out_shape,
        compiler_params=pltpu.CompilerParams(dimension_semantics=("arbitrary",), vmem_limit_bytes=VMEM_LIMIT),
    )(*[r[0] for r in rows], *consts)
    return res


def _rowwise_bwd(f, rows, consts, cts, row_grads, const_grads, *, name, add=None, out_dtypes=None, tm=256):
    nr, nc, nct = len(rows), len(consts), len(cts)
    all_rows = list(rows) + list(cts) + ([add] if add is not None else [])

    def fn(*args):
        nrow = len(all_rows)
        prim = [x.astype(F32) for x in args[:nr]] + [x.astype(F32) for x in args[nrow:]]
        ct = tuple(x.astype(F32) for x in args[nr:nr + nct])
        out, vjp = jax.vjp(f, *prim)
        gs = vjp(ct if isinstance(out, (tuple, list)) else ct[0])
        res = [gs[k] for k in row_grads]
        if add is not None:
            res[0] = res[0] + args[nrow - 1]
        return tuple(res) + tuple(gs[nr + k] for k in const_grads)

    out_dtypes = out_dtypes or [F32] * len(row_grads)
    outs = [(rows[k][1], dt) for k, dt in zip(row_grads, out_dtypes)]
    accs = [consts[k].shape for k in const_grads]
    return _rowwise(fn, all_rows, consts, outs, accs, name=name, tm=tm)


def _rms(x, g, n=None):
    ms = jnp.sum(x * x, axis=-1, keepdims=True) * (1.0 / (n or x.shape[-1]))
    return x * lax.rsqrt(ms + EPS) * g


def _sigmoid(x):
    return 1.0 / (1.0 + jnp.exp(-x))


def _swap_halves(x):
    r = lax.broadcasted_iota(jnp.int32, (DH, DH), 0)
    c = lax.broadcasted_iota(jnp.int32, (DH, DH), 1)
    half = ROPE // 2
    perm = jnp.where(((r < half) & (c == r + half)) | ((r >= half) & (r < ROPE) & (c == r - half)), 1.0, 0.0)
    return lax.dot_general(x, perm.astype(F32), NN, precision=lax.Precision.HIGHEST, preferred_element_type=F32)


def _rope128(x, g128, cs):
    y = _rms(x, g128, n=ROPE)
    return y * cs[:, :DH] + _swap_halves(y) * cs[:, DH:]


def _f_norm(x, g):
    return _rms(x, g)


def _f_norm2(x, g1, g2):
    xn = x * lax.rsqrt(jnp.mean(x * x, axis=-1, keepdims=True) + EPS)
    return xn * g1, xn * g2


def _f_kv1(ckr, cs, g_ckv, g_kr):
    w = g_ckv.shape[-1]
    return _rms(ckr[:, :w], g_ckv), _rope128(ckr[:, w:], g_kr, cs)


def _f_kv2(kv, kr, g_kn):
    ks, vs = [], []
    for h in range(H_MLA):
        ks += [_rms(kv[:, 2 * DH * h:2 * DH * h + DH], g_kn), kr]
        vs.append(kv[:, 2 * DH * h + DH:2 * DH * (h + 1)])
    return jnp.concatenate(ks, axis=1), jnp.concatenate(vs, axis=1)


def _f_q2(q, cs, g_n, g_r):
    out = []
    for h in range(H_MLA):
        out += [_rms(q[:, 2 * DH * h:2 * DH * h + DH], g_n), _rope128(q[:, 2 * DH * h + DH:2 * DH * (h + 1)], g_r, cs)]
    return jnp.concatenate(out, axis=1)


def _f_mix(att, g_att, q_m, g_m, mkv, g_q, g_k):
    mem_w = H_MEM * DH
    heads = []
    for h in range(H_MEM):
        kh = _rms(mkv[:, h * DH:(h + 1) * DH], g_k)
        vh = mkv[:, mem_w + h * DH:mem_w + (h + 1) * DH]
        qh = _rms(q_m[:, h * DH:(h + 1) * DH], g_q)
        s = _bdot(qh, kh, "nt") * (DH ** -0.5)
        p = jnp.exp(s - lax.stop_gradient(jnp.max(s, axis=-1, keepdims=True)))
        p = p / jnp.sum(p, axis=-1, keepdims=True)
        heads.append(_bdot(p, vh, "nn"))
    mo = jnp.concatenate(heads, axis=1)
    return jnp.concatenate([att * (g_att * _sigmoid(g_att)), mo * (g_m * _sigmoid(g_m))], axis=1)


def _split_dot(x, u):
    hi = x.astype(BF16)
    lo = (x - hi.astype(F32)).astype(BF16)
    return (lax.dot_general(hi, u, NN, preferred_element_type=F32)
            + lax.dot_general(lo, u, NN, preferred_element_type=F32))


def _tri(t):
    r = lax.broadcasted_iota(jnp.int32, (t, t), 0)
    c = lax.broadcasted_iota(jnp.int32, (t, t), 1)
    return r, c


def _softplus(z):
    return jnp.maximum(z, 0.0) + jnp.log(1.0 + jnp.exp(-jnp.abs(z)))


def _sb_fwd(qkv, *, tq, name):
    S = qkv.shape[0]
    tq = min(tq, S)
    H = H_SB
    scale = DH ** -0.5

    def body(q_ref, k_ref, v_ref, o_ref):
        i = pl.program_id(1)
        q = q_ref[...]
        r, c = _tri(tq)
        below = r > c
        u = below.astype(BF16)

        def step(jj, carry):
            acc, cb = carry
            off = pl.multiple_of((i - jj) * tq, tq)
            k = k_ref[pl.ds(off, tq), :]
            v = v_ref[pl.ds(off, tq), :]
            z = lax.dot_general(q, k, NT, preferred_element_type=F32) * scale
            sp = _softplus(z)
            mask = (r - c + jj * tq) > 0
            l = jnp.where(mask, -sp, 0.0)
            a = jnp.where(mask, jnp.exp(z - sp + _split_dot(l, u) + cb), 0.0)
            acc = acc + lax.dot_general(a.astype(BF16), v, NN, preferred_element_type=F32)
            return acc, cb + jnp.sum(l, axis=1, keepdims=True)

        acc, _ = lax.fori_loop(0, i + 1, step, (jnp.zeros((tq, DH), F32), jnp.zeros((tq, 1), F32)))
        o_ref[...] = acc

    return pl.pallas_call(
        body, name=name, grid=(H, S // tq),
        in_specs=[pl.BlockSpec((tq, DH), lambda h, i: (i, h)),
                  pl.BlockSpec((S, DH), lambda h, i: (0, H + h)),
                  pl.BlockSpec((S, DH), lambda h, i: (0, 2 * H + h))],
        out_specs=pl.BlockSpec((tq, DH), lambda h, i: (i, h)),
        out_shape=jax.ShapeDtypeStruct((S, H * DH), F32),
        compiler_params=pltpu.CompilerParams(dimension_semantics=("arbitrary", "arbitrary"), vmem_limit_bytes=VMEM_LIMIT),
    )(qkv, qkv, qkv)


def _sb_bwd(qkv, o, do, *, tq, name):
    S = qkv.shape[0]
    tq = min(tq, S)
    H = H_SB
    scale = DH ** -0.5

    def body(q_ref, k_ref, v_ref, o_ref, do_ref, dq_ref, dk_ref, dv_ref):
        i = pl.program_id(1)

        @pl.when(i == 0)
        def _():
            dk_ref[...] = jnp.zeros_like(dk_ref)
            dv_ref[...] = jnp.zeros_like(dv_ref)

        q = q_ref[...]
        do = do_ref[...]
        do16 = do.astype(BF16)
        dsum = jnp.sum(do16.astype(F32) * o_ref[...], axis=1, keepdims=True)
        r, c = _tri(tq)
        below = r > c
        u = below.astype(BF16)

        def step(jj, carry):
            dq, cb, ce = carry
            off = pl.multiple_of((i - jj) * tq, tq)
            k = k_ref[pl.ds(off, tq), :]
            v = v_ref[pl.ds(off, tq), :]
            z = lax.dot_general(q, k, NT, preferred_element_type=F32) * scale
            sp = _softplus(z)
            mask = (r - c + jj * tq) > 0
            l = jnp.where(mask, -sp, 0.0)
            a16 = jnp.where(mask, jnp.exp(z - sp + _split_dot(l, u) + cb), 0.0).astype(BF16)
            e = a16.astype(F32) * lax.dot_general(do16, v, NT, preferred_element_type=F32)
            left = dsum - (ce + _split_dot(e, u) + e)
            beta = jnp.exp(z - sp)
            dz = jnp.where(mask, (e - beta * (e + left)) * scale, 0.0).astype(BF16)
            dq = dq + lax.dot_general(dz, k, NN, preferred_element_type=F32)
            dk_ref[pl.ds(off, tq), :] += lax.dot_general(dz, q, TN, preferred_element_type=F32)
            dv_ref[pl.ds(off, tq), :] += lax.dot_general(a16, do16, TN, preferred_element_type=F32)
            return dq, cb + jnp.sum(l, axis=1, keepdims=True), ce + jnp.sum(e, axis=1, keepdims=True)

        zero = jnp.zeros((tq, 1), F32)
        dq, _, _ = lax.fori_loop(0, i + 1, step, (jnp.zeros((tq, DH), F32), zero, zero))
        dq_ref[...] = dq

    blk = pl.BlockSpec((tq, DH), lambda h, i: (i, h))
    whole = pl.BlockSpec((S, DH), lambda h, i: (0, h))
    shp = jax.ShapeDtypeStruct((S, H * DH), F32)
    return pl.pallas_call(
        body, name=name, grid=(H, S // tq),
        in_specs=[blk, pl.BlockSpec((S, DH), lambda h, i: (0, H + h)),
                  pl.BlockSpec((S, DH), lambda h, i: (0, 2 * H + h)), blk, blk],
        out_specs=[blk, whole, whole], out_shape=[shp, shp, shp],
        compiler_params=pltpu.CompilerParams(dimension_semantics=("arbitrary", "arbitrary"), vmem_limit_bytes=VMEM_LIMIT),
    )(qkv, qkv, qkv, o, do)


def _mla_fwd(q, k, v, *, tq, name):
    S = q.shape[0]
    tq = min(tq, S)
    H = H_MLA
    scale = MLA_QK ** -0.5

    def body(q_ref, k_ref, v_ref, o_ref, lse_ref):
        i = pl.program_id(1)
        qb = q_ref[...]
        r, c = _tri(tq)

        def step(j, carry):
            m, den, acc = carry
            off = pl.multiple_of(j * tq, tq)
            kb = k_ref[pl.ds(off, tq), :]
            vb = v_ref[pl.ds(off, tq), :]
            s = lax.dot_general(qb, kb, NT, preferred_element_type=F32) * scale
            s = jnp.where((r - c + (i - j) * tq) >= 0, s, -1e30)
            m_new = jnp.maximum(m, jnp.max(s, axis=1, keepdims=True))
            p = jnp.exp(s - m_new)
            alpha = jnp.exp(m - m_new)
            den = alpha * den + jnp.sum(p, axis=1, keepdims=True)
            acc = alpha * acc + lax.dot_general(p.astype(BF16), vb, NN, preferred_element_type=F32)
            return m_new, den, acc

        init = (jnp.full((tq, 1), -1e30, F32), jnp.zeros((tq, 1), F32), jnp.zeros((tq, DH), F32))
        m, den, acc = lax.fori_loop(0, i + 1, step, init)
        o_ref[...] = acc / den
        lse_ref[0] = m + jnp.log(den)

    return pl.pallas_call(
        body, name=name, grid=(H, S // tq),
        in_specs=[pl.BlockSpec((tq, 2 * DH), lambda h, i: (i, h)),
                  pl.BlockSpec((S, 2 * DH), lambda h, i: (0, h)),
                  pl.BlockSpec((S, DH), lambda h, i: (0, h))],
        out_specs=[pl.BlockSpec((tq, DH), lambda h, i: (i, h)), pl.BlockSpec((1, tq, 1), lambda h, i: (h, i, 0))],
        out_shape=[jax.ShapeDtypeStruct((S, H * DH), F32), jax.ShapeDtypeStruct((H, S, 1), F32)],
        compiler_params=pltpu.CompilerParams(dimension_semantics=("arbitrary", "arbitrary"), vmem_limit_bytes=VMEM_LIMIT),
    )(q, k, v)


def _mla_bwd(q, k, v, o, do, lse, *, tq, name):
    S = q.shape[0]
    tq = min(tq, S)
    H = H_MLA
    scale = MLA_QK ** -0.5

    def body(q_ref, k_ref, v_ref, o_ref, do_ref, lse_ref, dq_ref, dk_ref, dv_ref):
        i = pl.program_id(1)

        @pl.when(i == 0)
        def _():
            dk_ref[...] = jnp.zeros_like(dk_ref)
            dv_ref[...] = jnp.zeros_like(dv_ref)

        qb = q_ref[...]
        do = do_ref[...]
        do16 = do.astype(BF16)
        dsum = jnp.sum(do * o_ref[...], axis=1, keepdims=True)
        lse = lse_ref[0]
        r, c = _tri(tq)

        def step(j, dq):
            off = pl.multiple_of(j * tq, tq)
            kb = k_ref[pl.ds(off, tq), :]
            vb = v_ref[pl.ds(off, tq), :]
            s = lax.dot_general(qb, kb, NT, preferred_element_type=F32) * scale
            p = jnp.where((r - c + (i - j) * tq) >= 0, jnp.exp(s - lse), 0.0)
            dp = lax.dot_general(do16, vb, NT, preferred_element_type=F32)
            ds = (p * (dp - dsum) * scale).astype(BF16)
            dk_ref[pl.ds(off, tq), :] += lax.dot_general(ds, qb, TN, preferred_element_type=F32)
            dv_ref[pl.ds(off, tq), :] += lax.dot_general(p.astype(BF16), do16, TN, preferred_element_type=F32)
            return dq + lax.dot_general(ds, kb, NN, preferred_element_type=F32)

        dq_ref[...] = lax.fori_loop(0, i + 1, step, jnp.zeros((tq, 2 * DH), F32))

    blk = pl.BlockSpec((tq, DH), lambda h, i: (i, h))
    blk2 = pl.BlockSpec((tq, 2 * DH), lambda h, i: (i, h))
    return pl.pallas_call(
        body, name=name, grid=(H, S // tq),
        in_specs=[blk2, pl.BlockSpec((S, 2 * DH), lambda h, i: (0, h)), pl.BlockSpec((S, DH), lambda h, i: (0, h)),
                  blk, blk, pl.BlockSpec((1, tq, 1), lambda h, i: (h, i, 0))],
        out_specs=[blk2, pl.BlockSpec((S, 2 * DH), lambda h, i: (0, h)), pl.BlockSpec((S, DH), lambda h, i: (0, h))],
        out_shape=[jax.ShapeDtypeStruct((S, H * 2 * DH), F32), jax.ShapeDtypeStruct((S, H * 2 * DH), F32),
                   jax.ShapeDtypeStruct((S, H * DH), F32)],
        compiler_params=pltpu.CompilerParams(dimension_semantics=("arbitrary", "arbitrary"), vmem_limit_bytes=VMEM_LIMIT_BIG),
    )(q, k, v, o, do, lse)


def _mesh_pos():
    return lax.axis_index("x"), lax.axis_index("y"), lax.axis_index("c")


def _all_gather(block, *, name):
    R, C = block.shape

    def body(x_ref, out_ref, send_sems, recv_sems, local_sem):
        x, y, c = _mesh_pos()
        me, sibling = (x, y, c), (x, y, 1 - c)
        chips = [(1 - x, y), (x, 1 - y), (1 - x, 1 - y)]

        def slot(px, py, pc):
            return out_ref.at[4 * px + 2 * py + pc]

        def copy(k, blk, to, src=None):
            return pltpu.make_async_remote_copy(
                src_ref=slot(*blk) if src is None else src, dst_ref=slot(*blk),
                send_sem=send_sems.at[k], recv_sem=recv_sems.at[k],
                device_id=to, device_id_type=pl.DeviceIdType.MESH)

        mine = pltpu.make_async_copy(x_ref, slot(*me), local_sem)
        mine.start()
        first = [copy(0, me, sibling, src=x_ref)]
        first += [copy(1 + j, me, (*chip, c), src=x_ref) for j, chip in enumerate(chips)]
        for cp in first:
            cp.start()
        passed = [copy(4 + j, (*chip, c), sibling) for j, chip in enumerate(chips)]
        for j, chip in enumerate(chips):
            copy(1 + j, (*chip, c), me).wait_recv()
            passed[j].start()
        copy(0, sibling, me).wait_recv()
        for j, chip in enumerate(chips):
            copy(4 + j, (*chip, 1 - c), me).wait_recv()
        for cp in first + passed:
            cp.wait_send()
        mine.wait()

    return pl.pallas_call(
        body, name=name,
        out_shape=jax.ShapeDtypeStruct((N_DEV, R, C), block.dtype),
        in_specs=[pl.BlockSpec(memory_space=pl.ANY)], out_specs=pl.BlockSpec(memory_space=pl.ANY),
        scratch_shapes=[pltpu.SemaphoreType.DMA((7,)), pltpu.SemaphoreType.DMA((7,)), pltpu.SemaphoreType.DMA],
    )(block)


def _all_to_all(send, *, name):
    def body(s_ref, r_ref, send_sems, recv_sems, local_sem):
        x, y, c = _mesh_pos()
        me = 4 * x + 2 * y + c
        mine = pltpu.make_async_copy(s_ref.at[me], r_ref.at[me], local_sem)
        mine.start()
        peers = []
        for k in range(1, N_DEV):
            px = 1 - x if k & 4 else x
            py = 1 - y if k & 2 else y
            pc = 1 - c if k & 1 else c
            peers.append((px, py, pc))
        sends = [pltpu.make_async_remote_copy(
            src_ref=s_ref.at[4 * p[0] + 2 * p[1] + p[2]], dst_ref=r_ref.at[me],
            send_sem=send_sems.at[k], recv_sem=recv_sems.at[k],
            device_id=p, device_id_type=pl.DeviceIdType.MESH) for k, p in enumerate(peers)]
        for cp in sends:
            cp.start()
        for k, p in enumerate(peers):
            lin = 4 * p[0] + 2 * p[1] + p[2]
            pltpu.make_async_remote_copy(
                src_ref=s_ref.at[lin], dst_ref=r_ref.at[lin], send_sem=send_sems.at[k], recv_sem=recv_sems.at[k],
                device_id=p, device_id_type=pl.DeviceIdType.MESH).wait_recv()
        for cp in sends:
            cp.wait_send()
        mine.wait()

    return pl.pallas_call(
        body, name=name,
        out_shape=jax.ShapeDtypeStruct(send.shape, send.dtype),
        in_specs=[pl.BlockSpec(memory_space=pl.ANY)], out_specs=pl.BlockSpec(memory_space=pl.ANY),
        scratch_shapes=[pltpu.SemaphoreType.DMA((7,)), pltpu.SemaphoreType.DMA((7,)), pltpu.SemaphoreType.DMA],
    )(send)


def _reduce_adamw(recv, w, m, v, *, name, tr=128):
    R = w.shape[0]

    def body(g_ref, w_ref, m_ref, v_ref, og_ref, od_ref, om_ref, ov_ref):
        g = g_ref[0]
        for s in range(1, N_DEV):
            g = g + g_ref[s]
        mn = ADAM_B1 * m_ref[...] + (1.0 - ADAM_B1) * g
        vn = ADAM_B2 * v_ref[...] + (1.0 - ADAM_B2) * jnp.square(g)
        m_hat = mn / (1.0 - ADAM_B1 ** ADAM_STEP)
        v_hat = vn / (1.0 - ADAM_B2 ** ADAM_STEP)
        og_ref[...] = g
        od_ref[...] = -ADAM_LR * (m_hat / (jnp.sqrt(v_hat) + ADAM_EPS) + ADAM_WD * w_ref[...])
        om_ref[...] = mn
        ov_ref[...] = vn

    blk = pl.BlockSpec((tr, LANES), lambda i: (i, 0))
    shp = jax.ShapeDtypeStruct((R, LANES), F32)
    return pl.pallas_call(
        body, name=name, grid=(R // tr,),
        in_specs=[pl.BlockSpec((N_DEV, tr, LANES), lambda i: (0, i, 0)), blk, blk, blk],
        out_specs=[blk, blk, blk, blk], out_shape=[shp, shp, shp, shp],
        compiler_params=pltpu.CompilerParams(dimension_semantics=("parallel",), vmem_limit_bytes=VMEM_LIMIT),
    )(recv, w, m, v)


SHARDED = (("a_norm", 1), ("a_w_in", 2), ("a_w_out", 1), ("w_dkv", 0), ("w_ukv", 1), ("b_w_in", 2),
           ("b_w_uq", 2), ("b_w_out", 1), ("w_mem_kv", 1))
SMALL = ("kv_norm", "g_ckv", "g_k_nope", "g_k_rope", "b_norm", "b_g_q_lat", "b_g_q_nope", "b_g_q_rope",
         "mem_norm", "g_mem_q", "g_mem_k")
WEIGHTS = ("a_norm", "a_w_in", "a_w_out", "kv_norm", "w_dkv", "g_ckv", "w_ukv", "g_k_nope", "g_k_rope", "b_norm",
           "b_w_in", "b_g_q_lat", "b_w_uq", "b_g_q_nope", "b_g_q_rope", "b_w_out", "mem_norm", "w_mem_kv",
           "g_mem_q", "g_mem_k")
ROW_MULT = 8
ROW_BLOCK = 128


def _rows_of(n, mult):
    rows = -(-n // LANES)
    return -(-rows // mult) * mult


def _to_rows(flat, mult):
    n = flat.shape[-1]
    rows = _rows_of(n, mult)
    pad = [(0, 0)] * (flat.ndim - 1) + [(0, rows * LANES - n)]
    return jnp.pad(flat, pad).reshape(*flat.shape[:-1], rows, LANES)


def _split8(full, axis):
    shp = full.shape
    t = full.reshape(*shp[:axis], N_DEV, shp[axis] // N_DEV, *shp[axis + 1:])
    return jnp.moveaxis(t, axis, 0).reshape(N_DEV, -1)


def _join8(rows, axis, shard_shape):
    t = rows.reshape(N_DEV, *shard_shape)
    t = jnp.moveaxis(t, 0, axis)
    return t.reshape(*shard_shape[:axis], N_DEV * shard_shape[axis], *shard_shape[axis + 1:])


def _pack_local(vals, loss_slot):
    parts = [_to_rows(vals[n].reshape(-1), ROW_MULT) for n, _ in SHARDED]
    parts += [_to_rows(vals[n].reshape(-1), ROW_MULT) for n in SMALL]
    parts.append(_to_rows(loss_slot.reshape(-1), ROW_MULT))
    slab = jnp.concatenate(parts, axis=0)
    return jnp.pad(slab, ((0, -slab.shape[0] % ROW_BLOCK), (0, 0)))


def _pack_send(full_grads, loss_part):
    parts = [_to_rows(_split8(full_grads[n], ax), ROW_MULT) for n, ax in SHARDED]
    for n in SMALL:
        p = _to_rows(full_grads[n].reshape(-1), ROW_MULT)
        parts.append(jnp.broadcast_to(p[None], (N_DEV, *p.shape)))
    p = _to_rows(loss_part.reshape(-1), ROW_MULT)
    parts.append(jnp.broadcast_to(p[None], (N_DEV, *p.shape)))
    slab = jnp.concatenate(parts, axis=1)
    return jnp.pad(slab, ((0, 0), (0, -slab.shape[1] % ROW_BLOCK), (0, 0)))


def _unpack_local(slab, shapes):
    out, row = {}, 0
    for n in [n for n, _ in SHARDED] + list(SMALL):
        size = 1
        for d in shapes[n]:
            size *= d
        rows = _rows_of(size, ROW_MULT)
        out[n] = slab[row:row + rows].reshape(-1)[:size].reshape(shapes[n])
        row += rows
    return out, slab[row, 0]


def kernel(x, mem, positions, a_norm, a_w_in, a_w_out, kv_norm, w_dkv, g_ckv, w_ukv, g_k_nope, g_k_rope, b_norm, b_w_in, b_g_q_lat, b_w_uq, b_g_q_nope, b_g_q_rope, b_w_out, mem_norm, w_mem_kv, g_mem_q, g_mem_k, loss_target, m_a_norm, m_a_w_in, m_a_w_out, m_kv_norm, m_w_dkv, m_g_ckv, m_w_ukv, m_g_k_nope, m_g_k_rope, m_b_norm, m_b_w_in, m_b_g_q_lat, m_b_w_uq, m_b_g_q_nope, m_b_g_q_rope, m_b_w_out, m_mem_norm, m_w_mem_kv, m_g_mem_q, m_g_mem_k, v_a_norm, v_a_w_in, v_a_w_out, v_kv_norm, v_w_dkv, v_g_ckv, v_w_ukv, v_g_k_nope, v_g_k_rope, v_b_norm, v_b_w_in, v_b_g_q_lat, v_b_w_uq, v_b_g_q_nope, v_b_g_q_rope, v_b_w_out, v_mem_norm, v_w_mem_kv, v_g_mem_q, v_g_mem_k):
    wts = dict(a_norm=a_norm, a_w_in=a_w_in, a_w_out=a_w_out, kv_norm=kv_norm, w_dkv=w_dkv, g_ckv=g_ckv, w_ukv=w_ukv,
               g_k_nope=g_k_nope, g_k_rope=g_k_rope, b_norm=b_norm, b_w_in=b_w_in, b_g_q_lat=b_g_q_lat, b_w_uq=b_w_uq,
               b_g_q_nope=b_g_q_nope, b_g_q_rope=b_g_q_rope, b_w_out=b_w_out, mem_norm=mem_norm, w_mem_kv=w_mem_kv,
               g_mem_q=g_mem_q, g_mem_k=g_mem_k)
    mom = dict(a_norm=m_a_norm, a_w_in=m_a_w_in, a_w_out=m_a_w_out, kv_norm=m_kv_norm, w_dkv=m_w_dkv, g_ckv=m_g_ckv,
               w_ukv=m_w_ukv, g_k_nope=m_g_k_nope, g_k_rope=m_g_k_rope, b_norm=m_b_norm, b_w_in=m_b_w_in,
               b_g_q_lat=m_b_g_q_lat, b_w_uq=m_b_w_uq, b_g_q_nope=m_b_g_q_nope, b_g_q_rope=m_b_g_q_rope,
               b_w_out=m_b_w_out, mem_norm=m_mem_norm, w_mem_kv=m_w_mem_kv, g_mem_q=m_g_mem_q, g_mem_k=m_g_mem_k)
    var = dict(a_norm=v_a_norm, a_w_in=v_a_w_in, a_w_out=v_a_w_out, kv_norm=v_kv_norm, w_dkv=v_w_dkv, g_ckv=v_g_ckv,
               w_ukv=v_w_ukv, g_k_nope=v_g_k_nope, g_k_rope=v_g_k_rope, b_norm=v_b_norm, b_w_in=v_b_w_in,
               b_g_q_lat=v_b_g_q_lat, b_w_uq=v_b_w_uq, b_g_q_nope=v_b_g_q_nope, b_g_q_rope=v_b_g_q_rope,
               b_w_out=v_b_w_out, mem_norm=v_mem_norm, w_mem_kv=v_w_mem_kv, g_mem_q=v_g_mem_q, g_mem_k=v_g_mem_k)
    shapes = {n: wts[n].shape for n in WEIGHTS}
    S, D = x.shape[1], x.shape[2]
    xs, ms, tgt = x[0], mem[0], loss_target[0]
    sb_w, mem_w, mla_w = H_SB * DH, H_MEM * DH, H_MLA * DH
    q_lora, kv_lora = b_g_q_lat.shape[-1], g_ckv.shape[-1]

    pieces = [_to_rows(lax.bitcast_convert_type(a_norm.reshape(-1), BF16).reshape(-1), 16)]
    pieces += [_to_rows(wts[n].astype(BF16).reshape(-1), 16) for n, _ in SHARDED[1:]]
    gathered = _all_gather(jnp.concatenate(pieces, axis=0), name="gather_weights")
    full, row = {}, 0
    for (n, ax), p in zip(SHARDED, pieces):
        size = wts[n].size * (2 if n == "a_norm" else 1)
        flat = gathered[:, row:row + p.shape[0]].reshape(N_DEV, -1)[:, :size]
        if n == "a_norm":
            flat = lax.bitcast_convert_type(flat.reshape(N_DEV, -1, 2), F32)
        full[n] = _join8(flat, ax, wts[n].shape)
        row += p.shape[0]
    g_a = full["a_norm"]
    w_a_in, w_a_out = full["a_w_in"][0], full["a_w_out"][0]
    w_dkv_p = jnp.pad(full["w_dkv"], ((0, 0), (0, ROPE)))
    w_ukv_f = full["w_ukv"]
    wb = full["b_w_in"][0]
    w_b_in = jnp.concatenate([wb[:, q_lora:q_lora + mla_w], wb[:, :q_lora], wb[:, q_lora + mla_w:]], axis=1)
    w_uq_p = jnp.pad(full["b_w_uq"][0].reshape(q_lora, H_MLA, MLA_QK),
                     ((0, 0), (0, 0), (0, 2 * DH - MLA_QK))).reshape(q_lora, H_MLA * 2 * DH)
    w_b_out = full["b_w_out"][0]
    w_mem = full["w_mem_kv"]

    pad128 = lambda g: jnp.pad(g.reshape(1, -1), ((0, 0), (0, DH - ROPE)))
    row2 = lambda g: g.reshape(1, -1)
    g_kr, g_qr = pad128(g_k_rope), pad128(b_g_q_rope[0])
    g_kv, g_b, g_c, g_kn = row2(kv_norm), row2(b_norm[0]), row2(g_ckv), row2(g_k_nope)
    g_ql, g_qn = row2(b_g_q_lat[0]), row2(b_g_q_nope[0])

    inv_freq = jnp.power(ROPE_THETA, -jnp.arange(0, ROPE, 2, dtype=F32) / ROPE)
    ang = positions[0].astype(F32)[:, None] * inv_freq
    z64 = jnp.zeros((S, DH - ROPE), F32)
    cs = jnp.concatenate([jnp.cos(ang), jnp.cos(ang), z64, -jnp.sin(ang), jnp.sin(ang), z64], axis=1)

    mn, mkv = [], []
    for l in range(2):
        mn.append(_rowwise(_f_norm, [(ms, D, 0)], [row2(mem_norm[l])], [(D, BF16)], name=f"mem_norm{l}")[0])
        mkv.append(_mm(mn[l], w_mem[l], "nn", name=f"mem_kv{l}"))
    g_mq = [row2(g_mem_q[l]) for l in range(2)]
    g_mk = [row2(g_mem_k[l]) for l in range(2)]

    h0 = _rowwise(_f_norm, [(xs, D, 0)], [g_a], [(D, BF16)], name="a_norm_fwd")[0]
    pa = _mm(h0, w_a_in, "nn", name="a_in")
    qkv = pa[:, :3 * sb_w].astype(BF16)
    sb = _sb_fwd(qkv, tq=256, name="sb_fwd")
    mix_a_rows = [(sb, sb_w, 0), (pa, sb_w, 3), (pa, mem_w, 4 * sb_w // mem_w), (pa, mem_w, 4 * sb_w // mem_w + 1)]
    mixed_a = _rowwise(_f_mix, mix_a_rows, [mkv[0], g_mq[0], g_mk[0]], [(sb_w + mem_w, BF16)], name="a_mix_fwd")[0]
    x1 = _mm(mixed_a, w_a_out, "nn", add=xs, name="a_out")

    hk, hb = _rowwise(_f_norm2, [(x1, D, 0)], [g_kv, g_b], [(D, BF16), (D, BF16)], name="b_norm_fwd")
    ckr = _mm(hk, w_dkv_p, "nn", name="kv_down")
    cn, kr = _rowwise(_f_kv1, [(ckr, kv_lora + DH, 0), (cs, 2 * DH, 0)], [g_c, g_kr],
                      [(kv_lora, BF16), (DH, F32)], name="kv1_fwd")
    kvu = _mm(cn, w_ukv_f, "nn", name="kv_up")
    k2, v2 = _rowwise(_f_kv2, [(kvu, H_MLA * 2 * DH, 0), (kr, DH, 0)], [g_kn],
                      [(H_MLA * 2 * DH, BF16), (mla_w, BF16)], name="kv2_fwd")
    pb = _mm(hb, w_b_in, "nn", name="b_in")
    ql = _rowwise(_f_norm, [(pb, q_lora, mla_w // q_lora)], [g_ql], [(q_lora, BF16)], name="q_lat_fwd")[0]
    qraw = _mm(ql, w_uq_p, "nn", name="q_up")
    q2 = _rowwise(_f_q2, [(qraw, H_MLA * 2 * DH, 0), (cs, 2 * DH, 0)], [g_qn, g_qr],
                  [(H_MLA * 2 * DH, BF16)], name="q2_fwd")[0]
    att, lse = _mla_fwd(q2, k2, v2, tq=256, name="mla_fwd")
    cb = (mla_w + q_lora) // mem_w
    mix_b_rows = [(att, mla_w, 0), (pb, mla_w, 0), (pb, mem_w, cb), (pb, mem_w, cb + 1)]
    mixed_b = _rowwise(_f_mix, mix_b_rows, [mkv[1], g_mq[1], g_mk[1]], [(mla_w + mem_w, BF16)], name="b_mix_fwd")[0]
    y = _mm(mixed_b, w_b_out, "nn", add=x1, name="b_out")

    def loss_fn(yb, tb):
        err = yb - tb
        part = 0.5 * jnp.sum(jnp.sum(err * err, axis=-1, keepdims=True) * (1.0 / D))
        return err * (1.0 / D), jnp.full((1, DH), part, F32)

    dy, loss_part = _rowwise(loss_fn, [(y, D, 0), (tgt, D, 0)], [], [(D, F32)], [(1, DH)], name="loss")

    gr = {}
    d_mixed_b = _mm(dy, w_b_out, "nt", name="b_out_dx")
    gr["b_w_out"] = _mm(mixed_b, dy, "tn", name="b_out_dw")[None]
    d_att, d_gmla, d_qm_b, d_gm_b, d_mkv1, d_gq1, d_gk1 = _rowwise_bwd(
        _f_mix, mix_b_rows, [mkv[1], g_mq[1], g_mk[1]], [(d_mixed_b, mla_w + mem_w, 0)],
        [0, 1, 2, 3], [0, 1, 2], name="b_mix_bwd")
    dq2, dk2, dv2 = _mla_bwd(q2, k2, v2, att, d_att, lse, tq=256, name="mla_bwd")
    d_qraw, d_gqn, d_gqr = _rowwise_bwd(
        _f_q2, [(qraw, H_MLA * 2 * DH, 0), (cs, 2 * DH, 0)], [g_qn, g_qr], [(dq2, H_MLA * 2 * DH, 0)],
        [0], [0, 1], name="q2_bwd")
    d_ql = _mm(d_qraw, w_uq_p, "nt", name="q_up_dx")
    d_wuq = _mm(ql, d_qraw, "tn", name="q_up_dw")
    gr["b_w_uq"] = d_wuq.reshape(q_lora, H_MLA, 2 * DH)[:, :, :MLA_QK].reshape(1, q_lora, H_MLA * MLA_QK)
    d_qlat, d_gql = _rowwise_bwd(_f_norm, [(pb, q_lora, mla_w // q_lora)], [g_ql], [(d_ql, q_lora, 0)],
                                 [0], [0], name="q_lat_bwd")
    d_pb = jnp.concatenate([d_gmla, d_qlat, d_qm_b, d_gm_b], axis=1)
    d_hb = _mm(d_pb, w_b_in, "nt", name="b_in_dx")
    d_wbin = _mm(hb, d_pb, "tn", name="b_in_dw")
    gr["b_w_in"] = jnp.concatenate([d_wbin[:, mla_w:mla_w + q_lora], d_wbin[:, :mla_w], d_wbin[:, mla_w + q_lora:]],
                                   axis=1)[None]
    d_kvu, d_kr, d_gkn = _rowwise_bwd(
        _f_kv2, [(kvu, H_MLA * 2 * DH, 0), (kr, DH, 0)], [g_kn], [(dk2, H_MLA * 2 * DH, 0), (dv2, mla_w, 0)],
        [0, 1], [0], name="kv2_bwd")
    d_cn = _mm(d_kvu, w_ukv_f, "nt", name="kv_up_dx")
    gr["w_ukv"] = _mm(cn, d_kvu, "tn", name="kv_up_dw")
    d_ckr, d_gc, d_gkr = _rowwise_bwd(
        _f_kv1, [(ckr, kv_lora + DH, 0), (cs, 2 * DH, 0)], [g_c, g_kr], [(d_cn, kv_lora, 0), (d_kr, DH, 0)],
        [0], [0, 1], name="kv1_bwd")
    d_hk = _mm(d_ckr, w_dkv_p, "nt", name="kv_down_dx")
    gr["w_dkv"] = _mm(hk, d_ckr, "tn", name="kv_down_dw")[:, :kv_lora + ROPE]
    d_x1, d_gkv, d_gb = _rowwise_bwd(_f_norm2, [(x1, D, 0)], [g_kv, g_b], [(d_hk, D, 0), (d_hb, D, 0)],
                                     [0], [0, 1], add=(dy, D, 0), name="b_norm_bwd")
    d_mixed_a = _mm(d_x1, w_a_out, "nt", name="a_out_dx")
    gr["a_w_out"] = _mm(mixed_a, d_x1, "tn", name="a_out_dw")[None]
    d_sb, d_gsb, d_qm_a, d_gm_a, d_mkv0, d_gq0, d_gk0 = _rowwise_bwd(
        _f_mix, mix_a_rows, [mkv[0], g_mq[0], g_mk[0]], [(d_mixed_a, sb_w + mem_w, 0)],
        [0, 1, 2, 3], [0, 1, 2], name="a_mix_bwd")
    dq, dk, dv = _sb_bwd(qkv, sb, d_sb, tq=256, name="sb_bwd")
    d_pa = jnp.concatenate([dq, dk, dv, d_gsb, d_qm_a, d_gm_a], axis=1)
    d_h0 = _mm(d_pa, w_a_in, "nt", name="a_in_dx")
    gr["a_w_in"] = _mm(h0, d_pa, "tn", name="a_in_dw")[None]
    grad_x, d_ga = _rowwise_bwd(_f_norm, [(xs, D, 0)], [g_a], [(d_h0, D, 0)], [0], [0], add=(d_x1, D, 0),
                                name="a_norm_bwd")
    gr["a_norm"] = d_ga

    d_wmem, d_mnorm = [], []
    for l, d_mkv in enumerate((d_mkv0, d_mkv1)):
        d_mn = _mm(d_mkv, w_mem[l], "nt", name=f"mem_kv_dx{l}")
        d_wmem.append(_mm(mn[l], d_mkv, "tn", name=f"mem_kv_dw{l}"))
        d_mnorm.append(_rowwise_bwd(_f_norm, [(ms, D, 0)], [row2(mem_norm[l])], [(d_mn, D, 0)], [], [0],
                                    name=f"mem_norm_bwd{l}")[0])
    gr["w_mem_kv"] = jnp.stack(d_wmem)
    gr["mem_norm"] = jnp.concatenate(d_mnorm, axis=0)
    gr["g_mem_q"] = jnp.concatenate([d_gq0, d_gq1], axis=0)
    gr["g_mem_k"] = jnp.concatenate([d_gk0, d_gk1], axis=0)
    gr["kv_norm"], gr["b_norm"], gr["g_ckv"], gr["g_k_nope"] = d_gkv, d_gb, d_gc, d_gkn
    gr["g_k_rope"], gr["b_g_q_rope"] = d_gkr[:, :ROPE], d_gqr[:, :ROPE]
    gr["b_g_q_lat"], gr["b_g_q_nope"] = d_gql, d_gqn

    recv = _all_to_all(_pack_send(gr, loss_part[0, :1]), name="exchange_grads")
    zero = jnp.zeros((1,), F32)
    og, od, om, ov = _reduce_adamw(recv, _pack_local(wts, zero), _pack_local(mom, zero), _pack_local(var, zero),
                                   name="reduce_adamw", tr=ROW_BLOCK)
    grads, loss = _unpack_local(og, shapes)
    delta, _ = _unpack_local(od, shapes)
    new_m, _ = _unpack_local(om, shapes)
    new_v, _ = _unpack_local(ov, shapes)
    return (loss, grad_x[None], *[grads[n] for n in WEIGHTS], *[delta[n] for n in WEIGHTS],
            *[new_m[n] for n in WEIGHTS], *[new_v[n] for n in WEIGHTS])
```

```python
import functools

import jax
import jax.numpy as jnp
from jax import lax
from jax.experimental import pallas as pl
from jax.experimental.pallas import tpu as pltpu

F32, BF16 = jnp.float32, jnp.bfloat16

N_DEV = 8
DH = 128
H_SB, H_MEM, H_MLA = 12, 4, 12
ROPE = 64
MLA_QK = DH + ROPE
EPS = 1e-6
ROPE_THETA = 10000.0
ADAM_LR, ADAM_B1, ADAM_B2, ADAM_EPS, ADAM_WD, ADAM_STEP = 0.001, 0.9, 0.999, 1e-08, 0.01, 10

LANES = 1024
VMEM_LIMIT = 48 * 1024 * 1024
VMEM_LIMIT_BIG = 56 * 1024 * 1024

NN = (((1,), (0,)), ((), ()))
NT = (((1,), (1,)), ((), ()))
TN = (((0,), (0,)), ((), ()))
_DIMS = {"nn": NN, "nt": NT, "tn": TN}


def _dot16(a, b, dims):
    return lax.dot_general(a.astype(BF16), b.astype(BF16), _DIMS[dims], preferred_element_type=F32)


@functools.partial(jax.custom_vjp, nondiff_argnums=(2,))
def _bdot(a, b, dims):
    return _dot16(a, b, dims)


def _bdot_fwd(a, b, dims):
    return _dot16(a, b, dims), (a, b)


def _bdot_bwd(dims, res, g):
    a, b = res
    if dims == "nn":
        return _dot16(g, b, "nt"), _dot16(a, g, "tn")
    return _dot16(g, b, "nn"), _dot16(g, a, "tn")


_bdot.defvjp(_bdot_fwd, _bdot_bwd)


def _tile(n, pref):
    if n <= pref:
        return n
    t = (pref // 128) * 128
    while n % t:
        t -= 128
    return t


def _mm(a, b, dims, *, name, out_dtype=F32, add=None, tm=1024, tn=1024, tk=1024):
    if dims == "tn":
        (K, M), (_, N) = a.shape, b.shape
    elif dims == "nt":
        (M, K), (N, _) = a.shape, b.shape
    else:
        (M, K), (_, N) = a.shape, b.shape
    tm, tn, tk = _tile(M, tm), _tile(N, tn), _tile(K, tk)
    nk = K // tk

    def body(*refs):
        if add is None:
            a_ref, b_ref, o_ref, acc_ref = refs
        else:
            a_ref, b_ref, r_ref, o_ref, acc_ref = refs
        k = pl.program_id(2)

        @pl.when(k == 0)
        def _():
            acc_ref[...] = jnp.zeros_like(acc_ref)

        acc_ref[...] += _dot16(a_ref[...], b_ref[...], dims)

        @pl.when(k == nk - 1)
        def _():
            r = acc_ref[...]
            if add is not None:
                r = r + r_ref[...]
            o_ref[...] = r.astype(o_ref.dtype)

    a_spec = pl.BlockSpec((tk, tm), lambda i, j, k: (k, i)) if dims == "tn" else pl.BlockSpec((tm, tk), lambda i, j, k: (i, k))
    b_spec = pl.BlockSpec((tn, tk), lambda i, j, k: (j, k)) if dims == "nt" else pl.BlockSpec((tk, tn), lambda i, j, k: (k, j))
    o_spec = pl.BlockSpec((tm, tn), lambda i, j, k: (i, j))
    in_specs, args = [a_spec, b_spec], [a, b]
    if add is not None:
        in_specs.append(o_spec)
        args.append(add)
    return pl.pallas_call(
        body, name=name, grid=(M // tm, N // tn, nk),
        in_specs=in_specs, out_specs=o_spec,
        out_shape=jax.ShapeDtypeStruct((M, N), out_dtype),
        scratch_shapes=[pltpu.VMEM((tm, tn), F32)],
        compiler_params=pltpu.CompilerParams(
            dimension_semantics=("parallel", "parallel", "arbitrary"), vmem_limit_bytes=VMEM_LIMIT),
    )(*args)


def _rowwise(fn, rows, consts, outs, accs=(), *, name, tm=256):
    S = rows[0][0].shape[0]
    tm = min(tm, S)
    nr, nc, no = len(rows), len(consts), len(outs)

    def body(*refs):
        res = fn(*[r[...] for r in refs[:nr + nc]])
        res = tuple(res) if isinstance(res, (tuple, list)) else (res,)
        orefs, arefs = refs[nr + nc:nr + nc + no], refs[nr + nc + no:]
        for r, v in zip(orefs, res[:no]):
            r[...] = v.astype(r.dtype)
        if arefs:
            @pl.when(pl.program_id(0) == 0)
            def _():
                for r in arefs:
                    r[...] = jnp.zeros_like(r)

            for r, v in zip(arefs, res[no:]):
                r[...] += v

    in_specs = [pl.BlockSpec((tm, w), lambda i, cb=cb: (i, cb)) for (_, w, cb) in rows]
    in_specs += [pl.BlockSpec(c.shape, lambda i: (0, 0)) for c in consts]
    out_specs = [pl.BlockSpec((tm, w), lambda i: (i, 0)) for (w, _) in outs]
    out_specs += [pl.BlockSpec(s, lambda i: (0, 0)) for s in accs]
    out_shape = [jax.ShapeDtypeStruct((S, w), dt) for (w, dt) in outs]
    out_shape += [jax.ShapeDtypeStruct(s, F32) for s in accs]
    res = pl.pallas_call(
        body, name=name, grid=(S // tm,), in_specs=in_specs, out_specs=out_specs, out_shape=out_shape,
        compiler_params=pltpu.CompilerParams(dimension_semantics=("arbitrary",), vmem_limit_bytes=VMEM_LIMIT),
    )(*[r[0] for r in rows], *consts)
    return res


def _rowwise_bwd(f, rows, consts, cts, row_grads, const_grads, *, name, add=None, out_dtypes=None, tm=256):
    nr, nc, nct = len(rows), len(consts), len(cts)
    all_rows = list(rows) + list(cts) + ([add] if add is not None else [])

    def fn(*args):
        nrow = len(all_rows)
        prim = [x.astype(F32) for x in args[:nr]] + [x.astype(F32) for x in args[nrow:]]
        ct = tuple(x.astype(F32) for x in args[nr:nr + nct])
        out, vjp = jax.vjp(f, *prim)
        gs = vjp(ct if isinstance(out, (tuple, list)) else ct[0])
        res = [gs[k] for k in row_grads]
        if add is not None:
            res[0] = res[0] + args[nrow - 1]
        return tuple(res) + tuple(gs[nr + k] for k in const_grads)

    out_dtypes = out_dtypes or [F32] * len(row_grads)
    outs = [(rows[k][1], dt) for k, dt in zip(row_grads, out_dtypes)]
    accs = [consts[k].shape for k in const_grads]
    return _rowwise(fn, all_rows, consts, outs, accs, name=name, tm=tm)


def _rms(x, g, n=None):
    ms = jnp.sum(x * x, axis=-1, keepdims=True) * (1.0 / (n or x.shape[-1]))
    return x * lax.rsqrt(ms + EPS) * g


def _sigmoid(x):
    return 1.0 / (1.0 + jnp.exp(-x))


def _swap_halves(x):
    r = lax.broadcasted_iota(jnp.int32, (DH, DH), 0)
    c = lax.broadcasted_iota(jnp.int32, (DH, DH), 1)
    half = ROPE // 2
    perm = jnp.where(((r < half) & (c == r + half)) | ((r >= half) & (r < ROPE) & (c == r - half)), 1.0, 0.0)
    return lax.dot_general(x, perm.astype(F32), NN, precision=lax.Precision.HIGHEST, preferred_element_type=F32)


def _rope128(x, g128, cs):
    y = _rms(x, g128, n=ROPE)
    return y * cs[:, :DH] + _swap_halves(y) * cs[:, DH:]


def _f_norm(x, g):
    return _rms(x, g)


def _f_norm2(x, g1, g2):
    xn = x * lax.rsqrt(jnp.mean(x * x, axis=-1, keepdims=True) + EPS)
    return xn * g1, xn * g2


def _f_kv1(ckr, cs, g_ckv, g_kr):
    w = g_ckv.shape[-1]
    return _rms(ckr[:, :w], g_ckv), _rope128(ckr[:, w:], g_kr, cs)


def _f_kv2(kv, kr, g_kn):
    ks, vs = [], []
    for h in range(H_MLA):
        ks += [_rms(kv[:, 2 * DH * h:2 * DH * h + DH], g_kn), kr]
        vs.append(kv[:, 2 * DH * h + DH:2 * DH * (h + 1)])
    return jnp.concatenate(ks, axis=1), jnp.concatenate(vs, axis=1)


def _f_q2(q, cs, g_n, g_r):
    out = []
    for h in range(H_MLA):
        out += [_rms(q[:, 2 * DH * h:2 * DH * h + DH], g_n), _rope128(q[:, 2 * DH * h + DH:2 * DH * (h + 1)], g_r, cs)]
    return jnp.concatenate(out, axis=1)


def _f_mix(att, g_att, q_m, g_m, mkv, g_q, g_k):
    mem_w = H_MEM * DH
    heads = []
    for h in range(H_MEM):
        kh = _rms(mkv[:, h * DH:(h + 1) * DH], g_k)
        vh = mkv[:, mem_w + h * DH:mem_w + (h + 1) * DH]
        qh = _rms(q_m[:, h * DH:(h + 1) * DH], g_q)
        s = _bdot(qh, kh, "nt") * (DH ** -0.5)
        p = jnp.exp(s - lax.stop_gradient(jnp.max(s, axis=-1, keepdims=True)))
        p = p / jnp.sum(p, axis=-1, keepdims=True)
        heads.append(_bdot(p, vh, "nn"))
    mo = jnp.concatenate(heads, axis=1)
    return jnp.concatenate([att * (g_att * _sigmoid(g_att)), mo * (g_m * _sigmoid(g_m))], axis=1)


def _split_dot(x, u):
    hi = x.astype(BF16)
    lo = (x - hi.astype(F32)).astype(BF16)
    return (lax.dot_general(hi, u, NN, preferred_element_type=F32)
            + lax.dot_general(lo, u, NN, preferred_element_type=F32))


def _tri(t):
    r = lax.broadcasted_iota(jnp.int32, (t, t), 0)
    c = lax.broadcasted_iota(jnp.int32, (t, t), 1)
    return r, c


def _strict_lower(t):
    r, c = _tri(t)
    return (r > c).astype(BF16)


def _rows_ahead(tq, tk):
    return lax.broadcasted_iota(jnp.int32, (tq, tk), 0) - lax.broadcasted_iota(jnp.int32, (tq, tk), 1)


def _log_one_minus_beta(zr, scale):
    zs, nz = zr * scale, zr * (-scale)
    return zs, jnp.minimum(nz, 0.0) - jnp.log(1.0 + jnp.exp(jnp.minimum(zs, nz)))


def _sb_fwd(qkv, *, tq, tk, name):
    S = qkv.shape[0]
    tq, tk = min(tq, S), min(tk, tq, S)
    nd = tq // tk
    H = H_SB
    scale = DH ** -0.5

    def body(q_ref, k_ref, v_ref, o_ref):
        i = pl.program_id(1)
        q = q_ref[...]
        u = _strict_lower(tk)
        ahead = _rows_ahead(tq, tk)

        def block(j, acc, cb, keep):
            off = pl.multiple_of(j * tk, tk)
            k = k_ref[pl.ds(off, tk), :]
            v = v_ref[pl.ds(off, tk), :]
            z, l = _log_one_minus_beta(lax.dot_general(q, k, NT, preferred_element_type=F32), scale)
            if keep is not None:
                l = jnp.where(keep, l, 0.0)
            a = jnp.exp((z + l) + (_split_dot(l, u) + cb))
            if keep is not None:
                a = jnp.where(keep, a, 0.0)
            acc = acc + lax.dot_general(a.astype(BF16), v, NN, preferred_element_type=F32)
            return acc, cb + jnp.sum(l, axis=1, keepdims=True)

        carry = (jnp.zeros((tq, DH), F32), jnp.zeros((tq, 1), F32))
        for t in reversed(range(nd)):
            carry = block(i * nd + t, *carry, ahead > t * tk)
        acc, _ = lax.fori_loop(0, i * nd, lambda jj, carry: block(i * nd - 1 - jj, *carry, None), carry)
        o_ref[...] = acc

    return pl.pallas_call(
        body, name=name, grid=(H, S // tq),
        in_specs=[pl.BlockSpec((tq, DH), lambda h, i: (i, h)),
                  pl.BlockSpec((S, DH), lambda h, i: (0, H + h)),
                  pl.BlockSpec((S, DH), lambda h, i: (0, 2 * H + h))],
        out_specs=pl.BlockSpec((tq, DH), lambda h, i: (i, h)),
        out_shape=jax.ShapeDtypeStruct((S, H * DH), F32),
        compiler_params=pltpu.CompilerParams(dimension_semantics=("arbitrary", "arbitrary"), vmem_limit_bytes=VMEM_LIMIT),
    )(qkv, qkv, qkv)


def _sb_bwd(qkv, o, do, *, tq, tk, name):
    S = qkv.shape[0]
    tq, tk = min(tq, S), min(tk, tq, S)
    nd = tq // tk
    H = H_SB
    scale = DH ** -0.5

    def body(q_ref, k_ref, v_ref, o_ref, do_ref, dq_ref, dk_ref, dv_ref):
        i = pl.program_id(1)

        @pl.when(i == 0)
        def _():
            dk_ref[...] = jnp.zeros_like(dk_ref)
            dv_ref[...] = jnp.zeros_like(dv_ref)

        q = q_ref[...]
        do = do_ref[...]
        do16 = do.astype(BF16)
        dsum = jnp.sum(do16.astype(F32) * o_ref[...], axis=1, keepdims=True)
        u = _strict_lower(tk)
        ahead = _rows_ahead(tq, tk)

        def block(j, dq, cb, ce, keep):
            off = pl.multiple_of(j * tk, tk)
            k = k_ref[pl.ds(off, tk), :]
            v = v_ref[pl.ds(off, tk), :]
            z, l = _log_one_minus_beta(lax.dot_general(q, k, NT, preferred_element_type=F32), scale)
            if keep is not None:
                l = jnp.where(keep, l, 0.0)
            log_beta = z + l
            a = jnp.exp(log_beta + (_split_dot(l, u) + cb))
            if keep is not None:
                a = jnp.where(keep, a, 0.0)
            a16 = a.astype(BF16)
            e = a16.astype(F32) * lax.dot_general(do16, v, NT, preferred_element_type=F32)
            left = dsum - (ce + _split_dot(e, u) + e)
            dz = (e - jnp.exp(log_beta) * (e + left)) * scale
            if keep is not None:
                dz = jnp.where(keep, dz, 0.0)
            dz = dz.astype(BF16)
            dq = dq + lax.dot_general(dz, k, NN, preferred_element_type=F32)
            dk_ref[pl.ds(off, tk), :] += lax.dot_general(dz, q, TN, preferred_element_type=F32)
            dv_ref[pl.ds(off, tk), :] += lax.dot_general(a16, do16, TN, preferred_element_type=F32)
            return dq, cb + jnp.sum(l, axis=1, keepdims=True), ce + jnp.sum(e, axis=1, keepdims=True)

        zero = jnp.zeros((tq, 1), F32)
        carry = (jnp.zeros((tq, DH), F32), zero, zero)
        for t in reversed(range(nd)):
            carry = block(i * nd + t, *carry, ahead > t * tk)
        dq, _, _ = lax.fori_loop(0, i * nd, lambda jj, carry: block(i * nd - 1 - jj, *carry, None), carry)
        dq_ref[...] = dq

    blk = pl.BlockSpec((tq, DH), lambda h, i: (i, h))
    whole = pl.BlockSpec((S, DH), lambda h, i: (0, h))
    shp = jax.ShapeDtypeStruct((S, H * DH), F32)
    return pl.pallas_call(
        body, name=name, grid=(H, S // tq),
        in_specs=[blk, pl.BlockSpec((S, DH), lambda h, i: (0, H + h)),
                  pl.BlockSpec((S, DH), lambda h, i: (0, 2 * H + h)), blk, blk],
        out_specs=[blk, whole, whole], out_shape=[shp, shp, shp],
        compiler_params=pltpu.CompilerParams(dimension_semantics=("arbitrary", "arbitrary"), vmem_limit_bytes=VMEM_LIMIT),
    )(qkv, qkv, qkv, o, do)


def _mla_fwd(q, k, v, *, tq, tk, name):
    S = q.shape[0]
    tq, tk = min(tq, S), min(tk, tq, S)
    nd = tq // tk
    H = H_MLA
    scale = MLA_QK ** -0.5

    def body(q_ref, k_ref, v_ref, o_ref, lse_ref):
        i = pl.program_id(1)
        qb = q_ref[...]
        ahead = _rows_ahead(tq, tk)

        def block(j, m, den, acc, keep):
            off = pl.multiple_of(j * tk, tk)
            kb = k_ref[pl.ds(off, tk), :]
            vb = v_ref[pl.ds(off, tk), :]
            s = lax.dot_general(qb, kb, NT, preferred_element_type=F32) * scale
            if keep is not None:
                s = jnp.where(keep, s, -1e30)
            m_new = jnp.maximum(m, jnp.max(s, axis=1, keepdims=True))
            p = jnp.exp(s - m_new)
            alpha = jnp.exp(m - m_new)
            den = alpha * den + jnp.sum(p, axis=1, keepdims=True)
            acc = alpha * acc + lax.dot_general(p.astype(BF16), vb, NN, preferred_element_type=F32)
            return m_new, den, acc

        init = (jnp.full((tq, 1), -1e30, F32), jnp.zeros((tq, 1), F32), jnp.zeros((tq, DH), F32))
        carry = lax.fori_loop(0, i * nd, lambda j, carry: block(j, *carry, None), init)
        for t in range(nd):
            carry = block(i * nd + t, *carry, ahead >= t * tk)
        m, den, acc = carry
        o_ref[...] = acc / den
        lse_ref[0] = m + jnp.log(den)

    return pl.pallas_call(
        body, name=name, grid=(H, S // tq),
        in_specs=[pl.BlockSpec((tq, 2 * DH), lambda h, i: (i, h)),
                  pl.BlockSpec((S, 2 * DH), lambda h, i: (0, h)),
                  pl.BlockSpec((S, DH), lambda h, i: (0, h))],
        out_specs=[pl.BlockSpec((tq, DH), lambda h, i: (i, h)), pl.BlockSpec((1, tq, 1), lambda h, i: (h, i, 0))],
        out_shape=[jax.ShapeDtypeStruct((S, H * DH), F32), jax.ShapeDtypeStruct((H, S, 1), F32)],
        compiler_params=pltpu.CompilerParams(dimension_semantics=("arbitrary", "arbitrary"), vmem_limit_bytes=VMEM_LIMIT),
    )(q, k, v)


def _mla_bwd(q, k, v, o, do, lse, *, tq, tk, name):
    S = q.shape[0]
    tq, tk = min(tq, S), min(tk, tq, S)
    nd = tq // tk
    H = H_MLA
    scale = MLA_QK ** -0.5

    def body(q_ref, k_ref, v_ref, o_ref, do_ref, lse_ref, dq_ref, dk_ref, dv_ref):
        i = pl.program_id(1)

        @pl.when(i == 0)
        def _():
            dk_ref[...] = jnp.zeros_like(dk_ref)
            dv_ref[...] = jnp.zeros_like(dv_ref)

        qb = q_ref[...]
        do = do_ref[...]
        do16 = do.astype(BF16)
        dsum = jnp.sum(do * o_ref[...], axis=1, keepdims=True)
        lse = lse_ref[0]
        ahead = _rows_ahead(tq, tk)

        def block(j, dq, keep):
            off = pl.multiple_of(j * tk, tk)
            kb = k_ref[pl.ds(off, tk), :]
            vb = v_ref[pl.ds(off, tk), :]
            s = lax.dot_general(qb, kb, NT, preferred_element_type=F32) * scale
            if keep is not None:
                s = jnp.where(keep, s, -1e30)
            p = jnp.exp(s - lse)
            dp = lax.dot_general(do16, vb, NT, preferred_element_type=F32)
            ds = (p * (dp - dsum) * scale).astype(BF16)
            dk_ref[pl.ds(off, tk), :] += lax.dot_general(ds, qb, TN, preferred_element_type=F32)
            dv_ref[pl.ds(off, tk), :] += lax.dot_general(p.astype(BF16), do16, TN, preferred_element_type=F32)
            return dq + lax.dot_general(ds, kb, NN, preferred_element_type=F32)

        dq = lax.fori_loop(0, i * nd, lambda j, dq: block(j, dq, None), jnp.zeros((tq, 2 * DH), F32))
        for t in range(nd):
            dq = block(i * nd + t, dq, ahead >= t * tk)
        dq_ref[...] = dq

    blk = pl.BlockSpec((tq, DH), lambda h, i: (i, h))
    blk2 = pl.BlockSpec((tq, 2 * DH), lambda h, i: (i, h))
    return pl.pallas_call(
        body, name=name, grid=(H, S // tq),
        in_specs=[blk2, pl.BlockSpec((S, 2 * DH), lambda h, i: (0, h)), pl.BlockSpec((S, DH), lambda h, i: (0, h)),
                  blk, blk, pl.BlockSpec((1, tq, 1), lambda h, i: (h, i, 0))],
        out_specs=[blk2, pl.BlockSpec((S, 2 * DH), lambda h, i: (0, h)), pl.BlockSpec((S, DH), lambda h, i: (0, h))],
        out_shape=[jax.ShapeDtypeStruct((S, H * 2 * DH), F32), jax.ShapeDtypeStruct((S, H * 2 * DH), F32),
                   jax.ShapeDtypeStruct((S, H * DH), F32)],
        compiler_params=pltpu.CompilerParams(dimension_semantics=("arbitrary", "arbitrary"), vmem_limit_bytes=VMEM_LIMIT_BIG),
    )(q, k, v, o, do, lse)


def _mesh_pos():
    return lax.axis_index("x"), lax.axis_index("y"), lax.axis_index("c")


def _all_gather(block, *, name):
    R, C = block.shape

    def body(x_ref, out_ref, send_sems, recv_sems, local_sem):
        x, y, c = _mesh_pos()
        me, sibling = (x, y, c), (x, y, 1 - c)
        chips = [(1 - x, y), (x, 1 - y), (1 - x, 1 - y)]

        def slot(px, py, pc):
            return out_ref.at[4 * px + 2 * py + pc]

        def copy(k, blk, to, src=None):
            return pltpu.make_async_remote_copy(
                src_ref=slot(*blk) if src is None else src, dst_ref=slot(*blk),
                send_sem=send_sems.at[k], recv_sem=recv_sems.at[k],
                device_id=to, device_id_type=pl.DeviceIdType.MESH)

        mine = pltpu.make_async_copy(x_ref, slot(*me), local_sem)
        mine.start()
        first = [copy(0, me, sibling, src=x_ref)]
        first += [copy(1 + j, me, (*chip, c), src=x_ref) for j, chip in enumerate(chips)]
        for cp in first:
            cp.start()
        passed = [copy(4 + j, (*chip, c), sibling) for j, chip in enumerate(chips)]
        for j, chip in enumerate(chips):
            copy(1 + j, (*chip, c), me).wait_recv()
            passed[j].start()
        copy(0, sibling, me).wait_recv()
        for j, chip in enumerate(chips):
            copy(4 + j, (*chip, 1 - c), me).wait_recv()
        for cp in first + passed:
            cp.wait_send()
        mine.wait()

    return pl.pallas_call(
        body, name=name,
        out_shape=jax.ShapeDtypeStruct((N_DEV, R, C), block.dtype),
        in_specs=[pl.BlockSpec(memory_space=pl.ANY)], out_specs=pl.BlockSpec(memory_space=pl.ANY),
        scratch_shapes=[pltpu.SemaphoreType.DMA((7,)), pltpu.SemaphoreType.DMA((7,)), pltpu.SemaphoreType.DMA],
    )(block)


def _all_to_all(send, *, name):
    def body(s_ref, r_ref, send_sems, recv_sems, local_sem):
        x, y, c = _mesh_pos()
        me = 4 * x + 2 * y + c
        mine = pltpu.make_async_copy(s_ref.at[me], r_ref.at[me], local_sem)
        mine.start()
        peers = []
        for k in range(1, N_DEV):
            px = 1 - x if k & 4 else x
            py = 1 - y if k & 2 else y
            pc = 1 - c if k & 1 else c
            peers.append((px, py, pc))
        sends = [pltpu.make_async_remote_copy(
            src_ref=s_ref.at[4 * p[0] + 2 * p[1] + p[2]], dst_ref=r_ref.at[me],
            send_sem=send_sems.at[k], recv_sem=recv_sems.at[k],
            device_id=p, device_id_type=pl.DeviceIdType.MESH) for k, p in enumerate(peers)]
        for cp in sends:
            cp.start()
        for k, p in enumerate(peers):
            lin = 4 * p[0] + 2 * p[1] + p[2]
            pltpu.make_async_remote_copy(
                src_ref=s_ref.at[lin], dst_ref=r_ref.at[lin], send_sem=send_sems.at[k], recv_sem=recv_sems.at[k],
                device_id=p, device_id_type=pl.DeviceIdType.MESH).wait_recv()
        for cp in sends:
            cp.wait_send()
        mine.wait()

    return pl.pallas_call(
        body, name=name,
        out_shape=jax.ShapeDtypeStruct(send.shape, send.dtype),
        in_specs=[pl.BlockSpec(memory_space=pl.ANY)], out_specs=pl.BlockSpec(memory_space=pl.ANY),
        scratch_shapes=[pltpu.SemaphoreType.DMA((7,)), pltpu.SemaphoreType.DMA((7,)), pltpu.SemaphoreType.DMA],
    )(send)


def _reduce_adamw(recv, w, m, v, *, name, tr=128):
    R = w.shape[0]

    def body(g_ref, w_ref, m_ref, v_ref, og_ref, od_ref, om_ref, ov_ref):
        g = g_ref[0]
        for s in range(1, N_DEV):
            g = g + g_ref[s]
        mn = ADAM_B1 * m_ref[...] + (1.0 - ADAM_B1) * g
        vn = ADAM_B2 * v_ref[...] + (1.0 - ADAM_B2) * jnp.square(g)
        m_hat = mn / (1.0 - ADAM_B1 ** ADAM_STEP)
        v_hat = vn / (1.0 - ADAM_B2 ** ADAM_STEP)
        og_ref[...] = g
        od_ref[...] = -ADAM_LR * (m_hat / (jnp.sqrt(v_hat) + ADAM_EPS) + ADAM_WD * w_ref[...])
        om_ref[...] = mn
        ov_ref[...] = vn

    blk = pl.BlockSpec((tr, LANES), lambda i: (i, 0))
    shp = jax.ShapeDtypeStruct((R, LANES), F32)
    return pl.pallas_call(
        body, name=name, grid=(R // tr,),
        in_specs=[pl.BlockSpec((N_DEV, tr, LANES), lambda i: (0, i, 0)), blk, blk, blk],
        out_specs=[blk, blk, blk, blk], out_shape=[shp, shp, shp, shp],
        compiler_params=pltpu.CompilerParams(dimension_semantics=("parallel",), vmem_limit_bytes=VMEM_LIMIT),
    )(recv, w, m, v)


SHARDED = (("a_norm", 1), ("a_w_in", 2), ("a_w_out", 1), ("w_dkv", 0), ("w_ukv", 1), ("b_w_in", 2),
           ("b_w_uq", 2), ("b_w_out", 1), ("w_mem_kv", 1))
SMALL = ("kv_norm", "g_ckv", "g_k_nope", "g_k_rope", "b_norm", "b_g_q_lat", "b_g_q_nope", "b_g_q_rope",
         "mem_norm", "g_mem_q", "g_mem_k")
WEIGHTS = ("a_norm", "a_w_in", "a_w_out", "kv_norm", "w_dkv", "g_ckv", "w_ukv", "g_k_nope", "g_k_rope", "b_norm",
           "b_w_in", "b_g_q_lat", "b_w_uq", "b_g_q_nope", "b_g_q_rope", "b_w_out", "mem_norm", "w_mem_kv",
           "g_mem_q", "g_mem_k")
ROW_MULT = 8
ROW_BLOCK = 128


def _rows_of(n, mult):
    rows = -(-n // LANES)
    return -(-rows // mult) * mult


def _to_rows(flat, mult):
    n = flat.shape[-1]
    rows = _rows_of(n, mult)
    pad = [(0, 0)] * (flat.ndim - 1) + [(0, rows * LANES - n)]
    return jnp.pad(flat, pad).reshape(*flat.shape[:-1], rows, LANES)


def _split8(full, axis):
    shp = full.shape
    t = full.reshape(*shp[:axis], N_DEV, shp[axis] // N_DEV, *shp[axis + 1:])
    return jnp.moveaxis(t, axis, 0).reshape(N_DEV, -1)


def _join8(rows, axis, shard_shape):
    t = rows.reshape(N_DEV, *shard_shape)
    t = jnp.moveaxis(t, 0, axis)
    return t.reshape(*shard_shape[:axis], N_DEV * shard_shape[axis], *shard_shape[axis + 1:])


def _pack_local(vals, loss_slot):
    parts = [_to_rows(vals[n].reshape(-1), ROW_MULT) for n, _ in SHARDED]
    parts += [_to_rows(vals[n].reshape(-1), ROW_MULT) for n in SMALL]
    parts.append(_to_rows(loss_slot.reshape(-1), ROW_MULT))
    slab = jnp.concatenate(parts, axis=0)
    return jnp.pad(slab, ((0, -slab.shape[0] % ROW_BLOCK), (0, 0)))


def _pack_send(full_grads, loss_part):
    parts = [_to_rows(_split8(full_grads[n], ax), ROW_MULT) for n, ax in SHARDED]
    for n in SMALL:
        p = _to_rows(full_grads[n].reshape(-1), ROW_MULT)
        parts.append(jnp.broadcast_to(p[None], (N_DEV, *p.shape)))
    p = _to_rows(loss_part.reshape(-1), ROW_MULT)
    parts.append(jnp.broadcast_to(p[None], (N_DEV, *p.shape)))
    slab = jnp.concatenate(parts, axis=1)
    return jnp.pad(slab, ((0, 0), (0, -slab.shape[1] % ROW_BLOCK), (0, 0)))


def _unpack_local(slab, shapes):
    out, row = {}, 0
    for n in [n for n, _ in SHARDED] + list(SMALL):
        size = 1
        for d in shapes[n]:
            size *= d
        rows = _rows_of(size, ROW_MULT)
        out[n] = slab[row:row + rows].reshape(-1)[:size].reshape(shapes[n])
        row += rows
    return out, slab[row, 0]


def kernel(x, mem, positions, a_norm, a_w_in, a_w_out, kv_norm, w_dkv, g_ckv, w_ukv, g_k_nope, g_k_rope, b_norm, b_w_in, b_g_q_lat, b_w_uq, b_g_q_nope, b_g_q_rope, b_w_out, mem_norm, w_mem_kv, g_mem_q, g_mem_k, loss_target, m_a_norm, m_a_w_in, m_a_w_out, m_kv_norm, m_w_dkv, m_g_ckv, m_w_ukv, m_g_k_nope, m_g_k_rope, m_b_norm, m_b_w_in, m_b_g_q_lat, m_b_w_uq, m_b_g_q_nope, m_b_g_q_rope, m_b_w_out, m_mem_norm, m_w_mem_kv, m_g_mem_q, m_g_mem_k, v_a_norm, v_a_w_in, v_a_w_out, v_kv_norm, v_w_dkv, v_g_ckv, v_w_ukv, v_g_k_nope, v_g_k_rope, v_b_norm, v_b_w_in, v_b_g_q_lat, v_b_w_uq, v_b_g_q_nope, v_b_g_q_rope, v_b_w_out, v_mem_norm, v_w_mem_kv, v_g_mem_q, v_g_mem_k):
    wts = dict(a_norm=a_norm, a_w_in=a_w_in, a_w_out=a_w_out, kv_norm=kv_norm, w_dkv=w_dkv, g_ckv=g_ckv, w_ukv=w_ukv,
               g_k_nope=g_k_nope, g_k_rope=g_k_rope, b_norm=b_norm, b_w_in=b_w_in, b_g_q_lat=b_g_q_lat, b_w_uq=b_w_uq,
               b_g_q_nope=b_g_q_nope, b_g_q_rope=b_g_q_rope, b_w_out=b_w_out, mem_norm=mem_norm, w_mem_kv=w_mem_kv,
               g_mem_q=g_mem_q, g_mem_k=g_mem_k)
    mom = dict(a_norm=m_a_norm, a_w_in=m_a_w_in, a_w_out=m_a_w_out, kv_norm=m_kv_norm, w_dkv=m_w_dkv, g_ckv=m_g_ckv,
               w_ukv=m_w_ukv, g_k_nope=m_g_k_nope, g_k_rope=m_g_k_rope, b_norm=m_b_norm, b_w_in=m_b_w_in,
               b_g_q_lat=m_b_g_q_lat, b_w_uq=m_b_w_uq, b_g_q_nope=m_b_g_q_nope, b_g_q_rope=m_b_g_q_rope,
               b_w_out=m_b_w_out, mem_norm=m_mem_norm, w_mem_kv=m_w_mem_kv, g_mem_q=m_g_mem_q, g_mem_k=m_g_mem_k)
    var = dict(a_norm=v_a_norm, a_w_in=v_a_w_in, a_w_out=v_a_w_out, kv_norm=v_kv_norm, w_dkv=v_w_dkv, g_ckv=v_g_ckv,
               w_ukv=v_w_ukv, g_k_nope=v_g_k_nope, g_k_rope=v_g_k_rope, b_norm=v_b_norm, b_w_in=v_b_w_in,
               b_g_q_lat=v_b_g_q_lat, b_w_uq=v_b_w_uq, b_g_q_nope=v_b_g_q_nope, b_g_q_rope=v_b_g_q_rope,
               b_w_out=v_b_w_out, mem_norm=v_mem_norm, w_mem_kv=v_w_mem_kv, g_mem_q=v_g_mem_q, g_mem_k=v_g_mem_k)
    shapes = {n: wts[n].shape for n in WEIGHTS}
    S, D = x.shape[1], x.shape[2]
    xs, ms, tgt = x[0], mem[0], loss_target[0]
    sb_w, mem_w, mla_w = H_SB * DH, H_MEM * DH, H_MLA * DH
    q_lora, kv_lora = b_g_q_lat.shape[-1], g_ckv.shape[-1]

    pieces = [_to_rows(lax.bitcast_convert_type(a_norm.reshape(-1), BF16).reshape(-1), 16)]
    pieces += [_to_rows(wts[n].astype(BF16).reshape(-1), 16) for n, _ in SHARDED[1:]]
    gathered = _all_gather(jnp.concatenate(pieces, axis=0), name="gather_weights")
    full, row = {}, 0
    for (n, ax), p in zip(SHARDED, pieces):
        size = wts[n].size * (2 if n == "a_norm" else 1)
        flat = gathered[:, row:row + p.shape[0]].reshape(N_DEV, -1)[:, :size]
        if n == "a_norm":
            flat = lax.bitcast_convert_type(flat.reshape(N_DEV, -1, 2), F32)
        full[n] = _join8(flat, ax, wts[n].shape)
        row += p.shape[0]
    g_a = full["a_norm"]
    w_a_in, w_a_out = full["a_w_in"][0], full["a_w_out"][0]
    w_dkv_p = jnp.pad(full["w_dkv"], ((0, 0), (0, ROPE)))
    w_ukv_f = full["w_ukv"]
    wb = full["b_w_in"][0]
    w_b_in = jnp.concatenate([wb[:, q_lora:q_lora + mla_w], wb[:, :q_lora], wb[:, q_lora + mla_w:]], axis=1)
    w_uq_p = jnp.pad(full["b_w_uq"][0].reshape(q_lora, H_MLA, MLA_QK),
                     ((0, 0), (0, 0), (0, 2 * DH - MLA_QK))).reshape(q_lora, H_MLA * 2 * DH)
    w_b_out = full["b_w_out"][0]
    w_mem = full["w_mem_kv"]

    pad128 = lambda g: jnp.pad(g.reshape(1, -1), ((0, 0), (0, DH - ROPE)))
    row2 = lambda g: g.reshape(1, -1)
    g_kr, g_qr = pad128(g_k_rope), pad128(b_g_q_rope[0])
    g_kv, g_b, g_c, g_kn = row2(kv_norm), row2(b_norm[0]), row2(g_ckv), row2(g_k_nope)
    g_ql, g_qn = row2(b_g_q_lat[0]), row2(b_g_q_nope[0])

    inv_freq = jnp.power(ROPE_THETA, -jnp.arange(0, ROPE, 2, dtype=F32) / ROPE)
    ang = positions[0].astype(F32)[:, None] * inv_freq
    z64 = jnp.zeros((S, DH - ROPE), F32)
    cs = jnp.concatenate([jnp.cos(ang), jnp.cos(ang), z64, -jnp.sin(ang), jnp.sin(ang), z64], axis=1)

    mn, mkv = [], []
    for l in range(2):
        mn.append(_rowwise(_f_norm, [(ms, D, 0)], [row2(mem_norm[l])], [(D, BF16)], name=f"mem_norm{l}")[0])
        mkv.append(_mm(mn[l], w_mem[l], "nn", name=f"mem_kv{l}"))
    g_mq = [row2(g_mem_q[l]) for l in range(2)]
    g_mk = [row2(g_mem_k[l]) for l in range(2)]

    h0 = _rowwise(_f_norm, [(xs, D, 0)], [g_a], [(D, BF16)], name="a_norm_fwd")[0]
    pa = _mm(h0, w_a_in, "nn", name="a_in")
    qkv = pa[:, :3 * sb_w].astype(BF16)
    sb = _sb_fwd(qkv, tq=1024, tk=256, name="sb_fwd")
    mix_a_rows = [(sb, sb_w, 0), (pa, sb_w, 3), (pa, mem_w, 4 * sb_w // mem_w), (pa, mem_w, 4 * sb_w // mem_w + 1)]
    mixed_a = _rowwise(_f_mix, mix_a_rows, [mkv[0], g_mq[0], g_mk[0]], [(sb_w + mem_w, BF16)], name="a_mix_fwd")[0]
    x1 = _mm(mixed_a, w_a_out, "nn", add=xs, name="a_out")

    hk, hb = _rowwise(_f_norm2, [(x1, D, 0)], [g_kv, g_b], [(D, BF16), (D, BF16)], name="b_norm_fwd")
    ckr = _mm(hk, w_dkv_p, "nn", name="kv_down")
    cn, kr = _rowwise(_f_kv1, [(ckr, kv_lora + DH, 0), (cs, 2 * DH, 0)], [g_c, g_kr],
                      [(kv_lora, BF16), (DH, F32)], name="kv1_fwd")
    kvu = _mm(cn, w_ukv_f, "nn", name="kv_up")
    k2, v2 = _rowwise(_f_kv2, [(kvu, H_MLA * 2 * DH, 0), (kr, DH, 0)], [g_kn],
                      [(H_MLA * 2 * DH, BF16), (mla_w, BF16)], name="kv2_fwd")
    pb = _mm(hb, w_b_in, "nn", name="b_in")
    ql = _rowwise(_f_norm, [(pb, q_lora, mla_w // q_lora)], [g_ql], [(q_lora, BF16)], name="q_lat_fwd")[0]
    qraw = _mm(ql, w_uq_p, "nn", name="q_up")
    q2 = _rowwise(_f_q2, [(qraw, H_MLA * 2 * DH, 0), (cs, 2 * DH, 0)], [g_qn, g_qr],
                  [(H_MLA * 2 * DH, BF16)], name="q2_fwd")[0]
    att, lse = _mla_fwd(q2, k2, v2, tq=1024, tk=1024, name="mla_fwd")
    cb = (mla_w + q_lora) // mem_w
    mix_b_rows = [(att, mla_w, 0), (pb, mla_w, 0), (pb, mem_w, cb), (pb, mem_w, cb + 1)]
    mixed_b = _rowwise(_f_mix, mix_b_rows, [mkv[1], g_mq[1], g_mk[1]], [(mla_w + mem_w, BF16)], name="b_mix_fwd")[0]
    y = _mm(mixed_b, w_b_out, "nn", add=x1, name="b_out")

    def loss_fn(yb, tb):
        err = yb - tb
        part = 0.5 * jnp.sum(jnp.sum(err * err, axis=-1, keepdims=True) * (1.0 / D))
        return err * (1.0 / D), jnp.full((1, DH), part, F32)

    dy, loss_part = _rowwise(loss_fn, [(y, D, 0), (tgt, D, 0)], [], [(D, F32)], [(1, DH)], name="loss")

    gr = {}
    d_mixed_b = _mm(dy, w_b_out, "nt", name="b_out_dx")
    gr["b_w_out"] = _mm(mixed_b, dy, "tn", name="b_out_dw")[None]
    d_att, d_gmla, d_qm_b, d_gm_b, d_mkv1, d_gq1, d_gk1 = _rowwise_bwd(
        _f_mix, mix_b_rows, [mkv[1], g_mq[1], g_mk[1]], [(d_mixed_b, mla_w + mem_w, 0)],
        [0, 1, 2, 3], [0, 1, 2], name="b_mix_bwd")
    dq2, dk2, dv2 = _mla_bwd(q2, k2, v2, att, d_att, lse, tq=512, tk=512, name="mla_bwd")
    d_qraw, d_gqn, d_gqr = _rowwise_bwd(
        _f_q2, [(qraw, H_MLA * 2 * DH, 0), (cs, 2 * DH, 0)], [g_qn, g_qr], [(dq2, H_MLA * 2 * DH, 0)],
        [0], [0, 1], name="q2_bwd")
    d_ql = _mm(d_qraw, w_uq_p, "nt", name="q_up_dx")
    d_wuq = _mm(ql, d_qraw, "tn", name="q_up_dw")
    gr["b_w_uq"] = d_wuq.reshape(q_lora, H_MLA, 2 * DH)[:, :, :MLA_QK].reshape(1, q_lora, H_MLA * MLA_QK)
    d_qlat, d_gql = _rowwise_bwd(_f_norm, [(pb, q_lora, mla_w // q_lora)], [g_ql], [(d_ql, q_lora, 0)],
                                 [0], [0], name="q_lat_bwd")
    d_pb = jnp.concatenate([d_gmla, d_qlat, d_qm_b, d_gm_b], axis=1)
    d_hb = _mm(d_pb, w_b_in, "nt", name="b_in_dx")
    d_wbin = _mm(hb, d_pb, "tn", name="b_in_dw")
    gr["b_w_in"] = jnp.concatenate([d_wbin[:, mla_w:mla_w + q_lora], d_wbin[:, :mla_w], d_wbin[:, mla_w + q_lora:]],
                                   axis=1)[None]
    d_kvu, d_kr, d_gkn = _rowwise_bwd(
        _f_kv2, [(kvu, H_MLA * 2 * DH, 0), (kr, DH, 0)], [g_kn], [(dk2, H_MLA * 2 * DH, 0), (dv2, mla_w, 0)],
        [0, 1], [0], name="kv2_bwd")
    d_cn = _mm(d_kvu, w_ukv_f, "nt", name="kv_up_dx")
    gr["w_ukv"] = _mm(cn, d_kvu, "tn", name="kv_up_dw")
    d_ckr, d_gc, d_gkr = _rowwise_bwd(
        _f_kv1, [(ckr, kv_lora + DH, 0), (cs, 2 * DH, 0)], [g_c, g_kr], [(d_cn, kv_lora, 0), (d_kr, DH, 0)],
        [0], [0, 1], name="kv1_bwd")
    d_hk = _mm(d_ckr, w_dkv_p, "nt", name="kv_down_dx")
    gr["w_dkv"] = _mm(hk, d_ckr, "tn", name="kv_down_dw")[:, :kv_lora + ROPE]
    d_x1, d_gkv, d_gb = _rowwise_bwd(_f_norm2, [(x1, D, 0)], [g_kv, g_b], [(d_hk, D, 0), (d_hb, D, 0)],
                                     [0], [0, 1], add=(dy, D, 0), name="b_norm_bwd")
    d_mixed_a = _mm(d_x1, w_a_out, "nt", name="a_out_dx")
    gr["a_w_out"] = _mm(mixed_a, d_x1, "tn", name="a_out_dw")[None]
    d_sb, d_gsb, d_qm_a, d_gm_a, d_mkv0, d_gq0, d_gk0 = _rowwise_bwd(
        _f_mix, mix_a_rows, [mkv[0], g_mq[0], g_mk[0]], [(d_mixed_a, sb_w + mem_w, 0)],
        [0, 1, 2, 3], [0, 1, 2], name="a_mix_bwd")
    dq, dk, dv = _sb_bwd(qkv, sb, d_sb, tq=1024, tk=256, name="sb_bwd")
    d_pa = jnp.concatenate([dq, dk, dv, d_gsb, d_qm_a, d_gm_a], axis=1)
    d_h0 = _mm(d_pa, w_a_in, "nt", name="a_in_dx")
    gr["a_w_in"] = _mm(h0, d_pa, "tn", name="a_in_dw")[None]
    grad_x, d_ga = _rowwise_bwd(_f_norm, [(xs, D, 0)], [g_a], [(d_h0, D, 0)], [0], [0], add=(d_x1, D, 0),
                                name="a_norm_bwd")
    gr["a_norm"] = d_ga

    d_wmem, d_mnorm = [], []
    for l, d_mkv in enumerate((d_mkv0, d_mkv1)):
        d_mn = _mm(d_mkv, w_mem[l], "nt", name=f"mem_kv_dx{l}")
        d_wmem.append(_mm(mn[l], d_mkv, "tn", name=f"mem_kv_dw{l}"))
        d_mnorm.append(_rowwise_bwd(_f_norm, [(ms, D, 0)], [row2(mem_norm[l])], [(d_mn, D, 0)], [], [0],
                                    name=f"mem_norm_bwd{l}")[0])
    gr["w_mem_kv"] = jnp.stack(d_wmem)
    gr["mem_norm"] = jnp.concatenate(d_mnorm, axis=0)
    gr["g_mem_q"] = jnp.concatenate([d_gq0, d_gq1], axis=0)
    gr["g_mem_k"] = jnp.concatenate([d_gk0, d_gk1], axis=0)
    gr["kv_norm"], gr["b_norm"], gr["g_ckv"], gr["g_k_nope"] = d_gkv, d_gb, d_gc, d_gkn
    gr["g_k_rope"], gr["b_g_q_rope"] = d_gkr[:, :ROPE], d_gqr[:, :ROPE]
    gr["b_g_q_lat"], gr["b_g_q_nope"] = d_gql, d_gqn

    recv = _all_to_all(_pack_send(gr, loss_part[0, :1]), name="exchange_grads")
    zero = jnp.zeros((1,), F32)
    og, od, om, ov = _reduce_adamw(recv, _pack_local(wts, zero), _pack_local(mom, zero), _pack_local(var, zero),
                                   name="reduce_adamw", tr=ROW_BLOCK)
    grads, loss = _unpack_local(og, shapes)
    delta, _ = _unpack_local(od, shapes)
    new_m, _ = _unpack_local(om, shapes)
    new_v, _ = _unpack_local(ov, shapes)
    return (loss, grad_x[None], *[grads[n] for n in WEIGHTS], *[delta[n] for n in WEIGHTS],
            *[new_m[n] for n in WEIGHTS], *[new_v[n] for n in WEIGHTS])
```

```python
import functools

import jax
import jax.numpy as jnp
from jax import lax
from jax.experimental import pallas as pl
from jax.experimental.pallas import tpu as pltpu

F32, BF16 = jnp.float32, jnp.bfloat16

N_DEV = 8
DH = 128
H_SB, H_MEM, H_MLA = 12, 4, 12
ROPE = 64
MLA_QK = DH + ROPE
EPS = 1e-6
ROPE_THETA = 10000.0
ADAM_LR, ADAM_B1, ADAM_B2, ADAM_EPS, ADAM_WD, ADAM_STEP = 0.001, 0.9, 0.999, 1e-08, 0.01, 10

LANES = 1024
VMEM_LIMIT = 48 * 1024 * 1024
VMEM_LIMIT_BIG = 56 * 1024 * 1024

NN = (((1,), (0,)), ((), ()))
NT = (((1,), (1,)), ((), ()))
TN = (((0,), (0,)), ((), ()))
_DIMS = {"nn": NN, "nt": NT, "tn": TN}


def _dot16(a, b, dims):
    return lax.dot_general(a.astype(BF16), b.astype(BF16), _DIMS[dims], preferred_element_type=F32)


@functools.partial(jax.custom_vjp, nondiff_argnums=(2,))
def _bdot(a, b, dims):
    return _dot16(a, b, dims)


def _bdot_fwd(a, b, dims):
    return _dot16(a, b, dims), (a, b)


def _bdot_bwd(dims, res, g):
    a, b = res
    if dims == "nn":
        return _dot16(g, b, "nt"), _dot16(a, g, "tn")
    return _dot16(g, b, "nn"), _dot16(g, a, "tn")


_bdot.defvjp(_bdot_fwd, _bdot_bwd)


def _tile(n, pref):
    if n <= pref:
        return n
    t = (pref // 128) * 128
    while n % t:
        t -= 128
    return t


def _mesh_pos():
    return lax.axis_index("x"), lax.axis_index("y"), lax.axis_index("c")


def _ride_shape(kind, src):
    return jax.ShapeDtypeStruct((N_DEV, *src.shape) if kind == "gather" else src.shape, src.dtype)


def _ride_scratch():
    return [pltpu.SemaphoreType.DMA((N_DEV - 1,)), pltpu.SemaphoreType.DMA((N_DEV - 1,)), pltpu.SemaphoreType.DMA]


def _ride(kind, phase, s_ref, r_ref, send_sems, recv_sems, local_sem):
    x, y, c = _mesh_pos()
    me = 4 * x + 2 * y + c
    src = (lambda lin: s_ref) if kind == "gather" else (lambda lin: s_ref.at[lin])
    local = pltpu.make_async_copy(src(me), r_ref.at[me], local_sem)
    if phase == "start":
        local.start()
    for k in range(1, N_DEV):
        p = (1 - x if k & 4 else x, 1 - y if k & 2 else y, 1 - c if k & 1 else c)
        lin = 4 * p[0] + 2 * p[1] + p[2]
        cp = pltpu.make_async_remote_copy(
            src_ref=src(lin), dst_ref=r_ref.at[me] if phase == "start" else r_ref.at[lin],
            send_sem=send_sems.at[k - 1], recv_sem=recv_sems.at[k - 1],
            device_id=p, device_id_type=pl.DeviceIdType.MESH)
        if phase == "start":
            cp.start()
        else:
            cp.wait_recv()
            cp.wait_send()
    if phase == "wait":
        local.wait()


def _mm(a, b, dims, *, name, out_dtype=F32, add=None, ride=None, out_split=None, tm=1024, tn=1024, tk=1024):
    if dims == "tn":
        (K, M), (_, N) = a.shape, b.shape
    elif dims == "nt":
        (M, K), (N, _) = a.shape, b.shape
    else:
        (M, K), (_, N) = a.shape, b.shape
    tm, tk = _tile(M, tm), _tile(K, tk)
    tn = N // out_split if out_split else _tile(N, tn)
    ni, nj, nk = M // tm, N // tn, K // tk
    n_in = 2 + (add is not None) + (ride is not None)

    def body(*refs):
        a_ref, b_ref, o_ref = refs[0], refs[1], refs[n_in]
        acc_ref = refs[n_in + 1 + (ride is not None)]
        i, j, k = pl.program_id(0), pl.program_id(1), pl.program_id(2)
        if ride is not None:
            ride_refs = (refs[n_in - 1], refs[n_in + 1], *refs[-3:])

            @pl.when((i == 0) & (j == 0) & (k == 0))
            def _():
                _ride(ride[0], "start", *ride_refs)

        @pl.when(k == 0)
        def _():
            acc_ref[...] = jnp.zeros_like(acc_ref)

        acc_ref[...] += _dot16(a_ref[...], b_ref[...], dims)

        @pl.when(k == nk - 1)
        def _():
            r = acc_ref[...]
            if add is not None:
                r = r + refs[2][...]
            o_ref[...] = r.astype(o_ref.dtype)

        if ride is not None:
            @pl.when((i == ni - 1) & (j == nj - 1) & (k == nk - 1))
            def _():
                _ride(ride[0], "wait", *ride_refs)

    a_spec = pl.BlockSpec((tk, tm), lambda i, j, k: (k, i)) if dims == "tn" else pl.BlockSpec((tm, tk), lambda i, j, k: (i, k))
    b_spec = pl.BlockSpec((tn, tk), lambda i, j, k: (j, k)) if dims == "nt" else pl.BlockSpec((tk, tn), lambda i, j, k: (k, j))
    o_spec = pl.BlockSpec((tm, tn), lambda i, j, k: (i, j))
    in_specs, args = [a_spec, b_spec], [a, b]
    if add is not None:
        in_specs.append(o_spec)
        args.append(add)
    out_specs, out_shape, scratch = [o_spec], [jax.ShapeDtypeStruct((M, N), out_dtype)], [pltpu.VMEM((tm, tn), F32)]
    if out_split:
        out_specs = [pl.BlockSpec((None, tm, tn), lambda i, j, k: (j, i, 0))]
        out_shape = [jax.ShapeDtypeStruct((out_split, M, tn), out_dtype)]
    if ride is not None:
        in_specs.append(pl.BlockSpec(memory_space=pl.ANY))
        args.append(ride[1])
        out_specs.append(pl.BlockSpec(memory_space=pl.ANY))
        out_shape.append(_ride_shape(*ride))
        scratch += _ride_scratch()
    sem = ("arbitrary",) * 3 if ride is not None else ("parallel", "parallel", "arbitrary")
    res = pl.pallas_call(
        body, name=name, grid=(ni, nj, nk), in_specs=in_specs, out_specs=out_specs, out_shape=out_shape,
        scratch_shapes=scratch,
        compiler_params=pltpu.CompilerParams(dimension_semantics=sem, vmem_limit_bytes=VMEM_LIMIT),
    )(*args)
    return res[0] if ride is None else res


def _rowwise(fn, rows, consts, outs, accs=(), *, name, tm=256):
    S = rows[0][0].shape[0]
    tm = min(tm, S)
    nr, nc, no = len(rows), len(consts), len(outs)

    def body(*refs):
        res = fn(*[r[...] for r in refs[:nr + nc]])
        res = tuple(res) if isinstance(res, (tuple, list)) else (res,)
        orefs, arefs = refs[nr + nc:nr + nc + no], refs[nr + nc + no:]
        for r, v in zip(orefs, res[:no]):
            r[...] = v.astype(r.dtype)
        if arefs:
            @pl.when(pl.program_id(0) == 0)
            def _():
                for r in arefs:
                    r[...] = jnp.zeros_like(r)

            for r, v in zip(arefs, res[no:]):
                r[...] += v

    in_specs = [pl.BlockSpec((tm, w), lambda i, cb=cb: (i, cb)) for (_, w, cb) in rows]
    in_specs += [pl.BlockSpec(c.shape, lambda i: (0, 0)) for c in consts]
    out_specs = [pl.BlockSpec((tm, w), lambda i: (i, 0)) for (w, _) in outs]
    out_specs += [pl.BlockSpec(s, lambda i: (0, 0)) for s in accs]
    out_shape = [jax.ShapeDtypeStruct((S, w), dt) for (w, dt) in outs]
    out_shape += [jax.ShapeDtypeStruct(s, F32) for s in accs]
    res = pl.pallas_call(
        body, name=name, grid=(S // tm,), in_specs=in_specs, out_specs=out_specs, out_shape=out_shape,
        compiler_params=pltpu.CompilerParams(dimension_semantics=("arbitrary",), vmem_limit_bytes=VMEM_LIMIT),
    )(*[r[0] for r in rows], *consts)
    return res


def _rowwise_bwd(f, rows, consts, cts, row_grads, const_grads, *, name, add=None, out_dtypes=None, tm=256):
    nr, nc, nct = len(rows), len(consts), len(cts)
    all_rows = list(rows) + list(cts) + ([add] if add is not None else [])

    def fn(*args):
        nrow = len(all_rows)
        prim = [x.astype(F32) for x in args[:nr]] + [x.astype(F32) for x in args[nrow:]]
        ct = tuple(x.astype(F32) for x in args[nr:nr + nct])
        out, vjp = jax.vjp(f, *prim)
        gs = vjp(ct if isinstance(out, (tuple, list)) else ct[0])
        res = [gs[k] for k in row_grads]
        if add is not None:
            res[0] = res[0] + args[nrow - 1]
        return tuple(res) + tuple(gs[nr + k] for k in const_grads)

    out_dtypes = out_dtypes or [F32] * len(row_grads)
    outs = [(rows[k][1], dt) for k, dt in zip(row_grads, out_dtypes)]
    accs = [consts[k].shape for k in const_grads]
    return _rowwise(fn, all_rows, consts, outs, accs, name=name, tm=tm)


def _rms(x, g, n=None):
    ms = jnp.sum(x * x, axis=-1, keepdims=True) * (1.0 / (n or x.shape[-1]))
    return x * lax.rsqrt(ms + EPS) * g


def _sigmoid(x):
    return 1.0 / (1.0 + jnp.exp(-x))


def _swap_halves(x):
    r = lax.broadcasted_iota(jnp.int32, (DH, DH), 0)
    c = lax.broadcasted_iota(jnp.int32, (DH, DH), 1)
    half = ROPE // 2
    perm = jnp.where(((r < half) & (c == r + half)) | ((r >= half) & (r < ROPE) & (c == r - half)), 1.0, 0.0)
    return lax.dot_general(x, perm.astype(F32), NN, precision=lax.Precision.HIGHEST, preferred_element_type=F32)


def _rope128(x, g128, cs):
    y = _rms(x, g128, n=ROPE)
    return y * cs[:, :DH] + _swap_halves(y) * cs[:, DH:]


def _f_norm(x, g):
    return _rms(x, g)


def _f_norm2(x, g1, g2):
    xn = x * lax.rsqrt(jnp.mean(x * x, axis=-1, keepdims=True) + EPS)
    return xn * g1, xn * g2


def _f_kv1(ckr, cs, g_ckv, g_kr):
    w = g_ckv.shape[-1]
    return _rms(ckr[:, :w], g_ckv), _rope128(ckr[:, w:], g_kr, cs)


def _f_kv2(kv, kr, g_kn):
    ks, vs = [], []
    for h in range(H_MLA):
        ks += [_rms(kv[:, 2 * DH * h:2 * DH * h + DH], g_kn), kr]
        vs.append(kv[:, 2 * DH * h + DH:2 * DH * (h + 1)])
    return jnp.concatenate(ks, axis=1), jnp.concatenate(vs, axis=1)


def _f_q2(q, cs, g_n, g_r):
    out = []
    for h in range(H_MLA):
        out += [_rms(q[:, 2 * DH * h:2 * DH * h + DH], g_n), _rope128(q[:, 2 * DH * h + DH:2 * DH * (h + 1)], g_r, cs)]
    return jnp.concatenate(out, axis=1)


def _f_mix(att, g_att, q_m, g_m, mkv, g_q, g_k):
    mem_w = H_MEM * DH
    heads = []
    for h in range(H_MEM):
        kh = _rms(mkv[:, h * DH:(h + 1) * DH], g_k)
        vh = mkv[:, mem_w + h * DH:mem_w + (h + 1) * DH]
        qh = _rms(q_m[:, h * DH:(h + 1) * DH], g_q)
        s = _bdot(qh, kh, "nt") * (DH ** -0.5)
        p = jnp.exp(s - lax.stop_gradient(jnp.max(s, axis=-1, keepdims=True)))
        p = p / jnp.sum(p, axis=-1, keepdims=True)
        heads.append(_bdot(p, vh, "nn"))
    mo = jnp.concatenate(heads, axis=1)
    return jnp.concatenate([att * (g_att * _sigmoid(g_att)), mo * (g_m * _sigmoid(g_m))], axis=1)


def _split_dot(x, u):
    hi = x.astype(BF16)
    lo = (x - hi.astype(F32)).astype(BF16)
    return (lax.dot_general(hi, u, NN, preferred_element_type=F32)
            + lax.dot_general(lo, u, NN, preferred_element_type=F32))


def _tri(t):
    r = lax.broadcasted_iota(jnp.int32, (t, t), 0)
    c = lax.broadcasted_iota(jnp.int32, (t, t), 1)
    return r, c


def _strict_lower(t):
    r, c = _tri(t)
    return (r > c).astype(BF16)


def _rows_ahead(tq, tk):
    return lax.broadcasted_iota(jnp.int32, (tq, tk), 0) - lax.broadcasted_iota(jnp.int32, (tq, tk), 1)


def _log_one_minus_beta(zr, scale):
    zs, nz = zr * scale, zr * (-scale)
    return zs, jnp.minimum(nz, 0.0) - jnp.log(1.0 + jnp.exp(jnp.minimum(zs, nz)))


def _sb_fwd(qkv, *, tq, tk, name, ride=None):
    S = qkv.shape[0]
    tq, tk = min(tq, S), min(tk, tq, S)
    nd = tq // tk
    H = H_SB
    scale = DH ** -0.5

    nq = S // tq

    def body(q_ref, k_ref, v_ref, *rest):
        o_ref = rest[1] if ride is not None else rest[0]
        h, i = pl.program_id(0), pl.program_id(1)
        if ride is not None:
            ride_refs = (rest[0], *rest[2:])

            @pl.when((h == 0) & (i == 0))
            def _():
                _ride(ride[0], "start", *ride_refs)

        q = q_ref[...]
        u = _strict_lower(tk)
        ahead = _rows_ahead(tq, tk)

        def block(j, acc, cb, keep):
            off = pl.multiple_of(j * tk, tk)
            k = k_ref[pl.ds(off, tk), :]
            v = v_ref[pl.ds(off, tk), :]
            z, l = _log_one_minus_beta(lax.dot_general(q, k, NT, preferred_element_type=F32), scale)
            if keep is not None:
                l = jnp.where(keep, l, 0.0)
            a = jnp.exp((z + l) + (_split_dot(l, u) + cb))
            if keep is not None:
                a = jnp.where(keep, a, 0.0)
            acc = acc + lax.dot_general(a.astype(BF16), v, NN, preferred_element_type=F32)
            return acc, cb + jnp.sum(l, axis=1, keepdims=True)

        carry = (jnp.zeros((tq, DH), F32), jnp.zeros((tq, 1), F32))
        for t in reversed(range(nd)):
            carry = block(i * nd + t, *carry, ahead > t * tk)
        acc, _ = lax.fori_loop(0, i * nd, lambda jj, carry: block(i * nd - 1 - jj, *carry, None), carry)
        o_ref[...] = acc

        if ride is not None:
            @pl.when((h == H - 1) & (i == nq - 1))
            def _():
                _ride(ride[0], "wait", *ride_refs)

    in_specs = [pl.BlockSpec((tq, DH), lambda h, i: (i, h)),
                pl.BlockSpec((S, DH), lambda h, i: (0, H + h)),
                pl.BlockSpec((S, DH), lambda h, i: (0, 2 * H + h))]
    out_specs = [pl.BlockSpec((tq, DH), lambda h, i: (i, h))]
    out_shape = [jax.ShapeDtypeStruct((S, H * DH), F32)]
    args, scratch = [qkv, qkv, qkv], []
    if ride is not None:
        in_specs.append(pl.BlockSpec(memory_space=pl.ANY))
        args.append(ride[1])
        out_specs.append(pl.BlockSpec(memory_space=pl.ANY))
        out_shape.append(_ride_shape(*ride))
        scratch = _ride_scratch()
    res = pl.pallas_call(
        body, name=name, grid=(H, nq), in_specs=in_specs, out_specs=out_specs, out_shape=out_shape,
        scratch_shapes=scratch,
        compiler_params=pltpu.CompilerParams(dimension_semantics=("arbitrary", "arbitrary"), vmem_limit_bytes=VMEM_LIMIT),
    )(*args)
    return res[0] if ride is None else res


def _sb_bwd(qkv, o, do, *, tq, tk, name, ride=None):
    S = qkv.shape[0]
    tq, tk = min(tq, S), min(tk, tq, S)
    nd = tq // tk
    H = H_SB
    scale = DH ** -0.5

    nq = S // tq

    def body(q_ref, k_ref, v_ref, o_ref, do_ref, *rest):
        dq_ref, dk_ref, dv_ref = rest[1:4] if ride is not None else rest[:3]
        h, i = pl.program_id(0), pl.program_id(1)
        if ride is not None:
            ride_refs = (rest[0], *rest[4:])

            @pl.when((h == 0) & (i == 0))
            def _():
                _ride(ride[0], "start", *ride_refs)

        @pl.when(i == 0)
        def _():
            dk_ref[...] = jnp.zeros_like(dk_ref)
            dv_ref[...] = jnp.zeros_like(dv_ref)

        q = q_ref[...]
        do = do_ref[...]
        do16 = do.astype(BF16)
        dsum = jnp.sum(do16.astype(F32) * o_ref[...], axis=1, keepdims=True)
        u = _strict_lower(tk)
        ahead = _rows_ahead(tq, tk)

        def block(j, dq, cb, ce, keep):
            off = pl.multiple_of(j * tk, tk)
            k = k_ref[pl.ds(off, tk), :]
            v = v_ref[pl.ds(off, tk), :]
            z, l = _log_one_minus_beta(lax.dot_general(q, k, NT, preferred_element_type=F32), scale)
            if keep is not None:
                l = jnp.where(keep, l, 0.0)
            log_beta = z + l
            a = jnp.exp(log_beta + (_split_dot(l, u) + cb))
            if keep is not None:
                a = jnp.where(keep, a, 0.0)
            a16 = a.astype(BF16)
            e = a16.astype(F32) * lax.dot_general(do16, v, NT, preferred_element_type=F32)
            left = dsum - (ce + _split_dot(e, u) + e)
            dz = (e - jnp.exp(log_beta) * (e + left)) * scale
            if keep is not None:
                dz = jnp.where(keep, dz, 0.0)
            dz = dz.astype(BF16)
            dq = dq + lax.dot_general(dz, k, NN, preferred_element_type=F32)
            dk_ref[pl.ds(off, tk), :] += lax.dot_general(dz, q, TN, preferred_element_type=F32)
            dv_ref[pl.ds(off, tk), :] += lax.dot_general(a16, do16, TN, preferred_element_type=F32)
            return dq, cb + jnp.sum(l, axis=1, keepdims=True), ce + jnp.sum(e, axis=1, keepdims=True)

        zero = jnp.zeros((tq, 1), F32)
        carry = (jnp.zeros((tq, DH), F32), zero, zero)
        for t in reversed(range(nd)):
            carry = block(i * nd + t, *carry, ahead > t * tk)
        dq, _, _ = lax.fori_loop(0, i * nd, lambda jj, carry: block(i * nd - 1 - jj, *carry, None), carry)
        dq_ref[...] = dq

        if ride is not None:
            @pl.when((h == H - 1) & (i == nq - 1))
            def _():
                _ride(ride[0], "wait", *ride_refs)

    blk = pl.BlockSpec((tq, DH), lambda h, i: (i, h))
    whole = pl.BlockSpec((S, DH), lambda h, i: (0, h))
    shp = jax.ShapeDtypeStruct((S, H * DH), F32)
    in_specs = [blk, pl.BlockSpec((S, DH), lambda h, i: (0, H + h)), pl.BlockSpec((S, DH), lambda h, i: (0, 2 * H + h)),
                blk, blk]
    out_specs, out_shape = [blk, whole, whole], [shp, shp, shp]
    args, scratch = [qkv, qkv, qkv, o, do], []
    if ride is not None:
        in_specs.append(pl.BlockSpec(memory_space=pl.ANY))
        args.append(ride[1])
        out_specs.append(pl.BlockSpec(memory_space=pl.ANY))
        out_shape.append(_ride_shape(*ride))
        scratch = _ride_scratch()
    return pl.pallas_call(
        body, name=name, grid=(H, nq), in_specs=in_specs, out_specs=out_specs, out_shape=out_shape,
        scratch_shapes=scratch,
        compiler_params=pltpu.CompilerParams(dimension_semantics=("arbitrary", "arbitrary"), vmem_limit_bytes=VMEM_LIMIT),
    )(*args)


def _mla_fwd(q, k, v, *, tq, tk, name):
    S = q.shape[0]
    tq, tk = min(tq, S), min(tk, tq, S)
    nd = tq // tk
    H = H_MLA
    scale = MLA_QK ** -0.5

    def body(q_ref, k_ref, v_ref, o_ref, lse_ref):
        i = pl.program_id(1)
        qb = q_ref[...]
        ahead = _rows_ahead(tq, tk)

        def block(j, m, den, acc, keep):
            off = pl.multiple_of(j * tk, tk)
            kb = k_ref[pl.ds(off, tk), :]
            vb = v_ref[pl.ds(off, tk), :]
            s = lax.dot_general(qb, kb, NT, preferred_element_type=F32) * scale
            if keep is not None:
                s = jnp.where(keep, s, -1e30)
            m_new = jnp.maximum(m, jnp.max(s, axis=1, keepdims=True))
            p = jnp.exp(s - m_new)
            alpha = jnp.exp(m - m_new)
            den = alpha * den + jnp.sum(p, axis=1, keepdims=True)
            acc = alpha * acc + lax.dot_general(p.astype(BF16), vb, NN, preferred_element_type=F32)
            return m_new, den, acc

        init = (jnp.full((tq, 1), -1e30, F32), jnp.zeros((tq, 1), F32), jnp.zeros((tq, DH), F32))
        carry = lax.fori_loop(0, i * nd, lambda j, carry: block(j, *carry, None), init)
        for t in range(nd):
            carry = block(i * nd + t, *carry, ahead >= t * tk)
        m, den, acc = carry
        o_ref[...] = acc / den
        lse_ref[0] = m + jnp.log(den)

    return pl.pallas_call(
        body, name=name, grid=(H, S // tq),
        in_specs=[pl.BlockSpec((tq, 2 * DH), lambda h, i: (i, h)),
                  pl.BlockSpec((S, 2 * DH), lambda h, i: (0, h)),
                  pl.BlockSpec((S, DH), lambda h, i: (0, h))],
        out_specs=[pl.BlockSpec((tq, DH), lambda h, i: (i, h)), pl.BlockSpec((1, tq, 1), lambda h, i: (h, i, 0))],
        out_shape=[jax.ShapeDtypeStruct((S, H * DH), F32), jax.ShapeDtypeStruct((H, S, 1), F32)],
        compiler_params=pltpu.CompilerParams(dimension_semantics=("arbitrary", "arbitrary"), vmem_limit_bytes=VMEM_LIMIT),
    )(q, k, v)


def _mla_bwd(q, k, v, o, do, lse, *, tq, tk, name):
    S = q.shape[0]
    tq, tk = min(tq, S), min(tk, tq, S)
    nd = tq // tk
    H = H_MLA
    scale = MLA_QK ** -0.5

    def body(q_ref, k_ref, v_ref, o_ref, do_ref, lse_ref, dq_ref, dk_ref, dv_ref):
        i = pl.program_id(1)

        @pl.when(i == 0)
        def _():
            dk_ref[...] = jnp.zeros_like(dk_ref)
            dv_ref[...] = jnp.zeros_like(dv_ref)

        qb = q_ref[...]
        do = do_ref[...]
        do16 = do.astype(BF16)
        dsum = jnp.sum(do * o_ref[...], axis=1, keepdims=True)
        lse = lse_ref[0]
        ahead = _rows_ahead(tq, tk)

        def block(j, dq, keep):
            off = pl.multiple_of(j * tk, tk)
            kb = k_ref[pl.ds(off, tk), :]
            vb = v_ref[pl.ds(off, tk), :]
            s = lax.dot_general(qb, kb, NT, preferred_element_type=F32) * scale
            if keep is not None:
                s = jnp.where(keep, s, -1e30)
            p = jnp.exp(s - lse)
            dp = lax.dot_general(do16, vb, NT, preferred_element_type=F32)
            ds = (p * (dp - dsum) * scale).astype(BF16)
            dk_ref[pl.ds(off, tk), :] += lax.dot_general(ds, qb, TN, preferred_element_type=F32)
            dv_ref[pl.ds(off, tk), :] += lax.dot_general(p.astype(BF16), do16, TN, preferred_element_type=F32)
            return dq + lax.dot_general(ds, kb, NN, preferred_element_type=F32)

        dq = lax.fori_loop(0, i * nd, lambda j, dq: block(j, dq, None), jnp.zeros((tq, 2 * DH), F32))
        for t in range(nd):
            dq = block(i * nd + t, dq, ahead >= t * tk)
        dq_ref[...] = dq

    blk = pl.BlockSpec((tq, DH), lambda h, i: (i, h))
    blk2 = pl.BlockSpec((tq, 2 * DH), lambda h, i: (i, h))
    return pl.pallas_call(
        body, name=name, grid=(H, S // tq),
        in_specs=[blk2, pl.BlockSpec((S, 2 * DH), lambda h, i: (0, h)), pl.BlockSpec((S, DH), lambda h, i: (0, h)),
                  blk, blk, pl.BlockSpec((1, tq, 1), lambda h, i: (h, i, 0))],
        out_specs=[blk2, pl.BlockSpec((S, 2 * DH), lambda h, i: (0, h)), pl.BlockSpec((S, DH), lambda h, i: (0, h))],
        out_shape=[jax.ShapeDtypeStruct((S, H * 2 * DH), F32), jax.ShapeDtypeStruct((S, H * 2 * DH), F32),
                   jax.ShapeDtypeStruct((S, H * DH), F32)],
        compiler_params=pltpu.CompilerParams(dimension_semantics=("arbitrary", "arbitrary"), vmem_limit_bytes=VMEM_LIMIT_BIG),
    )(q, k, v, o, do, lse)


def _all_gather(block, *, name):
    R, C = block.shape

    def body(x_ref, out_ref, send_sems, recv_sems, local_sem):
        x, y, c = _mesh_pos()
        me, sibling = (x, y, c), (x, y, 1 - c)
        chips = [(1 - x, y), (x, 1 - y), (1 - x, 1 - y)]

        def slot(px, py, pc):
            return out_ref.at[4 * px + 2 * py + pc]

        def copy(k, blk, to, src=None):
            return pltpu.make_async_remote_copy(
                src_ref=slot(*blk) if src is None else src, dst_ref=slot(*blk),
                send_sem=send_sems.at[k], recv_sem=recv_sems.at[k],
                device_id=to, device_id_type=pl.DeviceIdType.MESH)

        mine = pltpu.make_async_copy(x_ref, slot(*me), local_sem)
        mine.start()
        first = [copy(0, me, sibling, src=x_ref)]
        first += [copy(1 + j, me, (*chip, c), src=x_ref) for j, chip in enumerate(chips)]
        for cp in first:
            cp.start()
        passed = [copy(4 + j, (*chip, c), sibling) for j, chip in enumerate(chips)]
        for j, chip in enumerate(chips):
            copy(1 + j, (*chip, c), me).wait_recv()
            passed[j].start()
        copy(0, sibling, me).wait_recv()
        for j, chip in enumerate(chips):
            copy(4 + j, (*chip, 1 - c), me).wait_recv()
        for cp in first + passed:
            cp.wait_send()
        mine.wait()

    return pl.pallas_call(
        body, name=name,
        out_shape=jax.ShapeDtypeStruct((N_DEV, R, C), block.dtype),
        in_specs=[pl.BlockSpec(memory_space=pl.ANY)], out_specs=pl.BlockSpec(memory_space=pl.ANY),
        scratch_shapes=[pltpu.SemaphoreType.DMA((7,)), pltpu.SemaphoreType.DMA((7,)), pltpu.SemaphoreType.DMA],
    )(block)


def _all_to_all(send, *, name):
    def body(*refs):
        _ride("a2a", "start", *refs)
        _ride("a2a", "wait", *refs)

    return pl.pallas_call(
        body, name=name, out_shape=_ride_shape("a2a", send),
        in_specs=[pl.BlockSpec(memory_space=pl.ANY)], out_specs=pl.BlockSpec(memory_space=pl.ANY),
        scratch_shapes=_ride_scratch(),
    )(send)


def _reduce_adamw(recv, w, m, v, *, name):
    R = w.shape[0]
    tr = next(t for t in (128, 64, 32, 16, 8) if R % t == 0)

    def body(g_ref, w_ref, m_ref, v_ref, og_ref, od_ref, om_ref, ov_ref):
        g = g_ref[0].astype(F32)
        for s in range(1, N_DEV):
            g = g + g_ref[s].astype(F32)
        mn = ADAM_B1 * m_ref[...] + (1.0 - ADAM_B1) * g
        vn = ADAM_B2 * v_ref[...] + (1.0 - ADAM_B2) * jnp.square(g)
        m_hat = mn / (1.0 - ADAM_B1 ** ADAM_STEP)
        v_hat = vn / (1.0 - ADAM_B2 ** ADAM_STEP)
        og_ref[...] = g
        od_ref[...] = -ADAM_LR * (m_hat / (jnp.sqrt(v_hat) + ADAM_EPS) + ADAM_WD * w_ref[...])
        om_ref[...] = mn
        ov_ref[...] = vn

    blk = pl.BlockSpec((tr, LANES), lambda i: (i, 0))
    shp = jax.ShapeDtypeStruct((R, LANES), F32)
    return pl.pallas_call(
        body, name=name, grid=(R // tr,),
        in_specs=[pl.BlockSpec((N_DEV, tr, LANES), lambda i: (0, i, 0)), blk, blk, blk],
        out_specs=[blk, blk, blk, blk], out_shape=[shp, shp, shp, shp],
        compiler_params=pltpu.CompilerParams(dimension_semantics=("parallel",), vmem_limit_bytes=VMEM_LIMIT),
    )(recv, w, m, v)


SHARDED = (("a_norm", 1), ("a_w_in", 2), ("a_w_out", 1), ("w_dkv", 0), ("w_ukv", 1), ("b_w_in", 2),
           ("b_w_uq", 2), ("b_w_out", 1), ("w_mem_kv", 1))
SMALL = ("kv_norm", "g_ckv", "g_k_nope", "g_k_rope", "b_norm", "b_g_q_lat", "b_g_q_nope", "b_g_q_rope",
         "mem_norm", "g_mem_q", "g_mem_k")
WEIGHTS = ("a_norm", "a_w_in", "a_w_out", "kv_norm", "w_dkv", "g_ckv", "w_ukv", "g_k_nope", "g_k_rope", "b_norm",
           "b_w_in", "b_g_q_lat", "b_w_uq", "b_g_q_nope", "b_g_q_rope", "b_w_out", "mem_norm", "w_mem_kv",
           "g_mem_q", "g_mem_k")
ROW_MULT = 8
ROW_BLOCK = 128


def _rows_of(n, mult):
    rows = -(-n // LANES)
    return -(-rows // mult) * mult


def _to_rows(flat, mult):
    n = flat.shape[-1]
    rows = _rows_of(n, mult)
    pad = [(0, 0)] * (flat.ndim - 1) + [(0, rows * LANES - n)]
    return jnp.pad(flat, pad).reshape(*flat.shape[:-1], rows, LANES)


def _split8(full, axis):
    shp = full.shape
    t = full.reshape(*shp[:axis], N_DEV, shp[axis] // N_DEV, *shp[axis + 1:])
    return jnp.moveaxis(t, axis, 0).reshape(N_DEV, -1)


def _join8(rows, axis, shard_shape):
    t = rows.reshape(N_DEV, *shard_shape)
    t = jnp.moveaxis(t, 0, axis)
    return t.reshape(*shard_shape[:axis], N_DEV * shard_shape[axis], *shard_shape[axis + 1:])


def _stack_rows(parts, block=ROW_BLOCK):
    rows = sum(p.shape[-2] for p in parts)
    if rows % block:
        parts = list(parts) + [jnp.zeros((*parts[0].shape[:-2], -rows % block, LANES), parts[0].dtype)]
    return jnp.concatenate(parts, axis=-2)


def _pack_local(vals, names, mult, extra=None):
    parts = [_to_rows(vals[n].reshape(-1), mult) for n in names]
    if extra is not None:
        parts.append(_to_rows(extra.reshape(-1), mult))
    return _stack_rows(parts)


def _unpack_local(slab, names, shapes, mult):
    out, row = {}, 0
    for n in names:
        size = 1
        for d in shapes[n]:
            size *= d
        rows = _rows_of(size, mult)
        out[n] = slab[row:row + rows].reshape(-1)[:size].reshape(shapes[n])
        row += rows
    return out, row


def kernel(x, mem, positions, a_norm, a_w_in, a_w_out, kv_norm, w_dkv, g_ckv, w_ukv, g_k_nope, g_k_rope, b_norm, b_w_in, b_g_q_lat, b_w_uq, b_g_q_nope, b_g_q_rope, b_w_out, mem_norm, w_mem_kv, g_mem_q, g_mem_k, loss_target, m_a_norm, m_a_w_in, m_a_w_out, m_kv_norm, m_w_dkv, m_g_ckv, m_w_ukv, m_g_k_nope, m_g_k_rope, m_b_norm, m_b_w_in, m_b_g_q_lat, m_b_w_uq, m_b_g_q_nope, m_b_g_q_rope, m_b_w_out, m_mem_norm, m_w_mem_kv, m_g_mem_q, m_g_mem_k, v_a_norm, v_a_w_in, v_a_w_out, v_kv_norm, v_w_dkv, v_g_ckv, v_w_ukv, v_g_k_nope, v_g_k_rope, v_b_norm, v_b_w_in, v_b_g_q_lat, v_b_w_uq, v_b_g_q_nope, v_b_g_q_rope, v_b_w_out, v_mem_norm, v_w_mem_kv, v_g_mem_q, v_g_mem_k):
    wts = dict(a_norm=a_norm, a_w_in=a_w_in, a_w_out=a_w_out, kv_norm=kv_norm, w_dkv=w_dkv, g_ckv=g_ckv, w_ukv=w_ukv,
               g_k_nope=g_k_nope, g_k_rope=g_k_rope, b_norm=b_norm, b_w_in=b_w_in, b_g_q_lat=b_g_q_lat, b_w_uq=b_w_uq,
               b_g_q_nope=b_g_q_nope, b_g_q_rope=b_g_q_rope, b_w_out=b_w_out, mem_norm=mem_norm, w_mem_kv=w_mem_kv,
               g_mem_q=g_mem_q, g_mem_k=g_mem_k)
    mom = dict(a_norm=m_a_norm, a_w_in=m_a_w_in, a_w_out=m_a_w_out, kv_norm=m_kv_norm, w_dkv=m_w_dkv, g_ckv=m_g_ckv,
               w_ukv=m_w_ukv, g_k_nope=m_g_k_nope, g_k_rope=m_g_k_rope, b_norm=m_b_norm, b_w_in=m_b_w_in,
               b_g_q_lat=m_b_g_q_lat, b_w_uq=m_b_w_uq, b_g_q_nope=m_b_g_q_nope, b_g_q_rope=m_b_g_q_rope,
               b_w_out=m_b_w_out, mem_norm=m_mem_norm, w_mem_kv=m_w_mem_kv, g_mem_q=m_g_mem_q, g_mem_k=m_g_mem_k)
    var = dict(a_norm=v_a_norm, a_w_in=v_a_w_in, a_w_out=v_a_w_out, kv_norm=v_kv_norm, w_dkv=v_w_dkv, g_ckv=v_g_ckv,
               w_ukv=v_w_ukv, g_k_nope=v_g_k_nope, g_k_rope=v_g_k_rope, b_norm=v_b_norm, b_w_in=v_b_w_in,
               b_g_q_lat=v_b_g_q_lat, b_w_uq=v_b_w_uq, b_g_q_nope=v_b_g_q_nope, b_g_q_rope=v_b_g_q_rope,
               b_w_out=v_b_w_out, mem_norm=v_mem_norm, w_mem_kv=v_w_mem_kv, g_mem_q=v_g_mem_q, g_mem_k=v_g_mem_k)
    shapes = {n: wts[n].shape for n in WEIGHTS}
    S, D = x.shape[1], x.shape[2]
    xs, ms, tgt = x[0], mem[0], loss_target[0]
    sb_w, mem_w, mla_w = H_SB * DH, H_MEM * DH, H_MLA * DH
    q_lora, kv_lora = b_g_q_lat.shape[-1], g_ckv.shape[-1]

    def pieces_of(names):
        return [_to_rows(lax.bitcast_convert_type(a_norm.reshape(-1), BF16).reshape(-1), 16) if n == "a_norm"
                else _to_rows(wts[n].astype(BF16).reshape(-1), 16) for n in names]

    def unpack_gathered(gathered, names):
        full, row = {}, 0
        for n in names:
            size = wts[n].size * (2 if n == "a_norm" else 1)
            rows = _rows_of(size, 16)
            flat = gathered[:, row:row + rows].reshape(N_DEV, -1)[:, :size]
            if n == "a_norm":
                flat = lax.bitcast_convert_type(flat.reshape(N_DEV, -1, 2), F32)
            full[n] = _join8(flat, dict(SHARDED)[n], wts[n].shape)
            row += rows
        return full

    first, later = ("a_norm", "a_w_in"), tuple(n for n, _ in SHARDED[2:])
    full = unpack_gathered(_all_gather(jnp.concatenate(pieces_of(first), axis=0), name="gather_first"), first)
    g_a, w_a_in = full["a_norm"], full["a_w_in"][0]

    row2 = lambda g: g.reshape(1, -1)
    h0 = _rowwise(_f_norm, [(xs, D, 0)], [g_a], [(D, BF16)], name="a_norm_fwd")[0]
    pa = _mm(h0, w_a_in, "nn", name="a_in")
    qkv = pa[:, :3 * sb_w].astype(BF16)
    sb, gathered = _sb_fwd(qkv, tq=1024, tk=256, name="sb_fwd",
                           ride=("gather", jnp.concatenate(pieces_of(later), axis=0)))
    full = unpack_gathered(gathered, later)
    w_a_out = full["a_w_out"][0]
    w_dkv_p = jnp.pad(full["w_dkv"], ((0, 0), (0, ROPE)))
    w_ukv_f = full["w_ukv"]
    wb = full["b_w_in"][0]
    w_b_in = jnp.concatenate([wb[:, q_lora:q_lora + mla_w], wb[:, :q_lora], wb[:, q_lora + mla_w:]], axis=1)
    w_uq_p = jnp.pad(full["b_w_uq"][0].reshape(q_lora, H_MLA, MLA_QK),
                     ((0, 0), (0, 0), (0, 2 * DH - MLA_QK))).reshape(q_lora, H_MLA * 2 * DH)
    w_b_out = full["b_w_out"][0]
    w_mem = full["w_mem_kv"]

    pad128 = lambda g: jnp.pad(g.reshape(1, -1), ((0, 0), (0, DH - ROPE)))
    g_kr, g_qr = pad128(g_k_rope), pad128(b_g_q_rope[0])
    g_kv, g_b, g_c, g_kn = row2(kv_norm), row2(b_norm[0]), row2(g_ckv), row2(g_k_nope)
    g_ql, g_qn = row2(b_g_q_lat[0]), row2(b_g_q_nope[0])

    inv_freq = jnp.power(ROPE_THETA, -jnp.arange(0, ROPE, 2, dtype=F32) / ROPE)
    ang = positions[0].astype(F32)[:, None] * inv_freq
    z64 = jnp.zeros((S, DH - ROPE), F32)
    cs = jnp.concatenate([jnp.cos(ang), jnp.cos(ang), z64, -jnp.sin(ang), jnp.sin(ang), z64], axis=1)

    mn, mkv = [], []
    for l in range(2):
        mn.append(_rowwise(_f_norm, [(ms, D, 0)], [row2(mem_norm[l])], [(D, BF16)], name=f"mem_norm{l}")[0])
        mkv.append(_mm(mn[l], w_mem[l], "nn", name=f"mem_kv{l}"))
    g_mq = [row2(g_mem_q[l]) for l in range(2)]
    g_mk = [row2(g_mem_k[l]) for l in range(2)]

    mix_a_rows = [(sb, sb_w, 0), (pa, sb_w, 3), (pa, mem_w, 4 * sb_w // mem_w), (pa, mem_w, 4 * sb_w // mem_w + 1)]
    mixed_a = _rowwise(_f_mix, mix_a_rows, [mkv[0], g_mq[0], g_mk[0]], [(sb_w + mem_w, BF16)], name="a_mix_fwd")[0]
    x1 = _mm(mixed_a, w_a_out, "nn", add=xs, name="a_out")

    hk, hb = _rowwise(_f_norm2, [(x1, D, 0)], [g_kv, g_b], [(D, BF16), (D, BF16)], name="b_norm_fwd")
    ckr = _mm(hk, w_dkv_p, "nn", name="kv_down")
    cn, kr = _rowwise(_f_kv1, [(ckr, kv_lora + DH, 0), (cs, 2 * DH, 0)], [g_c, g_kr],
                      [(kv_lora, BF16), (DH, F32)], name="kv1_fwd")
    kvu = _mm(cn, w_ukv_f, "nn", name="kv_up")
    k2, v2 = _rowwise(_f_kv2, [(kvu, H_MLA * 2 * DH, 0), (kr, DH, 0)], [g_kn],
                      [(H_MLA * 2 * DH, BF16), (mla_w, BF16)], name="kv2_fwd")
    pb = _mm(hb, w_b_in, "nn", name="b_in")
    ql = _rowwise(_f_norm, [(pb, q_lora, mla_w // q_lora)], [g_ql], [(q_lora, BF16)], name="q_lat_fwd")[0]
    qraw = _mm(ql, w_uq_p, "nn", name="q_up")
    q2 = _rowwise(_f_q2, [(qraw, H_MLA * 2 * DH, 0), (cs, 2 * DH, 0)], [g_qn, g_qr],
                  [(H_MLA * 2 * DH, BF16)], name="q2_fwd")[0]
    att, lse = _mla_fwd(q2, k2, v2, tq=1024, tk=1024, name="mla_fwd")
    cb = (mla_w + q_lora) // mem_w
    mix_b_rows = [(att, mla_w, 0), (pb, mla_w, 0), (pb, mem_w, cb), (pb, mem_w, cb + 1)]
    mixed_b = _rowwise(_f_mix, mix_b_rows, [mkv[1], g_mq[1], g_mk[1]], [(mla_w + mem_w, BF16)], name="b_mix_fwd")[0]
    y = _mm(mixed_b, w_b_out, "nn", add=x1, name="b_out")

    def loss_fn(yb, tb):
        err = yb - tb
        part = 0.5 * jnp.sum(jnp.sum(err * err, axis=-1, keepdims=True) * (1.0 / D))
        return err * (1.0 / D), jnp.full((1, DH), part, F32)

    dy, loss_part = _rowwise(loss_fn, [(y, D, 0), (tgt, D, 0)], [], [(D, F32)], [(1, DH)], name="loss")

    gr = {}
    d_mixed_b = _mm(dy, w_b_out, "nt", name="b_out_dx")
    gr["b_w_out"] = _mm(mixed_b, dy, "tn", out_dtype=BF16, name="b_out_dw")[None]
    d_att, d_gmla, d_qm_b, d_gm_b, d_mkv1, d_gq1, d_gk1 = _rowwise_bwd(
        _f_mix, mix_b_rows, [mkv[1], g_mq[1], g_mk[1]], [(d_mixed_b, mla_w + mem_w, 0)],
        [0, 1, 2, 3], [0, 1, 2], name="b_mix_bwd")
    dq2, dk2, dv2 = _mla_bwd(q2, k2, v2, att, d_att, lse, tq=512, tk=512, name="mla_bwd")
    d_qraw, d_gqn, d_gqr = _rowwise_bwd(
        _f_q2, [(qraw, H_MLA * 2 * DH, 0), (cs, 2 * DH, 0)], [g_qn, g_qr], [(dq2, H_MLA * 2 * DH, 0)],
        [0], [0, 1], name="q2_bwd")
    d_ql = _mm(d_qraw, w_uq_p, "nt", name="q_up_dx")
    d_wuq = _mm(ql, d_qraw, "tn", out_dtype=BF16, name="q_up_dw")
    gr["b_w_uq"] = d_wuq.reshape(q_lora, H_MLA, 2 * DH)[:, :, :MLA_QK].reshape(1, q_lora, H_MLA * MLA_QK)
    d_qlat, d_gql = _rowwise_bwd(_f_norm, [(pb, q_lora, mla_w // q_lora)], [g_ql], [(d_ql, q_lora, 0)],
                                 [0], [0], name="q_lat_bwd")
    d_pb = jnp.concatenate([d_gmla, d_qlat, d_qm_b, d_gm_b], axis=1)
    d_hb = _mm(d_pb, w_b_in, "nt", name="b_in_dx")
    d_wbin = _mm(hb, d_pb, "tn", out_dtype=BF16, name="b_in_dw")
    gr["b_w_in"] = jnp.concatenate([d_wbin[:, mla_w:mla_w + q_lora], d_wbin[:, :mla_w], d_wbin[:, mla_w + q_lora:]],
                                   axis=1)[None]
    d_kvu, d_kr, d_gkn = _rowwise_bwd(
        _f_kv2, [(kvu, H_MLA * 2 * DH, 0), (kr, DH, 0)], [g_kn], [(dk2, H_MLA * 2 * DH, 0), (dv2, mla_w, 0)],
        [0, 1], [0], name="kv2_bwd")
    d_cn = _mm(d_kvu, w_ukv_f, "nt", name="kv_up_dx")
    gr["w_ukv"] = _mm(cn, d_kvu, "tn", out_dtype=BF16, name="kv_up_dw")
    d_ckr, d_gc, d_gkr = _rowwise_bwd(
        _f_kv1, [(ckr, kv_lora + DH, 0), (cs, 2 * DH, 0)], [g_c, g_kr], [(d_cn, kv_lora, 0), (d_kr, DH, 0)],
        [0], [0, 1], name="kv1_bwd")
    d_hk = _mm(d_ckr, w_dkv_p, "nt", name="kv_down_dx")
    gr["w_dkv"] = _mm(hk, d_ckr, "tn", out_dtype=BF16, name="kv_down_dw")[:, :kv_lora + ROPE]
    d_x1, d_gkv, d_gb = _rowwise_bwd(_f_norm2, [(x1, D, 0)], [g_kv, g_b], [(d_hk, D, 0), (d_hb, D, 0)],
                                     [0], [0, 1], add=(dy, D, 0), name="b_norm_bwd")
    d_mixed_a = _mm(d_x1, w_a_out, "nt", name="a_out_dx")
    gr["a_w_out"] = _mm(mixed_a, d_x1, "tn", out_dtype=BF16, name="a_out_dw")[None]
    d_sb, d_gsb, d_qm_a, d_gm_a, d_mkv0, d_gq0, d_gk0 = _rowwise_bwd(
        _f_mix, mix_a_rows, [mkv[0], g_mq[0], g_mk[0]], [(d_mixed_a, sb_w + mem_w, 0)],
        [0, 1, 2, 3], [0, 1, 2], name="a_mix_bwd")
    d_wmem, d_mnorm = [], []
    for l, d_mkv in enumerate((d_mkv0, d_mkv1)):
        d_mn = _mm(d_mkv, w_mem[l], "nt", name=f"mem_kv_dx{l}")
        d_wmem.append(_mm(mn[l], d_mkv, "tn", out_dtype=BF16, name=f"mem_kv_dw{l}"))
        d_mnorm.append(_rowwise_bwd(_f_norm, [(ms, D, 0)], [row2(mem_norm[l])], [(d_mn, D, 0)], [], [0],
                                    name=f"mem_norm_bwd{l}")[0])
    gr["w_mem_kv"] = jnp.stack(d_wmem)

    mid = tuple(n for n, _ in SHARDED[2:])
    send_mid = _stack_rows([_to_rows(_split8(gr[n], ax), 16) for n, ax in SHARDED[2:]])
    dq, dk, dv, recv_mid = _sb_bwd(qkv, sb, d_sb, tq=1024, tk=256, name="sb_bwd", ride=("a2a", send_mid))
    d_pa = jnp.concatenate([dq, dk, dv, d_gsb, d_qm_a, d_gm_a], axis=1)
    send_ain = _mm(h0, d_pa, "tn", out_dtype=BF16, out_split=N_DEV, name="a_in_dw")
    d_h0, recv_ain = _mm(d_pa, w_a_in, "nt", name="a_in_dx", ride=("a2a", send_ain.reshape(N_DEV, -1, LANES)))
    grad_x, d_ga = _rowwise_bwd(_f_norm, [(xs, D, 0)], [g_a], [(d_h0, D, 0)], [0], [0], add=(d_x1, D, 0),
                                name="a_norm_bwd")
    gr["mem_norm"] = jnp.concatenate(d_mnorm, axis=0)
    gr["g_mem_q"] = jnp.concatenate([d_gq0, d_gq1], axis=0)
    gr["g_mem_k"] = jnp.concatenate([d_gk0, d_gk1], axis=0)
    gr["kv_norm"], gr["b_norm"], gr["g_ckv"], gr["g_k_nope"] = d_gkv, d_gb, d_gc, d_gkn
    gr["g_k_rope"], gr["b_g_q_rope"] = d_gkr[:, :ROPE], d_gqr[:, :ROPE]
    gr["b_g_q_lat"], gr["b_g_q_nope"] = d_gql, d_gqn
    last = ("a_norm",) + SMALL
    parts = [_to_rows(_split8(d_ga, 1), ROW_MULT)]
    parts += [jnp.broadcast_to(_to_rows(gr[n].reshape(-1), ROW_MULT)[None], (N_DEV, _rows_of(gr[n].size, ROW_MULT), LANES))
              for n in SMALL]
    parts.append(jnp.broadcast_to(_to_rows(loss_part[0, :1], ROW_MULT)[None], (N_DEV, ROW_MULT, LANES)))
    recv_last = _all_to_all(_stack_rows(parts), name="exchange_small")

    out = [{}, {}, {}, {}]
    zero = jnp.zeros((1,), F32)
    for tag, recv, names, mult, extra in (("mid", recv_mid, mid, 16, None), ("last", recv_last, last, ROW_MULT, zero)):
        slabs = _reduce_adamw(recv, *[_pack_local(t, names, mult, extra) for t in (wts, mom, var)],
                              name=f"reduce_adamw_{tag}")
        for o, slab in zip(out, slabs):
            o.update(_unpack_local(slab, names, shapes, mult)[0])
        if extra is not None:
            loss = slabs[0][_unpack_local(slabs[0], names, shapes, mult)[1], 0]
    slabs = _reduce_adamw(recv_ain, *[t["a_w_in"].reshape(-1, LANES) for t in (wts, mom, var)], name="reduce_adamw_ain")
    for o, slab in zip(out, slabs):
        o["a_w_in"] = slab.reshape(shapes["a_w_in"])
    return (loss, grad_x[None], *[o[n] for o in out for n in WEIGHTS])
```

```python
import functools

import jax
import jax.numpy as jnp
from jax import lax
from jax.experimental import pallas as pl
from jax.experimental.pallas import tpu as pltpu

F32, BF16 = jnp.float32, jnp.bfloat16

N_DEV = 8
DH = 128
H_SB, H_MEM, H_MLA = 12, 4, 12
ROPE = 64
MLA_QK = DH + ROPE
EPS = 1e-6
ROPE_THETA = 10000.0
ADAM_LR, ADAM_B1, ADAM_B2, ADAM_EPS, ADAM_WD, ADAM_STEP = 0.001, 0.9, 0.999, 1e-08, 0.01, 10

LANES = 1024
VMEM_LIMIT = 48 * 1024 * 1024
VMEM_LIMIT_BIG = 56 * 1024 * 1024

NN = (((1,), (0,)), ((), ()))
NT = (((1,), (1,)), ((), ()))
TN = (((0,), (0,)), ((), ()))
_DIMS = {"nn": NN, "nt": NT, "tn": TN}


def _dot16(a, b, dims):
    return lax.dot_general(a.astype(BF16), b.astype(BF16), _DIMS[dims], preferred_element_type=F32)


@functools.partial(jax.custom_vjp, nondiff_argnums=(2,))
def _bdot(a, b, dims):
    return _dot16(a, b, dims)


def _bdot_fwd(a, b, dims):
    return _dot16(a, b, dims), (a, b)


def _bdot_bwd(dims, res, g):
    a, b = res
    if dims == "nn":
        return _dot16(g, b, "nt"), _dot16(a, g, "tn")
    return _dot16(g, b, "nn"), _dot16(g, a, "tn")


_bdot.defvjp(_bdot_fwd, _bdot_bwd)


def _tile(n, pref):
    if n <= pref:
        return n
    t = (pref // 128) * 128
    while n % t:
        t -= 128
    return t


def _mesh_pos():
    return lax.axis_index("x"), lax.axis_index("y"), lax.axis_index("c")


def _ride_shape(kind, src):
    return jax.ShapeDtypeStruct((N_DEV, *src.shape) if kind == "gather" else src.shape, src.dtype)


def _ride_scratch():
    return [pltpu.SemaphoreType.DMA((N_DEV - 1,)), pltpu.SemaphoreType.DMA((N_DEV - 1,)), pltpu.SemaphoreType.DMA]


def _ride(kind, phase, s_ref, r_ref, send_sems, recv_sems, local_sem):
    x, y, c = _mesh_pos()
    me = 4 * x + 2 * y + c
    src = (lambda lin: s_ref) if kind == "gather" else (lambda lin: s_ref.at[lin])
    local = pltpu.make_async_copy(src(me), r_ref.at[me], local_sem)
    if phase == "start":
        local.start()
    for k in range(1, N_DEV):
        p = (1 - x if k & 4 else x, 1 - y if k & 2 else y, 1 - c if k & 1 else c)
        lin = 4 * p[0] + 2 * p[1] + p[2]
        cp = pltpu.make_async_remote_copy(
            src_ref=src(lin), dst_ref=r_ref.at[me] if phase == "start" else r_ref.at[lin],
            send_sem=send_sems.at[k - 1], recv_sem=recv_sems.at[k - 1],
            device_id=p, device_id_type=pl.DeviceIdType.MESH)
        if phase == "start":
            cp.start()
        else:
            cp.wait_recv()
            cp.wait_send()
    if phase == "wait":
        local.wait()


def _mm(a, b, dims, *, name, out_dtype=F32, add=None, ride=None, out_split=None, tm=1024, tn=1024, tk=1024):
    if dims == "tn":
        (K, M), (_, N) = a.shape, b.shape
    elif dims == "nt":
        (M, K), (N, _) = a.shape, b.shape
    else:
        (M, K), (_, N) = a.shape, b.shape
    tm, tk = _tile(M, tm), _tile(K, tk)
    tn = N // out_split if out_split else _tile(N, tn)
    ni, nj, nk = M // tm, N // tn, K // tk
    n_in = 2 + (add is not None) + (ride is not None)

    def body(*refs):
        a_ref, b_ref, o_ref = refs[0], refs[1], refs[n_in]
        acc_ref = refs[n_in + 1 + (ride is not None)]
        i, j, k = pl.program_id(0), pl.program_id(1), pl.program_id(2)
        if ride is not None:
            ride_refs = (refs[n_in - 1], refs[n_in + 1], *refs[-3:])

            @pl.when((i == 0) & (j == 0) & (k == 0))
            def _():
                _ride(ride[0], "start", *ride_refs)

        @pl.when(k == 0)
        def _():
            acc_ref[...] = jnp.zeros_like(acc_ref)

        acc_ref[...] += _dot16(a_ref[...], b_ref[...], dims)

        @pl.when(k == nk - 1)
        def _():
            r = acc_ref[...]
            if add is not None:
                r = r + refs[2][...]
            o_ref[...] = r.astype(o_ref.dtype)

        if ride is not None:
            @pl.when((i == ni - 1) & (j == nj - 1) & (k == nk - 1))
            def _():
                _ride(ride[0], "wait", *ride_refs)

    a_spec = pl.BlockSpec((tk, tm), lambda i, j, k: (k, i)) if dims == "tn" else pl.BlockSpec((tm, tk), lambda i, j, k: (i, k))
    b_spec = pl.BlockSpec((tn, tk), lambda i, j, k: (j, k)) if dims == "nt" else pl.BlockSpec((tk, tn), lambda i, j, k: (k, j))
    o_spec = pl.BlockSpec((tm, tn), lambda i, j, k: (i, j))
    in_specs, args = [a_spec, b_spec], [a, b]
    if add is not None:
        in_specs.append(o_spec)
        args.append(add)
    out_specs, out_shape, scratch = [o_spec], [jax.ShapeDtypeStruct((M, N), out_dtype)], [pltpu.VMEM((tm, tn), F32)]
    if out_split:
        out_specs = [pl.BlockSpec((None, tm, tn), lambda i, j, k: (j, i, 0))]
        out_shape = [jax.ShapeDtypeStruct((out_split, M, tn), out_dtype)]
    if ride is not None:
        in_specs.append(pl.BlockSpec(memory_space=pl.ANY))
        args.append(ride[1])
        out_specs.append(pl.BlockSpec(memory_space=pl.ANY))
        out_shape.append(_ride_shape(*ride))
        scratch += _ride_scratch()
    sem = ("arbitrary",) * 3 if ride is not None else ("parallel", "parallel", "arbitrary")
    res = pl.pallas_call(
        body, name=name, grid=(ni, nj, nk), in_specs=in_specs, out_specs=out_specs, out_shape=out_shape,
        scratch_shapes=scratch,
        compiler_params=pltpu.CompilerParams(dimension_semantics=sem, vmem_limit_bytes=VMEM_LIMIT),
    )(*args)
    return res[0] if ride is None else res


def _rowwise(fn, rows, consts, outs, accs=(), *, name, tm=256):
    S = rows[0][0].shape[0]
    tm = min(tm, S)
    nr, nc, no = len(rows), len(consts), len(outs)

    def body(*refs):
        res = fn(*[r[...] for r in refs[:nr + nc]])
        res = tuple(res) if isinstance(res, (tuple, list)) else (res,)
        orefs, arefs = refs[nr + nc:nr + nc + no], refs[nr + nc + no:]
        for r, v in zip(orefs, res[:no]):
            r[...] = v.astype(r.dtype)
        if arefs:
            @pl.when(pl.program_id(0) == 0)
            def _():
                for r in arefs:
                    r[...] = jnp.zeros_like(r)

            for r, v in zip(arefs, res[no:]):
                r[...] += v

    in_specs = [pl.BlockSpec((tm, w), lambda i, cb=cb: (i, cb)) for (_, w, cb) in rows]
    in_specs += [pl.BlockSpec(c.shape, lambda i: (0, 0)) for c in consts]
    out_specs = [pl.BlockSpec((tm, w), lambda i: (i, 0)) for (w, _) in outs]
    out_specs += [pl.BlockSpec(s, lambda i: (0, 0)) for s in accs]
    out_shape = [jax.ShapeDtypeStruct((S, w), dt) for (w, dt) in outs]
    out_shape += [jax.ShapeDtypeStruct(s, F32) for s in accs]
    res = pl.pallas_call(
        body, name=name, grid=(S // tm,), in_specs=in_specs, out_specs=out_specs, out_shape=out_shape,
        compiler_params=pltpu.CompilerParams(dimension_semantics=("arbitrary",), vmem_limit_bytes=VMEM_LIMIT),
    )(*[r[0] for r in rows], *consts)
    return res


def _rowwise_bwd(f, rows, consts, cts, row_grads, const_grads, *, name, add=None, out_dtypes=None, tm=256):
    nr, nc, nct = len(rows), len(consts), len(cts)
    all_rows = list(rows) + list(cts) + ([add] if add is not None else [])

    def fn(*args):
        nrow = len(all_rows)
        prim = [x.astype(F32) for x in args[:nr]] + [x.astype(F32) for x in args[nrow:]]
        ct = tuple(x.astype(F32) for x in args[nr:nr + nct])
        out, vjp = jax.vjp(f, *prim)
        gs = vjp(ct if isinstance(out, (tuple, list)) else ct[0])
        res = [gs[k] for k in row_grads]
        if add is not None:
            res[0] = res[0] + args[nrow - 1]
        return tuple(res) + tuple(gs[nr + k] for k in const_grads)

    out_dtypes = out_dtypes or [F32] * len(row_grads)
    outs = [(rows[k][1], dt) for k, dt in zip(row_grads, out_dtypes)]
    accs = [consts[k].shape for k in const_grads]
    return _rowwise(fn, all_rows, consts, outs, accs, name=name, tm=tm)


def _rms(x, g, n=None):
    ms = jnp.sum(x * x, axis=-1, keepdims=True) * (1.0 / (n or x.shape[-1]))
    return x * lax.rsqrt(ms + EPS) * g


def _sigmoid(x):
    return 1.0 / (1.0 + jnp.exp(-x))


def _swap_halves(x):
    r = lax.broadcasted_iota(jnp.int32, (DH, DH), 0)
    c = lax.broadcasted_iota(jnp.int32, (DH, DH), 1)
    half = ROPE // 2
    perm = jnp.where(((r < half) & (c == r + half)) | ((r >= half) & (r < ROPE) & (c == r - half)), 1.0, 0.0)
    return lax.dot_general(x, perm.astype(F32), NN, precision=lax.Precision.HIGHEST, preferred_element_type=F32)


def _rope128(x, g128, cs):
    y = _rms(x, g128, n=ROPE)
    return y * cs[:, :DH] + _swap_halves(y) * cs[:, DH:]


def _f_norm(x, g):
    return _rms(x, g)


def _f_norm2(x, g1, g2):
    xn = x * lax.rsqrt(jnp.mean(x * x, axis=-1, keepdims=True) + EPS)
    return xn * g1, xn * g2


def _f_kv1(ckr, cs, g_ckv, g_kr):
    w = g_ckv.shape[-1]
    return _rms(ckr[:, :w], g_ckv), _rope128(ckr[:, w:], g_kr, cs)


def _f_kv2(kv, kr, g_kn):
    ks, vs = [], []
    for h in range(H_MLA):
        ks += [_rms(kv[:, 2 * DH * h:2 * DH * h + DH], g_kn), kr]
        vs.append(kv[:, 2 * DH * h + DH:2 * DH * (h + 1)])
    return jnp.concatenate(ks, axis=1), jnp.concatenate(vs, axis=1)


def _f_q2(q, cs, g_n, g_r):
    out = []
    for h in range(H_MLA):
        out += [_rms(q[:, 2 * DH * h:2 * DH * h + DH], g_n), _rope128(q[:, 2 * DH * h + DH:2 * DH * (h + 1)], g_r, cs)]
    return jnp.concatenate(out, axis=1)


def _f_mix(att, g_att, q_m, g_m, mkv, g_q, g_k):
    mem_w = H_MEM * DH
    heads = []
    for h in range(H_MEM):
        kh = _rms(mkv[:, h * DH:(h + 1) * DH], g_k)
        vh = mkv[:, mem_w + h * DH:mem_w + (h + 1) * DH]
        qh = _rms(q_m[:, h * DH:(h + 1) * DH], g_q)
        s = _bdot(qh, kh, "nt") * (DH ** -0.5)
        p = jnp.exp(s - lax.stop_gradient(jnp.max(s, axis=-1, keepdims=True)))
        p = p / jnp.sum(p, axis=-1, keepdims=True)
        heads.append(_bdot(p, vh, "nn"))
    mo = jnp.concatenate(heads, axis=1)
    return jnp.concatenate([att * (g_att * _sigmoid(g_att)), mo * (g_m * _sigmoid(g_m))], axis=1)


def _split_dot(x, u):
    hi = x.astype(BF16)
    lo = (x - hi.astype(F32)).astype(BF16)
    return (lax.dot_general(hi, u, NN, preferred_element_type=F32)
            + lax.dot_general(lo, u, NN, preferred_element_type=F32))


def _tri(t):
    r = lax.broadcasted_iota(jnp.int32, (t, t), 0)
    c = lax.broadcasted_iota(jnp.int32, (t, t), 1)
    return r, c


def _strict_lower(t):
    r, c = _tri(t)
    return (r > c).astype(BF16)


def _rows_ahead(tq, tk):
    return lax.broadcasted_iota(jnp.int32, (tq, tk), 0) - lax.broadcasted_iota(jnp.int32, (tq, tk), 1)


EXP_UNDERFLOW = -110.0


def _log_one_minus_beta(zr, scale):
    zs, nz = zr * scale, zr * (-scale)
    return zs, jnp.minimum(nz, 0.0) - jnp.log(1.0 + jnp.exp(jnp.minimum(zs, nz)))


def _sb_fwd(qkv, *, tq, tk, name, ride=None):
    S = qkv.shape[0]
    tq, tk = min(tq, S), min(tk, tq, S)
    nd = tq // tk
    H = H_SB
    scale = DH ** -0.5

    nq = S // tq

    def body(q_ref, k_ref, v_ref, *rest):
        o_ref = rest[1] if ride is not None else rest[0]
        h, i = pl.program_id(0), pl.program_id(1)
        if ride is not None:
            ride_refs = (rest[0], *rest[2:])

            @pl.when((h == 0) & (i == 0))
            def _():
                _ride(ride[0], "start", *ride_refs)

        q = q_ref[...]
        u = _strict_lower(tk)
        ahead = _rows_ahead(tq, tk)

        def block(j, acc, cb, keep):
            off = pl.multiple_of(j * tk, tk)
            k = k_ref[pl.ds(off, tk), :]
            v = v_ref[pl.ds(off, tk), :]
            z, l = _log_one_minus_beta(lax.dot_general(q, k, NT, preferred_element_type=F32), scale)
            if keep is not None:
                l = jnp.where(keep, l, 0.0)
            a = jnp.exp((z + l) + (_split_dot(l, u) + cb))
            if keep is not None:
                a = jnp.where(keep, a, 0.0)
            acc = acc + lax.dot_general(a.astype(BF16), v, NN, preferred_element_type=F32)
            return acc, cb + jnp.sum(l, axis=1, keepdims=True)

        carry = (jnp.zeros((tq, DH), F32), jnp.zeros((tq, 1), F32))
        for t in reversed(range(nd)):
            carry = block(i * nd + t, *carry, ahead > t * tk)
        _, acc, _ = lax.while_loop(
            lambda st: (st[0] < i * nd) & (jnp.max(st[2]) > EXP_UNDERFLOW),
            lambda st: (st[0] + 1, *block(i * nd - 1 - st[0], st[1], st[2], None)), (jnp.int32(0), *carry))
        o_ref[...] = acc

        if ride is not None:
            @pl.when((h == H - 1) & (i == nq - 1))
            def _():
                _ride(ride[0], "wait", *ride_refs)

    in_specs = [pl.BlockSpec((tq, DH), lambda h, i: (i, h)),
                pl.BlockSpec((S, DH), lambda h, i: (0, H + h)),
                pl.BlockSpec((S, DH), lambda h, i: (0, 2 * H + h))]
    out_specs = [pl.BlockSpec((tq, DH), lambda h, i: (i, h))]
    out_shape = [jax.ShapeDtypeStruct((S, H * DH), F32)]
    args, scratch = [qkv, qkv, qkv], []
    if ride is not None:
        in_specs.append(pl.BlockSpec(memory_space=pl.ANY))
        args.append(ride[1])
        out_specs.append(pl.BlockSpec(memory_space=pl.ANY))
        out_shape.append(_ride_shape(*ride))
        scratch = _ride_scratch()
    res = pl.pallas_call(
        body, name=name, grid=(H, nq), in_specs=in_specs, out_specs=out_specs, out_shape=out_shape,
        scratch_shapes=scratch,
        compiler_params=pltpu.CompilerParams(dimension_semantics=("arbitrary", "arbitrary"), vmem_limit_bytes=VMEM_LIMIT),
    )(*args)
    return res[0] if ride is None else res


def _sb_bwd(qkv, o, do, *, tq, tk, name, ride=None):
    S = qkv.shape[0]
    tq, tk = min(tq, S), min(tk, tq, S)
    nd = tq // tk
    H = H_SB
    scale = DH ** -0.5

    nq = S // tq

    def body(q_ref, k_ref, v_ref, o_ref, do_ref, *rest):
        dq_ref, dk_ref, dv_ref = rest[1:4] if ride is not None else rest[:3]
        h, i = pl.program_id(0), pl.program_id(1)
        if ride is not None:
            ride_refs = (rest[0], *rest[4:])

            @pl.when((h == 0) & (i == 0))
            def _():
                _ride(ride[0], "start", *ride_refs)

        @pl.when(i == 0)
        def _():
            dk_ref[...] = jnp.zeros_like(dk_ref)
            dv_ref[...] = jnp.zeros_like(dv_ref)

        q = q_ref[...]
        do = do_ref[...]
        do16 = do.astype(BF16)
        dsum = jnp.sum(do16.astype(F32) * o_ref[...], axis=1, keepdims=True)
        u = _strict_lower(tk)
        ahead = _rows_ahead(tq, tk)

        def block(j, dq, cb, ce, keep):
            off = pl.multiple_of(j * tk, tk)
            k = k_ref[pl.ds(off, tk), :]
            v = v_ref[pl.ds(off, tk), :]
            z, l = _log_one_minus_beta(lax.dot_general(q, k, NT, preferred_element_type=F32), scale)
            if keep is not None:
                l = jnp.where(keep, l, 0.0)
            log_beta = z + l
            a = jnp.exp(log_beta + (_split_dot(l, u) + cb))
            if keep is not None:
                a = jnp.where(keep, a, 0.0)
            a16 = a.astype(BF16)
            e = a16.astype(F32) * lax.dot_general(do16, v, NT, preferred_element_type=F32)
            left = dsum - (ce + _split_dot(e, u) + e)
            dz = (e - jnp.exp(log_beta) * (e + left)) * scale
            if keep is not None:
                dz = jnp.where(keep, dz, 0.0)
            dz = dz.astype(BF16)
            dq = dq + lax.dot_general(dz, k, NN, preferred_element_type=F32)
            dk_ref[pl.ds(off, tk), :] += lax.dot_general(dz, q, TN, preferred_element_type=F32)
            dv_ref[pl.ds(off, tk), :] += lax.dot_general(a16, do16, TN, preferred_element_type=F32)
            return dq, cb + jnp.sum(l, axis=1, keepdims=True), ce + jnp.sum(e, axis=1, keepdims=True)

        zero = jnp.zeros((tq, 1), F32)
        carry = (jnp.zeros((tq, DH), F32), zero, zero)
        for t in reversed(range(nd)):
            carry = block(i * nd + t, *carry, ahead > t * tk)
        _, dq, _, _ = lax.while_loop(
            lambda st: (st[0] < i * nd) & (jnp.max(st[2]) > EXP_UNDERFLOW),
            lambda st: (st[0] + 1, *block(i * nd - 1 - st[0], st[1], st[2], st[3], None)), (jnp.int32(0), *carry))
        dq_ref[...] = dq

        if ride is not None:
            @pl.when((h == H - 1) & (i == nq - 1))
            def _():
                _ride(ride[0], "wait", *ride_refs)

    blk = pl.BlockSpec((tq, DH), lambda h, i: (i, h))
    whole = pl.BlockSpec((S, DH), lambda h, i: (0, h))
    shp = jax.ShapeDtypeStruct((S, H * DH), F32)
    in_specs = [blk, pl.BlockSpec((S, DH), lambda h, i: (0, H + h)), pl.BlockSpec((S, DH), lambda h, i: (0, 2 * H + h)),
                blk, blk]
    out_specs, out_shape = [blk, whole, whole], [shp, shp, shp]
    args, scratch = [qkv, qkv, qkv, o, do], []
    if ride is not None:
        in_specs.append(pl.BlockSpec(memory_space=pl.ANY))
        args.append(ride[1])
        out_specs.append(pl.BlockSpec(memory_space=pl.ANY))
        out_shape.append(_ride_shape(*ride))
        scratch = _ride_scratch()
    return pl.pallas_call(
        body, name=name, grid=(H, nq), in_specs=in_specs, out_specs=out_specs, out_shape=out_shape,
        scratch_shapes=scratch,
        compiler_params=pltpu.CompilerParams(dimension_semantics=("arbitrary", "arbitrary"), vmem_limit_bytes=VMEM_LIMIT),
    )(*args)


def _mla_fwd(q, k, v, *, tq, tk, name):
    S = q.shape[0]
    tq, tk = min(tq, S), min(tk, tq, S)
    nd = tq // tk
    H = H_MLA
    scale = MLA_QK ** -0.5

    def body(q_ref, k_ref, v_ref, o_ref, lse_ref):
        i = pl.program_id(1)
        qb = q_ref[...]
        ahead = _rows_ahead(tq, tk)

        def block(j, m, den, acc, keep):
            off = pl.multiple_of(j * tk, tk)
            kb = k_ref[pl.ds(off, tk), :]
            vb = v_ref[pl.ds(off, tk), :]
            s = lax.dot_general(qb, kb, NT, preferred_element_type=F32) * scale
            if keep is not None:
                s = jnp.where(keep, s, -1e30)
            m_new = jnp.maximum(m, jnp.max(s, axis=1, keepdims=True))
            p = jnp.exp(s - m_new)
            alpha = jnp.exp(m - m_new)
            den = alpha * den + jnp.sum(p, axis=1, keepdims=True)
            acc = alpha * acc + lax.dot_general(p.astype(BF16), vb, NN, preferred_element_type=F32)
            return m_new, den, acc

        init = (jnp.full((tq, 1), -1e30, F32), jnp.zeros((tq, 1), F32), jnp.zeros((tq, DH), F32))
        carry = lax.fori_loop(0, i * nd, lambda j, carry: block(j, *carry, None), init)
        for t in range(nd):
            carry = block(i * nd + t, *carry, ahead >= t * tk)
        m, den, acc = carry
        o_ref[...] = acc / den
        lse_ref[0] = m + jnp.log(den)

    return pl.pallas_call(
        body, name=name, grid=(H, S // tq),
        in_specs=[pl.BlockSpec((tq, 2 * DH), lambda h, i: (i, h)),
                  pl.BlockSpec((S, 2 * DH), lambda h, i: (0, h)),
                  pl.BlockSpec((S, DH), lambda h, i: (0, h))],
        out_specs=[pl.BlockSpec((tq, DH), lambda h, i: (i, h)), pl.BlockSpec((1, tq, 1), lambda h, i: (h, i, 0))],
        out_shape=[jax.ShapeDtypeStruct((S, H * DH), F32), jax.ShapeDtypeStruct((H, S, 1), F32)],
        compiler_params=pltpu.CompilerParams(dimension_semantics=("arbitrary", "arbitrary"), vmem_limit_bytes=VMEM_LIMIT),
    )(q, k, v)


def _mla_bwd(q, k, v, o, do, lse, *, tq, tk, name):
    S = q.shape[0]
    tq, tk = min(tq, S), min(tk, tq, S)
    nd = tq // tk
    H = H_MLA
    scale = MLA_QK ** -0.5

    def body(q_ref, k_ref, v_ref, o_ref, do_ref, lse_ref, dq_ref, dk_ref, dv_ref):
        i = pl.program_id(1)

        @pl.when(i == 0)
        def _():
            dk_ref[...] = jnp.zeros_like(dk_ref)
            dv_ref[...] = jnp.zeros_like(dv_ref)

        qb = q_ref[...]
        do = do_ref[...]
        do16 = do.astype(BF16)
        dsum = jnp.sum(do * o_ref[...], axis=1, keepdims=True)
        lse = lse_ref[0]
        ahead = _rows_ahead(tq, tk)

        def block(j, dq, keep):
            off = pl.multiple_of(j * tk, tk)
            kb = k_ref[pl.ds(off, tk), :]
            vb = v_ref[pl.ds(off, tk), :]
            s = lax.dot_general(qb, kb, NT, preferred_element_type=F32) * scale
            if keep is not None:
                s = jnp.where(keep, s, -1e30)
            p = jnp.exp(s - lse)
            dp = lax.dot_general(do16, vb, NT, preferred_element_type=F32)
            ds = (p * (dp - dsum) * scale).astype(BF16)
            dk_ref[pl.ds(off, tk), :] += lax.dot_general(ds, qb, TN, preferred_element_type=F32)
            dv_ref[pl.ds(off, tk), :] += lax.dot_general(p.astype(BF16), do16, TN, preferred_element_type=F32)
            return dq + lax.dot_general(ds, kb, NN, preferred_element_type=F32)

        dq = lax.fori_loop(0, i * nd, lambda j, dq: block(j, dq, None), jnp.zeros((tq, 2 * DH), F32))
        for t in range(nd):
            dq = block(i * nd + t, dq, ahead >= t * tk)
        dq_ref[...] = dq

    blk = pl.BlockSpec((tq, DH), lambda h, i: (i, h))
    blk2 = pl.BlockSpec((tq, 2 * DH), lambda h, i: (i, h))
    return pl.pallas_call(
        body, name=name, grid=(H, S // tq),
        in_specs=[blk2, pl.BlockSpec((S, 2 * DH), lambda h, i: (0, h)), pl.BlockSpec((S, DH), lambda h, i: (0, h)),
                  blk, blk, pl.BlockSpec((1, tq, 1), lambda h, i: (h, i, 0))],
        out_specs=[blk2, pl.BlockSpec((S, 2 * DH), lambda h, i: (0, h)), pl.BlockSpec((S, DH), lambda h, i: (0, h))],
        out_shape=[jax.ShapeDtypeStruct((S, H * 2 * DH), F32), jax.ShapeDtypeStruct((S, H * 2 * DH), F32),
                   jax.ShapeDtypeStruct((S, H * DH), F32)],
        compiler_params=pltpu.CompilerParams(dimension_semantics=("arbitrary", "arbitrary"), vmem_limit_bytes=VMEM_LIMIT_BIG),
    )(q, k, v, o, do, lse)


def _all_gather(block, *, name):
    R, C = block.shape

    def body(x_ref, out_ref, send_sems, recv_sems, local_sem):
        x, y, c = _mesh_pos()
        me, sibling = (x, y, c), (x, y, 1 - c)
        chips = [(1 - x, y), (x, 1 - y), (1 - x, 1 - y)]

        def slot(px, py, pc):
            return out_ref.at[4 * px + 2 * py + pc]

        def copy(k, blk, to, src=None):
            return pltpu.make_async_remote_copy(
                src_ref=slot(*blk) if src is None else src, dst_ref=slot(*blk),
                send_sem=send_sems.at[k], recv_sem=recv_sems.at[k],
                device_id=to, device_id_type=pl.DeviceIdType.MESH)

        mine = pltpu.make_async_copy(x_ref, slot(*me), local_sem)
        mine.start()
        first = [copy(0, me, sibling, src=x_ref)]
        first += [copy(1 + j, me, (*chip, c), src=x_ref) for j, chip in enumerate(chips)]
        for cp in first:
            cp.start()
        passed = [copy(4 + j, (*chip, c), sibling) for j, chip in enumerate(chips)]
        for j, chip in enumerate(chips):
            copy(1 + j, (*chip, c), me).wait_recv()
            passed[j].start()
        copy(0, sibling, me).wait_recv()
        for j, chip in enumerate(chips):
            copy(4 + j, (*chip, 1 - c), me).wait_recv()
        for cp in first + passed:
            cp.wait_send()
        mine.wait()

    return pl.pallas_call(
        body, name=name,
        out_shape=jax.ShapeDtypeStruct((N_DEV, R, C), block.dtype),
        in_specs=[pl.BlockSpec(memory_space=pl.ANY)], out_specs=pl.BlockSpec(memory_space=pl.ANY),
        scratch_shapes=[pltpu.SemaphoreType.DMA((7,)), pltpu.SemaphoreType.DMA((7,)), pltpu.SemaphoreType.DMA],
    )(block)


def _all_to_all(send, *, name):
    def body(*refs):
        _ride("a2a", "start", *refs)
        _ride("a2a", "wait", *refs)

    return pl.pallas_call(
        body, name=name, out_shape=_ride_shape("a2a", send),
        in_specs=[pl.BlockSpec(memory_space=pl.ANY)], out_specs=pl.BlockSpec(memory_space=pl.ANY),
        scratch_shapes=_ride_scratch(),
    )(send)


def _reduce_adamw(recv, w, m, v, *, name):
    R = w.shape[0]
    tr = next(t for t in (128, 64, 32, 16, 8) if R % t == 0)

    def body(g_ref, w_ref, m_ref, v_ref, og_ref, od_ref, om_ref, ov_ref):
        g = g_ref[0].astype(F32)
        for s in range(1, N_DEV):
            g = g + g_ref[s].astype(F32)
        mn = ADAM_B1 * m_ref[...] + (1.0 - ADAM_B1) * g
        vn = ADAM_B2 * v_ref[...] + (1.0 - ADAM_B2) * jnp.square(g)
        m_hat = mn / (1.0 - ADAM_B1 ** ADAM_STEP)
        v_hat = vn / (1.0 - ADAM_B2 ** ADAM_STEP)
        og_ref[...] = g
        od_ref[...] = -ADAM_LR * (m_hat / (jnp.sqrt(v_hat) + ADAM_EPS) + ADAM_WD * w_ref[...])
        om_ref[...] = mn
        ov_ref[...] = vn

    blk = pl.BlockSpec((tr, LANES), lambda i: (i, 0))
    shp = jax.ShapeDtypeStruct((R, LANES), F32)
    return pl.pallas_call(
        body, name=name, grid=(R // tr,),
        in_specs=[pl.BlockSpec((N_DEV, tr, LANES), lambda i: (0, i, 0)), blk, blk, blk],
        out_specs=[blk, blk, blk, blk], out_shape=[shp, shp, shp, shp],
        compiler_params=pltpu.CompilerParams(dimension_semantics=("parallel",), vmem_limit_bytes=VMEM_LIMIT),
    )(recv, w, m, v)


SHARDED = (("a_norm", 1), ("a_w_in", 2), ("a_w_out", 1), ("w_dkv", 0), ("w_ukv", 1), ("b_w_in", 2),
           ("b_w_uq", 2), ("b_w_out", 1), ("w_mem_kv", 1))
SMALL = ("kv_norm", "g_ckv", "g_k_nope", "g_k_rope", "b_norm", "b_g_q_lat", "b_g_q_nope", "b_g_q_rope",
         "mem_norm", "g_mem_q", "g_mem_k")
WEIGHTS = ("a_norm", "a_w_in", "a_w_out", "kv_norm", "w_dkv", "g_ckv", "w_ukv", "g_k_nope", "g_k_rope", "b_norm",
           "b_w_in", "b_g_q_lat", "b_w_uq", "b_g_q_nope", "b_g_q_rope", "b_w_out", "mem_norm", "w_mem_kv",
           "g_mem_q", "g_mem_k")
ROW_MULT = 8
ROW_BLOCK = 128


def _rows_of(n, mult):
    rows = -(-n // LANES)
    return -(-rows // mult) * mult


def _to_rows(flat, mult):
    n = flat.shape[-1]
    rows = _rows_of(n, mult)
    pad = [(0, 0)] * (flat.ndim - 1) + [(0, rows * LANES - n)]
    return jnp.pad(flat, pad).reshape(*flat.shape[:-1], rows, LANES)


def _split8(full, axis):
    shp = full.shape
    t = full.reshape(*shp[:axis], N_DEV, shp[axis] // N_DEV, *shp[axis + 1:])
    return jnp.moveaxis(t, axis, 0).reshape(N_DEV, -1)


def _join8(rows, axis, shard_shape):
    t = rows.reshape(N_DEV, *shard_shape)
    t = jnp.moveaxis(t, 0, axis)
    return t.reshape(*shard_shape[:axis], N_DEV * shard_shape[axis], *shard_shape[axis + 1:])


def _stack_rows(parts, block=ROW_BLOCK):
    rows = sum(p.shape[-2] for p in parts)
    if rows % block:
        parts = list(parts) + [jnp.zeros((*parts[0].shape[:-2], -rows % block, LANES), parts[0].dtype)]
    return jnp.concatenate(parts, axis=-2)


def _pack_local(vals, names, mult, extra=None):
    parts = [_to_rows(vals[n].reshape(-1), mult) for n in names]
    if extra is not None:
        parts.append(_to_rows(extra.reshape(-1), mult))
    return _stack_rows(parts)


def _unpack_local(slab, names, shapes, mult):
    out, row = {}, 0
    for n in names:
        size = 1
        for d in shapes[n]:
            size *= d
        rows = _rows_of(size, mult)
        out[n] = slab[row:row + rows].reshape(-1)[:size].reshape(shapes[n])
        row += rows
    return out, row


def kernel(x, mem, positions, a_norm, a_w_in, a_w_out, kv_norm, w_dkv, g_ckv, w_ukv, g_k_nope, g_k_rope, b_norm, b_w_in, b_g_q_lat, b_w_uq, b_g_q_nope, b_g_q_rope, b_w_out, mem_norm, w_mem_kv, g_mem_q, g_mem_k, loss_target, m_a_norm, m_a_w_in, m_a_w_out, m_kv_norm, m_w_dkv, m_g_ckv, m_w_ukv, m_g_k_nope, m_g_k_rope, m_b_norm, m_b_w_in, m_b_g_q_lat, m_b_w_uq, m_b_g_q_nope, m_b_g_q_rope, m_b_w_out, m_mem_norm, m_w_mem_kv, m_g_mem_q, m_g_mem_k, v_a_norm, v_a_w_in, v_a_w_out, v_kv_norm, v_w_dkv, v_g_ckv, v_w_ukv, v_g_k_nope, v_g_k_rope, v_b_norm, v_b_w_in, v_b_g_q_lat, v_b_w_uq, v_b_g_q_nope, v_b_g_q_rope, v_b_w_out, v_mem_norm, v_w_mem_kv, v_g_mem_q, v_g_mem_k):
    wts = dict(a_norm=a_norm, a_w_in=a_w_in, a_w_out=a_w_out, kv_norm=kv_norm, w_dkv=w_dkv, g_ckv=g_ckv, w_ukv=w_ukv,
               g_k_nope=g_k_nope, g_k_rope=g_k_rope, b_norm=b_norm, b_w_in=b_w_in, b_g_q_lat=b_g_q_lat, b_w_uq=b_w_uq,
               b_g_q_nope=b_g_q_nope, b_g_q_rope=b_g_q_rope, b_w_out=b_w_out, mem_norm=mem_norm, w_mem_kv=w_mem_kv,
               g_mem_q=g_mem_q, g_mem_k=g_mem_k)
    mom = dict(a_norm=m_a_norm, a_w_in=m_a_w_in, a_w_out=m_a_w_out, kv_norm=m_kv_norm, w_dkv=m_w_dkv, g_ckv=m_g_ckv,
               w_ukv=m_w_ukv, g_k_nope=m_g_k_nope, g_k_rope=m_g_k_rope, b_norm=m_b_norm, b_w_in=m_b_w_in,
               b_g_q_lat=m_b_g_q_lat, b_w_uq=m_b_w_uq, b_g_q_nope=m_b_g_q_nope, b_g_q_rope=m_b_g_q_rope,
               b_w_out=m_b_w_out, mem_norm=m_mem_norm, w_mem_kv=m_w_mem_kv, g_mem_q=m_g_mem_q, g_mem_k=m_g_mem_k)
    var = dict(a_norm=v_a_norm, a_w_in=v_a_w_in, a_w_out=v_a_w_out, kv_norm=v_kv_norm, w_dkv=v_w_dkv, g_ckv=v_g_ckv,
               w_ukv=v_w_ukv, g_k_nope=v_g_k_nope, g_k_rope=v_g_k_rope, b_norm=v_b_norm, b_w_in=v_b_w_in,
               b_g_q_lat=v_b_g_q_lat, b_w_uq=v_b_w_uq, b_g_q_nope=v_b_g_q_nope, b_g_q_rope=v_b_g_q_rope,
               b_w_out=v_b_w_out, mem_norm=v_mem_norm, w_mem_kv=v_w_mem_kv, g_mem_q=v_g_mem_q, g_mem_k=v_g_mem_k)
    shapes = {n: wts[n].shape for n in WEIGHTS}
    S, D = x.shape[1], x.shape[2]
    xs, ms, tgt = x[0], mem[0], loss_target[0]
    sb_w, mem_w, mla_w = H_SB * DH, H_MEM * DH, H_MLA * DH
    q_lora, kv_lora = b_g_q_lat.shape[-1], g_ckv.shape[-1]

    def pieces_of(names):
        return [_to_rows(lax.bitcast_convert_type(a_norm.reshape(-1), BF16).reshape(-1), 16) if n == "a_norm"
                else _to_rows(wts[n].astype(BF16).reshape(-1), 16) for n in names]

    def unpack_gathered(gathered, names):
        full, row = {}, 0
        for n in names:
            size = wts[n].size * (2 if n == "a_norm" else 1)
            rows = _rows_of(size, 16)
            flat = gathered[:, row:row + rows].reshape(N_DEV, -1)[:, :size]
            if n == "a_norm":
                flat = lax.bitcast_convert_type(flat.reshape(N_DEV, -1, 2), F32)
            full[n] = _join8(flat, dict(SHARDED)[n], wts[n].shape)
            row += rows
        return full

    first, second, third = ("a_norm", "a_w_in"), ("a_w_out", "w_dkv", "w_ukv", "w_mem_kv"), ("b_w_in", "b_w_uq", "b_w_out")
    full = unpack_gathered(_all_gather(jnp.concatenate(pieces_of(first), axis=0), name="gather_first"), first)
    g_a, w_a_in = full["a_norm"], full["a_w_in"][0]

    row2 = lambda g: g.reshape(1, -1)
    h0 = _rowwise(_f_norm, [(xs, D, 0)], [g_a], [(D, BF16)], name="a_norm_fwd")[0]
    pa, gathered = _mm(h0, w_a_in, "nn", name="a_in",
                       ride=("gather", jnp.concatenate(pieces_of(second), axis=0)))
    full = unpack_gathered(gathered, second)
    qkv = pa[:, :3 * sb_w].astype(BF16)
    sb, gathered = _sb_fwd(qkv, tq=512, tk=256, name="sb_fwd",
                           ride=("gather", jnp.concatenate(pieces_of(third), axis=0)))
    full.update(unpack_gathered(gathered, third))
    w_a_out = full["a_w_out"][0]
    w_dkv_p = jnp.pad(full["w_dkv"], ((0, 0), (0, ROPE)))
    w_ukv_f = full["w_ukv"]
    wb = full["b_w_in"][0]
    w_b_in = jnp.concatenate([wb[:, q_lora:q_lora + mla_w], wb[:, :q_lora], wb[:, q_lora + mla_w:]], axis=1)
    w_uq_p = jnp.pad(full["b_w_uq"][0].reshape(q_lora, H_MLA, MLA_QK),
                     ((0, 0), (0, 0), (0, 2 * DH - MLA_QK))).reshape(q_lora, H_MLA * 2 * DH)
    w_b_out = full["b_w_out"][0]
    w_mem = full["w_mem_kv"]

    pad128 = lambda g: jnp.pad(g.reshape(1, -1), ((0, 0), (0, DH - ROPE)))
    g_kr, g_qr = pad128(g_k_rope), pad128(b_g_q_rope[0])
    g_kv, g_b, g_c, g_kn = row2(kv_norm), row2(b_norm[0]), row2(g_ckv), row2(g_k_nope)
    g_ql, g_qn = row2(b_g_q_lat[0]), row2(b_g_q_nope[0])

    inv_freq = jnp.power(ROPE_THETA, -jnp.arange(0, ROPE, 2, dtype=F32) / ROPE)
    ang = positions[0].astype(F32)[:, None] * inv_freq
    z64 = jnp.zeros((S, DH - ROPE), F32)
    cs = jnp.concatenate([jnp.cos(ang), jnp.cos(ang), z64, -jnp.sin(ang), jnp.sin(ang), z64], axis=1)

    mn, mkv = [], []
    for l in range(2):
        mn.append(_rowwise(_f_norm, [(ms, D, 0)], [row2(mem_norm[l])], [(D, BF16)], name=f"mem_norm{l}")[0])
        mkv.append(_mm(mn[l], w_mem[l], "nn", name=f"mem_kv{l}"))
    g_mq = [row2(g_mem_q[l]) for l in range(2)]
    g_mk = [row2(g_mem_k[l]) for l in range(2)]

    mix_a_rows = [(sb, sb_w, 0), (pa, sb_w, 3), (pa, mem_w, 4 * sb_w // mem_w), (pa, mem_w, 4 * sb_w // mem_w + 1)]
    mixed_a = _rowwise(_f_mix, mix_a_rows, [mkv[0], g_mq[0], g_mk[0]], [(sb_w + mem_w, BF16)], name="a_mix_fwd")[0]
    x1 = _mm(mixed_a, w_a_out, "nn", add=xs, name="a_out")

    hk, hb = _rowwise(_f_norm2, [(x1, D, 0)], [g_kv, g_b], [(D, BF16), (D, BF16)], name="b_norm_fwd")
    ckr = _mm(hk, w_dkv_p, "nn", name="kv_down")
    cn, kr = _rowwise(_f_kv1, [(ckr, kv_lora + DH, 0), (cs, 2 * DH, 0)], [g_c, g_kr],
                      [(kv_lora, BF16), (DH, F32)], name="kv1_fwd")
    kvu = _mm(cn, w_ukv_f, "nn", name="kv_up")
    k2, v2 = _rowwise(_f_kv2, [(kvu, H_MLA * 2 * DH, 0), (kr, DH, 0)], [g_kn],
                      [(H_MLA * 2 * DH, BF16), (mla_w, BF16)], name="kv2_fwd")
    pb = _mm(hb, w_b_in, "nn", name="b_in")
    ql = _rowwise(_f_norm, [(pb, q_lora, mla_w // q_lora)], [g_ql], [(q_lora, BF16)], name="q_lat_fwd")[0]
    qraw = _mm(ql, w_uq_p, "nn", name="q_up")
    q2 = _rowwise(_f_q2, [(qraw, H_MLA * 2 * DH, 0), (cs, 2 * DH, 0)], [g_qn, g_qr],
                  [(H_MLA * 2 * DH, BF16)], name="q2_fwd")[0]
    att, lse = _mla_fwd(q2, k2, v2, tq=1024, tk=1024, name="mla_fwd")
    cb = (mla_w + q_lora) // mem_w
    mix_b_rows = [(att, mla_w, 0), (pb, mla_w, 0), (pb, mem_w, cb), (pb, mem_w, cb + 1)]
    mixed_b = _rowwise(_f_mix, mix_b_rows, [mkv[1], g_mq[1], g_mk[1]], [(mla_w + mem_w, BF16)], name="b_mix_fwd")[0]
    y = _mm(mixed_b, w_b_out, "nn", add=x1, name="b_out")

    def loss_fn(yb, tb):
        err = yb - tb
        part = 0.5 * jnp.sum(jnp.sum(err * err, axis=-1, keepdims=True) * (1.0 / D))
        return err * (1.0 / D), jnp.full((1, DH), part, F32)

    dy, loss_part = _rowwise(loss_fn, [(y, D, 0), (tgt, D, 0)], [], [(D, F32)], [(1, DH)], name="loss")

    gr = {}
    d_mixed_b = _mm(dy, w_b_out, "nt", name="b_out_dx")
    gr["b_w_out"] = _mm(mixed_b, dy, "tn", out_dtype=BF16, name="b_out_dw")[None]
    d_att, d_gmla, d_qm_b, d_gm_b, d_mkv1, d_gq1, d_gk1 = _rowwise_bwd(
        _f_mix, mix_b_rows, [mkv[1], g_mq[1], g_mk[1]], [(d_mixed_b, mla_w + mem_w, 0)],
        [0, 1, 2, 3], [0, 1, 2], name="b_mix_bwd")
    dq2, dk2, dv2 = _mla_bwd(q2, k2, v2, att, d_att, lse, tq=512, tk=512, name="mla_bwd")
    d_qraw, d_gqn, d_gqr = _rowwise_bwd(
        _f_q2, [(qraw, H_MLA * 2 * DH, 0), (cs, 2 * DH, 0)], [g_qn, g_qr], [(dq2, H_MLA * 2 * DH, 0)],
        [0], [0, 1], name="q2_bwd")
    d_ql = _mm(d_qraw, w_uq_p, "nt", name="q_up_dx")
    d_wuq = _mm(ql, d_qraw, "tn", out_dtype=BF16, name="q_up_dw")
    gr["b_w_uq"] = d_wuq.reshape(q_lora, H_MLA, 2 * DH)[:, :, :MLA_QK].reshape(1, q_lora, H_MLA * MLA_QK)
    d_qlat, d_gql = _rowwise_bwd(_f_norm, [(pb, q_lora, mla_w // q_lora)], [g_ql], [(d_ql, q_lora, 0)],
                                 [0], [0], name="q_lat_bwd")
    d_pb = jnp.concatenate([d_gmla, d_qlat, d_qm_b, d_gm_b], axis=1)
    d_hb = _mm(d_pb, w_b_in, "nt", name="b_in_dx")
    d_wbin = _mm(hb, d_pb, "tn", out_dtype=BF16, name="b_in_dw")
    gr["b_w_in"] = jnp.concatenate([d_wbin[:, mla_w:mla_w + q_lora], d_wbin[:, :mla_w], d_wbin[:, mla_w + q_lora:]],
                                   axis=1)[None]
    d_kvu, d_kr, d_gkn = _rowwise_bwd(
        _f_kv2, [(kvu, H_MLA * 2 * DH, 0), (kr, DH, 0)], [g_kn], [(dk2, H_MLA * 2 * DH, 0), (dv2, mla_w, 0)],
        [0, 1], [0], name="kv2_bwd")
    d_cn = _mm(d_kvu, w_ukv_f, "nt", name="kv_up_dx")
    gr["w_ukv"] = _mm(cn, d_kvu, "tn", out_dtype=BF16, name="kv_up_dw")
    d_ckr, d_gc, d_gkr = _rowwise_bwd(
        _f_kv1, [(ckr, kv_lora + DH, 0), (cs, 2 * DH, 0)], [g_c, g_kr], [(d_cn, kv_lora, 0), (d_kr, DH, 0)],
        [0], [0, 1], name="kv1_bwd")
    d_hk = _mm(d_ckr, w_dkv_p, "nt", name="kv_down_dx")
    gr["w_dkv"] = _mm(hk, d_ckr, "tn", out_dtype=BF16, name="kv_down_dw")[:, :kv_lora + ROPE]
    d_x1, d_gkv, d_gb = _rowwise_bwd(_f_norm2, [(x1, D, 0)], [g_kv, g_b], [(d_hk, D, 0), (d_hb, D, 0)],
                                     [0], [0, 1], add=(dy, D, 0), name="b_norm_bwd")
    d_mixed_a = _mm(d_x1, w_a_out, "nt", name="a_out_dx")
    gr["a_w_out"] = _mm(mixed_a, d_x1, "tn", out_dtype=BF16, name="a_out_dw")[None]
    d_sb, d_gsb, d_qm_a, d_gm_a, d_mkv0, d_gq0, d_gk0 = _rowwise_bwd(
        _f_mix, mix_a_rows, [mkv[0], g_mq[0], g_mk[0]], [(d_mixed_a, sb_w + mem_w, 0)],
        [0, 1, 2, 3], [0, 1, 2], name="a_mix_bwd")
    d_wmem, d_mnorm = [], []
    for l, d_mkv in enumerate((d_mkv0, d_mkv1)):
        d_mn = _mm(d_mkv, w_mem[l], "nt", name=f"mem_kv_dx{l}")
        d_wmem.append(_mm(mn[l], d_mkv, "tn", out_dtype=BF16, name=f"mem_kv_dw{l}"))
        d_mnorm.append(_rowwise_bwd(_f_norm, [(ms, D, 0)], [row2(mem_norm[l])], [(d_mn, D, 0)], [], [0],
                                    name=f"mem_norm_bwd{l}")[0])
    gr["w_mem_kv"] = jnp.stack(d_wmem)

    mid = tuple(n for n, _ in SHARDED[2:])
    send_mid = _stack_rows([_to_rows(_split8(gr[n], ax), 16) for n, ax in SHARDED[2:]])
    dq, dk, dv, recv_mid = _sb_bwd(qkv, sb, d_sb, tq=512, tk=256, name="sb_bwd", ride=("a2a", send_mid))
    d_pa = jnp.concatenate([dq, dk, dv, d_gsb, d_qm_a, d_gm_a], axis=1)
    send_ain = _mm(h0, d_pa, "tn", out_dtype=BF16, out_split=N_DEV, name="a_in_dw")
    d_h0, recv_ain = _mm(d_pa, w_a_in, "nt", name="a_in_dx", ride=("a2a", send_ain.reshape(N_DEV, -1, LANES)))
    grad_x, d_ga = _rowwise_bwd(_f_norm, [(xs, D, 0)], [g_a], [(d_h0, D, 0)], [0], [0], add=(d_x1, D, 0),
                                name="a_norm_bwd")
    gr["mem_norm"] = jnp.concatenate(d_mnorm, axis=0)
    gr["g_mem_q"] = jnp.concatenate([d_gq0, d_gq1], axis=0)
    gr["g_mem_k"] = jnp.concatenate([d_gk0, d_gk1], axis=0)
    gr["kv_norm"], gr["b_norm"], gr["g_ckv"], gr["g_k_nope"] = d_gkv, d_gb, d_gc, d_gkn
    gr["g_k_rope"], gr["b_g_q_rope"] = d_gkr[:, :ROPE], d_gqr[:, :ROPE]
    gr["b_g_q_lat"], gr["b_g_q_nope"] = d_gql, d_gqn
    last = ("a_norm",) + SMALL
    parts = [_to_rows(_split8(d_ga, 1), ROW_MULT)]
    parts += [jnp.broadcast_to(_to_rows(gr[n].reshape(-1), ROW_MULT)[None], (N_DEV, _rows_of(gr[n].size, ROW_MULT), LANES))
              for n in SMALL]
    parts.append(jnp.broadcast_to(_to_rows(loss_part[0, :1], ROW_MULT)[None], (N_DEV, ROW_MULT, LANES)))
    recv_last = _all_to_all(_stack_rows(parts), name="exchange_small")

    out = [{}, {}, {}, {}]
    zero = jnp.zeros((1,), F32)
    for tag, recv, names, mult, extra in (("mid", recv_mid, mid, 16, None), ("last", recv_last, last, ROW_MULT, zero)):
        slabs = _reduce_adamw(recv, *[_pack_local(t, names, mult, extra) for t in (wts, mom, var)],
                              name=f"reduce_adamw_{tag}")
        for o, slab in zip(out, slabs):
            o.update(_unpack_local(slab, names, shapes, mult)[0])
        if extra is not None:
            loss = slabs[0][_unpack_local(slabs[0], names, shapes, mult)[1], 0]
    slabs = _reduce_adamw(recv_ain, *[t["a_w_in"].reshape(-1, LANES) for t in (wts, mom, var)], name="reduce_adamw_ain")
    for o, slab in zip(out, slabs):
        o["a_w_in"] = slab.reshape(shapes["a_w_in"])
    return (loss, grad_x[None], *[o[n] for o in out for n in WEIGHTS])
```

```python
import functools

import jax
import jax.numpy as jnp
from jax import lax
from jax.experimental import pallas as pl
from jax.experimental.pallas import tpu as pltpu

F32, BF16 = jnp.float32, jnp.bfloat16

N_DEV = 8
DH = 128
H_SB, H_MEM, H_MLA = 12, 4, 12
ROPE = 64
MLA_QK = DH + ROPE
EPS = 1e-6
ROPE_THETA = 10000.0
ADAM_LR, ADAM_B1, ADAM_B2, ADAM_EPS, ADAM_WD, ADAM_STEP = 0.001, 0.9, 0.999, 1e-08, 0.01, 10

LANES = 1024
VMEM_LIMIT = 48 * 1024 * 1024
VMEM_LIMIT_BIG = 56 * 1024 * 1024

NN = (((1,), (0,)), ((), ()))
NT = (((1,), (1,)), ((), ()))
TN = (((0,), (0,)), ((), ()))
_DIMS = {"nn": NN, "nt": NT, "tn": TN}


def _dot16(a, b, dims):
    return lax.dot_general(a.astype(BF16), b.astype(BF16), _DIMS[dims], preferred_element_type=F32)


@functools.partial(jax.custom_vjp, nondiff_argnums=(2,))
def _bdot(a, b, dims):
    return _dot16(a, b, dims)


def _bdot_fwd(a, b, dims):
    return _dot16(a, b, dims), (a, b)


def _bdot_bwd(dims, res, g):
    a, b = res
    if dims == "nn":
        return _dot16(g, b, "nt"), _dot16(a, g, "tn")
    return _dot16(g, b, "nn"), _dot16(g, a, "tn")


_bdot.defvjp(_bdot_fwd, _bdot_bwd)


def _tile(n, pref):
    if n <= pref:
        return n
    t = (pref // 128) * 128
    while n % t:
        t -= 128
    return t


def _mesh_pos():
    return lax.axis_index("x"), lax.axis_index("y"), lax.axis_index("c")


def _ride_shape(kind, src):
    return jax.ShapeDtypeStruct((N_DEV, *src.shape) if kind == "gather" else src.shape, src.dtype)


def _ride_scratch():
    return [pltpu.SemaphoreType.DMA((N_DEV - 1,)), pltpu.SemaphoreType.DMA((N_DEV - 1,)), pltpu.SemaphoreType.DMA]


def _ride(kind, phase, s_ref, r_ref, send_sems, recv_sems, local_sem):
    x, y, c = _mesh_pos()
    me = 4 * x + 2 * y + c
    src = (lambda lin: s_ref) if kind == "gather" else (lambda lin: s_ref.at[lin])
    local = pltpu.make_async_copy(src(me), r_ref.at[me], local_sem)
    if phase == "start":
        local.start()
    for k in range(1, N_DEV):
        p = (1 - x if k & 4 else x, 1 - y if k & 2 else y, 1 - c if k & 1 else c)
        lin = 4 * p[0] + 2 * p[1] + p[2]
        cp = pltpu.make_async_remote_copy(
            src_ref=src(lin), dst_ref=r_ref.at[me] if phase == "start" else r_ref.at[lin],
            send_sem=send_sems.at[k - 1], recv_sem=recv_sems.at[k - 1],
            device_id=p, device_id_type=pl.DeviceIdType.MESH)
        if phase == "start":
            cp.start()
        else:
            cp.wait_recv()
            cp.wait_send()
    if phase == "wait":
        local.wait()


def _mm(a, b, dims, *, name, out_dtype=F32, add=None, ride=None, out_split=None, tm=1024, tn=1024, tk=2048):
    if dims == "tn":
        (K, M), (_, N) = a.shape, b.shape
    elif dims == "nt":
        (M, K), (N, _) = a.shape, b.shape
    else:
        (M, K), (_, N) = a.shape, b.shape
    tm, tk = _tile(M, tm), _tile(K, tk)
    tn = N // out_split if out_split else _tile(N, tn)
    ni, nj, nk = M // tm, N // tn, K // tk
    n_in = 2 + (add is not None) + (ride is not None)

    def body(*refs):
        a_ref, b_ref, o_ref = refs[0], refs[1], refs[n_in]
        acc_ref = refs[n_in + 1 + (ride is not None)]
        i, j, k = pl.program_id(0), pl.program_id(1), pl.program_id(2)
        if ride is not None:
            ride_refs = (refs[n_in - 1], refs[n_in + 1], *refs[-3:])

            @pl.when((i == 0) & (j == 0) & (k == 0))
            def _():
                _ride(ride[0], "start", *ride_refs)

        @pl.when(k == 0)
        def _():
            acc_ref[...] = jnp.zeros_like(acc_ref)

        acc_ref[...] += _dot16(a_ref[...], b_ref[...], dims)

        @pl.when(k == nk - 1)
        def _():
            r = acc_ref[...]
            if add is not None:
                r = r + refs[2][...]
            o_ref[...] = r.astype(o_ref.dtype)

        if ride is not None:
            @pl.when((i == ni - 1) & (j == nj - 1) & (k == nk - 1))
            def _():
                _ride(ride[0], "wait", *ride_refs)

    a_spec = pl.BlockSpec((tk, tm), lambda i, j, k: (k, i)) if dims == "tn" else pl.BlockSpec((tm, tk), lambda i, j, k: (i, k))
    b_spec = pl.BlockSpec((tn, tk), lambda i, j, k: (j, k)) if dims == "nt" else pl.BlockSpec((tk, tn), lambda i, j, k: (k, j))
    o_spec = pl.BlockSpec((tm, tn), lambda i, j, k: (i, j))
    in_specs, args = [a_spec, b_spec], [a, b]
    if add is not None:
        in_specs.append(o_spec)
        args.append(add)
    out_specs, out_shape, scratch = [o_spec], [jax.ShapeDtypeStruct((M, N), out_dtype)], [pltpu.VMEM((tm, tn), F32)]
    if out_split:
        out_specs = [pl.BlockSpec((None, tm, tn), lambda i, j, k: (j, i, 0))]
        out_shape = [jax.ShapeDtypeStruct((out_split, M, tn), out_dtype)]
    if ride is not None:
        in_specs.append(pl.BlockSpec(memory_space=pl.ANY))
        args.append(ride[1])
        out_specs.append(pl.BlockSpec(memory_space=pl.ANY))
        out_shape.append(_ride_shape(*ride))
        scratch += _ride_scratch()
    sem = ("arbitrary",) * 3 if ride is not None else ("parallel", "parallel", "arbitrary")
    res = pl.pallas_call(
        body, name=name, grid=(ni, nj, nk), in_specs=in_specs, out_specs=out_specs, out_shape=out_shape,
        scratch_shapes=scratch,
        compiler_params=pltpu.CompilerParams(dimension_semantics=sem, vmem_limit_bytes=VMEM_LIMIT),
    )(*args)
    return res[0] if ride is None else res


def _rowwise(fn, rows, consts, outs, accs=(), *, name, tm=256):
    S = rows[0][0].shape[0]
    tm = min(tm, S)
    nr, nc, no = len(rows), len(consts), len(outs)

    def body(*refs):
        res = fn(*[r[...] for r in refs[:nr + nc]])
        res = tuple(res) if isinstance(res, (tuple, list)) else (res,)
        orefs, arefs = refs[nr + nc:nr + nc + no], refs[nr + nc + no:]
        for r, v in zip(orefs, res[:no]):
            r[...] = v.astype(r.dtype)
        if arefs:
            @pl.when(pl.program_id(0) == 0)
            def _():
                for r in arefs:
                    r[...] = jnp.zeros_like(r)

            for r, v in zip(arefs, res[no:]):
                r[...] += v

    in_specs = [pl.BlockSpec((tm, w), lambda i, cb=cb: (i, cb)) for (_, w, cb) in rows]
    in_specs += [pl.BlockSpec(c.shape, lambda i: (0, 0)) for c in consts]
    out_specs = [pl.BlockSpec((tm, w), lambda i: (i, 0)) for (w, _) in outs]
    out_specs += [pl.BlockSpec(s, lambda i: (0, 0)) for s in accs]
    out_shape = [jax.ShapeDtypeStruct((S, w), dt) for (w, dt) in outs]
    out_shape += [jax.ShapeDtypeStruct(s, F32) for s in accs]
    res = pl.pallas_call(
        body, name=name, grid=(S // tm,), in_specs=in_specs, out_specs=out_specs, out_shape=out_shape,
        compiler_params=pltpu.CompilerParams(dimension_semantics=("arbitrary",), vmem_limit_bytes=VMEM_LIMIT),
    )(*[r[0] for r in rows], *consts)
    return res


def _rowwise_bwd(f, rows, consts, cts, row_grads, const_grads, *, name, add=None, out_dtypes=None, tm=256):
    nr, nc, nct = len(rows), len(consts), len(cts)
    all_rows = list(rows) + list(cts) + ([add] if add is not None else [])

    def fn(*args):
        nrow = len(all_rows)
        prim = [x.astype(F32) for x in args[:nr]] + [x.astype(F32) for x in args[nrow:]]
        ct = tuple(x.astype(F32) for x in args[nr:nr + nct])
        out, vjp = jax.vjp(f, *prim)
        gs = vjp(ct if isinstance(out, (tuple, list)) else ct[0])
        res = [gs[k] for k in row_grads]
        if add is not None:
            res[0] = res[0] + args[nrow - 1]
        return tuple(res) + tuple(gs[nr + k] for k in const_grads)

    out_dtypes = out_dtypes or [F32] * len(row_grads)
    outs = [(rows[k][1], dt) for k, dt in zip(row_grads, out_dtypes)]
    accs = [consts[k].shape for k in const_grads]
    return _rowwise(fn, all_rows, consts, outs, accs, name=name, tm=tm)


def _rms(x, g, n=None):
    ms = jnp.sum(x * x, axis=-1, keepdims=True) * (1.0 / (n or x.shape[-1]))
    return x * lax.rsqrt(ms + EPS) * g


def _sigmoid(x):
    return 1.0 / (1.0 + jnp.exp(-x))


def _swap_halves(x):
    r = lax.broadcasted_iota(jnp.int32, (DH, DH), 0)
    c = lax.broadcasted_iota(jnp.int32, (DH, DH), 1)
    half = ROPE // 2
    perm = jnp.where(((r < half) & (c == r + half)) | ((r >= half) & (r < ROPE) & (c == r - half)), 1.0, 0.0)
    return lax.dot_general(x, perm.astype(F32), NN, precision=lax.Precision.HIGHEST, preferred_element_type=F32)


def _rope128(x, g128, cs):
    y = _rms(x, g128, n=ROPE)
    return y * cs[:, :DH] + _swap_halves(y) * cs[:, DH:]


def _f_norm(x, g):
    return _rms(x, g)


def _f_norm2(x, g1, g2):
    xn = x * lax.rsqrt(jnp.mean(x * x, axis=-1, keepdims=True) + EPS)
    return xn * g1, xn * g2


def _f_kv1(ckr, cs, g_ckv, g_kr):
    w = g_ckv.shape[-1]
    return _rms(ckr[:, :w], g_ckv), _rope128(ckr[:, w:], g_kr, cs)


def _f_kv2(kv, kr, g_kn):
    ks, vs = [], []
    for h in range(H_MLA):
        ks += [_rms(kv[:, 2 * DH * h:2 * DH * h + DH], g_kn), kr]
        vs.append(kv[:, 2 * DH * h + DH:2 * DH * (h + 1)])
    return jnp.concatenate(ks, axis=1), jnp.concatenate(vs, axis=1)


def _f_q2(q, cs, g_n, g_r):
    out = []
    for h in range(H_MLA):
        out += [_rms(q[:, 2 * DH * h:2 * DH * h + DH], g_n), _rope128(q[:, 2 * DH * h + DH:2 * DH * (h + 1)], g_r, cs)]
    return jnp.concatenate(out, axis=1)


def _f_mix(att, g_att, q_m, g_m, mkv, g_q, g_k):
    mem_w = H_MEM * DH
    heads = []
    for h in range(H_MEM):
        kh = _rms(mkv[:, h * DH:(h + 1) * DH], g_k)
        vh = mkv[:, mem_w + h * DH:mem_w + (h + 1) * DH]
        qh = _rms(q_m[:, h * DH:(h + 1) * DH], g_q)
        s = _bdot(qh, kh, "nt") * (DH ** -0.5)
        p = jnp.exp(s - lax.stop_gradient(jnp.max(s, axis=-1, keepdims=True)))
        p = p / jnp.sum(p, axis=-1, keepdims=True)
        heads.append(_bdot(p, vh, "nn"))
    mo = jnp.concatenate(heads, axis=1)
    return jnp.concatenate([att * (g_att * _sigmoid(g_att)), mo * (g_m * _sigmoid(g_m))], axis=1)


def _split_dot(x, u):
    hi = x.astype(BF16)
    lo = (x - hi.astype(F32)).astype(BF16)
    return (lax.dot_general(hi, u, NN, preferred_element_type=F32)
            + lax.dot_general(lo, u, NN, preferred_element_type=F32))


def _tri(t):
    r = lax.broadcasted_iota(jnp.int32, (t, t), 0)
    c = lax.broadcasted_iota(jnp.int32, (t, t), 1)
    return r, c


def _strict_lower(t):
    r, c = _tri(t)
    return (r > c).astype(BF16)


def _rows_ahead(tq, tk):
    return lax.broadcasted_iota(jnp.int32, (tq, tk), 0) - lax.broadcasted_iota(jnp.int32, (tq, tk), 1)


EXP_UNDERFLOW = -110.0


def _log_one_minus_beta(zr, scale):
    zs, nz = zr * scale, zr * (-scale)
    return zs, jnp.minimum(nz, 0.0) - jnp.log(1.0 + jnp.exp(jnp.minimum(zs, nz)))


def _sb_fwd(qkv, *, tq, tk, name, ride=None):
    S = qkv.shape[0]
    tq, tk = min(tq, S), min(tk, tq, S)
    nd = tq // tk
    H = H_SB
    scale = DH ** -0.5

    nq = S // tq

    def body(q_ref, k_ref, v_ref, *rest):
        o_ref = rest[1] if ride is not None else rest[0]
        h, i = pl.program_id(0), pl.program_id(1)
        if ride is not None:
            ride_refs = (rest[0], *rest[2:])

            @pl.when((h == 0) & (i == 0))
            def _():
                _ride(ride[0], "start", *ride_refs)

        q = q_ref[...]
        u = _strict_lower(tk)
        ahead = _rows_ahead(tq, tk)

        def block(j, acc, cb, keep):
            off = pl.multiple_of(j * tk, tk)
            k = k_ref[pl.ds(off, tk), :]
            v = v_ref[pl.ds(off, tk), :]
            z, l = _log_one_minus_beta(lax.dot_general(q, k, NT, preferred_element_type=F32), scale)
            if keep is not None:
                l = jnp.where(keep, l, 0.0)
            a = jnp.exp((z + l) + (_split_dot(l, u) + cb))
            if keep is not None:
                a = jnp.where(keep, a, 0.0)
            acc = acc + lax.dot_general(a.astype(BF16), v, NN, preferred_element_type=F32)
            return acc, cb + jnp.sum(l, axis=1, keepdims=True)

        carry = (jnp.zeros((tq, DH), F32), jnp.zeros((tq, 1), F32))
        for t in reversed(range(nd)):
            carry = block(i * nd + t, *carry, ahead > t * tk)
        _, acc, _ = lax.while_loop(
            lambda st: (st[0] < i * nd) & (jnp.max(st[2]) > EXP_UNDERFLOW),
            lambda st: (st[0] + 1, *block(i * nd - 1 - st[0], st[1], st[2], None)), (jnp.int32(0), *carry))
        o_ref[...] = acc

        if ride is not None:
            @pl.when((h == H - 1) & (i == nq - 1))
            def _():
                _ride(ride[0], "wait", *ride_refs)

    in_specs = [pl.BlockSpec((tq, DH), lambda h, i: (i, h)),
                pl.BlockSpec((S, DH), lambda h, i: (0, H + h)),
                pl.BlockSpec((S, DH), lambda h, i: (0, 2 * H + h))]
    out_specs = [pl.BlockSpec((tq, DH), lambda h, i: (i, h))]
    out_shape = [jax.ShapeDtypeStruct((S, H * DH), F32)]
    args, scratch = [qkv, qkv, qkv], []
    if ride is not None:
        in_specs.append(pl.BlockSpec(memory_space=pl.ANY))
        args.append(ride[1])
        out_specs.append(pl.BlockSpec(memory_space=pl.ANY))
        out_shape.append(_ride_shape(*ride))
        scratch = _ride_scratch()
    res = pl.pallas_call(
        body, name=name, grid=(H, nq), in_specs=in_specs, out_specs=out_specs, out_shape=out_shape,
        scratch_shapes=scratch,
        compiler_params=pltpu.CompilerParams(dimension_semantics=("arbitrary", "arbitrary"), vmem_limit_bytes=VMEM_LIMIT),
    )(*args)
    return res[0] if ride is None else res


def _sb_bwd(qkv, o, do, *, tq, tk, name, ride=None):
    S = qkv.shape[0]
    tq, tk = min(tq, S), min(tk, tq, S)
    nd = tq // tk
    H = H_SB
    scale = DH ** -0.5

    nq = S // tq

    def body(q_ref, k_ref, v_ref, o_ref, do_ref, *rest):
        if ride is not None:
            dq_ref, dk_out, dv_out, dk_ref, dv_ref = rest[1], rest[2], rest[3], rest[5], rest[6]
            ride_refs = (rest[0], rest[4], *rest[7:])
        else:
            dq_ref, dk_out, dv_out, dk_ref, dv_ref = rest
        h, i = pl.program_id(0), pl.program_id(1)
        if ride is not None:
            @pl.when((h == 0) & (i == 0))
            def _():
                _ride(ride[0], "start", *ride_refs)

        @pl.when(i == 0)
        def _():
            dk_ref[...] = jnp.zeros_like(dk_ref)
            dv_ref[...] = jnp.zeros_like(dv_ref)

        q = q_ref[...]
        do = do_ref[...]
        do16 = do.astype(BF16)
        dsum = jnp.sum(do16.astype(F32) * o_ref[...], axis=1, keepdims=True)
        u = _strict_lower(tk)
        ahead = _rows_ahead(tq, tk)

        def block(j, dq, cb, ce, keep):
            off = pl.multiple_of(j * tk, tk)
            k = k_ref[pl.ds(off, tk), :]
            v = v_ref[pl.ds(off, tk), :]
            z, l = _log_one_minus_beta(lax.dot_general(q, k, NT, preferred_element_type=F32), scale)
            if keep is not None:
                l = jnp.where(keep, l, 0.0)
            log_beta = z + l
            a = jnp.exp(log_beta + (_split_dot(l, u) + cb))
            if keep is not None:
                a = jnp.where(keep, a, 0.0)
            a16 = a.astype(BF16)
            e = a16.astype(F32) * lax.dot_general(do16, v, NT, preferred_element_type=F32)
            left = dsum - (ce + _split_dot(e, u) + e)
            dz = (e - jnp.exp(log_beta) * (e + left)) * scale
            if keep is not None:
                dz = jnp.where(keep, dz, 0.0)
            dz = dz.astype(BF16)
            dq = dq + lax.dot_general(dz, k, NN, preferred_element_type=F32)
            dk_ref[pl.ds(off, tk), :] += lax.dot_general(dz, q, TN, preferred_element_type=F32)
            dv_ref[pl.ds(off, tk), :] += lax.dot_general(a16, do16, TN, preferred_element_type=F32)
            return dq, cb + jnp.sum(l, axis=1, keepdims=True), ce + jnp.sum(e, axis=1, keepdims=True)

        zero = jnp.zeros((tq, 1), F32)
        carry = (jnp.zeros((tq, DH), F32), zero, zero)
        for t in reversed(range(nd)):
            carry = block(i * nd + t, *carry, ahead > t * tk)
        _, dq, _, _ = lax.while_loop(
            lambda st: (st[0] < i * nd) & (jnp.max(st[2]) > EXP_UNDERFLOW),
            lambda st: (st[0] + 1, *block(i * nd - 1 - st[0], st[1], st[2], st[3], None)), (jnp.int32(0), *carry))
        dq_ref[...] = dq.astype(dq_ref.dtype)

        @pl.when(i == nq - 1)
        def _():
            dk_out[...] = dk_ref[...].astype(dk_out.dtype)
            dv_out[...] = dv_ref[...].astype(dv_out.dtype)

        if ride is not None:
            @pl.when((h == H - 1) & (i == nq - 1))
            def _():
                _ride(ride[0], "wait", *ride_refs)

    blk = pl.BlockSpec((tq, DH), lambda h, i: (i, h))
    whole = pl.BlockSpec((S, DH), lambda h, i: (0, h))
    shp = jax.ShapeDtypeStruct((S, H * DH), BF16)
    in_specs = [blk, pl.BlockSpec((S, DH), lambda h, i: (0, H + h)), pl.BlockSpec((S, DH), lambda h, i: (0, 2 * H + h)),
                blk, blk]
    out_specs, out_shape = [blk, whole, whole], [shp, shp, shp]
    args, scratch = [qkv, qkv, qkv, o, do], [pltpu.VMEM((S, DH), F32), pltpu.VMEM((S, DH), F32)]
    if ride is not None:
        in_specs.append(pl.BlockSpec(memory_space=pl.ANY))
        args.append(ride[1])
        out_specs.append(pl.BlockSpec(memory_space=pl.ANY))
        out_shape.append(_ride_shape(*ride))
        scratch += _ride_scratch()
    return pl.pallas_call(
        body, name=name, grid=(H, nq), in_specs=in_specs, out_specs=out_specs, out_shape=out_shape,
        scratch_shapes=scratch,
        compiler_params=pltpu.CompilerParams(dimension_semantics=("arbitrary", "arbitrary"), vmem_limit_bytes=VMEM_LIMIT),
    )(*args)


def _mla_fwd(q, k, v, *, tq, tk, name):
    S = q.shape[0]
    tq, tk = min(tq, S), min(tk, tq, S)
    nd = tq // tk
    H = H_MLA
    scale = MLA_QK ** -0.5

    def body(q_ref, k_ref, v_ref, o_ref, lse_ref):
        i = pl.program_id(1)
        qb = q_ref[...]
        ahead = _rows_ahead(tq, tk)

        def block(j, m, den, acc, keep):
            off = pl.multiple_of(j * tk, tk)
            kb = k_ref[pl.ds(off, tk), :]
            vb = v_ref[pl.ds(off, tk), :]
            s = lax.dot_general(qb, kb, NT, preferred_element_type=F32) * scale
            if keep is not None:
                s = jnp.where(keep, s, -1e30)
            m_new = jnp.maximum(m, jnp.max(s, axis=1, keepdims=True))
            p = jnp.exp(s - m_new)
            alpha = jnp.exp(m - m_new)
            den = alpha * den + jnp.sum(p, axis=1, keepdims=True)
            acc = alpha * acc + lax.dot_general(p.astype(BF16), vb, NN, preferred_element_type=F32)
            return m_new, den, acc

        init = (jnp.full((tq, 1), -1e30, F32), jnp.zeros((tq, 1), F32), jnp.zeros((tq, DH), F32))
        carry = lax.fori_loop(0, i * nd, lambda j, carry: block(j, *carry, None), init)
        for t in range(nd):
            carry = block(i * nd + t, *carry, ahead >= t * tk)
        m, den, acc = carry
        o_ref[...] = acc / den
        lse_ref[0] = m + jnp.log(den)

    return pl.pallas_call(
        body, name=name, grid=(H, S // tq),
        in_specs=[pl.BlockSpec((tq, 2 * DH), lambda h, i: (i, h)),
                  pl.BlockSpec((S, 2 * DH), lambda h, i: (0, h)),
                  pl.BlockSpec((S, DH), lambda h, i: (0, h))],
        out_specs=[pl.BlockSpec((tq, DH), lambda h, i: (i, h)), pl.BlockSpec((1, tq, 1), lambda h, i: (h, i, 0))],
        out_shape=[jax.ShapeDtypeStruct((S, H * DH), F32), jax.ShapeDtypeStruct((H, S, 1), F32)],
        compiler_params=pltpu.CompilerParams(dimension_semantics=("arbitrary", "arbitrary"), vmem_limit_bytes=VMEM_LIMIT),
    )(q, k, v)


def _mla_bwd(q, k, v, o, do, lse, *, tq, tk, name):
    S = q.shape[0]
    tq, tk = min(tq, S), min(tk, tq, S)
    nd = tq // tk
    H = H_MLA
    scale = MLA_QK ** -0.5

    def body(q_ref, k_ref, v_ref, o_ref, do_ref, lse_ref, dq_ref, dk_ref, dv_ref):
        i = pl.program_id(1)

        @pl.when(i == 0)
        def _():
            dk_ref[...] = jnp.zeros_like(dk_ref)
            dv_ref[...] = jnp.zeros_like(dv_ref)

        qb = q_ref[...]
        do = do_ref[...]
        do16 = do.astype(BF16)
        dsum = jnp.sum(do * o_ref[...], axis=1, keepdims=True)
        lse = lse_ref[0]
        ahead = _rows_ahead(tq, tk)

        def block(j, dq, keep):
            off = pl.multiple_of(j * tk, tk)
            kb = k_ref[pl.ds(off, tk), :]
            vb = v_ref[pl.ds(off, tk), :]
            s = lax.dot_general(qb, kb, NT, preferred_element_type=F32) * scale
            if keep is not None:
                s = jnp.where(keep, s, -1e30)
            p = jnp.exp(s - lse)
            dp = lax.dot_general(do16, vb, NT, preferred_element_type=F32)
            ds = (p * (dp - dsum) * scale).astype(BF16)
            dk_ref[pl.ds(off, tk), :] += lax.dot_general(ds, qb, TN, preferred_element_type=F32)
            dv_ref[pl.ds(off, tk), :] += lax.dot_general(p.astype(BF16), do16, TN, preferred_element_type=F32)
            return dq + lax.dot_general(ds, kb, NN, preferred_element_type=F32)

        dq = lax.fori_loop(0, i * nd, lambda j, dq: block(j, dq, None), jnp.zeros((tq, 2 * DH), F32))
        for t in range(nd):
            dq = block(i * nd + t, dq, ahead >= t * tk)
        dq_ref[...] = dq

    blk = pl.BlockSpec((tq, DH), lambda h, i: (i, h))
    blk2 = pl.BlockSpec((tq, 2 * DH), lambda h, i: (i, h))
    return pl.pallas_call(
        body, name=name, grid=(H, S // tq),
        in_specs=[blk2, pl.BlockSpec((S, 2 * DH), lambda h, i: (0, h)), pl.BlockSpec((S, DH), lambda h, i: (0, h)),
                  blk, blk, pl.BlockSpec((1, tq, 1), lambda h, i: (h, i, 0))],
        out_specs=[blk2, pl.BlockSpec((S, 2 * DH), lambda h, i: (0, h)), pl.BlockSpec((S, DH), lambda h, i: (0, h))],
        out_shape=[jax.ShapeDtypeStruct((S, H * 2 * DH), F32), jax.ShapeDtypeStruct((S, H * 2 * DH), F32),
                   jax.ShapeDtypeStruct((S, H * DH), F32)],
        compiler_params=pltpu.CompilerParams(dimension_semantics=("arbitrary", "arbitrary"), vmem_limit_bytes=VMEM_LIMIT_BIG),
    )(q, k, v, o, do, lse)


def _all_gather(block, *, name):
    R, C = block.shape

    def body(x_ref, out_ref, send_sems, recv_sems, local_sem):
        x, y, c = _mesh_pos()
        me, sibling = (x, y, c), (x, y, 1 - c)
        chips = [(1 - x, y), (x, 1 - y), (1 - x, 1 - y)]

        def slot(px, py, pc):
            return out_ref.at[4 * px + 2 * py + pc]

        def copy(k, blk, to, src=None):
            return pltpu.make_async_remote_copy(
                src_ref=slot(*blk) if src is None else src, dst_ref=slot(*blk),
                send_sem=send_sems.at[k], recv_sem=recv_sems.at[k],
                device_id=to, device_id_type=pl.DeviceIdType.MESH)

        mine = pltpu.make_async_copy(x_ref, slot(*me), local_sem)
        mine.start()
        first = [copy(0, me, sibling, src=x_ref)]
        first += [copy(1 + j, me, (*chip, c), src=x_ref) for j, chip in enumerate(chips)]
        for cp in first:
            cp.start()
        passed = [copy(4 + j, (*chip, c), sibling) for j, chip in enumerate(chips)]
        for j, chip in enumerate(chips):
            copy(1 + j, (*chip, c), me).wait_recv()
            passed[j].start()
        copy(0, sibling, me).wait_recv()
        for j, chip in enumerate(chips):
            copy(4 + j, (*chip, 1 - c), me).wait_recv()
        for cp in first + passed:
            cp.wait_send()
        mine.wait()

    return pl.pallas_call(
        body, name=name,
        out_shape=jax.ShapeDtypeStruct((N_DEV, R, C), block.dtype),
        in_specs=[pl.BlockSpec(memory_space=pl.ANY)], out_specs=pl.BlockSpec(memory_space=pl.ANY),
        scratch_shapes=[pltpu.SemaphoreType.DMA((7,)), pltpu.SemaphoreType.DMA((7,)), pltpu.SemaphoreType.DMA],
    )(block)


def _all_to_all(send, *, name):
    def body(*refs):
        _ride("a2a", "start", *refs)
        _ride("a2a", "wait", *refs)

    return pl.pallas_call(
        body, name=name, out_shape=_ride_shape("a2a", send),
        in_specs=[pl.BlockSpec(memory_space=pl.ANY)], out_specs=pl.BlockSpec(memory_space=pl.ANY),
        scratch_shapes=_ride_scratch(),
    )(send)


def _reduce_adamw(recv, w, m, v, *, name):
    R, C = w.shape
    tr = next(t for t in (128, 64, 32, 16, 8) if R % t == 0)

    def body(g_ref, w_ref, m_ref, v_ref, og_ref, od_ref, om_ref, ov_ref):
        g = g_ref[0].astype(F32)
        for s in range(1, N_DEV):
            g = g + g_ref[s].astype(F32)
        mn = ADAM_B1 * m_ref[...] + (1.0 - ADAM_B1) * g
        vn = ADAM_B2 * v_ref[...] + (1.0 - ADAM_B2) * jnp.square(g)
        m_hat = mn / (1.0 - ADAM_B1 ** ADAM_STEP)
        v_hat = vn / (1.0 - ADAM_B2 ** ADAM_STEP)
        og_ref[...] = g
        od_ref[...] = -ADAM_LR * (m_hat / (jnp.sqrt(v_hat) + ADAM_EPS) + ADAM_WD * w_ref[...])
        om_ref[...] = mn
        ov_ref[...] = vn

    blk = pl.BlockSpec((tr, C), lambda i: (i, 0))
    shp = jax.ShapeDtypeStruct((R, C), F32)
    return pl.pallas_call(
        body, name=name, grid=(R // tr,),
        in_specs=[pl.BlockSpec((N_DEV, tr, C), lambda i: (0, i, 0)), blk, blk, blk],
        out_specs=[blk, blk, blk, blk], out_shape=[shp, shp, shp, shp],
        compiler_params=pltpu.CompilerParams(dimension_semantics=("parallel",), vmem_limit_bytes=VMEM_LIMIT),
    )(recv, w, m, v)


SHARDED = (("a_norm", 1), ("a_w_in", 2), ("a_w_out", 1), ("w_dkv", 0), ("w_ukv", 1), ("b_w_in", 2),
           ("b_w_uq", 2), ("b_w_out", 1), ("w_mem_kv", 1))
SMALL = ("kv_norm", "g_ckv", "g_k_nope", "g_k_rope", "b_norm", "b_g_q_lat", "b_g_q_nope", "b_g_q_rope",
         "mem_norm", "g_mem_q", "g_mem_k")
WEIGHTS = ("a_norm", "a_w_in", "a_w_out", "kv_norm", "w_dkv", "g_ckv", "w_ukv", "g_k_nope", "g_k_rope", "b_norm",
           "b_w_in", "b_g_q_lat", "b_w_uq", "b_g_q_nope", "b_g_q_rope", "b_w_out", "mem_norm", "w_mem_kv",
           "g_mem_q", "g_mem_k")
ROW_MULT = 8
ROW_BLOCK = 128


def _rows_of(n, mult):
    rows = -(-n // LANES)
    return -(-rows // mult) * mult


def _to_rows(flat, mult):
    n = flat.shape[-1]
    rows = _rows_of(n, mult)
    pad = [(0, 0)] * (flat.ndim - 1) + [(0, rows * LANES - n)]
    return jnp.pad(flat, pad).reshape(*flat.shape[:-1], rows, LANES)


def _split8(full, axis):
    shp = full.shape
    t = full.reshape(*shp[:axis], N_DEV, shp[axis] // N_DEV, *shp[axis + 1:])
    return jnp.moveaxis(t, axis, 0).reshape(N_DEV, -1)


def _join8(rows, axis, shard_shape):
    t = rows.reshape(N_DEV, *shard_shape)
    t = jnp.moveaxis(t, 0, axis)
    return t.reshape(*shard_shape[:axis], N_DEV * shard_shape[axis], *shard_shape[axis + 1:])


def _stack_rows(parts, block=ROW_BLOCK):
    rows = sum(p.shape[-2] for p in parts)
    if rows % block:
        parts = list(parts) + [jnp.zeros((*parts[0].shape[:-2], -rows % block, LANES), parts[0].dtype)]
    return jnp.concatenate(parts, axis=-2)


def _pack_local(vals, names, mult, extra=None):
    parts = [_to_rows(vals[n].reshape(-1), mult) for n in names]
    if extra is not None:
        parts.append(_to_rows(extra.reshape(-1), mult))
    return _stack_rows(parts)


def _unpack_local(slab, names, shapes, mult):
    out, row = {}, 0
    for n in names:
        size = 1
        for d in shapes[n]:
            size *= d
        rows = _rows_of(size, mult)
        out[n] = slab[row:row + rows].reshape(-1)[:size].reshape(shapes[n])
        row += rows
    return out, row


def kernel(x, mem, positions, a_norm, a_w_in, a_w_out, kv_norm, w_dkv, g_ckv, w_ukv, g_k_nope, g_k_rope, b_norm, b_w_in, b_g_q_lat, b_w_uq, b_g_q_nope, b_g_q_rope, b_w_out, mem_norm, w_mem_kv, g_mem_q, g_mem_k, loss_target, m_a_norm, m_a_w_in, m_a_w_out, m_kv_norm, m_w_dkv, m_g_ckv, m_w_ukv, m_g_k_nope, m_g_k_rope, m_b_norm, m_b_w_in, m_b_g_q_lat, m_b_w_uq, m_b_g_q_nope, m_b_g_q_rope, m_b_w_out, m_mem_norm, m_w_mem_kv, m_g_mem_q, m_g_mem_k, v_a_norm, v_a_w_in, v_a_w_out, v_kv_norm, v_w_dkv, v_g_ckv, v_w_ukv, v_g_k_nope, v_g_k_rope, v_b_norm, v_b_w_in, v_b_g_q_lat, v_b_w_uq, v_b_g_q_nope, v_b_g_q_rope, v_b_w_out, v_mem_norm, v_w_mem_kv, v_g_mem_q, v_g_mem_k):
    wts = dict(a_norm=a_norm, a_w_in=a_w_in, a_w_out=a_w_out, kv_norm=kv_norm, w_dkv=w_dkv, g_ckv=g_ckv, w_ukv=w_ukv,
               g_k_nope=g_k_nope, g_k_rope=g_k_rope, b_norm=b_norm, b_w_in=b_w_in, b_g_q_lat=b_g_q_lat, b_w_uq=b_w_uq,
               b_g_q_nope=b_g_q_nope, b_g_q_rope=b_g_q_rope, b_w_out=b_w_out, mem_norm=mem_norm, w_mem_kv=w_mem_kv,
               g_mem_q=g_mem_q, g_mem_k=g_mem_k)
    mom = dict(a_norm=m_a_norm, a_w_in=m_a_w_in, a_w_out=m_a_w_out, kv_norm=m_kv_norm, w_dkv=m_w_dkv, g_ckv=m_g_ckv,
               w_ukv=m_w_ukv, g_k_nope=m_g_k_nope, g_k_rope=m_g_k_rope, b_norm=m_b_norm, b_w_in=m_b_w_in,
               b_g_q_lat=m_b_g_q_lat, b_w_uq=m_b_w_uq, b_g_q_nope=m_b_g_q_nope, b_g_q_rope=m_b_g_q_rope,
               b_w_out=m_b_w_out, mem_norm=m_mem_norm, w_mem_kv=m_w_mem_kv, g_mem_q=m_g_mem_q, g_mem_k=m_g_mem_k)
    var = dict(a_norm=v_a_norm, a_w_in=v_a_w_in, a_w_out=v_a_w_out, kv_norm=v_kv_norm, w_dkv=v_w_dkv, g_ckv=v_g_ckv,
               w_ukv=v_w_ukv, g_k_nope=v_g_k_nope, g_k_rope=v_g_k_rope, b_norm=v_b_norm, b_w_in=v_b_w_in,
               b_g_q_lat=v_b_g_q_lat, b_w_uq=v_b_w_uq, b_g_q_nope=v_b_g_q_nope, b_g_q_rope=v_b_g_q_rope,
               b_w_out=v_b_w_out, mem_norm=v_mem_norm, w_mem_kv=v_w_mem_kv, g_mem_q=v_g_mem_q, g_mem_k=v_g_mem_k)
    shapes = {n: wts[n].shape for n in WEIGHTS}
    S, D = x.shape[1], x.shape[2]
    xs, ms, tgt = x[0], mem[0], loss_target[0]
    sb_w, mem_w, mla_w = H_SB * DH, H_MEM * DH, H_MLA * DH
    q_lora, kv_lora = b_g_q_lat.shape[-1], g_ckv.shape[-1]

    def pieces_of(names):
        return [_to_rows(lax.bitcast_convert_type(a_norm.reshape(-1), BF16).reshape(-1), 16) if n == "a_norm"
                else _to_rows(wts[n].astype(BF16).reshape(-1), 16) for n in names]

    def unpack_gathered(gathered, names):
        full, row = {}, 0
        for n in names:
            size = wts[n].size * (2 if n == "a_norm" else 1)
            rows = _rows_of(size, 16)
            flat = gathered[:, row:row + rows].reshape(N_DEV, -1)[:, :size]
            if n == "a_norm":
                flat = lax.bitcast_convert_type(flat.reshape(N_DEV, -1, 2), F32)
            full[n] = _join8(flat, dict(SHARDED)[n], wts[n].shape)
            row += rows
        return full

    first, second, third = ("a_norm", "a_w_in"), ("a_w_out", "w_dkv", "w_ukv", "w_mem_kv"), ("b_w_in", "b_w_uq", "b_w_out")
    full = unpack_gathered(_all_gather(jnp.concatenate(pieces_of(first), axis=0), name="gather_first"), first)
    g_a, w_a_in = full["a_norm"], full["a_w_in"][0]

    row2 = lambda g: g.reshape(1, -1)
    h0 = _rowwise(_f_norm, [(xs, D, 0)], [g_a], [(D, BF16)], name="a_norm_fwd")[0]
    qkv, gathered = _mm(h0, w_a_in[:, :3 * sb_w], "nn", out_dtype=BF16, name="a_in_qkv",
                        ride=("gather", jnp.concatenate(pieces_of(second), axis=0)))
    full = unpack_gathered(gathered, second)
    pa = _mm(h0, w_a_in[:, 3 * sb_w:], "nn", name="a_in_rest")
    sb, gathered = _sb_fwd(qkv, tq=512, tk=256, name="sb_fwd",
                           ride=("gather", jnp.concatenate(pieces_of(third), axis=0)))
    full.update(unpack_gathered(gathered, third))
    w_a_out = full["a_w_out"][0]
    w_dkv_p = jnp.pad(full["w_dkv"], ((0, 0), (0, ROPE)))
    w_ukv_f = full["w_ukv"]
    wb = full["b_w_in"][0]
    w_b_in = jnp.concatenate([wb[:, q_lora:q_lora + mla_w], wb[:, :q_lora], wb[:, q_lora + mla_w:]], axis=1)
    w_uq_p = jnp.pad(full["b_w_uq"][0].reshape(q_lora, H_MLA, MLA_QK),
                     ((0, 0), (0, 0), (0, 2 * DH - MLA_QK))).reshape(q_lora, H_MLA * 2 * DH)
    w_b_out = full["b_w_out"][0]
    w_mem = full["w_mem_kv"]

    pad128 = lambda g: jnp.pad(g.reshape(1, -1), ((0, 0), (0, DH - ROPE)))
    g_kr, g_qr = pad128(g_k_rope), pad128(b_g_q_rope[0])
    g_kv, g_b, g_c, g_kn = row2(kv_norm), row2(b_norm[0]), row2(g_ckv), row2(g_k_nope)
    g_ql, g_qn = row2(b_g_q_lat[0]), row2(b_g_q_nope[0])

    inv_freq = jnp.power(ROPE_THETA, -jnp.arange(0, ROPE, 2, dtype=F32) / ROPE)
    ang = positions[0].astype(F32)[:, None] * inv_freq
    z64 = jnp.zeros((S, DH - ROPE), F32)
    cs = jnp.concatenate([jnp.cos(ang), jnp.cos(ang), z64, -jnp.sin(ang), jnp.sin(ang), z64], axis=1)

    mn, mkv = [], []
    for l in range(2):
        mn.append(_rowwise(_f_norm, [(ms, D, 0)], [row2(mem_norm[l])], [(D, BF16)], name=f"mem_norm{l}")[0])
        mkv.append(_mm(mn[l], w_mem[l], "nn", name=f"mem_kv{l}"))
    g_mq = [row2(g_mem_q[l]) for l in range(2)]
    g_mk = [row2(g_mem_k[l]) for l in range(2)]

    mix_a_rows = [(sb, sb_w, 0), (pa, sb_w, 0), (pa, mem_w, sb_w // mem_w), (pa, mem_w, sb_w // mem_w + 1)]
    mixed_a = _rowwise(_f_mix, mix_a_rows, [mkv[0], g_mq[0], g_mk[0]], [(sb_w + mem_w, BF16)], name="a_mix_fwd")[0]
    x1 = _mm(mixed_a, w_a_out, "nn", add=xs, name="a_out")

    hk, hb = _rowwise(_f_norm2, [(x1, D, 0)], [g_kv, g_b], [(D, BF16), (D, BF16)], name="b_norm_fwd")
    ckr = _mm(hk, w_dkv_p, "nn", name="kv_down")
    cn, kr = _rowwise(_f_kv1, [(ckr, kv_lora + DH, 0), (cs, 2 * DH, 0)], [g_c, g_kr],
                      [(kv_lora, BF16), (DH, F32)], name="kv1_fwd")
    kvu = _mm(cn, w_ukv_f, "nn", name="kv_up")
    k2, v2 = _rowwise(_f_kv2, [(kvu, H_MLA * 2 * DH, 0), (kr, DH, 0)], [g_kn],
                      [(H_MLA * 2 * DH, BF16), (mla_w, BF16)], name="kv2_fwd")
    pb = _mm(hb, w_b_in, "nn", name="b_in")
    ql = _rowwise(_f_norm, [(pb, q_lora, mla_w // q_lora)], [g_ql], [(q_lora, BF16)], name="q_lat_fwd")[0]
    qraw = _mm(ql, w_uq_p, "nn", name="q_up")
    q2 = _rowwise(_f_q2, [(qraw, H_MLA * 2 * DH, 0), (cs, 2 * DH, 0)], [g_qn, g_qr],
                  [(H_MLA * 2 * DH, BF16)], name="q2_fwd")[0]
    att, lse = _mla_fwd(q2, k2, v2, tq=1024, tk=1024, name="mla_fwd")
    cb = (mla_w + q_lora) // mem_w
    mix_b_rows = [(att, mla_w, 0), (pb, mla_w, 0), (pb, mem_w, cb), (pb, mem_w, cb + 1)]
    mixed_b = _rowwise(_f_mix, mix_b_rows, [mkv[1], g_mq[1], g_mk[1]], [(mla_w + mem_w, BF16)], name="b_mix_fwd")[0]
    y = _mm(mixed_b, w_b_out, "nn", add=x1, name="b_out")

    def loss_fn(yb, tb):
        err = yb - tb
        part = 0.5 * jnp.sum(jnp.sum(err * err, axis=-1, keepdims=True) * (1.0 / D))
        return err * (1.0 / D), jnp.full((1, DH), part, F32)

    dy, loss_part = _rowwise(loss_fn, [(y, D, 0), (tgt, D, 0)], [], [(D, F32)], [(1, DH)], name="loss")

    gr = {}
    d_mixed_b = _mm(dy, w_b_out, "nt", name="b_out_dx")
    gr["b_w_out"] = _mm(mixed_b, dy, "tn", out_dtype=BF16, name="b_out_dw")[None]
    d_att, d_gmla, d_qm_b, d_gm_b, d_mkv1, d_gq1, d_gk1 = _rowwise_bwd(
        _f_mix, mix_b_rows, [mkv[1], g_mq[1], g_mk[1]], [(d_mixed_b, mla_w + mem_w, 0)],
        [0, 1, 2, 3], [0, 1, 2], out_dtypes=[F32, BF16, BF16, BF16], name="b_mix_bwd")
    dq2, dk2, dv2 = _mla_bwd(q2, k2, v2, att, d_att, lse, tq=512, tk=512, name="mla_bwd")
    d_qraw, d_gqn, d_gqr = _rowwise_bwd(
        _f_q2, [(qraw, H_MLA * 2 * DH, 0), (cs, 2 * DH, 0)], [g_qn, g_qr], [(dq2, H_MLA * 2 * DH, 0)],
        [0], [0, 1], out_dtypes=[BF16], name="q2_bwd")
    d_ql = _mm(d_qraw, w_uq_p, "nt", name="q_up_dx")
    d_wuq = _mm(ql, d_qraw, "tn", out_dtype=BF16, name="q_up_dw")
    gr["b_w_uq"] = d_wuq.reshape(q_lora, H_MLA, 2 * DH)[:, :, :MLA_QK].reshape(1, q_lora, H_MLA * MLA_QK)
    d_qlat, d_gql = _rowwise_bwd(_f_norm, [(pb, q_lora, mla_w // q_lora)], [g_ql], [(d_ql, q_lora, 0)],
                                 [0], [0], out_dtypes=[BF16], name="q_lat_bwd")
    d_pb = jnp.concatenate([d_gmla, d_qlat, d_qm_b, d_gm_b], axis=1)
    d_hb = _mm(d_pb, w_b_in, "nt", name="b_in_dx")
    d_wbin = _mm(hb, d_pb, "tn", out_dtype=BF16, name="b_in_dw")
    gr["b_w_in"] = jnp.concatenate([d_wbin[:, mla_w:mla_w + q_lora], d_wbin[:, :mla_w], d_wbin[:, mla_w + q_lora:]],
                                   axis=1)[None]
    d_kvu, d_kr, d_gkn = _rowwise_bwd(
        _f_kv2, [(kvu, H_MLA * 2 * DH, 0), (kr, DH, 0)], [g_kn], [(dk2, H_MLA * 2 * DH, 0), (dv2, mla_w, 0)],
        [0, 1], [0], out_dtypes=[BF16, F32], name="kv2_bwd")
    d_cn = _mm(d_kvu, w_ukv_f, "nt", name="kv_up_dx")
    gr["w_ukv"] = _mm(cn, d_kvu, "tn", out_dtype=BF16, name="kv_up_dw")
    d_ckr, d_gc, d_gkr = _rowwise_bwd(
        _f_kv1, [(ckr, kv_lora + DH, 0), (cs, 2 * DH, 0)], [g_c, g_kr], [(d_cn, kv_lora, 0), (d_kr, DH, 0)],
        [0], [0, 1], out_dtypes=[BF16], name="kv1_bwd")
    d_hk = _mm(d_ckr, w_dkv_p, "nt", name="kv_down_dx")
    gr["w_dkv"] = _mm(hk, d_ckr, "tn", out_dtype=BF16, name="kv_down_dw")[:, :kv_lora + ROPE]
    d_x1, d_gkv, d_gb = _rowwise_bwd(_f_norm2, [(x1, D, 0)], [g_kv, g_b], [(d_hk, D, 0), (d_hb, D, 0)],
                                     [0], [0, 1], add=(dy, D, 0), name="b_norm_bwd")
    d_mixed_a = _mm(d_x1, w_a_out, "nt", name="a_out_dx")
    gr["a_w_out"] = _mm(mixed_a, d_x1, "tn", out_dtype=BF16, name="a_out_dw")[None]
    d_sb, d_gsb, d_qm_a, d_gm_a, d_mkv0, d_gq0, d_gk0 = _rowwise_bwd(
        _f_mix, mix_a_rows, [mkv[0], g_mq[0], g_mk[0]], [(d_mixed_a, sb_w + mem_w, 0)],
        [0, 1, 2, 3], [0, 1, 2], out_dtypes=[F32, BF16, BF16, BF16], name="a_mix_bwd")
    d_wmem, d_mnorm = [], []
    for l, d_mkv in enumerate((d_mkv0, d_mkv1)):
        d_mn = _mm(d_mkv, w_mem[l], "nt", name=f"mem_kv_dx{l}")
        d_wmem.append(_mm(mn[l], d_mkv, "tn", out_dtype=BF16, name=f"mem_kv_dw{l}"))
        d_mnorm.append(_rowwise_bwd(_f_norm, [(ms, D, 0)], [row2(mem_norm[l])], [(d_mn, D, 0)], [], [0],
                                    name=f"mem_norm_bwd{l}")[0])
    gr["w_mem_kv"] = jnp.stack(d_wmem)

    mid = tuple(n for n, _ in SHARDED[2:])
    send_mid = _stack_rows([_to_rows(_split8(gr[n], ax), 16) for n, ax in SHARDED[2:]])
    dq, dk, dv, recv_mid = _sb_bwd(qkv, sb, d_sb, tq=512, tk=256, name="sb_bwd", ride=("a2a", send_mid))
    d_pa = jnp.concatenate([dq, dk, dv, d_gsb, d_qm_a, d_gm_a], axis=1)
    send_ain = _mm(h0, d_pa, "tn", out_dtype=BF16, out_split=N_DEV, name="a_in_dw")
    d_h0, recv_ain = _mm(d_pa, w_a_in, "nt", name="a_in_dx", ride=("a2a", send_ain))
    grad_x, d_ga = _rowwise_bwd(_f_norm, [(xs, D, 0)], [g_a], [(d_h0, D, 0)], [0], [0], add=(d_x1, D, 0),
                                name="a_norm_bwd")
    gr["mem_norm"] = jnp.concatenate(d_mnorm, axis=0)
    gr["g_mem_q"] = jnp.concatenate([d_gq0, d_gq1], axis=0)
    gr["g_mem_k"] = jnp.concatenate([d_gk0, d_gk1], axis=0)
    gr["kv_norm"], gr["b_norm"], gr["g_ckv"], gr["g_k_nope"] = d_gkv, d_gb, d_gc, d_gkn
    gr["g_k_rope"], gr["b_g_q_rope"] = d_gkr[:, :ROPE], d_gqr[:, :ROPE]
    gr["b_g_q_lat"], gr["b_g_q_nope"] = d_gql, d_gqn
    last = ("a_norm",) + SMALL
    parts = [_to_rows(_split8(d_ga, 1), ROW_MULT)]
    parts += [jnp.broadcast_to(_to_rows(gr[n].reshape(-1), ROW_MULT)[None], (N_DEV, _rows_of(gr[n].size, ROW_MULT), LANES))
              for n in SMALL]
    parts.append(jnp.broadcast_to(_to_rows(loss_part[0, :1], ROW_MULT)[None], (N_DEV, ROW_MULT, LANES)))
    recv_last = _all_to_all(_stack_rows(parts), name="exchange_small")

    out = [{}, {}, {}, {}]
    zero = jnp.zeros((1,), F32)
    for tag, recv, names, mult, extra in (("mid", recv_mid, mid, 16, None), ("last", recv_last, last, ROW_MULT, zero)):
        slabs = _reduce_adamw(recv, *[_pack_local(t, names, mult, extra) for t in (wts, mom, var)],
                              name=f"reduce_adamw_{tag}")
        for o, slab in zip(out, slabs):
            o.update(_unpack_local(slab, names, shapes, mult)[0])
        if extra is not None:
            loss = slabs[0][_unpack_local(slabs[0], names, shapes, mult)[1], 0]
    slabs = _reduce_adamw(recv_ain, *[t["a_w_in"][0] for t in (wts, mom, var)], name="reduce_adamw_ain")
    for o, slab in zip(out, slabs):
        o["a_w_in"] = slab[None]
    return (loss, grad_x[None], *[o[n] for o in out for n in WEIGHTS])
```

```python
import functools

import jax
import jax.numpy as jnp
from jax import lax
from jax.experimental import pallas as pl
from jax.experimental.pallas import tpu as pltpu

F32, BF16 = jnp.float32, jnp.bfloat16

N_DEV = 8
DH = 128
H_SB, H_MEM, H_MLA = 12, 4, 12
ROPE = 64
MLA_QK = DH + ROPE
EPS = 1e-6
ROPE_THETA = 10000.0
ADAM_LR, ADAM_B1, ADAM_B2, ADAM_EPS, ADAM_WD, ADAM_STEP = 0.001, 0.9, 0.999, 1e-08, 0.01, 10

LANES = 1024
VMEM_LIMIT = 48 * 1024 * 1024
VMEM_LIMIT_BIG = 56 * 1024 * 1024

NN = (((1,), (0,)), ((), ()))
NT = (((1,), (1,)), ((), ()))
TN = (((0,), (0,)), ((), ()))
_DIMS = {"nn": NN, "nt": NT, "tn": TN}


def _dot16(a, b, dims):
    return lax.dot_general(a.astype(BF16), b.astype(BF16), _DIMS[dims], preferred_element_type=F32)


@functools.partial(jax.custom_vjp, nondiff_argnums=(2,))
def _bdot(a, b, dims):
    return _dot16(a, b, dims)


def _bdot_fwd(a, b, dims):
    return _dot16(a, b, dims), (a, b)


def _bdot_bwd(dims, res, g):
    a, b = res
    if dims == "nn":
        return _dot16(g, b, "nt"), _dot16(a, g, "tn")
    return _dot16(g, b, "nn"), _dot16(g, a, "tn")


_bdot.defvjp(_bdot_fwd, _bdot_bwd)


def _tile(n, pref):
    if n <= pref:
        return n
    t = (pref // 128) * 128
    while n % t:
        t -= 128
    return t


def _mesh_pos():
    return lax.axis_index("x"), lax.axis_index("y"), lax.axis_index("c")


def _ride_shape(kind, src):
    return jax.ShapeDtypeStruct((N_DEV, *src.shape) if kind == "gather" else src.shape, src.dtype)


def _ride_scratch():
    return [pltpu.SemaphoreType.DMA((N_DEV - 1,)), pltpu.SemaphoreType.DMA((N_DEV - 1,)), pltpu.SemaphoreType.DMA]


def _ride(kind, phase, s_ref, r_ref, send_sems, recv_sems, local_sem):
    x, y, c = _mesh_pos()
    me = 4 * x + 2 * y + c
    src = (lambda lin: s_ref) if kind == "gather" else (lambda lin: s_ref.at[lin])
    local = pltpu.make_async_copy(src(me), r_ref.at[me], local_sem)
    if phase == "start":
        local.start()
    for k in range(1, N_DEV):
        p = (1 - x if k & 4 else x, 1 - y if k & 2 else y, 1 - c if k & 1 else c)
        lin = 4 * p[0] + 2 * p[1] + p[2]
        cp = pltpu.make_async_remote_copy(
            src_ref=src(lin), dst_ref=r_ref.at[me] if phase == "start" else r_ref.at[lin],
            send_sem=send_sems.at[k - 1], recv_sem=recv_sems.at[k - 1],
            device_id=p, device_id_type=pl.DeviceIdType.MESH)
        if phase == "start":
            cp.start()
        else:
            cp.wait_recv()
            cp.wait_send()
    if phase == "wait":
        local.wait()


def _mm(a, b, dims, *, name, out_dtype=F32, add=None, ride=None, out_split=None, tm=1024, tn=1024, tk=2048):
    if dims == "tn":
        (K, M), (_, N) = a.shape, b.shape
    elif dims == "nt":
        (M, K), (N, _) = a.shape, b.shape
    else:
        (M, K), (_, N) = a.shape, b.shape
    tm, tk = _tile(M, tm), _tile(K, tk)
    tn = N // out_split if out_split else _tile(N, tn)
    ni, nj, nk = M // tm, N // tn, K // tk
    n_in = 2 + (add is not None) + (ride is not None)

    def body(*refs):
        a_ref, b_ref, o_ref = refs[0], refs[1], refs[n_in]
        acc_ref = refs[n_in + 1 + (ride is not None)]
        i, j, k = pl.program_id(0), pl.program_id(1), pl.program_id(2)
        if ride is not None:
            ride_refs = (refs[n_in - 1], refs[n_in + 1], *refs[-3:])

            @pl.when((i == 0) & (j == 0) & (k == 0))
            def _():
                _ride(ride[0], "start", *ride_refs)

        @pl.when(k == 0)
        def _():
            acc_ref[...] = jnp.zeros_like(acc_ref)

        acc_ref[...] += _dot16(a_ref[...], b_ref[...], dims)

        @pl.when(k == nk - 1)
        def _():
            r = acc_ref[...]
            if add is not None:
                r = r + refs[2][...]
            o_ref[...] = r.astype(o_ref.dtype)

        if ride is not None:
            @pl.when((i == ni - 1) & (j == nj - 1) & (k == nk - 1))
            def _():
                _ride(ride[0], "wait", *ride_refs)

    a_spec = pl.BlockSpec((tk, tm), lambda i, j, k: (k, i)) if dims == "tn" else pl.BlockSpec((tm, tk), lambda i, j, k: (i, k))
    b_spec = pl.BlockSpec((tn, tk), lambda i, j, k: (j, k)) if dims == "nt" else pl.BlockSpec((tk, tn), lambda i, j, k: (k, j))
    o_spec = pl.BlockSpec((tm, tn), lambda i, j, k: (i, j))
    in_specs, args = [a_spec, b_spec], [a, b]
    if add is not None:
        in_specs.append(o_spec)
        args.append(add)
    out_specs, out_shape, scratch = [o_spec], [jax.ShapeDtypeStruct((M, N), out_dtype)], [pltpu.VMEM((tm, tn), F32)]
    if out_split:
        out_specs = [pl.BlockSpec((None, tm, tn), lambda i, j, k: (j, i, 0))]
        out_shape = [jax.ShapeDtypeStruct((out_split, M, tn), out_dtype)]
    if ride is not None:
        in_specs.append(pl.BlockSpec(memory_space=pl.ANY))
        args.append(ride[1])
        out_specs.append(pl.BlockSpec(memory_space=pl.ANY))
        out_shape.append(_ride_shape(*ride))
        scratch += _ride_scratch()
    sem = ("arbitrary",) * 3 if ride is not None else ("parallel", "parallel", "arbitrary")
    res = pl.pallas_call(
        body, name=name, grid=(ni, nj, nk), in_specs=in_specs, out_specs=out_specs, out_shape=out_shape,
        scratch_shapes=scratch,
        compiler_params=pltpu.CompilerParams(dimension_semantics=sem, vmem_limit_bytes=VMEM_LIMIT),
    )(*args)
    return res[0] if ride is None else res


def _rowwise(fn, rows, consts, outs, accs=(), *, name, tm=256):
    S = rows[0][0].shape[0]
    tm = min(tm, S)
    nr, nc, no = len(rows), len(consts), len(outs)

    def body(*refs):
        res = fn(*[r[...] for r in refs[:nr + nc]])
        res = tuple(res) if isinstance(res, (tuple, list)) else (res,)
        orefs, arefs = refs[nr + nc:nr + nc + no], refs[nr + nc + no:]
        for r, v in zip(orefs, res[:no]):
            r[...] = v.astype(r.dtype)
        if arefs:
            @pl.when(pl.program_id(0) == 0)
            def _():
                for r in arefs:
                    r[...] = jnp.zeros_like(r)

            for r, v in zip(arefs, res[no:]):
                r[...] += v

    in_specs = [pl.BlockSpec((tm, w), lambda i, cb=cb: (i, cb)) for (_, w, cb) in rows]
    in_specs += [pl.BlockSpec(c.shape, lambda i: (0, 0)) for c in consts]
    out_specs = [pl.BlockSpec((tm, w), lambda i: (i, 0)) for (w, _) in outs]
    out_specs += [pl.BlockSpec(s, lambda i: (0, 0)) for s in accs]
    out_shape = [jax.ShapeDtypeStruct((S, w), dt) for (w, dt) in outs]
    out_shape += [jax.ShapeDtypeStruct(s, F32) for s in accs]
    res = pl.pallas_call(
        body, name=name, grid=(S // tm,), in_specs=in_specs, out_specs=out_specs, out_shape=out_shape,
        compiler_params=pltpu.CompilerParams(dimension_semantics=("arbitrary",), vmem_limit_bytes=VMEM_LIMIT),
    )(*[r[0] for r in rows], *consts)
    return res


def _rowwise_bwd(f, rows, consts, cts, row_grads, const_grads, *, name, add=None, out_dtypes=None, tm=256):
    nr, nc, nct = len(rows), len(consts), len(cts)
    all_rows = list(rows) + list(cts) + ([add] if add is not None else [])

    def fn(*args):
        nrow = len(all_rows)
        prim = [x.astype(F32) for x in args[:nr]] + [x.astype(F32) for x in args[nrow:]]
        ct = tuple(x.astype(F32) for x in args[nr:nr + nct])
        out, vjp = jax.vjp(f, *prim)
        gs = vjp(ct if isinstance(out, (tuple, list)) else ct[0])
        res = [gs[k] for k in row_grads]
        if add is not None:
            res[0] = res[0] + args[nrow - 1]
        return tuple(res) + tuple(gs[nr + k] for k in const_grads)

    out_dtypes = out_dtypes or [F32] * len(row_grads)
    outs = [(rows[k][1], dt) for k, dt in zip(row_grads, out_dtypes)]
    accs = [consts[k].shape for k in const_grads]
    return _rowwise(fn, all_rows, consts, outs, accs, name=name, tm=tm)


def _rms(x, g, n=None):
    ms = jnp.sum(x * x, axis=-1, keepdims=True) * (1.0 / (n or x.shape[-1]))
    return x * lax.rsqrt(ms + EPS) * g


def _sigmoid(x):
    return 1.0 / (1.0 + jnp.exp(-x))


def _swap_halves(x):
    r = lax.broadcasted_iota(jnp.int32, (DH, DH), 0)
    c = lax.broadcasted_iota(jnp.int32, (DH, DH), 1)
    half = ROPE // 2
    perm = jnp.where(((r < half) & (c == r + half)) | ((r >= half) & (r < ROPE) & (c == r - half)), 1.0, 0.0)
    return lax.dot_general(x, perm.astype(F32), NN, precision=lax.Precision.HIGHEST, preferred_element_type=F32)


def _rope128(x, g128, cs):
    y = _rms(x, g128, n=ROPE)
    return y * cs[:, :DH] + _swap_halves(y) * cs[:, DH:]


def _f_norm(x, g):
    return _rms(x, g)


def _f_norm2(x, g1, g2):
    xn = x * lax.rsqrt(jnp.mean(x * x, axis=-1, keepdims=True) + EPS)
    return xn * g1, xn * g2


def _f_kv1(ckr, cs, g_ckv, g_kr):
    w = g_ckv.shape[-1]
    return _rms(ckr[:, :w], g_ckv), _rope128(ckr[:, w:], g_kr, cs)


def _f_kv2(kv, kr, g_kn):
    ks, vs = [], []
    for h in range(H_MLA):
        ks += [_rms(kv[:, 2 * DH * h:2 * DH * h + DH], g_kn), kr]
        vs.append(kv[:, 2 * DH * h + DH:2 * DH * (h + 1)])
    return jnp.concatenate(ks, axis=1), jnp.concatenate(vs, axis=1)


def _f_q2(q, cs, g_n, g_r):
    out = []
    for h in range(H_MLA):
        out += [_rms(q[:, 2 * DH * h:2 * DH * h + DH], g_n), _rope128(q[:, 2 * DH * h + DH:2 * DH * (h + 1)], g_r, cs)]
    return jnp.concatenate(out, axis=1)


def _f_mix(att, g_att, q_m, g_m, mkv, g_q, g_k):
    mem_w = H_MEM * DH
    heads = []
    for h in range(H_MEM):
        kh = _rms(mkv[:, h * DH:(h + 1) * DH], g_k)
        vh = mkv[:, mem_w + h * DH:mem_w + (h + 1) * DH]
        qh = _rms(q_m[:, h * DH:(h + 1) * DH], g_q)
        s = _bdot(qh, kh, "nt") * (DH ** -0.5)
        p = jnp.exp(s - lax.stop_gradient(jnp.max(s, axis=-1, keepdims=True)))
        p = p / jnp.sum(p, axis=-1, keepdims=True)
        heads.append(_bdot(p, vh, "nn"))
    mo = jnp.concatenate(heads, axis=1)
    return jnp.concatenate([att * (g_att * _sigmoid(g_att)), mo * (g_m * _sigmoid(g_m))], axis=1)


def _split_dot(x, u):
    hi = x.astype(BF16)
    lo = (x - hi.astype(F32)).astype(BF16)
    return (lax.dot_general(hi, u, NN, preferred_element_type=F32)
            + lax.dot_general(lo, u, NN, preferred_element_type=F32))


def _tri(t):
    r = lax.broadcasted_iota(jnp.int32, (t, t), 0)
    c = lax.broadcasted_iota(jnp.int32, (t, t), 1)
    return r, c


def _strict_lower(t):
    r, c = _tri(t)
    return (r > c).astype(BF16)


def _rows_ahead(tq, tk):
    return lax.broadcasted_iota(jnp.int32, (tq, tk), 0) - lax.broadcasted_iota(jnp.int32, (tq, tk), 1)


EXP_UNDERFLOW = -110.0


def _log_one_minus_beta(zr, scale):
    zs, nz = zr * scale, zr * (-scale)
    return zs, jnp.minimum(nz, 0.0) - jnp.log(1.0 + jnp.exp(jnp.minimum(zs, nz)))


def _sb_fwd(qkv, *, tq, tk, name, ride=None):
    S = qkv.shape[0]
    tq, tk = min(tq, S), min(tk, tq, S)
    nd = tq // tk
    H = H_SB
    scale = DH ** -0.5

    nq = S // tq

    def body(q_ref, k_ref, v_ref, *rest):
        o_ref = rest[1] if ride is not None else rest[0]
        h, i = pl.program_id(0), pl.program_id(1)
        if ride is not None:
            ride_refs = (rest[0], *rest[2:])

            @pl.when((h == 0) & (i == 0))
            def _():
                _ride(ride[0], "start", *ride_refs)

        q = q_ref[...]
        u = _strict_lower(tk)
        ahead = _rows_ahead(tq, tk)

        def block(j, acc, cb, keep):
            off = pl.multiple_of(j * tk, tk)
            k = k_ref[pl.ds(off, tk), :]
            v = v_ref[pl.ds(off, tk), :]
            z, l = _log_one_minus_beta(lax.dot_general(q, k, NT, preferred_element_type=F32), scale)
            if keep is not None:
                l = jnp.where(keep, l, 0.0)
            a = jnp.exp((z + l) + (_split_dot(l, u) + cb))
            if keep is not None:
                a = jnp.where(keep, a, 0.0)
            acc = acc + lax.dot_general(a.astype(BF16), v, NN, preferred_element_type=F32)
            return acc, cb + jnp.sum(l, axis=1, keepdims=True)

        carry = (jnp.zeros((tq, DH), F32), jnp.zeros((tq, 1), F32))
        for t in reversed(range(nd)):
            carry = block(i * nd + t, *carry, ahead > t * tk)
        _, acc, _ = lax.while_loop(
            lambda st: (st[0] < i * nd) & (jnp.max(st[2]) > EXP_UNDERFLOW),
            lambda st: (st[0] + 1, *block(i * nd - 1 - st[0], st[1], st[2], None)), (jnp.int32(0), *carry))
        o_ref[...] = acc

        if ride is not None:
            @pl.when((h == H - 1) & (i == nq - 1))
            def _():
                _ride(ride[0], "wait", *ride_refs)

    in_specs = [pl.BlockSpec((tq, DH), lambda h, i: (i, h)),
                pl.BlockSpec((S, DH), lambda h, i: (0, H + h)),
                pl.BlockSpec((S, DH), lambda h, i: (0, 2 * H + h))]
    out_specs = [pl.BlockSpec((tq, DH), lambda h, i: (i, h))]
    out_shape = [jax.ShapeDtypeStruct((S, H * DH), F32)]
    args, scratch = [qkv, qkv, qkv], []
    if ride is not None:
        in_specs.append(pl.BlockSpec(memory_space=pl.ANY))
        args.append(ride[1])
        out_specs.append(pl.BlockSpec(memory_space=pl.ANY))
        out_shape.append(_ride_shape(*ride))
        scratch = _ride_scratch()
    res = pl.pallas_call(
        body, name=name, grid=(H, nq), in_specs=in_specs, out_specs=out_specs, out_shape=out_shape,
        scratch_shapes=scratch,
        compiler_params=pltpu.CompilerParams(dimension_semantics=("arbitrary", "arbitrary"), vmem_limit_bytes=VMEM_LIMIT),
    )(*args)
    return res[0] if ride is None else res


def _sb_bwd(qkv, o, do, *, tq, tk, name, ride=None):
    S = qkv.shape[0]
    tq, tk = min(tq, S), min(tk, tq, S)
    nd = tq // tk
    H = H_SB
    scale = DH ** -0.5

    nq = S // tq

    def body(q_ref, k_ref, v_ref, o_ref, do_ref, *rest):
        if ride is not None:
            dq_ref, dk_out, dv_out, dk_ref, dv_ref = rest[1], rest[2], rest[3], rest[5], rest[6]
            ride_refs = (rest[0], rest[4], *rest[7:])
        else:
            dq_ref, dk_out, dv_out, dk_ref, dv_ref = rest
        h, i = pl.program_id(0), pl.program_id(1)
        if ride is not None:
            @pl.when((h == 0) & (i == 0))
            def _():
                _ride(ride[0], "start", *ride_refs)

        @pl.when(i == 0)
        def _():
            dk_ref[...] = jnp.zeros_like(dk_ref)
            dv_ref[...] = jnp.zeros_like(dv_ref)

        q = q_ref[...]
        do = do_ref[...]
        do16 = do.astype(BF16)
        dsum = jnp.sum(do16.astype(F32) * o_ref[...], axis=1, keepdims=True)
        u = _strict_lower(tk)
        ahead = _rows_ahead(tq, tk)

        def block(j, dq, cb, ce, keep):
            off = pl.multiple_of(j * tk, tk)
            k = k_ref[pl.ds(off, tk), :]
            v = v_ref[pl.ds(off, tk), :]
            z, l = _log_one_minus_beta(lax.dot_general(q, k, NT, preferred_element_type=F32), scale)
            if keep is not None:
                l = jnp.where(keep, l, 0.0)
            log_beta = z + l
            a = jnp.exp(log_beta + (_split_dot(l, u) + cb))
            if keep is not None:
                a = jnp.where(keep, a, 0.0)
            a16 = a.astype(BF16)
            e = a16.astype(F32) * lax.dot_general(do16, v, NT, preferred_element_type=F32)
            left = dsum - (ce + _split_dot(e, u) + e)
            dz = (e - jnp.exp(log_beta) * (e + left)) * scale
            if keep is not None:
                dz = jnp.where(keep, dz, 0.0)
            dz = dz.astype(BF16)
            dq = dq + lax.dot_general(dz, k, NN, preferred_element_type=F32)
            dk_ref[pl.ds(off, tk), :] += lax.dot_general(dz, q, TN, preferred_element_type=F32)
            dv_ref[pl.ds(off, tk), :] += lax.dot_general(a16, do16, TN, preferred_element_type=F32)
            return dq, cb + jnp.sum(l, axis=1, keepdims=True), ce + jnp.sum(e, axis=1, keepdims=True)

        zero = jnp.zeros((tq, 1), F32)
        carry = (jnp.zeros((tq, DH), F32), zero, zero)
        for t in reversed(range(nd)):
            carry = block(i * nd + t, *carry, ahead > t * tk)
        _, dq, _, _ = lax.while_loop(
            lambda st: (st[0] < i * nd) & (jnp.max(st[2]) > EXP_UNDERFLOW),
            lambda st: (st[0] + 1, *block(i * nd - 1 - st[0], st[1], st[2], st[3], None)), (jnp.int32(0), *carry))
        dq_ref[...] = dq.astype(dq_ref.dtype)

        @pl.when(i == nq - 1)
        def _():
            dk_out[...] = dk_ref[...].astype(dk_out.dtype)
            dv_out[...] = dv_ref[...].astype(dv_out.dtype)

        if ride is not None:
            @pl.when((h == H - 1) & (i == nq - 1))
            def _():
                _ride(ride[0], "wait", *ride_refs)

    blk = pl.BlockSpec((tq, DH), lambda h, i: (i, h))
    whole = pl.BlockSpec((S, DH), lambda h, i: (0, h))
    shp = jax.ShapeDtypeStruct((S, H * DH), BF16)
    in_specs = [blk, pl.BlockSpec((S, DH), lambda h, i: (0, H + h)), pl.BlockSpec((S, DH), lambda h, i: (0, 2 * H + h)),
                blk, blk]
    out_specs, out_shape = [blk, whole, whole], [shp, shp, shp]
    args, scratch = [qkv, qkv, qkv, o, do], [pltpu.VMEM((S, DH), F32), pltpu.VMEM((S, DH), F32)]
    if ride is not None:
        in_specs.append(pl.BlockSpec(memory_space=pl.ANY))
        args.append(ride[1])
        out_specs.append(pl.BlockSpec(memory_space=pl.ANY))
        out_shape.append(_ride_shape(*ride))
        scratch += _ride_scratch()
    return pl.pallas_call(
        body, name=name, grid=(H, nq), in_specs=in_specs, out_specs=out_specs, out_shape=out_shape,
        scratch_shapes=scratch,
        compiler_params=pltpu.CompilerParams(dimension_semantics=("arbitrary", "arbitrary"), vmem_limit_bytes=VMEM_LIMIT),
    )(*args)


def _mla_fwd(q, k, v, *, tq, tk, name):
    S = q.shape[0]
    tq, tk = min(tq, S), min(tk, tq, S)
    nd, nb = tq // tk, S // tk
    H = H_MLA
    scale = MLA_QK ** -0.5
    vt = v.reshape(nb, tk, H, DH).transpose(2, 0, 3, 1)

    def body(q_ref, k_ref, vt_ref, o_ref, lse_ref):
        i = pl.program_id(1)
        qb = q_ref[...]
        behind = lax.broadcasted_iota(jnp.int32, (tk, tq), 1) - lax.broadcasted_iota(jnp.int32, (tk, tq), 0)

        def block(j, m, den, acct, keep):
            off = pl.multiple_of(j * tk, tk)
            st = lax.dot_general(k_ref[pl.ds(off, tk), :], qb, NT, preferred_element_type=F32) * scale
            if keep is not None:
                st = jnp.where(keep, st, -1e30)
            m_new = jnp.maximum(m, jnp.max(st, axis=0, keepdims=True))
            pt = jnp.exp(st - m_new)
            alpha = jnp.exp(m - m_new)
            den = alpha * den + jnp.sum(pt, axis=0, keepdims=True)
            acct = alpha * acct + lax.dot_general(vt_ref[j], pt.astype(BF16), NN, preferred_element_type=F32)
            return m_new, den, acct

        init = (jnp.full((1, tq), -1e30, F32), jnp.zeros((1, tq), F32), jnp.zeros((DH, tq), F32))
        carry = lax.fori_loop(0, i * nd, lambda j, carry: block(j, *carry, None), init)
        for t in range(nd):
            carry = block(i * nd + t, *carry, behind >= t * tk)
        m, den, acct = carry
        o_ref[...] = (acct / den).T
        lse_ref[0] = m + jnp.log(den)

    return pl.pallas_call(
        body, name=name, grid=(H, S // tq),
        in_specs=[pl.BlockSpec((tq, 2 * DH), lambda h, i: (i, h)),
                  pl.BlockSpec((S, 2 * DH), lambda h, i: (0, h)),
                  pl.BlockSpec((None, nb, DH, tk), lambda h, i: (h, 0, 0, 0))],
        out_specs=[pl.BlockSpec((tq, DH), lambda h, i: (i, h)), pl.BlockSpec((1, 1, tq), lambda h, i: (h, 0, i))],
        out_shape=[jax.ShapeDtypeStruct((S, H * DH), F32), jax.ShapeDtypeStruct((H, 1, S), F32)],
        compiler_params=pltpu.CompilerParams(dimension_semantics=("arbitrary", "arbitrary"), vmem_limit_bytes=VMEM_LIMIT),
    )(q, k, vt)


def _mla_bwd(q, k, v, o, do, lse, *, tq, tk, name):
    S = q.shape[0]
    tq, tk = min(tq, S), min(tk, tq, S)
    nd, nb = tq // tk, S // tk
    H = H_MLA
    scale = MLA_QK ** -0.5
    kt = k.reshape(nb, tk, H, 2 * DH).transpose(2, 0, 3, 1)

    def body(q_ref, k_ref, kt_ref, v_ref, o_ref, do_ref, lse_ref, dq_ref, dk_ref, dv_ref):
        i = pl.program_id(1)

        @pl.when(i == 0)
        def _():
            dk_ref[...] = jnp.zeros_like(dk_ref)
            dv_ref[...] = jnp.zeros_like(dv_ref)

        qb = q_ref[...]
        do = do_ref[...]
        do16 = do.astype(BF16)
        dsum = jnp.sum((do * o_ref[...]).T, axis=0, keepdims=True)
        lse = lse_ref[0]
        behind = lax.broadcasted_iota(jnp.int32, (tk, tq), 1) - lax.broadcasted_iota(jnp.int32, (tk, tq), 0)

        def block(j, dqt, keep):
            off = pl.multiple_of(j * tk, tk)
            kb = k_ref[pl.ds(off, tk), :]
            vb = v_ref[pl.ds(off, tk), :]
            st = lax.dot_general(kb, qb, NT, preferred_element_type=F32) * scale
            if keep is not None:
                st = jnp.where(keep, st, -1e30)
            pt = jnp.exp(st - lse)
            dpt = lax.dot_general(vb, do16, NT, preferred_element_type=F32)
            dst = (pt * (dpt - dsum) * scale).astype(BF16)
            dk_ref[pl.ds(off, tk), :] += lax.dot_general(dst, qb, NN, preferred_element_type=F32)
            dv_ref[pl.ds(off, tk), :] += lax.dot_general(pt.astype(BF16), do16, NN, preferred_element_type=F32)
            return dqt + lax.dot_general(kt_ref[j], dst, NN, preferred_element_type=F32)

        dqt = lax.fori_loop(0, i * nd, lambda j, dqt: block(j, dqt, None), jnp.zeros((2 * DH, tq), F32))
        for t in range(nd):
            dqt = block(i * nd + t, dqt, behind >= t * tk)
        dq_ref[...] = dqt.T

    blk = pl.BlockSpec((tq, DH), lambda h, i: (i, h))
    blk2 = pl.BlockSpec((tq, 2 * DH), lambda h, i: (i, h))
    return pl.pallas_call(
        body, name=name, grid=(H, S // tq),
        in_specs=[blk2, pl.BlockSpec((S, 2 * DH), lambda h, i: (0, h)),
                  pl.BlockSpec((None, nb, 2 * DH, tk), lambda h, i: (h, 0, 0, 0)),
                  pl.BlockSpec((S, DH), lambda h, i: (0, h)),
                  blk, blk, pl.BlockSpec((1, 1, tq), lambda h, i: (h, 0, i))],
        out_specs=[blk2, pl.BlockSpec((S, 2 * DH), lambda h, i: (0, h)), pl.BlockSpec((S, DH), lambda h, i: (0, h))],
        out_shape=[jax.ShapeDtypeStruct((S, H * 2 * DH), F32), jax.ShapeDtypeStruct((S, H * 2 * DH), F32),
                   jax.ShapeDtypeStruct((S, H * DH), F32)],
        compiler_params=pltpu.CompilerParams(dimension_semantics=("arbitrary", "arbitrary"), vmem_limit_bytes=VMEM_LIMIT_BIG),
    )(q, k, kt, v, o, do, lse)


def _all_gather(block, *, name):
    R, C = block.shape

    def body(x_ref, out_ref, send_sems, recv_sems, local_sem):
        x, y, c = _mesh_pos()
        me, sibling = (x, y, c), (x, y, 1 - c)
        chips = [(1 - x, y), (x, 1 - y), (1 - x, 1 - y)]

        def slot(px, py, pc):
            return out_ref.at[4 * px + 2 * py + pc]

        def copy(k, blk, to, src=None):
            return pltpu.make_async_remote_copy(
                src_ref=slot(*blk) if src is None else src, dst_ref=slot(*blk),
                send_sem=send_sems.at[k], recv_sem=recv_sems.at[k],
                device_id=to, device_id_type=pl.DeviceIdType.MESH)

        mine = pltpu.make_async_copy(x_ref, slot(*me), local_sem)
        mine.start()
        first = [copy(0, me, sibling, src=x_ref)]
        first += [copy(1 + j, me, (*chip, c), src=x_ref) for j, chip in enumerate(chips)]
        for cp in first:
            cp.start()
        passed = [copy(4 + j, (*chip, c), sibling) for j, chip in enumerate(chips)]
        for j, chip in enumerate(chips):
            copy(1 + j, (*chip, c), me).wait_recv()
            passed[j].start()
        copy(0, sibling, me).wait_recv()
        for j, chip in enumerate(chips):
            copy(4 + j, (*chip, 1 - c), me).wait_recv()
        for cp in first + passed:
            cp.wait_send()
        mine.wait()

    return pl.pallas_call(
        body, name=name,
        out_shape=jax.ShapeDtypeStruct((N_DEV, R, C), block.dtype),
        in_specs=[pl.BlockSpec(memory_space=pl.ANY)], out_specs=pl.BlockSpec(memory_space=pl.ANY),
        scratch_shapes=[pltpu.SemaphoreType.DMA((7,)), pltpu.SemaphoreType.DMA((7,)), pltpu.SemaphoreType.DMA],
    )(block)


def _all_to_all(send, *, name):
    def body(*refs):
        _ride("a2a", "start", *refs)
        _ride("a2a", "wait", *refs)

    return pl.pallas_call(
        body, name=name, out_shape=_ride_shape("a2a", send),
        in_specs=[pl.BlockSpec(memory_space=pl.ANY)], out_specs=pl.BlockSpec(memory_space=pl.ANY),
        scratch_shapes=_ride_scratch(),
    )(send)


def _reduce_adamw(recv, w, m, v, *, name):
    R, C = w.shape
    tr = next(t for t in (128, 64, 32, 16, 8) if R % t == 0)

    def body(g_ref, w_ref, m_ref, v_ref, og_ref, od_ref, om_ref, ov_ref):
        g = g_ref[0].astype(F32)
        for s in range(1, N_DEV):
            g = g + g_ref[s].astype(F32)
        mn = ADAM_B1 * m_ref[...] + (1.0 - ADAM_B1) * g
        vn = ADAM_B2 * v_ref[...] + (1.0 - ADAM_B2) * jnp.square(g)
        m_hat = mn / (1.0 - ADAM_B1 ** ADAM_STEP)
        v_hat = vn / (1.0 - ADAM_B2 ** ADAM_STEP)
        og_ref[...] = g
        od_ref[...] = -ADAM_LR * (m_hat / (jnp.sqrt(v_hat) + ADAM_EPS) + ADAM_WD * w_ref[...])
        om_ref[...] = mn
        ov_ref[...] = vn

    blk = pl.BlockSpec((tr, C), lambda i: (i, 0))
    shp = jax.ShapeDtypeStruct((R, C), F32)
    return pl.pallas_call(
        body, name=name, grid=(R // tr,),
        in_specs=[pl.BlockSpec((N_DEV, tr, C), lambda i: (0, i, 0)), blk, blk, blk],
        out_specs=[blk, blk, blk, blk], out_shape=[shp, shp, shp, shp],
        compiler_params=pltpu.CompilerParams(dimension_semantics=("parallel",), vmem_limit_bytes=VMEM_LIMIT),
    )(recv, w, m, v)


SHARDED = (("a_norm", 1), ("a_w_in", 2), ("a_w_out", 1), ("w_dkv", 0), ("w_ukv", 1), ("b_w_in", 2),
           ("b_w_uq", 2), ("b_w_out", 1), ("w_mem_kv", 1))
SMALL = ("kv_norm", "g_ckv", "g_k_nope", "g_k_rope", "b_norm", "b_g_q_lat", "b_g_q_nope", "b_g_q_rope",
         "mem_norm", "g_mem_q", "g_mem_k")
WEIGHTS = ("a_norm", "a_w_in", "a_w_out", "kv_norm", "w_dkv", "g_ckv", "w_ukv", "g_k_nope", "g_k_rope", "b_norm",
           "b_w_in", "b_g_q_lat", "b_w_uq", "b_g_q_nope", "b_g_q_rope", "b_w_out", "mem_norm", "w_mem_kv",
           "g_mem_q", "g_mem_k")
ROW_MULT = 8
ROW_BLOCK = 128


def _rows_of(n, mult):
    rows = -(-n // LANES)
    return -(-rows // mult) * mult


def _to_rows(flat, mult):
    n = flat.shape[-1]
    rows = _rows_of(n, mult)
    pad = [(0, 0)] * (flat.ndim - 1) + [(0, rows * LANES - n)]
    return jnp.pad(flat, pad).reshape(*flat.shape[:-1], rows, LANES)


def _split8(full, axis):
    shp = full.shape
    t = full.reshape(*shp[:axis], N_DEV, shp[axis] // N_DEV, *shp[axis + 1:])
    return jnp.moveaxis(t, axis, 0).reshape(N_DEV, -1)


def _join8(rows, axis, shard_shape):
    t = rows.reshape(N_DEV, *shard_shape)
    t = jnp.moveaxis(t, 0, axis)
    return t.reshape(*shard_shape[:axis], N_DEV * shard_shape[axis], *shard_shape[axis + 1:])


def _stack_rows(parts, block=ROW_BLOCK):
    rows = sum(p.shape[-2] for p in parts)
    if rows % block:
        parts = list(parts) + [jnp.zeros((*parts[0].shape[:-2], -rows % block, LANES), parts[0].dtype)]
    return jnp.concatenate(parts, axis=-2)


def _pack_local(vals, names, mult, extra=None):
    parts = [_to_rows(vals[n].reshape(-1), mult) for n in names]
    if extra is not None:
        parts.append(_to_rows(extra.reshape(-1), mult))
    return _stack_rows(parts)


def _unpack_local(slab, names, shapes, mult):
    out, row = {}, 0
    for n in names:
        size = 1
        for d in shapes[n]:
            size *= d
        rows = _rows_of(size, mult)
        out[n] = slab[row:row + rows].reshape(-1)[:size].reshape(shapes[n])
        row += rows
    return out, row


def kernel(x, mem, positions, a_norm, a_w_in, a_w_out, kv_norm, w_dkv, g_ckv, w_ukv, g_k_nope, g_k_rope, b_norm, b_w_in, b_g_q_lat, b_w_uq, b_g_q_nope, b_g_q_rope, b_w_out, mem_norm, w_mem_kv, g_mem_q, g_mem_k, loss_target, m_a_norm, m_a_w_in, m_a_w_out, m_kv_norm, m_w_dkv, m_g_ckv, m_w_ukv, m_g_k_nope, m_g_k_rope, m_b_norm, m_b_w_in, m_b_g_q_lat, m_b_w_uq, m_b_g_q_nope, m_b_g_q_rope, m_b_w_out, m_mem_norm, m_w_mem_kv, m_g_mem_q, m_g_mem_k, v_a_norm, v_a_w_in, v_a_w_out, v_kv_norm, v_w_dkv, v_g_ckv, v_w_ukv, v_g_k_nope, v_g_k_rope, v_b_norm, v_b_w_in, v_b_g_q_lat, v_b_w_uq, v_b_g_q_nope, v_b_g_q_rope, v_b_w_out, v_mem_norm, v_w_mem_kv, v_g_mem_q, v_g_mem_k):
    wts = dict(a_norm=a_norm, a_w_in=a_w_in, a_w_out=a_w_out, kv_norm=kv_norm, w_dkv=w_dkv, g_ckv=g_ckv, w_ukv=w_ukv,
               g_k_nope=g_k_nope, g_k_rope=g_k_rope, b_norm=b_norm, b_w_in=b_w_in, b_g_q_lat=b_g_q_lat, b_w_uq=b_w_uq,
               b_g_q_nope=b_g_q_nope, b_g_q_rope=b_g_q_rope, b_w_out=b_w_out, mem_norm=mem_norm, w_mem_kv=w_mem_kv,
               g_mem_q=g_mem_q, g_mem_k=g_mem_k)
    mom = dict(a_norm=m_a_norm, a_w_in=m_a_w_in, a_w_out=m_a_w_out, kv_norm=m_kv_norm, w_dkv=m_w_dkv, g_ckv=m_g_ckv,
               w_ukv=m_w_ukv, g_k_nope=m_g_k_nope, g_k_rope=m_g_k_rope, b_norm=m_b_norm, b_w_in=m_b_w_in,
               b_g_q_lat=m_b_g_q_lat, b_w_uq=m_b_w_uq, b_g_q_nope=m_b_g_q_nope, b_g_q_rope=m_b_g_q_rope,
               b_w_out=m_b_w_out, mem_norm=m_mem_norm, w_mem_kv=m_w_mem_kv, g_mem_q=m_g_mem_q, g_mem_k=m_g_mem_k)
    var = dict(a_norm=v_a_norm, a_w_in=v_a_w_in, a_w_out=v_a_w_out, kv_norm=v_kv_norm, w_dkv=v_w_dkv, g_ckv=v_g_ckv,
               w_ukv=v_w_ukv, g_k_nope=v_g_k_nope, g_k_rope=v_g_k_rope, b_norm=v_b_norm, b_w_in=v_b_w_in,
               b_g_q_lat=v_b_g_q_lat, b_w_uq=v_b_w_uq, b_g_q_nope=v_b_g_q_nope, b_g_q_rope=v_b_g_q_rope,
               b_w_out=v_b_w_out, mem_norm=v_mem_norm, w_mem_kv=v_w_mem_kv, g_mem_q=v_g_mem_q, g_mem_k=v_g_mem_k)
    shapes = {n: wts[n].shape for n in WEIGHTS}
    S, D = x.shape[1], x.shape[2]
    xs, ms, tgt = x[0], mem[0], loss_target[0]
    sb_w, mem_w, mla_w = H_SB * DH, H_MEM * DH, H_MLA * DH
    q_lora, kv_lora = b_g_q_lat.shape[-1], g_ckv.shape[-1]

    def pieces_of(names):
        return [_to_rows(lax.bitcast_convert_type(a_norm.reshape(-1), BF16).reshape(-1), 16) if n == "a_norm"
                else _to_rows(wts[n].astype(BF16).reshape(-1), 16) for n in names]

    def unpack_gathered(gathered, names):
        full, row = {}, 0
        for n in names:
            size = wts[n].size * (2 if n == "a_norm" else 1)
            rows = _rows_of(size, 16)
            flat = gathered[:, row:row + rows].reshape(N_DEV, -1)[:, :size]
            if n == "a_norm":
                flat = lax.bitcast_convert_type(flat.reshape(N_DEV, -1, 2), F32)
            full[n] = _join8(flat, dict(SHARDED)[n], wts[n].shape)
            row += rows
        return full

    first, second, third = ("a_norm", "a_w_in"), ("a_w_out", "w_dkv", "w_ukv", "w_mem_kv"), ("b_w_in", "b_w_uq", "b_w_out")
    full = unpack_gathered(_all_gather(jnp.concatenate(pieces_of(first), axis=0), name="gather_first"), first)
    g_a, w_a_in = full["a_norm"], full["a_w_in"][0]

    row2 = lambda g: g.reshape(1, -1)
    h0 = _rowwise(_f_norm, [(xs, D, 0)], [g_a], [(D, BF16)], name="a_norm_fwd")[0]
    qkv, gathered = _mm(h0, w_a_in[:, :3 * sb_w], "nn", out_dtype=BF16, name="a_in_qkv",
                        ride=("gather", jnp.concatenate(pieces_of(second), axis=0)))
    full = unpack_gathered(gathered, second)
    pa = _mm(h0, w_a_in[:, 3 * sb_w:], "nn", name="a_in_rest")
    sb, gathered = _sb_fwd(qkv, tq=512, tk=256, name="sb_fwd",
                           ride=("gather", jnp.concatenate(pieces_of(third), axis=0)))
    full.update(unpack_gathered(gathered, third))
    w_a_out = full["a_w_out"][0]
    w_dkv_p = jnp.pad(full["w_dkv"], ((0, 0), (0, ROPE)))
    w_ukv_f = full["w_ukv"]
    wb = full["b_w_in"][0]
    w_b_in = jnp.concatenate([wb[:, q_lora:q_lora + mla_w], wb[:, :q_lora], wb[:, q_lora + mla_w:]], axis=1)
    w_uq_p = jnp.pad(full["b_w_uq"][0].reshape(q_lora, H_MLA, MLA_QK),
                     ((0, 0), (0, 0), (0, 2 * DH - MLA_QK))).reshape(q_lora, H_MLA * 2 * DH)
    w_b_out = full["b_w_out"][0]
    w_mem = full["w_mem_kv"]

    pad128 = lambda g: jnp.pad(g.reshape(1, -1), ((0, 0), (0, DH - ROPE)))
    g_kr, g_qr = pad128(g_k_rope), pad128(b_g_q_rope[0])
    g_kv, g_b, g_c, g_kn = row2(kv_norm), row2(b_norm[0]), row2(g_ckv), row2(g_k_nope)
    g_ql, g_qn = row2(b_g_q_lat[0]), row2(b_g_q_nope[0])

    inv_freq = jnp.power(ROPE_THETA, -jnp.arange(0, ROPE, 2, dtype=F32) / ROPE)
    ang = positions[0].astype(F32)[:, None] * inv_freq
    z64 = jnp.zeros((S, DH - ROPE), F32)
    cs = jnp.concatenate([jnp.cos(ang), jnp.cos(ang), z64, -jnp.sin(ang), jnp.sin(ang), z64], axis=1)

    mn, mkv = [], []
    for l in range(2):
        mn.append(_rowwise(_f_norm, [(ms, D, 0)], [row2(mem_norm[l])], [(D, BF16)], name=f"mem_norm{l}")[0])
        mkv.append(_mm(mn[l], w_mem[l], "nn", name=f"mem_kv{l}"))
    g_mq = [row2(g_mem_q[l]) for l in range(2)]
    g_mk = [row2(g_mem_k[l]) for l in range(2)]

    mix_a_rows = [(sb, sb_w, 0), (pa, sb_w, 0), (pa, mem_w, sb_w // mem_w), (pa, mem_w, sb_w // mem_w + 1)]
    mixed_a = _rowwise(_f_mix, mix_a_rows, [mkv[0], g_mq[0], g_mk[0]], [(sb_w + mem_w, BF16)], name="a_mix_fwd")[0]
    x1 = _mm(mixed_a, w_a_out, "nn", add=xs, name="a_out")

    hk, hb = _rowwise(_f_norm2, [(x1, D, 0)], [g_kv, g_b], [(D, BF16), (D, BF16)], name="b_norm_fwd")
    ckr = _mm(hk, w_dkv_p, "nn", name="kv_down")
    cn, kr = _rowwise(_f_kv1, [(ckr, kv_lora + DH, 0), (cs, 2 * DH, 0)], [g_c, g_kr],
                      [(kv_lora, BF16), (DH, F32)], name="kv1_fwd")
    kvu = _mm(cn, w_ukv_f, "nn", name="kv_up")
    k2, v2 = _rowwise(_f_kv2, [(kvu, H_MLA * 2 * DH, 0), (kr, DH, 0)], [g_kn],
                      [(H_MLA * 2 * DH, BF16), (mla_w, BF16)], name="kv2_fwd")
    pb = _mm(hb, w_b_in, "nn", name="b_in")
    ql = _rowwise(_f_norm, [(pb, q_lora, mla_w // q_lora)], [g_ql], [(q_lora, BF16)], name="q_lat_fwd")[0]
    qraw = _mm(ql, w_uq_p, "nn", name="q_up")
    q2 = _rowwise(_f_q2, [(qraw, H_MLA * 2 * DH, 0), (cs, 2 * DH, 0)], [g_qn, g_qr],
                  [(H_MLA * 2 * DH, BF16)], name="q2_fwd")[0]
    att, lse = _mla_fwd(q2, k2, v2, tq=1024, tk=1024, name="mla_fwd")
    cb = (mla_w + q_lora) // mem_w
    mix_b_rows = [(att, mla_w, 0), (pb, mla_w, 0), (pb, mem_w, cb), (pb, mem_w, cb + 1)]
    mixed_b = _rowwise(_f_mix, mix_b_rows, [mkv[1], g_mq[1], g_mk[1]], [(mla_w + mem_w, BF16)], name="b_mix_fwd")[0]
    y = _mm(mixed_b, w_b_out, "nn", add=x1, name="b_out")

    def loss_fn(yb, tb):
        err = yb - tb
        part = 0.5 * jnp.sum(jnp.sum(err * err, axis=-1, keepdims=True) * (1.0 / D))
        return err * (1.0 / D), jnp.full((1, DH), part, F32)

    dy, loss_part = _rowwise(loss_fn, [(y, D, 0), (tgt, D, 0)], [], [(D, F32)], [(1, DH)], name="loss")

    gr = {}
    d_mixed_b = _mm(dy, w_b_out, "nt", name="b_out_dx")
    gr["b_w_out"] = _mm(mixed_b, dy, "tn", out_dtype=BF16, name="b_out_dw")[None]
    d_att, d_gmla, d_qm_b, d_gm_b, d_mkv1, d_gq1, d_gk1 = _rowwise_bwd(
        _f_mix, mix_b_rows, [mkv[1], g_mq[1], g_mk[1]], [(d_mixed_b, mla_w + mem_w, 0)],
        [0, 1, 2, 3], [0, 1, 2], out_dtypes=[F32, BF16, BF16, BF16], name="b_mix_bwd")
    dq2, dk2, dv2 = _mla_bwd(q2, k2, v2, att, d_att, lse, tq=512, tk=512, name="mla_bwd")
    d_qraw, d_gqn, d_gqr = _rowwise_bwd(
        _f_q2, [(qraw, H_MLA * 2 * DH, 0), (cs, 2 * DH, 0)], [g_qn, g_qr], [(dq2, H_MLA * 2 * DH, 0)],
        [0], [0, 1], out_dtypes=[BF16], name="q2_bwd")
    d_ql = _mm(d_qraw, w_uq_p, "nt", name="q_up_dx")
    d_wuq = _mm(ql, d_qraw, "tn", out_dtype=BF16, name="q_up_dw")
    gr["b_w_uq"] = d_wuq.reshape(q_lora, H_MLA, 2 * DH)[:, :, :MLA_QK].reshape(1, q_lora, H_MLA * MLA_QK)
    d_qlat, d_gql = _rowwise_bwd(_f_norm, [(pb, q_lora, mla_w // q_lora)], [g_ql], [(d_ql, q_lora, 0)],
                                 [0], [0], out_dtypes=[BF16], name="q_lat_bwd")
    d_pb = jnp.concatenate([d_gmla, d_qlat, d_qm_b, d_gm_b], axis=1)
    d_hb = _mm(d_pb, w_b_in, "nt", name="b_in_dx")
    d_wbin = _mm(hb, d_pb, "tn", out_dtype=BF16, name="b_in_dw")
    gr["b_w_in"] = jnp.concatenate([d_wbin[:, mla_w:mla_w + q_lora], d_wbin[:, :mla_w], d_wbin[:, mla_w + q_lora:]],
                                   axis=1)[None]
    d_kvu, d_kr, d_gkn = _rowwise_bwd(
        _f_kv2, [(kvu, H_MLA * 2 * DH, 0), (kr, DH, 0)], [g_kn], [(dk2, H_MLA * 2 * DH, 0), (dv2, mla_w, 0)],
        [0, 1], [0], out_dtypes=[BF16, F32], name="kv2_bwd")
    d_cn = _mm(d_kvu, w_ukv_f, "nt", name="kv_up_dx")
    gr["w_ukv"] = _mm(cn, d_kvu, "tn", out_dtype=BF16, name="kv_up_dw")
    d_ckr, d_gc, d_gkr = _rowwise_bwd(
        _f_kv1, [(ckr, kv_lora + DH, 0), (cs, 2 * DH, 0)], [g_c, g_kr], [(d_cn, kv_lora, 0), (d_kr, DH, 0)],
        [0], [0, 1], out_dtypes=[BF16], name="kv1_bwd")
    d_hk = _mm(d_ckr, w_dkv_p, "nt", name="kv_down_dx")
    gr["w_dkv"] = _mm(hk, d_ckr, "tn", out_dtype=BF16, name="kv_down_dw")[:, :kv_lora + ROPE]
    d_x1, d_gkv, d_gb = _rowwise_bwd(_f_norm2, [(x1, D, 0)], [g_kv, g_b], [(d_hk, D, 0), (d_hb, D, 0)],
                                     [0], [0, 1], add=(dy, D, 0), name="b_norm_bwd")
    d_mixed_a = _mm(d_x1, w_a_out, "nt", name="a_out_dx")
    gr["a_w_out"] = _mm(mixed_a, d_x1, "tn", out_dtype=BF16, name="a_out_dw")[None]
    d_sb, d_gsb, d_qm_a, d_gm_a, d_mkv0, d_gq0, d_gk0 = _rowwise_bwd(
        _f_mix, mix_a_rows, [mkv[0], g_mq[0], g_mk[0]], [(d_mixed_a, sb_w + mem_w, 0)],
        [0, 1, 2, 3], [0, 1, 2], out_dtypes=[F32, BF16, BF16, BF16], name="a_mix_bwd")
    d_wmem, d_mnorm = [], []
    for l, d_mkv in enumerate((d_mkv0, d_mkv1)):
        d_mn = _mm(d_mkv, w_mem[l], "nt", name=f"mem_kv_dx{l}")
        d_wmem.append(_mm(mn[l], d_mkv, "tn", out_dtype=BF16, name=f"mem_kv_dw{l}"))
        d_mnorm.append(_rowwise_bwd(_f_norm, [(ms, D, 0)], [row2(mem_norm[l])], [(d_mn, D, 0)], [], [0],
                                    name=f"mem_norm_bwd{l}")[0])
    gr["w_mem_kv"] = jnp.stack(d_wmem)

    mid = tuple(n for n, _ in SHARDED[2:])
    send_mid = _stack_rows([_to_rows(_split8(gr[n], ax), 16) for n, ax in SHARDED[2:]])
    dq, dk, dv, recv_mid = _sb_bwd(qkv, sb, d_sb, tq=512, tk=256, name="sb_bwd", ride=("a2a", send_mid))
    d_pa = jnp.concatenate([dq, dk, dv, d_gsb, d_qm_a, d_gm_a], axis=1)
    send_ain = _mm(h0, d_pa, "tn", out_dtype=BF16, out_split=N_DEV, name="a_in_dw")
    d_h0, recv_ain = _mm(d_pa, w_a_in, "nt", name="a_in_dx", ride=("a2a", send_ain))
    grad_x, d_ga = _rowwise_bwd(_f_norm, [(xs, D, 0)], [g_a], [(d_h0, D, 0)], [0], [0], add=(d_x1, D, 0),
                                name="a_norm_bwd")
    gr["mem_norm"] = jnp.concatenate(d_mnorm, axis=0)
    gr["g_mem_q"] = jnp.concatenate([d_gq0, d_gq1], axis=0)
    gr["g_mem_k"] = jnp.concatenate([d_gk0, d_gk1], axis=0)
    gr["kv_norm"], gr["b_norm"], gr["g_ckv"], gr["g_k_nope"] = d_gkv, d_gb, d_gc, d_gkn
    gr["g_k_rope"], gr["b_g_q_rope"] = d_gkr[:, :ROPE], d_gqr[:, :ROPE]
    gr["b_g_q_lat"], gr["b_g_q_nope"] = d_gql, d_gqn
    last = ("a_norm",) + SMALL
    parts = [_to_rows(_split8(d_ga, 1), ROW_MULT)]
    parts += [jnp.broadcast_to(_to_rows(gr[n].reshape(-1), ROW_MULT)[None], (N_DEV, _rows_of(gr[n].size, ROW_MULT), LANES))
              for n in SMALL]
    parts.append(jnp.broadcast_to(_to_rows(loss_part[0, :1], ROW_MULT)[None], (N_DEV, ROW_MULT, LANES)))
    recv_last = _all_to_all(_stack_rows(parts), name="exchange_small")

    out = [{}, {}, {}, {}]
    zero = jnp.zeros((1,), F32)
    for tag, recv, names, mult, extra in (("mid", recv_mid, mid, 16, None), ("last", recv_last, last, ROW_MULT, zero)):
        slabs = _reduce_adamw(recv, *[_pack_local(t, names, mult, extra) for t in (wts, mom, var)],
                              name=f"reduce_adamw_{tag}")
        for o, slab in zip(out, slabs):
            o.update(_unpack_local(slab, names, shapes, mult)[0])
        if extra is not None:
            loss = slabs[0][_unpack_local(slabs[0], names, shapes, mult)[1], 0]
    slabs = _reduce_adamw(recv_ain, *[t["a_w_in"][0] for t in (wts, mom, var)], name="reduce_adamw_ain")
    for o, slab in zip(out, slabs):
        o["a_w_in"] = slab[None]
    return (loss, grad_x[None], *[o[n] for o in out for n in WEIGHTS])
```

```python
import functools

import jax
import jax.numpy as jnp
from jax import lax
from jax.experimental import pallas as pl
from jax.experimental.pallas import tpu as pltpu

F32, BF16 = jnp.float32, jnp.bfloat16

N_DEV = 8
DH = 128
H_SB, H_MEM, H_MLA = 12, 4, 12
ROPE = 64
MLA_QK = DH + ROPE
EPS = 1e-6
ROPE_THETA = 10000.0
ADAM_LR, ADAM_B1, ADAM_B2, ADAM_EPS, ADAM_WD, ADAM_STEP = 0.001, 0.9, 0.999, 1e-08, 0.01, 10

LANES = 1024
VMEM_LIMIT = 48 * 1024 * 1024
VMEM_LIMIT_BIG = 56 * 1024 * 1024

NN = (((1,), (0,)), ((), ()))
NT = (((1,), (1,)), ((), ()))
TN = (((0,), (0,)), ((), ()))
_DIMS = {"nn": NN, "nt": NT, "tn": TN}


def _dot16(a, b, dims):
    return lax.dot_general(a.astype(BF16), b.astype(BF16), _DIMS[dims], preferred_element_type=F32)


@functools.partial(jax.custom_vjp, nondiff_argnums=(2,))
def _bdot(a, b, dims):
    return _dot16(a, b, dims)


def _bdot_fwd(a, b, dims):
    return _dot16(a, b, dims), (a, b)


def _bdot_bwd(dims, res, g):
    a, b = res
    if dims == "nn":
        return _dot16(g, b, "nt"), _dot16(a, g, "tn")
    return _dot16(g, b, "nn"), _dot16(g, a, "tn")


_bdot.defvjp(_bdot_fwd, _bdot_bwd)


def _tile(n, pref):
    if n <= pref:
        return n
    t = (pref // 128) * 128
    while n % t:
        t -= 128
    return t


def _mesh_pos():
    return lax.axis_index("x"), lax.axis_index("y"), lax.axis_index("c")


def _ride_shape(kind, src):
    return jax.ShapeDtypeStruct((N_DEV, *src.shape) if kind == "gather" else src.shape, src.dtype)


def _ride_scratch():
    return [pltpu.SemaphoreType.DMA((N_DEV - 1,)), pltpu.SemaphoreType.DMA((N_DEV - 1,)), pltpu.SemaphoreType.DMA]


def _ride(kind, phase, s_ref, r_ref, send_sems, recv_sems, local_sem):
    x, y, c = _mesh_pos()
    me = 4 * x + 2 * y + c
    src = (lambda lin: s_ref) if kind == "gather" else (lambda lin: s_ref.at[lin])
    local = pltpu.make_async_copy(src(me), r_ref.at[me], local_sem)
    if phase == "start":
        local.start()
    for k in range(1, N_DEV):
        p = (1 - x if k & 4 else x, 1 - y if k & 2 else y, 1 - c if k & 1 else c)
        lin = 4 * p[0] + 2 * p[1] + p[2]
        cp = pltpu.make_async_remote_copy(
            src_ref=src(lin), dst_ref=r_ref.at[me] if phase == "start" else r_ref.at[lin],
            send_sem=send_sems.at[k - 1], recv_sem=recv_sems.at[k - 1],
            device_id=p, device_id_type=pl.DeviceIdType.MESH)
        if phase == "start":
            cp.start()
        else:
            cp.wait_recv()
            cp.wait_send()
    if phase == "wait":
        local.wait()


def _mm(a, b, dims, *, name, out_dtype=F32, add=None, ride=None, out_split=None, tm=1024, tn=1024, tk=2048):
    if dims == "tn":
        (K, M), (_, N) = a.shape, b.shape
    elif dims == "nt":
        (M, K), (N, _) = a.shape, b.shape
    else:
        (M, K), (_, N) = a.shape, b.shape
    tm, tk = _tile(M, tm), _tile(K, tk)
    tn = N // out_split if out_split else _tile(N, tn)
    ni, nj, nk = M // tm, N // tn, K // tk
    n_in = 2 + (add is not None) + (ride is not None)

    def body(*refs):
        a_ref, b_ref, o_ref = refs[0], refs[1], refs[n_in]
        acc_ref = refs[n_in + 1 + (ride is not None)]
        i, j, k = pl.program_id(0), pl.program_id(1), pl.program_id(2)
        if ride is not None:
            ride_refs = (refs[n_in - 1], refs[n_in + 1], *refs[-3:])

            @pl.when((i == 0) & (j == 0) & (k == 0))
            def _():
                _ride(ride[0], "start", *ride_refs)

        if nk == 1:
            r = _dot16(a_ref[...], b_ref[...], dims)
            o_ref[...] = (r if add is None else r + refs[2][...]).astype(o_ref.dtype)
        else:
            @pl.when(k == 0)
            def _():
                acc_ref[...] = jnp.zeros_like(acc_ref)

            acc_ref[...] += _dot16(a_ref[...], b_ref[...], dims)

            @pl.when(k == nk - 1)
            def _():
                r = acc_ref[...]
                if add is not None:
                    r = r + refs[2][...]
                o_ref[...] = r.astype(o_ref.dtype)

        if ride is not None:
            @pl.when((i == ni - 1) & (j == nj - 1) & (k == nk - 1))
            def _():
                _ride(ride[0], "wait", *ride_refs)

    a_spec = pl.BlockSpec((tk, tm), lambda i, j, k: (k, i)) if dims == "tn" else pl.BlockSpec((tm, tk), lambda i, j, k: (i, k))
    b_spec = pl.BlockSpec((tn, tk), lambda i, j, k: (j, k)) if dims == "nt" else pl.BlockSpec((tk, tn), lambda i, j, k: (k, j))
    o_spec = pl.BlockSpec((tm, tn), lambda i, j, k: (i, j))
    in_specs, args = [a_spec, b_spec], [a, b]
    if add is not None:
        in_specs.append(o_spec)
        args.append(add)
    out_specs, out_shape = [o_spec], [jax.ShapeDtypeStruct((M, N), out_dtype)]
    scratch = [pltpu.VMEM((tm, tn) if nk > 1 else (8, 128), F32)]
    if out_split:
        out_specs = [pl.BlockSpec((None, tm, tn), lambda i, j, k: (j, i, 0))]
        out_shape = [jax.ShapeDtypeStruct((out_split, M, tn), out_dtype)]
    if ride is not None:
        in_specs.append(pl.BlockSpec(memory_space=pl.ANY))
        args.append(ride[1])
        out_specs.append(pl.BlockSpec(memory_space=pl.ANY))
        out_shape.append(_ride_shape(*ride))
        scratch += _ride_scratch()
    sem = ("arbitrary",) * 3 if ride is not None else ("parallel", "parallel", "arbitrary")
    res = pl.pallas_call(
        body, name=name, grid=(ni, nj, nk), in_specs=in_specs, out_specs=out_specs, out_shape=out_shape,
        scratch_shapes=scratch,
        compiler_params=pltpu.CompilerParams(dimension_semantics=sem, vmem_limit_bytes=VMEM_LIMIT),
    )(*args)
    return res[0] if ride is None else res


def _rowwise(fn, rows, consts, outs, accs=(), *, name, tm=256):
    S = rows[0][0].shape[0]
    tm = min(tm, S)
    nr, nc, no = len(rows), len(consts), len(outs)

    def body(*refs):
        res = fn(*[r[...] for r in refs[:nr + nc]])
        res = tuple(res) if isinstance(res, (tuple, list)) else (res,)
        orefs, arefs = refs[nr + nc:nr + nc + no], refs[nr + nc + no:]
        for r, v in zip(orefs, res[:no]):
            r[...] = v.astype(r.dtype)
        if arefs:
            @pl.when(pl.program_id(0) == 0)
            def _():
                for r in arefs:
                    r[...] = jnp.zeros_like(r)

            for r, v in zip(arefs, res[no:]):
                r[...] += v

    in_specs = [pl.BlockSpec((tm, w), lambda i, cb=cb: (i, cb)) for (_, w, cb) in rows]
    in_specs += [pl.BlockSpec(c.shape, lambda i: (0, 0)) for c in consts]
    out_specs = [pl.BlockSpec((tm, w), lambda i: (i, 0)) for (w, _) in outs]
    out_specs += [pl.BlockSpec(s, lambda i: (0, 0)) for s in accs]
    out_shape = [jax.ShapeDtypeStruct((S, w), dt) for (w, dt) in outs]
    out_shape += [jax.ShapeDtypeStruct(s, F32) for s in accs]
    res = pl.pallas_call(
        body, name=name, grid=(S // tm,), in_specs=in_specs, out_specs=out_specs, out_shape=out_shape,
        compiler_params=pltpu.CompilerParams(dimension_semantics=("arbitrary",), vmem_limit_bytes=VMEM_LIMIT),
    )(*[r[0] for r in rows], *consts)
    return res


def _rowwise_bwd(f, rows, consts, cts, row_grads, const_grads, *, name, add=None, out_dtypes=None, tm=256):
    nr, nc, nct = len(rows), len(consts), len(cts)
    all_rows = list(rows) + list(cts) + ([add] if add is not None else [])

    def fn(*args):
        nrow = len(all_rows)
        prim = [x.astype(F32) for x in args[:nr]] + [x.astype(F32) for x in args[nrow:]]
        ct = tuple(x.astype(F32) for x in args[nr:nr + nct])
        out, vjp = jax.vjp(f, *prim)
        gs = vjp(ct if isinstance(out, (tuple, list)) else ct[0])
        res = [gs[k] for k in row_grads]
        if add is not None:
            res[0] = res[0] + args[nrow - 1]
        return tuple(res) + tuple(gs[nr + k] for k in const_grads)

    out_dtypes = out_dtypes or [F32] * len(row_grads)
    outs = [(rows[k][1], dt) for k, dt in zip(row_grads, out_dtypes)]
    accs = [consts[k].shape for k in const_grads]
    return _rowwise(fn, all_rows, consts, outs, accs, name=name, tm=tm)


def _rms(x, g, n=None):
    ms = jnp.sum(x * x, axis=-1, keepdims=True) * (1.0 / (n or x.shape[-1]))
    return x * lax.rsqrt(ms + EPS) * g


def _sigmoid(x):
    return 1.0 / (1.0 + jnp.exp(-x))


def _swap_halves(x):
    r = lax.broadcasted_iota(jnp.int32, (DH, DH), 0)
    c = lax.broadcasted_iota(jnp.int32, (DH, DH), 1)
    half = ROPE // 2
    perm = jnp.where(((r < half) & (c == r + half)) | ((r >= half) & (r < ROPE) & (c == r - half)), 1.0, 0.0)
    return lax.dot_general(x, perm.astype(F32), NN, precision=lax.Precision.HIGHEST, preferred_element_type=F32)


def _rope128(x, g128, cs):
    y = _rms(x, g128, n=ROPE)
    return y * cs[:, :DH] + _swap_halves(y) * cs[:, DH:]


def _f_norm(x, g):
    return _rms(x, g)


def _f_norm2(x, g1, g2):
    xn = x * lax.rsqrt(jnp.mean(x * x, axis=-1, keepdims=True) + EPS)
    return xn * g1, xn * g2


def _f_kv1(ckr, cs, g_ckv, g_kr):
    w = g_ckv.shape[-1]
    return _rms(ckr[:, :w], g_ckv), _rope128(ckr[:, w:], g_kr, cs)


def _f_kv2(kv, kr, g_kn):
    ks, vs = [], []
    for h in range(H_MLA):
        ks += [_rms(kv[:, 2 * DH * h:2 * DH * h + DH], g_kn), kr]
        vs.append(kv[:, 2 * DH * h + DH:2 * DH * (h + 1)])
    return jnp.concatenate(ks, axis=1), jnp.concatenate(vs, axis=1)


def _f_q2(q, cs, g_n, g_r):
    out = []
    for h in range(H_MLA):
        out += [_rms(q[:, 2 * DH * h:2 * DH * h + DH], g_n), _rope128(q[:, 2 * DH * h + DH:2 * DH * (h + 1)], g_r, cs)]
    return jnp.concatenate(out, axis=1)


def _f_mix(att, g_att, q_m, g_m, mkv, g_q, g_k):
    mem_w = H_MEM * DH
    heads = []
    for h in range(H_MEM):
        kh = _rms(mkv[:, h * DH:(h + 1) * DH], g_k)
        vh = mkv[:, mem_w + h * DH:mem_w + (h + 1) * DH]
        qh = _rms(q_m[:, h * DH:(h + 1) * DH], g_q)
        s = _bdot(qh, kh, "nt") * (DH ** -0.5)
        p = jnp.exp(s - lax.stop_gradient(jnp.max(s, axis=-1, keepdims=True)))
        p = p / jnp.sum(p, axis=-1, keepdims=True)
        heads.append(_bdot(p, vh, "nn"))
    mo = jnp.concatenate(heads, axis=1)
    return jnp.concatenate([att * (g_att * _sigmoid(g_att)), mo * (g_m * _sigmoid(g_m))], axis=1)


def _split_dot(x, u):
    hi = x.astype(BF16)
    lo = (x - hi.astype(F32)).astype(BF16)
    return (lax.dot_general(hi, u, NN, preferred_element_type=F32)
            + lax.dot_general(lo, u, NN, preferred_element_type=F32))


def _tri(t):
    r = lax.broadcasted_iota(jnp.int32, (t, t), 0)
    c = lax.broadcasted_iota(jnp.int32, (t, t), 1)
    return r, c


def _strict_lower(t):
    r, c = _tri(t)
    return (r > c).astype(BF16)


def _head_blocks_t(x, w, tk, *, name):
    S = x.shape[0]
    H = x.shape[1] // w

    def body(x_ref, o_ref):
        for h in range(H):
            o_ref[h] = x_ref[:, h * w:(h + 1) * w].T

    return pl.pallas_call(
        body, name=name, grid=(S // tk,),
        in_specs=[pl.BlockSpec((tk, H * w), lambda i: (i, 0))],
        out_specs=pl.BlockSpec((H, None, w, tk), lambda i: (0, i, 0, 0)),
        out_shape=jax.ShapeDtypeStruct((H, S // tk, w, tk), x.dtype),
        compiler_params=pltpu.CompilerParams(dimension_semantics=("parallel",), vmem_limit_bytes=VMEM_LIMIT),
    )(x)


def _rows_ahead(tq, tk):
    return lax.broadcasted_iota(jnp.int32, (tq, tk), 0) - lax.broadcasted_iota(jnp.int32, (tq, tk), 1)


EXP_UNDERFLOW = -110.0


def _log_one_minus_beta(zr, scale):
    zs, nz = zr * scale, zr * (-scale)
    return zs, jnp.minimum(nz, 0.0) - jnp.log(1.0 + jnp.exp(jnp.minimum(zs, nz)))


def _sb_fwd(qkv, *, tq, tk, name, ride=None):
    S = qkv.shape[0]
    tq, tk = min(tq, S), min(tk, tq, S)
    nd = tq // tk
    H = H_SB
    scale = DH ** -0.5

    nq = S // tq

    def body(q_ref, k_ref, v_ref, *rest):
        o_ref = rest[1] if ride is not None else rest[0]
        h, i = pl.program_id(0), pl.program_id(1)
        if ride is not None:
            ride_refs = (rest[0], *rest[2:])

            @pl.when((h == 0) & (i == 0))
            def _():
                _ride(ride[0], "start", *ride_refs)

        q = q_ref[...]
        u = _strict_lower(tk)
        ahead = _rows_ahead(tq, tk)

        def block(j, acc, cb, keep):
            off = pl.multiple_of(j * tk, tk)
            k = k_ref[pl.ds(off, tk), :]
            v = v_ref[pl.ds(off, tk), :]
            z, l = _log_one_minus_beta(lax.dot_general(q, k, NT, preferred_element_type=F32), scale)
            if keep is not None:
                l = jnp.where(keep, l, 0.0)
            a = jnp.exp((z + l) + (_split_dot(l, u) + cb))
            if keep is not None:
                a = jnp.where(keep, a, 0.0)
            acc = acc + lax.dot_general(a.astype(BF16), v, NN, preferred_element_type=F32)
            return acc, cb + jnp.sum(l, axis=1, keepdims=True)

        carry = (jnp.zeros((tq, DH), F32), jnp.zeros((tq, 1), F32))
        for t in reversed(range(nd)):
            carry = block(i * nd + t, *carry, ahead > t * tk)
        _, acc, _ = lax.while_loop(
            lambda st: (st[0] < i * nd) & (jnp.max(st[2]) > EXP_UNDERFLOW),
            lambda st: (st[0] + 1, *block(i * nd - 1 - st[0], st[1], st[2], None)), (jnp.int32(0), *carry))
        o_ref[...] = acc

        if ride is not None:
            @pl.when((h == H - 1) & (i == nq - 1))
            def _():
                _ride(ride[0], "wait", *ride_refs)

    in_specs = [pl.BlockSpec((tq, DH), lambda h, i: (i, h)),
                pl.BlockSpec((S, DH), lambda h, i: (0, H + h)),
                pl.BlockSpec((S, DH), lambda h, i: (0, 2 * H + h))]
    out_specs = [pl.BlockSpec((tq, DH), lambda h, i: (i, h))]
    out_shape = [jax.ShapeDtypeStruct((S, H * DH), F32)]
    args, scratch = [qkv, qkv, qkv], []
    if ride is not None:
        in_specs.append(pl.BlockSpec(memory_space=pl.ANY))
        args.append(ride[1])
        out_specs.append(pl.BlockSpec(memory_space=pl.ANY))
        out_shape.append(_ride_shape(*ride))
        scratch = _ride_scratch()
    res = pl.pallas_call(
        body, name=name, grid=(H, nq), in_specs=in_specs, out_specs=out_specs, out_shape=out_shape,
        scratch_shapes=scratch,
        compiler_params=pltpu.CompilerParams(dimension_semantics=("arbitrary", "arbitrary"), vmem_limit_bytes=VMEM_LIMIT),
    )(*args)
    return res[0] if ride is None else res


def _sb_bwd(qkv, o, do, *, tq, tk, name, ride=None):
    S = qkv.shape[0]
    tq, tk = min(tq, S), min(tk, tq, S)
    nd = tq // tk
    H = H_SB
    scale = DH ** -0.5

    nq = S // tq

    def body(q_ref, k_ref, v_ref, o_ref, do_ref, *rest):
        if ride is not None:
            dq_ref, dk_out, dv_out, dk_ref, dv_ref = rest[1], rest[2], rest[3], rest[5], rest[6]
            ride_refs = (rest[0], rest[4], *rest[7:])
        else:
            dq_ref, dk_out, dv_out, dk_ref, dv_ref = rest
        h, i = pl.program_id(0), pl.program_id(1)
        if ride is not None:
            @pl.when((h == 0) & (i == 0))
            def _():
                _ride(ride[0], "start", *ride_refs)

        @pl.when(i == 0)
        def _():
            dk_ref[...] = jnp.zeros_like(dk_ref)
            dv_ref[...] = jnp.zeros_like(dv_ref)

        q = q_ref[...]
        do = do_ref[...]
        do16 = do.astype(BF16)
        dsum = jnp.sum(do16.astype(F32) * o_ref[...], axis=1, keepdims=True)
        u = _strict_lower(tk)
        ahead = _rows_ahead(tq, tk)

        def block(j, dq, cb, ce, keep):
            off = pl.multiple_of(j * tk, tk)
            k = k_ref[pl.ds(off, tk), :]
            v = v_ref[pl.ds(off, tk), :]
            z, l = _log_one_minus_beta(lax.dot_general(q, k, NT, preferred_element_type=F32), scale)
            if keep is not None:
                l = jnp.where(keep, l, 0.0)
            log_beta = z + l
            a = jnp.exp(log_beta + (_split_dot(l, u) + cb))
            if keep is not None:
                a = jnp.where(keep, a, 0.0)
            a16 = a.astype(BF16)
            e = a16.astype(F32) * lax.dot_general(do16, v, NT, preferred_element_type=F32)
            left = dsum - (ce + _split_dot(e, u) + e)
            dz = (e - jnp.exp(log_beta) * (e + left)) * scale
            if keep is not None:
                dz = jnp.where(keep, dz, 0.0)
            dz = dz.astype(BF16)
            dq = dq + lax.dot_general(dz, k, NN, preferred_element_type=F32)
            dk_ref[pl.ds(off, tk), :] += lax.dot_general(dz, q, TN, preferred_element_type=F32)
            dv_ref[pl.ds(off, tk), :] += lax.dot_general(a16, do16, TN, preferred_element_type=F32)
            return dq, cb + jnp.sum(l, axis=1, keepdims=True), ce + jnp.sum(e, axis=1, keepdims=True)

        zero = jnp.zeros((tq, 1), F32)
        carry = (jnp.zeros((tq, DH), F32), zero, zero)
        for t in reversed(range(nd)):
            carry = block(i * nd + t, *carry, ahead > t * tk)
        _, dq, _, _ = lax.while_loop(
            lambda st: (st[0] < i * nd) & (jnp.max(st[2]) > EXP_UNDERFLOW),
            lambda st: (st[0] + 1, *block(i * nd - 1 - st[0], st[1], st[2], st[3], None)), (jnp.int32(0), *carry))
        dq_ref[...] = dq.astype(dq_ref.dtype)

        @pl.when(i == nq - 1)
        def _():
            dk_out[...] = dk_ref[...].astype(dk_out.dtype)
            dv_out[...] = dv_ref[...].astype(dv_out.dtype)

        if ride is not None:
            @pl.when((h == H - 1) & (i == nq - 1))
            def _():
                _ride(ride[0], "wait", *ride_refs)

    blk = pl.BlockSpec((tq, DH), lambda h, i: (i, h))
    whole = pl.BlockSpec((S, DH), lambda h, i: (0, h))
    shp = jax.ShapeDtypeStruct((S, H * DH), BF16)
    in_specs = [blk, pl.BlockSpec((S, DH), lambda h, i: (0, H + h)), pl.BlockSpec((S, DH), lambda h, i: (0, 2 * H + h)),
                blk, blk]
    out_specs, out_shape = [blk, whole, whole], [shp, shp, shp]
    args, scratch = [qkv, qkv, qkv, o, do], [pltpu.VMEM((S, DH), F32), pltpu.VMEM((S, DH), F32)]
    if ride is not None:
        in_specs.append(pl.BlockSpec(memory_space=pl.ANY))
        args.append(ride[1])
        out_specs.append(pl.BlockSpec(memory_space=pl.ANY))
        out_shape.append(_ride_shape(*ride))
        scratch += _ride_scratch()
    return pl.pallas_call(
        body, name=name, grid=(H, nq), in_specs=in_specs, out_specs=out_specs, out_shape=out_shape,
        scratch_shapes=scratch,
        compiler_params=pltpu.CompilerParams(dimension_semantics=("arbitrary", "arbitrary"), vmem_limit_bytes=VMEM_LIMIT),
    )(*args)


def _mla_fwd(q, k, v, *, tq, tk, name):
    S = q.shape[0]
    tq, tk = min(tq, S), min(tk, tq, S)
    nd, nb = tq // tk, S // tk
    H = H_MLA
    scale = MLA_QK ** -0.5
    vt = _head_blocks_t(v, DH, tk, name=name + "_vt")

    def body(q_ref, k_ref, vt_ref, o_ref, lse_ref):
        i = pl.program_id(1)
        qb = q_ref[...]
        behind = lax.broadcasted_iota(jnp.int32, (tk, tq), 1) - lax.broadcasted_iota(jnp.int32, (tk, tq), 0)

        def block(j, m, den, acct, keep):
            off = pl.multiple_of(j * tk, tk)
            st = lax.dot_general(k_ref[pl.ds(off, tk), :], qb, NT, preferred_element_type=F32) * scale
            if keep is not None:
                st = jnp.where(keep, st, -1e30)
            m_new = jnp.maximum(m, jnp.max(st, axis=0, keepdims=True))
            pt = jnp.exp(st - m_new)
            alpha = jnp.exp(m - m_new)
            den = alpha * den + jnp.sum(pt, axis=0, keepdims=True)
            acct = alpha * acct + lax.dot_general(vt_ref[j], pt.astype(BF16), NN, preferred_element_type=F32)
            return m_new, den, acct

        init = (jnp.full((1, tq), -1e30, F32), jnp.zeros((1, tq), F32), jnp.zeros((DH, tq), F32))
        carry = lax.fori_loop(0, i * nd, lambda j, carry: block(j, *carry, None), init)
        for t in range(nd):
            carry = block(i * nd + t, *carry, behind >= t * tk)
        m, den, acct = carry
        o_ref[...] = (acct / den).T
        lse_ref[0] = m + jnp.log(den)

    return pl.pallas_call(
        body, name=name, grid=(H, S // tq),
        in_specs=[pl.BlockSpec((tq, 2 * DH), lambda h, i: (i, h)),
                  pl.BlockSpec((S, 2 * DH), lambda h, i: (0, h)),
                  pl.BlockSpec((None, nb, DH, tk), lambda h, i: (h, 0, 0, 0))],
        out_specs=[pl.BlockSpec((tq, DH), lambda h, i: (i, h)), pl.BlockSpec((1, 1, tq), lambda h, i: (h, 0, i))],
        out_shape=[jax.ShapeDtypeStruct((S, H * DH), F32), jax.ShapeDtypeStruct((H, 1, S), F32)],
        compiler_params=pltpu.CompilerParams(dimension_semantics=("arbitrary", "arbitrary"), vmem_limit_bytes=VMEM_LIMIT),
    )(q, k, vt)


def _mla_bwd(q, k, v, o, do, lse, *, tq, tk, name):
    S = q.shape[0]
    tq, tk = min(tq, S), min(tk, tq, S)
    nd, nb = tq // tk, S // tk
    H = H_MLA
    scale = MLA_QK ** -0.5
    kt = _head_blocks_t(k, 2 * DH, tk, name=name + "_kt")

    def body(q_ref, k_ref, kt_ref, v_ref, o_ref, do_ref, lse_ref, dq_ref, dk_ref, dv_ref):
        i = pl.program_id(1)

        @pl.when(i == 0)
        def _():
            dk_ref[...] = jnp.zeros_like(dk_ref)
            dv_ref[...] = jnp.zeros_like(dv_ref)

        qb = q_ref[...]
        do = do_ref[...]
        do16 = do.astype(BF16)
        dsum = jnp.sum((do * o_ref[...]).T, axis=0, keepdims=True)
        lse = lse_ref[0]
        behind = lax.broadcasted_iota(jnp.int32, (tk, tq), 1) - lax.broadcasted_iota(jnp.int32, (tk, tq), 0)

        def block(j, dqt, keep):
            off = pl.multiple_of(j * tk, tk)
            kb = k_ref[pl.ds(off, tk), :]
            vb = v_ref[pl.ds(off, tk), :]
            st = lax.dot_general(kb, qb, NT, preferred_element_type=F32) * scale
            if keep is not None:
                st = jnp.where(keep, st, -1e30)
            pt = jnp.exp(st - lse)
            dpt = lax.dot_general(vb, do16, NT, preferred_element_type=F32)
            dst = (pt * (dpt - dsum) * scale).astype(BF16)
            dk_ref[pl.ds(off, tk), :] += lax.dot_general(dst, qb, NN, preferred_element_type=F32)
            dv_ref[pl.ds(off, tk), :] += lax.dot_general(pt.astype(BF16), do16, NN, preferred_element_type=F32)
            return dqt + lax.dot_general(kt_ref[j], dst, NN, preferred_element_type=F32)

        dqt = lax.fori_loop(0, i * nd, lambda j, dqt: block(j, dqt, None), jnp.zeros((2 * DH, tq), F32))
        for t in range(nd):
            dqt = block(i * nd + t, dqt, behind >= t * tk)
        dq_ref[...] = dqt.T

    blk = pl.BlockSpec((tq, DH), lambda h, i: (i, h))
    blk2 = pl.BlockSpec((tq, 2 * DH), lambda h, i: (i, h))
    return pl.pallas_call(
        body, name=name, grid=(H, S // tq),
        in_specs=[blk2, pl.BlockSpec((S, 2 * DH), lambda h, i: (0, h)),
                  pl.BlockSpec((None, nb, 2 * DH, tk), lambda h, i: (h, 0, 0, 0)),
                  pl.BlockSpec((S, DH), lambda h, i: (0, h)),
                  blk, blk, pl.BlockSpec((1, 1, tq), lambda h, i: (h, 0, i))],
        out_specs=[blk2, pl.BlockSpec((S, 2 * DH), lambda h, i: (0, h)), pl.BlockSpec((S, DH), lambda h, i: (0, h))],
        out_shape=[jax.ShapeDtypeStruct((S, H * 2 * DH), F32), jax.ShapeDtypeStruct((S, H * 2 * DH), F32),
                   jax.ShapeDtypeStruct((S, H * DH), F32)],
        compiler_params=pltpu.CompilerParams(dimension_semantics=("arbitrary", "arbitrary"), vmem_limit_bytes=VMEM_LIMIT_BIG),
    )(q, k, kt, v, o, do, lse)


def _all_gather(block, *, name):
    R, C = block.shape

    def body(x_ref, out_ref, send_sems, recv_sems, local_sem):
        x, y, c = _mesh_pos()
        me, sibling = (x, y, c), (x, y, 1 - c)
        chips = [(1 - x, y), (x, 1 - y), (1 - x, 1 - y)]

        def slot(px, py, pc):
            return out_ref.at[4 * px + 2 * py + pc]

        def copy(k, blk, to, src=None):
            return pltpu.make_async_remote_copy(
                src_ref=slot(*blk) if src is None else src, dst_ref=slot(*blk),
                send_sem=send_sems.at[k], recv_sem=recv_sems.at[k],
                device_id=to, device_id_type=pl.DeviceIdType.MESH)

        mine = pltpu.make_async_copy(x_ref, slot(*me), local_sem)
        mine.start()
        first = [copy(0, me, sibling, src=x_ref)]
        first += [copy(1 + j, me, (*chip, c), src=x_ref) for j, chip in enumerate(chips)]
        for cp in first:
            cp.start()
        passed = [copy(4 + j, (*chip, c), sibling) for j, chip in enumerate(chips)]
        for j, chip in enumerate(chips):
            copy(1 + j, (*chip, c), me).wait_recv()
            passed[j].start()
        copy(0, sibling, me).wait_recv()
        for j, chip in enumerate(chips):
            copy(4 + j, (*chip, 1 - c), me).wait_recv()
        for cp in first + passed:
            cp.wait_send()
        mine.wait()

    return pl.pallas_call(
        body, name=name,
        out_shape=jax.ShapeDtypeStruct((N_DEV, R, C), block.dtype),
        in_specs=[pl.BlockSpec(memory_space=pl.ANY)], out_specs=pl.BlockSpec(memory_space=pl.ANY),
        scratch_shapes=[pltpu.SemaphoreType.DMA((7,)), pltpu.SemaphoreType.DMA((7,)), pltpu.SemaphoreType.DMA],
    )(block)


def _all_to_all(send, *, name):
    def body(*refs):
        _ride("a2a", "start", *refs)
        _ride("a2a", "wait", *refs)

    return pl.pallas_call(
        body, name=name, out_shape=_ride_shape("a2a", send),
        in_specs=[pl.BlockSpec(memory_space=pl.ANY)], out_specs=pl.BlockSpec(memory_space=pl.ANY),
        scratch_shapes=_ride_scratch(),
    )(send)


def _reduce_adamw(recv, w, m, v, *, name):
    R, C = w.shape
    tr = next(t for t in (128, 64, 32, 16, 8) if R % t == 0)

    def body(g_ref, w_ref, m_ref, v_ref, og_ref, od_ref, om_ref, ov_ref):
        g = g_ref[0].astype(F32)
        for s in range(1, N_DEV):
            g = g + g_ref[s].astype(F32)
        mn = ADAM_B1 * m_ref[...] + (1.0 - ADAM_B1) * g
        vn = ADAM_B2 * v_ref[...] + (1.0 - ADAM_B2) * jnp.square(g)
        m_hat = mn / (1.0 - ADAM_B1 ** ADAM_STEP)
        v_hat = vn / (1.0 - ADAM_B2 ** ADAM_STEP)
        og_ref[...] = g
        od_ref[...] = -ADAM_LR * (m_hat / (jnp.sqrt(v_hat) + ADAM_EPS) + ADAM_WD * w_ref[...])
        om_ref[...] = mn
        ov_ref[...] = vn

    blk = pl.BlockSpec((tr, C), lambda i: (i, 0))
    shp = jax.ShapeDtypeStruct((R, C), F32)
    return pl.pallas_call(
        body, name=name, grid=(R // tr,),
        in_specs=[pl.BlockSpec((N_DEV, tr, C), lambda i: (0, i, 0)), blk, blk, blk],
        out_specs=[blk, blk, blk, blk], out_shape=[shp, shp, shp, shp],
        compiler_params=pltpu.CompilerParams(dimension_semantics=("parallel",), vmem_limit_bytes=VMEM_LIMIT),
    )(recv, w, m, v)


SHARDED = (("a_norm", 1), ("a_w_in", 2), ("a_w_out", 1), ("w_dkv", 0), ("w_ukv", 1), ("b_w_in", 2),
           ("b_w_uq", 2), ("b_w_out", 1), ("w_mem_kv", 1))
SMALL = ("kv_norm", "g_ckv", "g_k_nope", "g_k_rope", "b_norm", "b_g_q_lat", "b_g_q_nope", "b_g_q_rope",
         "mem_norm", "g_mem_q", "g_mem_k")
WEIGHTS = ("a_norm", "a_w_in", "a_w_out", "kv_norm", "w_dkv", "g_ckv", "w_ukv", "g_k_nope", "g_k_rope", "b_norm",
           "b_w_in", "b_g_q_lat", "b_w_uq", "b_g_q_nope", "b_g_q_rope", "b_w_out", "mem_norm", "w_mem_kv",
           "g_mem_q", "g_mem_k")
ROW_MULT = 8
ROW_BLOCK = 128


def _rows_of(n, mult):
    rows = -(-n // LANES)
    return -(-rows // mult) * mult


def _to_rows(flat, mult):
    n = flat.shape[-1]
    rows = _rows_of(n, mult)
    pad = [(0, 0)] * (flat.ndim - 1) + [(0, rows * LANES - n)]
    return jnp.pad(flat, pad).reshape(*flat.shape[:-1], rows, LANES)


def _split8(full, axis):
    shp = full.shape
    t = full.reshape(*shp[:axis], N_DEV, shp[axis] // N_DEV, *shp[axis + 1:])
    return jnp.moveaxis(t, axis, 0).reshape(N_DEV, -1)


def _join8(rows, axis, shard_shape):
    t = rows.reshape(N_DEV, *shard_shape)
    t = jnp.moveaxis(t, 0, axis)
    return t.reshape(*shard_shape[:axis], N_DEV * shard_shape[axis], *shard_shape[axis + 1:])


def _stack_rows(parts, block=ROW_BLOCK):
    rows = sum(p.shape[-2] for p in parts)
    if rows % block:
        parts = list(parts) + [jnp.zeros((*parts[0].shape[:-2], -rows % block, LANES), parts[0].dtype)]
    return jnp.concatenate(parts, axis=-2)


def _pack_local(vals, names, mult, extra=None):
    parts = [_to_rows(vals[n].reshape(-1), mult) for n in names]
    if extra is not None:
        parts.append(_to_rows(extra.reshape(-1), mult))
    return _stack_rows(parts)


def _unpack_local(slab, names, shapes, mult):
    out, row = {}, 0
    for n in names:
        size = 1
        for d in shapes[n]:
            size *= d
        rows = _rows_of(size, mult)
        out[n] = slab[row:row + rows].reshape(-1)[:size].reshape(shapes[n])
        row += rows
    return out, row


def kernel(x, mem, positions, a_norm, a_w_in, a_w_out, kv_norm, w_dkv, g_ckv, w_ukv, g_k_nope, g_k_rope, b_norm, b_w_in, b_g_q_lat, b_w_uq, b_g_q_nope, b_g_q_rope, b_w_out, mem_norm, w_mem_kv, g_mem_q, g_mem_k, loss_target, m_a_norm, m_a_w_in, m_a_w_out, m_kv_norm, m_w_dkv, m_g_ckv, m_w_ukv, m_g_k_nope, m_g_k_rope, m_b_norm, m_b_w_in, m_b_g_q_lat, m_b_w_uq, m_b_g_q_nope, m_b_g_q_rope, m_b_w_out, m_mem_norm, m_w_mem_kv, m_g_mem_q, m_g_mem_k, v_a_norm, v_a_w_in, v_a_w_out, v_kv_norm, v_w_dkv, v_g_ckv, v_w_ukv, v_g_k_nope, v_g_k_rope, v_b_norm, v_b_w_in, v_b_g_q_lat, v_b_w_uq, v_b_g_q_nope, v_b_g_q_rope, v_b_w_out, v_mem_norm, v_w_mem_kv, v_g_mem_q, v_g_mem_k):
    wts = dict(a_norm=a_norm, a_w_in=a_w_in, a_w_out=a_w_out, kv_norm=kv_norm, w_dkv=w_dkv, g_ckv=g_ckv, w_ukv=w_ukv,
               g_k_nope=g_k_nope, g_k_rope=g_k_rope, b_norm=b_norm, b_w_in=b_w_in, b_g_q_lat=b_g_q_lat, b_w_uq=b_w_uq,
               b_g_q_nope=b_g_q_nope, b_g_q_rope=b_g_q_rope, b_w_out=b_w_out, mem_norm=mem_norm, w_mem_kv=w_mem_kv,
               g_mem_q=g_mem_q, g_mem_k=g_mem_k)
    mom = dict(a_norm=m_a_norm, a_w_in=m_a_w_in, a_w_out=m_a_w_out, kv_norm=m_kv_norm, w_dkv=m_w_dkv, g_ckv=m_g_ckv,
               w_ukv=m_w_ukv, g_k_nope=m_g_k_nope, g_k_rope=m_g_k_rope, b_norm=m_b_norm, b_w_in=m_b_w_in,
               b_g_q_lat=m_b_g_q_lat, b_w_uq=m_b_w_uq, b_g_q_nope=m_b_g_q_nope, b_g_q_rope=m_b_g_q_rope,
               b_w_out=m_b_w_out, mem_norm=m_mem_norm, w_mem_kv=m_w_mem_kv, g_mem_q=m_g_mem_q, g_mem_k=m_g_mem_k)
    var = dict(a_norm=v_a_norm, a_w_in=v_a_w_in, a_w_out=v_a_w_out, kv_norm=v_kv_norm, w_dkv=v_w_dkv, g_ckv=v_g_ckv,
               w_ukv=v_w_ukv, g_k_nope=v_g_k_nope, g_k_rope=v_g_k_rope, b_norm=v_b_norm, b_w_in=v_b_w_in,
               b_g_q_lat=v_b_g_q_lat, b_w_uq=v_b_w_uq, b_g_q_nope=v_b_g_q_nope, b_g_q_rope=v_b_g_q_rope,
               b_w_out=v_b_w_out, mem_norm=v_mem_norm, w_mem_kv=v_w_mem_kv, g_mem_q=v_g_mem_q, g_mem_k=v_g_mem_k)
    shapes = {n: wts[n].shape for n in WEIGHTS}
    S, D = x.shape[1], x.shape[2]
    xs, ms, tgt = x[0], mem[0], loss_target[0]
    sb_w, mem_w, mla_w = H_SB * DH, H_MEM * DH, H_MLA * DH
    q_lora, kv_lora = b_g_q_lat.shape[-1], g_ckv.shape[-1]

    def pieces_of(names):
        return [_to_rows(lax.bitcast_convert_type(a_norm.reshape(-1), BF16).reshape(-1), 16) if n == "a_norm"
                else _to_rows(wts[n].astype(BF16).reshape(-1), 16) for n in names]

    def unpack_gathered(gathered, names):
        full, row = {}, 0
        for n in names:
            size = wts[n].size * (2 if n == "a_norm" else 1)
            rows = _rows_of(size, 16)
            flat = gathered[:, row:row + rows].reshape(N_DEV, -1)[:, :size]
            if n == "a_norm":
                flat = lax.bitcast_convert_type(flat.reshape(N_DEV, -1, 2), F32)
            full[n] = _join8(flat, dict(SHARDED)[n], wts[n].shape)
            row += rows
        return full

    first, second, third = ("a_norm", "a_w_in"), ("a_w_out", "w_dkv", "w_ukv", "w_mem_kv"), ("b_w_in", "b_w_uq", "b_w_out")
    full = unpack_gathered(_all_gather(jnp.concatenate(pieces_of(first), axis=0), name="gather_first"), first)
    g_a, w_a_in = full["a_norm"], full["a_w_in"][0]

    row2 = lambda g: g.reshape(1, -1)
    h0 = _rowwise(_f_norm, [(xs, D, 0)], [g_a], [(D, BF16)], name="a_norm_fwd", tm=512)[0]
    qkv, gathered = _mm(h0, w_a_in[:, :3 * sb_w], "nn", out_dtype=BF16, name="a_in_qkv",
                        ride=("gather", jnp.concatenate(pieces_of(second), axis=0)))
    full = unpack_gathered(gathered, second)
    pa = _mm(h0, w_a_in[:, 3 * sb_w:], "nn", name="a_in_rest")
    sb, gathered = _sb_fwd(qkv, tq=512, tk=256, name="sb_fwd",
                           ride=("gather", jnp.concatenate(pieces_of(third), axis=0)))
    full.update(unpack_gathered(gathered, third))
    w_a_out = full["a_w_out"][0]
    w_dkv_p = jnp.pad(full["w_dkv"], ((0, 0), (0, ROPE)))
    w_ukv_f = full["w_ukv"]
    wb = full["b_w_in"][0]
    w_b_in = jnp.concatenate([wb[:, q_lora:q_lora + mla_w], wb[:, :q_lora], wb[:, q_lora + mla_w:]], axis=1)
    w_uq_p = jnp.pad(full["b_w_uq"][0].reshape(q_lora, H_MLA, MLA_QK),
                     ((0, 0), (0, 0), (0, 2 * DH - MLA_QK))).reshape(q_lora, H_MLA * 2 * DH)
    w_b_out = full["b_w_out"][0]
    w_mem = full["w_mem_kv"]

    pad128 = lambda g: jnp.pad(g.reshape(1, -1), ((0, 0), (0, DH - ROPE)))
    g_kr, g_qr = pad128(g_k_rope), pad128(b_g_q_rope[0])
    g_kv, g_b, g_c, g_kn = row2(kv_norm), row2(b_norm[0]), row2(g_ckv), row2(g_k_nope)
    g_ql, g_qn = row2(b_g_q_lat[0]), row2(b_g_q_nope[0])

    inv_freq = jnp.power(ROPE_THETA, -jnp.arange(0, ROPE, 2, dtype=F32) / ROPE)
    ang = positions[0].astype(F32)[:, None] * inv_freq
    z64 = jnp.zeros((S, DH - ROPE), F32)
    cs = jnp.concatenate([jnp.cos(ang), jnp.cos(ang), z64, -jnp.sin(ang), jnp.sin(ang), z64], axis=1)

    mn, mkv = [], []
    for l in range(2):
        mn.append(_rowwise(_f_norm, [(ms, D, 0)], [row2(mem_norm[l])], [(D, BF16)], name=f"mem_norm{l}")[0])
        mkv.append(_mm(mn[l], w_mem[l], "nn", name=f"mem_kv{l}"))
    g_mq = [row2(g_mem_q[l]) for l in range(2)]
    g_mk = [row2(g_mem_k[l]) for l in range(2)]

    mix_a_rows = [(sb, sb_w, 0), (pa, sb_w, 0), (pa, mem_w, sb_w // mem_w), (pa, mem_w, sb_w // mem_w + 1)]
    mixed_a = _rowwise(_f_mix, mix_a_rows, [mkv[0], g_mq[0], g_mk[0]], [(sb_w + mem_w, BF16)], name="a_mix_fwd")[0]
    x1 = _mm(mixed_a, w_a_out, "nn", add=xs, name="a_out")

    hk, hb = _rowwise(_f_norm2, [(x1, D, 0)], [g_kv, g_b], [(D, BF16), (D, BF16)], name="b_norm_fwd", tm=512)
    ckr = _mm(hk, w_dkv_p, "nn", name="kv_down")
    cn, kr = _rowwise(_f_kv1, [(ckr, kv_lora + DH, 0), (cs, 2 * DH, 0)], [g_c, g_kr],
                      [(kv_lora, BF16), (DH, F32)], name="kv1_fwd", tm=512)
    kvu = _mm(cn, w_ukv_f, "nn", name="kv_up")
    k2, v2 = _rowwise(_f_kv2, [(kvu, H_MLA * 2 * DH, 0), (kr, DH, 0)], [g_kn],
                      [(H_MLA * 2 * DH, BF16), (mla_w, BF16)], name="kv2_fwd")
    pb = _mm(hb, w_b_in, "nn", name="b_in")
    ql = _rowwise(_f_norm, [(pb, q_lora, mla_w // q_lora)], [g_ql], [(q_lora, BF16)], name="q_lat_fwd", tm=512)[0]
    qraw = _mm(ql, w_uq_p, "nn", name="q_up")
    q2 = _rowwise(_f_q2, [(qraw, H_MLA * 2 * DH, 0), (cs, 2 * DH, 0)], [g_qn, g_qr],
                  [(H_MLA * 2 * DH, BF16)], name="q2_fwd")[0]
    att, lse = _mla_fwd(q2, k2, v2, tq=1024, tk=1024, name="mla_fwd")
    cb = (mla_w + q_lora) // mem_w
    mix_b_rows = [(att, mla_w, 0), (pb, mla_w, 0), (pb, mem_w, cb), (pb, mem_w, cb + 1)]
    mixed_b = _rowwise(_f_mix, mix_b_rows, [mkv[1], g_mq[1], g_mk[1]], [(mla_w + mem_w, BF16)], name="b_mix_fwd")[0]
    y = _mm(mixed_b, w_b_out, "nn", add=x1, name="b_out")

    def loss_fn(yb, tb):
        err = yb - tb
        part = 0.5 * jnp.sum(jnp.sum(err * err, axis=-1, keepdims=True) * (1.0 / D))
        return err * (1.0 / D), jnp.full((1, DH), part, F32)

    dy, loss_part = _rowwise(loss_fn, [(y, D, 0), (tgt, D, 0)], [], [(D, F32)], [(1, DH)], name="loss", tm=512)

    gr = {}
    d_mixed_b = _mm(dy, w_b_out, "nt", name="b_out_dx")
    gr["b_w_out"] = _mm(mixed_b, dy, "tn", out_dtype=BF16, name="b_out_dw")[None]
    d_att, d_gmla, d_qm_b, d_gm_b, d_mkv1, d_gq1, d_gk1 = _rowwise_bwd(
        _f_mix, mix_b_rows, [mkv[1], g_mq[1], g_mk[1]], [(d_mixed_b, mla_w + mem_w, 0)],
        [0, 1, 2, 3], [0, 1, 2], out_dtypes=[F32, BF16, BF16, BF16], name="b_mix_bwd")
    dq2, dk2, dv2 = _mla_bwd(q2, k2, v2, att, d_att, lse, tq=512, tk=512, name="mla_bwd")
    d_qraw, d_gqn, d_gqr = _rowwise_bwd(
        _f_q2, [(qraw, H_MLA * 2 * DH, 0), (cs, 2 * DH, 0)], [g_qn, g_qr], [(dq2, H_MLA * 2 * DH, 0)],
        [0], [0, 1], out_dtypes=[BF16], name="q2_bwd")
    d_ql = _mm(d_qraw, w_uq_p, "nt", name="q_up_dx")
    d_wuq = _mm(ql, d_qraw, "tn", out_dtype=BF16, name="q_up_dw")
    gr["b_w_uq"] = d_wuq.reshape(q_lora, H_MLA, 2 * DH)[:, :, :MLA_QK].reshape(1, q_lora, H_MLA * MLA_QK)
    d_qlat, d_gql = _rowwise_bwd(_f_norm, [(pb, q_lora, mla_w // q_lora)], [g_ql], [(d_ql, q_lora, 0)],
                                 [0], [0], out_dtypes=[BF16], name="q_lat_bwd", tm=512)
    d_pb = jnp.concatenate([d_gmla, d_qlat, d_qm_b, d_gm_b], axis=1)
    d_hb = _mm(d_pb, w_b_in, "nt", name="b_in_dx")
    d_wbin = _mm(hb, d_pb, "tn", out_dtype=BF16, name="b_in_dw")
    gr["b_w_in"] = jnp.concatenate([d_wbin[:, mla_w:mla_w + q_lora], d_wbin[:, :mla_w], d_wbin[:, mla_w + q_lora:]],
                                   axis=1)[None]
    d_kvu, d_kr, d_gkn = _rowwise_bwd(
        _f_kv2, [(kvu, H_MLA * 2 * DH, 0), (kr, DH, 0)], [g_kn], [(dk2, H_MLA * 2 * DH, 0), (dv2, mla_w, 0)],
        [0, 1], [0], out_dtypes=[BF16, F32], name="kv2_bwd")
    d_cn = _mm(d_kvu, w_ukv_f, "nt", name="kv_up_dx")
    gr["w_ukv"] = _mm(cn, d_kvu, "tn", out_dtype=BF16, name="kv_up_dw")
    d_ckr, d_gc, d_gkr = _rowwise_bwd(
        _f_kv1, [(ckr, kv_lora + DH, 0), (cs, 2 * DH, 0)], [g_c, g_kr], [(d_cn, kv_lora, 0), (d_kr, DH, 0)],
        [0], [0, 1], out_dtypes=[BF16], name="kv1_bwd", tm=512)
    d_hk = _mm(d_ckr, w_dkv_p, "nt", name="kv_down_dx")
    gr["w_dkv"] = _mm(hk, d_ckr, "tn", out_dtype=BF16, name="kv_down_dw")[:, :kv_lora + ROPE]
    d_x1, d_gkv, d_gb = _rowwise_bwd(_f_norm2, [(x1, D, 0)], [g_kv, g_b], [(d_hk, D, 0), (d_hb, D, 0)],
                                     [0], [0, 1], add=(dy, D, 0), name="b_norm_bwd")
    d_mixed_a = _mm(d_x1, w_a_out, "nt", name="a_out_dx")
    gr["a_w_out"] = _mm(mixed_a, d_x1, "tn", out_dtype=BF16, name="a_out_dw")[None]
    d_sb, d_gsb, d_qm_a, d_gm_a, d_mkv0, d_gq0, d_gk0 = _rowwise_bwd(
        _f_mix, mix_a_rows, [mkv[0], g_mq[0], g_mk[0]], [(d_mixed_a, sb_w + mem_w, 0)],
        [0, 1, 2, 3], [0, 1, 2], out_dtypes=[F32, BF16, BF16, BF16], name="a_mix_bwd")
    d_wmem, d_mnorm = [], []
    for l, d_mkv in enumerate((d_mkv0, d_mkv1)):
        d_mn = _mm(d_mkv, w_mem[l], "nt", name=f"mem_kv_dx{l}")
        d_wmem.append(_mm(mn[l], d_mkv, "tn", out_dtype=BF16, name=f"mem_kv_dw{l}"))
        d_mnorm.append(_rowwise_bwd(_f_norm, [(ms, D, 0)], [row2(mem_norm[l])], [(d_mn, D, 0)], [], [0],
                                    name=f"mem_norm_bwd{l}")[0])
    gr["w_mem_kv"] = jnp.stack(d_wmem)

    mid = tuple(n for n, _ in SHARDED[2:])
    send_mid = _stack_rows([_to_rows(_split8(gr[n], ax), 16) for n, ax in SHARDED[2:]])
    dq, dk, dv, recv_mid = _sb_bwd(qkv, sb, d_sb, tq=512, tk=256, name="sb_bwd", ride=("a2a", send_mid))
    d_pa = jnp.concatenate([dq, dk, dv, d_gsb, d_qm_a, d_gm_a], axis=1)
    send_ain = _mm(h0, d_pa, "tn", out_dtype=BF16, out_split=N_DEV, name="a_in_dw")
    d_h0, recv_ain = _mm(d_pa, w_a_in, "nt", name="a_in_dx", ride=("a2a", send_ain))
    grad_x, d_ga = _rowwise_bwd(_f_norm, [(xs, D, 0)], [g_a], [(d_h0, D, 0)], [0], [0], add=(d_x1, D, 0),
                                name="a_norm_bwd")
    gr["mem_norm"] = jnp.concatenate(d_mnorm, axis=0)
    gr["g_mem_q"] = jnp.concatenate([d_gq0, d_gq1], axis=0)
    gr["g_mem_k"] = jnp.concatenate([d_gk0, d_gk1], axis=0)
    gr["kv_norm"], gr["b_norm"], gr["g_ckv"], gr["g_k_nope"] = d_gkv, d_gb, d_gc, d_gkn
    gr["g_k_rope"], gr["b_g_q_rope"] = d_gkr[:, :ROPE], d_gqr[:, :ROPE]
    gr["b_g_q_lat"], gr["b_g_q_nope"] = d_gql, d_gqn
    last = ("a_norm",) + SMALL
    parts = [_to_rows(_split8(d_ga, 1), ROW_MULT)]
    parts += [jnp.broadcast_to(_to_rows(gr[n].reshape(-1), ROW_MULT)[None], (N_DEV, _rows_of(gr[n].size, ROW_MULT), LANES))
              for n in SMALL]
    parts.append(jnp.broadcast_to(_to_rows(loss_part[0, :1], ROW_MULT)[None], (N_DEV, ROW_MULT, LANES)))
    recv_last = _all_to_all(_stack_rows(parts), name="exchange_small")

    out = [{}, {}, {}, {}]
    zero = jnp.zeros((1,), F32)
    for tag, recv, names, mult, extra in (("mid", recv_mid, mid, 16, None), ("last", recv_last, last, ROW_MULT, zero)):
        slabs = _reduce_adamw(recv, *[_pack_local(t, names, mult, extra) for t in (wts, mom, var)],
                              name=f"reduce_adamw_{tag}")
        for o, slab in zip(out, slabs):
            o.update(_unpack_local(slab, names, shapes, mult)[0])
        if extra is not None:
            loss = slabs[0][_unpack_local(slabs[0], names, shapes, mult)[1], 0]
    slabs = _reduce_adamw(recv_ain, *[t["a_w_in"][0] for t in (wts, mom, var)], name="reduce_adamw_ain")
    for o, slab in zip(out, slabs):
        o["a_w_in"] = slab[None]
    return (loss, grad_x[None], *[o[n] for o in out for n in WEIGHTS])
```

```python
import functools

import jax
import jax.numpy as jnp
from jax import lax
from jax.experimental import pallas as pl
from jax.experimental.pallas import tpu as pltpu

F32, BF16 = jnp.float32, jnp.bfloat16

N_DEV = 8
DH = 128
H_SB, H_MEM, H_MLA = 12, 4, 12
ROPE = 64
MLA_QK = DH + ROPE
EPS = 1e-6
ROPE_THETA = 10000.0
ADAM_LR, ADAM_B1, ADAM_B2, ADAM_EPS, ADAM_WD, ADAM_STEP = 0.001, 0.9, 0.999, 1e-08, 0.01, 10

LANES = 1024
VMEM_LIMIT = 48 * 1024 * 1024
VMEM_LIMIT_BIG = 56 * 1024 * 1024

NN = (((1,), (0,)), ((), ()))
NT = (((1,), (1,)), ((), ()))
TN = (((0,), (0,)), ((), ()))
_DIMS = {"nn": NN, "nt": NT, "tn": TN}


def _dot16(a, b, dims):
    return lax.dot_general(a.astype(BF16), b.astype(BF16), _DIMS[dims], preferred_element_type=F32)


@functools.partial(jax.custom_vjp, nondiff_argnums=(2,))
def _bdot(a, b, dims):
    return _dot16(a, b, dims)


def _bdot_fwd(a, b, dims):
    return _dot16(a, b, dims), (a, b)


def _bdot_bwd(dims, res, g):
    a, b = res
    if dims == "nn":
        return _dot16(g, b, "nt"), _dot16(a, g, "tn")
    return _dot16(g, b, "nn"), _dot16(g, a, "tn")


_bdot.defvjp(_bdot_fwd, _bdot_bwd)


def _tile(n, pref):
    if n <= pref:
        return n
    t = (pref // 128) * 128
    while n % t:
        t -= 128
    return t


def _mesh_pos():
    return lax.axis_index("x"), lax.axis_index("y"), lax.axis_index("c")


def _ride_shape(kind, src):
    return jax.ShapeDtypeStruct((N_DEV, *src.shape) if kind == "gather" else src.shape, src.dtype)


def _ride_scratch():
    return [pltpu.SemaphoreType.DMA((N_DEV - 1,)), pltpu.SemaphoreType.DMA((N_DEV - 1,)), pltpu.SemaphoreType.DMA]


def _ride(kind, phase, s_ref, r_ref, send_sems, recv_sems, local_sem):
    x, y, c = _mesh_pos()
    me = 4 * x + 2 * y + c
    src = (lambda lin: s_ref) if kind == "gather" else (lambda lin: s_ref.at[lin])
    local = pltpu.make_async_copy(src(me), r_ref.at[me], local_sem)
    if phase == "start":
        local.start()
    for k in range(1, N_DEV):
        p = (1 - x if k & 4 else x, 1 - y if k & 2 else y, 1 - c if k & 1 else c)
        lin = 4 * p[0] + 2 * p[1] + p[2]
        cp = pltpu.make_async_remote_copy(
            src_ref=src(lin), dst_ref=r_ref.at[me] if phase == "start" else r_ref.at[lin],
            send_sem=send_sems.at[k - 1], recv_sem=recv_sems.at[k - 1],
            device_id=p, device_id_type=pl.DeviceIdType.MESH)
        if phase == "start":
            cp.start()
        else:
            cp.wait_recv()
            cp.wait_send()
    if phase == "wait":
        local.wait()


def _mm(a, b, dims, *, name, out_dtype=F32, add=None, ride=None, out_split=None, tm=1024, tn=1024, tk=2048):
    if dims == "tn":
        (K, M), (_, N) = a.shape, b.shape
    elif dims == "nt":
        (M, K), (N, _) = a.shape, b.shape
    else:
        (M, K), (_, N) = a.shape, b.shape
    tm, tk = _tile(M, tm), _tile(K, tk)
    tn = N // out_split if out_split else _tile(N, tn)
    ni, nj, nk = M // tm, N // tn, K // tk
    n_in = 2 + (add is not None) + (ride is not None)

    def body(*refs):
        a_ref, b_ref, o_ref = refs[0], refs[1], refs[n_in]
        acc_ref = refs[n_in + 1 + (ride is not None)]
        i, j, k = pl.program_id(0), pl.program_id(1), pl.program_id(2)
        if ride is not None:
            ride_refs = (refs[n_in - 1], refs[n_in + 1], *refs[-3:])

            @pl.when((i == 0) & (j == 0) & (k == 0))
            def _():
                _ride(ride[0], "start", *ride_refs)

        if nk == 1:
            r = _dot16(a_ref[...], b_ref[...], dims)
            o_ref[...] = (r if add is None else r + refs[2][...]).astype(o_ref.dtype)
        else:
            @pl.when(k == 0)
            def _():
                acc_ref[...] = jnp.zeros_like(acc_ref)

            acc_ref[...] += _dot16(a_ref[...], b_ref[...], dims)

            @pl.when(k == nk - 1)
            def _():
                r = acc_ref[...]
                if add is not None:
                    r = r + refs[2][...]
                o_ref[...] = r.astype(o_ref.dtype)

        if ride is not None:
            @pl.when((i == ni - 1) & (j == nj - 1) & (k == nk - 1))
            def _():
                _ride(ride[0], "wait", *ride_refs)

    a_spec = pl.BlockSpec((tk, tm), lambda i, j, k: (k, i)) if dims == "tn" else pl.BlockSpec((tm, tk), lambda i, j, k: (i, k))
    b_spec = pl.BlockSpec((tn, tk), lambda i, j, k: (j, k)) if dims == "nt" else pl.BlockSpec((tk, tn), lambda i, j, k: (k, j))
    o_spec = pl.BlockSpec((tm, tn), lambda i, j, k: (i, j))
    in_specs, args = [a_spec, b_spec], [a, b]
    if add is not None:
        in_specs.append(o_spec)
        args.append(add)
    out_specs, out_shape = [o_spec], [jax.ShapeDtypeStruct((M, N), out_dtype)]
    scratch = [pltpu.VMEM((tm, tn) if nk > 1 else (8, 128), F32)]
    if out_split:
        out_specs = [pl.BlockSpec((None, tm, tn), lambda i, j, k: (j, i, 0))]
        out_shape = [jax.ShapeDtypeStruct((out_split, M, tn), out_dtype)]
    if ride is not None:
        in_specs.append(pl.BlockSpec(memory_space=pl.ANY))
        args.append(ride[1])
        out_specs.append(pl.BlockSpec(memory_space=pl.ANY))
        out_shape.append(_ride_shape(*ride))
        scratch += _ride_scratch()
    sem = ("arbitrary",) * 3 if ride is not None else ("parallel", "parallel", "arbitrary")
    res = pl.pallas_call(
        body, name=name, grid=(ni, nj, nk), in_specs=in_specs, out_specs=out_specs, out_shape=out_shape,
        scratch_shapes=scratch,
        compiler_params=pltpu.CompilerParams(dimension_semantics=sem, vmem_limit_bytes=VMEM_LIMIT),
    )(*args)
    return res[0] if ride is None else res


def _rowwise(fn, rows, consts, outs, accs=(), *, name, tm=256):
    S = rows[0][0].shape[0]
    tm = min(tm, S)
    nr, nc, no = len(rows), len(consts), len(outs)

    def body(*refs):
        res = fn(*[r[...] for r in refs[:nr + nc]])
        res = tuple(res) if isinstance(res, (tuple, list)) else (res,)
        orefs, arefs = refs[nr + nc:nr + nc + no], refs[nr + nc + no:]
        for r, v in zip(orefs, res[:no]):
            r[...] = v.astype(r.dtype)
        if arefs:
            @pl.when(pl.program_id(0) == 0)
            def _():
                for r in arefs:
                    r[...] = jnp.zeros_like(r)

            for r, v in zip(arefs, res[no:]):
                r[...] += v

    in_specs = [pl.BlockSpec((tm, w), lambda i, cb=cb: (i, cb)) for (_, w, cb) in rows]
    in_specs += [pl.BlockSpec(c.shape, lambda i: (0, 0)) for c in consts]
    out_specs = [pl.BlockSpec((tm, w), lambda i: (i, 0)) for (w, _) in outs]
    out_specs += [pl.BlockSpec(s, lambda i: (0, 0)) for s in accs]
    out_shape = [jax.ShapeDtypeStruct((S, w), dt) for (w, dt) in outs]
    out_shape += [jax.ShapeDtypeStruct(s, F32) for s in accs]
    res = pl.pallas_call(
        body, name=name, grid=(S // tm,), in_specs=in_specs, out_specs=out_specs, out_shape=out_shape,
        compiler_params=pltpu.CompilerParams(dimension_semantics=("arbitrary",), vmem_limit_bytes=VMEM_LIMIT),
    )(*[r[0] for r in rows], *consts)
    return res


def _rowwise_bwd(f, rows, consts, cts, row_grads, const_grads, *, name, add=None, out_dtypes=None, tm=256):
    nr, nc, nct = len(rows), len(consts), len(cts)
    all_rows = list(rows) + list(cts) + ([add] if add is not None else [])

    def fn(*args):
        nrow = len(all_rows)
        prim = [x.astype(F32) for x in args[:nr]] + [x.astype(F32) for x in args[nrow:]]
        ct = tuple(x.astype(F32) for x in args[nr:nr + nct])
        out, vjp = jax.vjp(f, *prim)
        gs = vjp(ct if isinstance(out, (tuple, list)) else ct[0])
        res = [gs[k] for k in row_grads]
        if add is not None:
            res[0] = res[0] + args[nrow - 1]
        return tuple(res) + tuple(gs[nr + k] for k in const_grads)

    out_dtypes = out_dtypes or [F32] * len(row_grads)
    outs = [(rows[k][1], dt) for k, dt in zip(row_grads, out_dtypes)]
    accs = [consts[k].shape for k in const_grads]
    return _rowwise(fn, all_rows, consts, outs, accs, name=name, tm=tm)


def _rms(x, g, n=None):
    ms = jnp.sum(x * x, axis=-1, keepdims=True) * (1.0 / (n or x.shape[-1]))
    return x * lax.rsqrt(ms + EPS) * g


def _sigmoid(x):
    return 1.0 / (1.0 + jnp.exp(-x))


def _swap_halves(x):
    r = lax.broadcasted_iota(jnp.int32, (DH, DH), 0)
    c = lax.broadcasted_iota(jnp.int32, (DH, DH), 1)
    half = ROPE // 2
    perm = jnp.where(((r < half) & (c == r + half)) | ((r >= half) & (r < ROPE) & (c == r - half)), 1.0, 0.0)
    return lax.dot_general(x, perm.astype(F32), NN, precision=lax.Precision.HIGHEST, preferred_element_type=F32)


def _rope128(x, g128, cs):
    y = _rms(x, g128, n=ROPE)
    return y * cs[:, :DH] + _swap_halves(y) * cs[:, DH:]


def _f_norm(x, g):
    return _rms(x, g)


def _f_norm2(x, g1, g2):
    xn = x * lax.rsqrt(jnp.mean(x * x, axis=-1, keepdims=True) + EPS)
    return xn * g1, xn * g2


def _f_kv1(ckr, cs, g_ckv, g_kr):
    w = g_ckv.shape[-1]
    return _rms(ckr[:, :w], g_ckv), _rope128(ckr[:, w:], g_kr, cs)


def _f_kv2(kv, kr, g_kn):
    ks, vs = [], []
    for h in range(H_MLA):
        ks += [_rms(kv[:, 2 * DH * h:2 * DH * h + DH], g_kn), kr]
        vs.append(kv[:, 2 * DH * h + DH:2 * DH * (h + 1)])
    return jnp.concatenate(ks, axis=1), jnp.concatenate(vs, axis=1)


def _f_q2(q, cs, g_n, g_r):
    out = []
    for h in range(H_MLA):
        out += [_rms(q[:, 2 * DH * h:2 * DH * h + DH], g_n), _rope128(q[:, 2 * DH * h + DH:2 * DH * (h + 1)], g_r, cs)]
    return jnp.concatenate(out, axis=1)


def _f_mix(att, g_att, q_m, g_m, mkv, g_q, g_k):
    mem_w = H_MEM * DH
    heads = []
    for h in range(H_MEM):
        kh = _rms(mkv[:, h * DH:(h + 1) * DH], g_k)
        vh = mkv[:, mem_w + h * DH:mem_w + (h + 1) * DH]
        qh = _rms(q_m[:, h * DH:(h + 1) * DH], g_q)
        s = _bdot(qh, kh, "nt") * (DH ** -0.5)
        p = jnp.exp(s - lax.stop_gradient(jnp.max(s, axis=-1, keepdims=True)))
        p = p / jnp.sum(p, axis=-1, keepdims=True)
        heads.append(_bdot(p, vh, "nn"))
    mo = jnp.concatenate(heads, axis=1)
    return jnp.concatenate([att * (g_att * _sigmoid(g_att)), mo * (g_m * _sigmoid(g_m))], axis=1)


def _split_dot(x, u):
    hi = x.astype(BF16)
    lo = (x - hi.astype(F32)).astype(BF16)
    return (lax.dot_general(hi, u, NN, preferred_element_type=F32)
            + lax.dot_general(lo, u, NN, preferred_element_type=F32))


def _tri(t):
    r = lax.broadcasted_iota(jnp.int32, (t, t), 0)
    c = lax.broadcasted_iota(jnp.int32, (t, t), 1)
    return r, c


def _strict_lower(t):
    r, c = _tri(t)
    return (r > c).astype(BF16)


def _head_blocks_t(x, w, tk, *, name):
    S = x.shape[0]
    H = x.shape[1] // w

    def body(x_ref, o_ref):
        for h in range(H):
            o_ref[h] = x_ref[:, h * w:(h + 1) * w].T

    return pl.pallas_call(
        body, name=name, grid=(S // tk,),
        in_specs=[pl.BlockSpec((tk, H * w), lambda i: (i, 0))],
        out_specs=pl.BlockSpec((H, None, w, tk), lambda i: (0, i, 0, 0)),
        out_shape=jax.ShapeDtypeStruct((H, S // tk, w, tk), x.dtype),
        compiler_params=pltpu.CompilerParams(dimension_semantics=("parallel",), vmem_limit_bytes=VMEM_LIMIT),
    )(x)


def _rows_ahead(tq, tk):
    return lax.broadcasted_iota(jnp.int32, (tq, tk), 0) - lax.broadcasted_iota(jnp.int32, (tq, tk), 1)


EXP_UNDERFLOW = -110.0


def _log_one_minus_beta(zr, scale):
    zs, nz = zr * scale, zr * (-scale)
    return zs, jnp.minimum(nz, 0.0) - jnp.log(1.0 + jnp.exp(jnp.minimum(zs, nz)))


def _sb_fwd(qkv, *, tq, tk, name, ride=None):
    S = qkv.shape[0]
    tq, tk = min(tq, S), min(tk, tq, S)
    nd = tq // tk
    H = H_SB
    scale = DH ** -0.5

    nq = S // tq

    def body(q_ref, k_ref, v_ref, *rest):
        o_ref = rest[1] if ride is not None else rest[0]
        h, i = pl.program_id(0), pl.program_id(1)
        if ride is not None:
            ride_refs = (rest[0], *rest[2:])

            @pl.when((h == 0) & (i == 0))
            def _():
                _ride(ride[0], "start", *ride_refs)

        q = q_ref[...]
        u = _strict_lower(tk)
        ahead = _rows_ahead(tq, tk)

        def block(j, acc, cb, keep):
            off = pl.multiple_of(j * tk, tk)
            k = k_ref[pl.ds(off, tk), :]
            v = v_ref[pl.ds(off, tk), :]
            z, l = _log_one_minus_beta(lax.dot_general(q, k, NT, preferred_element_type=F32), scale)
            if keep is not None:
                l = jnp.where(keep, l, 0.0)
            a = jnp.exp((z + l) + (_split_dot(l, u) + cb))
            if keep is not None:
                a = jnp.where(keep, a, 0.0)
            acc = acc + lax.dot_general(a.astype(BF16), v, NN, preferred_element_type=F32)
            return acc, cb + jnp.sum(l, axis=1, keepdims=True)

        carry = (jnp.zeros((tq, DH), F32), jnp.zeros((tq, 1), F32))
        for t in reversed(range(nd)):
            carry = block(i * nd + t, *carry, ahead > t * tk)
        _, acc, _ = lax.while_loop(
            lambda st: (st[0] < i * nd) & (jnp.max(st[2]) > EXP_UNDERFLOW),
            lambda st: (st[0] + 1, *block(i * nd - 1 - st[0], st[1], st[2], None)), (jnp.int32(0), *carry))
        o_ref[...] = acc

        if ride is not None:
            @pl.when((h == H - 1) & (i == nq - 1))
            def _():
                _ride(ride[0], "wait", *ride_refs)

    in_specs = [pl.BlockSpec((tq, DH), lambda h, i: (i, h)),
                pl.BlockSpec((S, DH), lambda h, i: (0, H + h)),
                pl.BlockSpec((S, DH), lambda h, i: (0, 2 * H + h))]
    out_specs = [pl.BlockSpec((tq, DH), lambda h, i: (i, h))]
    out_shape = [jax.ShapeDtypeStruct((S, H * DH), F32)]
    args, scratch = [qkv, qkv, qkv], []
    if ride is not None:
        in_specs.append(pl.BlockSpec(memory_space=pl.ANY))
        args.append(ride[1])
        out_specs.append(pl.BlockSpec(memory_space=pl.ANY))
        out_shape.append(_ride_shape(*ride))
        scratch = _ride_scratch()
    res = pl.pallas_call(
        body, name=name, grid=(H, nq), in_specs=in_specs, out_specs=out_specs, out_shape=out_shape,
        scratch_shapes=scratch,
        compiler_params=pltpu.CompilerParams(dimension_semantics=("arbitrary", "arbitrary"), vmem_limit_bytes=VMEM_LIMIT),
    )(*args)
    return res[0] if ride is None else res


def _sb_bwd(qkv, o, do, *, tq, tk, name, ride=None):
    S = qkv.shape[0]
    tq, tk = min(tq, S), min(tk, tq, S)
    nd = tq // tk
    H = H_SB
    scale = DH ** -0.5

    nq = S // tq

    def body(q_ref, k_ref, v_ref, o_ref, do_ref, *rest):
        if ride is not None:
            dq_ref, dk_out, dv_out, dk_ref, dv_ref = rest[1], rest[2], rest[3], rest[5], rest[6]
            ride_refs = (rest[0], rest[4], *rest[7:])
        else:
            dq_ref, dk_out, dv_out, dk_ref, dv_ref = rest
        h, i = pl.program_id(0), pl.program_id(1)
        if ride is not None:
            @pl.when((h == 0) & (i == 0))
            def _():
                _ride(ride[0], "start", *ride_refs)

        @pl.when(i == 0)
        def _():
            dk_ref[...] = jnp.zeros_like(dk_ref)
            dv_ref[...] = jnp.zeros_like(dv_ref)

        q = q_ref[...]
        do = do_ref[...]
        do16 = do.astype(BF16)
        dsum = jnp.sum(do16.astype(F32) * o_ref[...], axis=1, keepdims=True)
        u = _strict_lower(tk)
        ahead = _rows_ahead(tq, tk)

        def block(j, dq, cb, ce, keep):
            off = pl.multiple_of(j * tk, tk)
            k = k_ref[pl.ds(off, tk), :]
            v = v_ref[pl.ds(off, tk), :]
            z, l = _log_one_minus_beta(lax.dot_general(q, k, NT, preferred_element_type=F32), scale)
            if keep is not None:
                l = jnp.where(keep, l, 0.0)
            log_beta = z + l
            a = jnp.exp(log_beta + (_split_dot(l, u) + cb))
            if keep is not None:
                a = jnp.where(keep, a, 0.0)
            a16 = a.astype(BF16)
            e = a16.astype(F32) * lax.dot_general(do16, v, NT, preferred_element_type=F32)
            left = dsum - (ce + _split_dot(e, u) + e)
            dz = (e - jnp.exp(log_beta) * (e + left)) * scale
            if keep is not None:
                dz = jnp.where(keep, dz, 0.0)
            dz = dz.astype(BF16)
            dq = dq + lax.dot_general(dz, k, NN, preferred_element_type=F32)
            dk_ref[pl.ds(off, tk), :] += lax.dot_general(dz, q, TN, preferred_element_type=F32)
            dv_ref[pl.ds(off, tk), :] += lax.dot_general(a16, do16, TN, preferred_element_type=F32)
            return dq, cb + jnp.sum(l, axis=1, keepdims=True), ce + jnp.sum(e, axis=1, keepdims=True)

        zero = jnp.zeros((tq, 1), F32)
        carry = (jnp.zeros((tq, DH), F32), zero, zero)
        for t in reversed(range(nd)):
            carry = block(i * nd + t, *carry, ahead > t * tk)
        _, dq, _, _ = lax.while_loop(
            lambda st: (st[0] < i * nd) & (jnp.max(st[2]) > EXP_UNDERFLOW),
            lambda st: (st[0] + 1, *block(i * nd - 1 - st[0], st[1], st[2], st[3], None)), (jnp.int32(0), *carry))
        dq_ref[...] = dq.astype(dq_ref.dtype)

        @pl.when(i == nq - 1)
        def _():
            dk_out[...] = dk_ref[...].astype(dk_out.dtype)
            dv_out[...] = dv_ref[...].astype(dv_out.dtype)

        if ride is not None:
            @pl.when((h == H - 1) & (i == nq - 1))
            def _():
                _ride(ride[0], "wait", *ride_refs)

    blk = pl.BlockSpec((tq, DH), lambda h, i: (i, h))
    whole = pl.BlockSpec((S, DH), lambda h, i: (0, h))
    shp = jax.ShapeDtypeStruct((S, H * DH), BF16)
    in_specs = [blk, pl.BlockSpec((S, DH), lambda h, i: (0, H + h)), pl.BlockSpec((S, DH), lambda h, i: (0, 2 * H + h)),
                blk, blk]
    out_specs, out_shape = [blk, whole, whole], [shp, shp, shp]
    args, scratch = [qkv, qkv, qkv, o, do], [pltpu.VMEM((S, DH), F32), pltpu.VMEM((S, DH), F32)]
    if ride is not None:
        in_specs.append(pl.BlockSpec(memory_space=pl.ANY))
        args.append(ride[1])
        out_specs.append(pl.BlockSpec(memory_space=pl.ANY))
        out_shape.append(_ride_shape(*ride))
        scratch += _ride_scratch()
    return pl.pallas_call(
        body, name=name, grid=(H, nq), in_specs=in_specs, out_specs=out_specs, out_shape=out_shape,
        scratch_shapes=scratch,
        compiler_params=pltpu.CompilerParams(dimension_semantics=("arbitrary", "arbitrary"), vmem_limit_bytes=VMEM_LIMIT),
    )(*args)


def _mla_fwd(q, k, v, *, tq, tk, name):
    S = q.shape[0]
    tq, tk = min(tq, S), min(tk, tq, S)
    nd, nb = tq // tk, S // tk
    H = H_MLA
    scale = MLA_QK ** -0.5
    vt = _head_blocks_t(v, DH, tk, name=name + "_vt")

    def body(q_ref, k_ref, vt_ref, o_ref, lse_ref):
        i = pl.program_id(1)
        qb = q_ref[...]
        behind = lax.broadcasted_iota(jnp.int32, (tk, tq), 1) - lax.broadcasted_iota(jnp.int32, (tk, tq), 0)

        def block(j, m, den, acct, keep):
            off = pl.multiple_of(j * tk, tk)
            st = lax.dot_general(k_ref[pl.ds(off, tk), :], qb, NT, preferred_element_type=F32) * scale
            if keep is not None:
                st = jnp.where(keep, st, -1e30)
            m_new = jnp.maximum(m, jnp.max(st, axis=0, keepdims=True))
            pt = jnp.exp(st - m_new)
            alpha = jnp.exp(m - m_new)
            den = alpha * den + jnp.sum(pt, axis=0, keepdims=True)
            acct = alpha * acct + lax.dot_general(vt_ref[j], pt.astype(BF16), NN, preferred_element_type=F32)
            return m_new, den, acct

        init = (jnp.full((1, tq), -1e30, F32), jnp.zeros((1, tq), F32), jnp.zeros((DH, tq), F32))
        carry = lax.fori_loop(0, i * nd, lambda j, carry: block(j, *carry, None), init)
        for t in range(nd):
            carry = block(i * nd + t, *carry, behind >= t * tk)
        m, den, acct = carry
        o_ref[...] = (acct / den).T
        lse_ref[0] = m + jnp.log(den)

    return pl.pallas_call(
        body, name=name, grid=(H, S // tq),
        in_specs=[pl.BlockSpec((tq, 2 * DH), lambda h, i: (i, h)),
                  pl.BlockSpec((S, 2 * DH), lambda h, i: (0, h)),
                  pl.BlockSpec((None, nb, DH, tk), lambda h, i: (h, 0, 0, 0))],
        out_specs=[pl.BlockSpec((tq, DH), lambda h, i: (i, h)), pl.BlockSpec((1, 1, tq), lambda h, i: (h, 0, i))],
        out_shape=[jax.ShapeDtypeStruct((S, H * DH), F32), jax.ShapeDtypeStruct((H, 1, S), F32)],
        compiler_params=pltpu.CompilerParams(dimension_semantics=("arbitrary", "arbitrary"), vmem_limit_bytes=VMEM_LIMIT),
    )(q, k, vt)


def _mla_bwd(q, k, v, o, do, lse, *, tq, tk, name):
    S = q.shape[0]
    tq, tk = min(tq, S), min(tk, tq, S)
    nd, nb = tq // tk, S // tk
    H = H_MLA
    scale = MLA_QK ** -0.5
    kt = _head_blocks_t(k, 2 * DH, tk, name=name + "_kt")

    def body(q_ref, k_ref, kt_ref, v_ref, o_ref, do_ref, lse_ref, dq_ref, dk_ref, dv_ref):
        i = pl.program_id(1)

        @pl.when(i == 0)
        def _():
            dk_ref[...] = jnp.zeros_like(dk_ref)
            dv_ref[...] = jnp.zeros_like(dv_ref)

        qb = q_ref[...]
        do = do_ref[...]
        do16 = do.astype(BF16)
        dsum = jnp.sum((do * o_ref[...]).T, axis=0, keepdims=True)
        lse = lse_ref[0]
        behind = lax.broadcasted_iota(jnp.int32, (tk, tq), 1) - lax.broadcasted_iota(jnp.int32, (tk, tq), 0)

        def block(j, dqt, keep):
            off = pl.multiple_of(j * tk, tk)
            kb = k_ref[pl.ds(off, tk), :]
            vb = v_ref[pl.ds(off, tk), :]
            st = lax.dot_general(kb, qb, NT, preferred_element_type=F32) * scale
            if keep is not None:
                st = jnp.where(keep, st, -1e30)
            pt = jnp.exp(st - lse)
            dpt = lax.dot_general(vb, do16, NT, preferred_element_type=F32)
            dst = (pt * (dpt - dsum) * scale).astype(BF16)
            dk_ref[pl.ds(off, tk), :] += lax.dot_general(dst, qb, NN, preferred_element_type=F32)
            dv_ref[pl.ds(off, tk), :] += lax.dot_general(pt.astype(BF16), do16, NN, preferred_element_type=F32)
            return dqt + lax.dot_general(kt_ref[j], dst, NN, preferred_element_type=F32)

        dqt = lax.fori_loop(0, i * nd, lambda j, dqt: block(j, dqt, None), jnp.zeros((2 * DH, tq), F32))
        for t in range(nd):
            dqt = block(i * nd + t, dqt, behind >= t * tk)
        dq_ref[...] = dqt.T

    blk = pl.BlockSpec((tq, DH), lambda h, i: (i, h))
    blk2 = pl.BlockSpec((tq, 2 * DH), lambda h, i: (i, h))
    return pl.pallas_call(
        body, name=name, grid=(H, S // tq),
        in_specs=[blk2, pl.BlockSpec((S, 2 * DH), lambda h, i: (0, h)),
                  pl.BlockSpec((None, nb, 2 * DH, tk), lambda h, i: (h, 0, 0, 0)),
                  pl.BlockSpec((S, DH), lambda h, i: (0, h)),
                  blk, blk, pl.BlockSpec((1, 1, tq), lambda h, i: (h, 0, i))],
        out_specs=[blk2, pl.BlockSpec((S, 2 * DH), lambda h, i: (0, h)), pl.BlockSpec((S, DH), lambda h, i: (0, h))],
        out_shape=[jax.ShapeDtypeStruct((S, H * 2 * DH), F32), jax.ShapeDtypeStruct((S, H * 2 * DH), F32),
                   jax.ShapeDtypeStruct((S, H * DH), F32)],
        compiler_params=pltpu.CompilerParams(dimension_semantics=("arbitrary", "arbitrary"), vmem_limit_bytes=VMEM_LIMIT_BIG),
    )(q, k, kt, v, o, do, lse)


def _all_gather(block, *, name):
    R, C = block.shape

    def body(x_ref, out_ref, send_sems, recv_sems, local_sem):
        x, y, c = _mesh_pos()
        me, sibling = (x, y, c), (x, y, 1 - c)
        chips = [(1 - x, y), (x, 1 - y), (1 - x, 1 - y)]

        def slot(px, py, pc):
            return out_ref.at[4 * px + 2 * py + pc]

        def copy(k, blk, to, src=None):
            return pltpu.make_async_remote_copy(
                src_ref=slot(*blk) if src is None else src, dst_ref=slot(*blk),
                send_sem=send_sems.at[k], recv_sem=recv_sems.at[k],
                device_id=to, device_id_type=pl.DeviceIdType.MESH)

        mine = pltpu.make_async_copy(x_ref, slot(*me), local_sem)
        mine.start()
        first = [copy(0, me, sibling, src=x_ref)]
        first += [copy(1 + j, me, (*chip, c), src=x_ref) for j, chip in enumerate(chips)]
        for cp in first:
            cp.start()
        passed = [copy(4 + j, (*chip, c), sibling) for j, chip in enumerate(chips)]
        for j, chip in enumerate(chips):
            copy(1 + j, (*chip, c), me).wait_recv()
            passed[j].start()
        copy(0, sibling, me).wait_recv()
        for j, chip in enumerate(chips):
            copy(4 + j, (*chip, 1 - c), me).wait_recv()
        for cp in first + passed:
            cp.wait_send()
        mine.wait()

    return pl.pallas_call(
        body, name=name,
        out_shape=jax.ShapeDtypeStruct((N_DEV, R, C), block.dtype),
        in_specs=[pl.BlockSpec(memory_space=pl.ANY)], out_specs=pl.BlockSpec(memory_space=pl.ANY),
        scratch_shapes=[pltpu.SemaphoreType.DMA((7,)), pltpu.SemaphoreType.DMA((7,)), pltpu.SemaphoreType.DMA],
    )(block)


def _all_to_all(send, *, name):
    def body(*refs):
        _ride("a2a", "start", *refs)
        _ride("a2a", "wait", *refs)

    return pl.pallas_call(
        body, name=name, out_shape=_ride_shape("a2a", send),
        in_specs=[pl.BlockSpec(memory_space=pl.ANY)], out_specs=pl.BlockSpec(memory_space=pl.ANY),
        scratch_shapes=_ride_scratch(),
    )(send)


def _reduce_adamw(recv, w, m, v, *, name):
    R, C = w.shape
    tr = next(t for t in (128, 64, 32, 16, 8) if R % t == 0)

    def body(g_ref, w_ref, m_ref, v_ref, og_ref, od_ref, om_ref, ov_ref):
        g = g_ref[0].astype(F32)
        for s in range(1, N_DEV):
            g = g + g_ref[s].astype(F32)
        mn = ADAM_B1 * m_ref[...] + (1.0 - ADAM_B1) * g
        vn = ADAM_B2 * v_ref[...] + (1.0 - ADAM_B2) * jnp.square(g)
        m_hat = mn / (1.0 - ADAM_B1 ** ADAM_STEP)
        v_hat = vn / (1.0 - ADAM_B2 ** ADAM_STEP)
        og_ref[...] = g
        od_ref[...] = -ADAM_LR * (m_hat / (jnp.sqrt(v_hat) + ADAM_EPS) + ADAM_WD * w_ref[...])
        om_ref[...] = mn
        ov_ref[...] = vn

    blk = pl.BlockSpec((tr, C), lambda i: (i, 0))
    shp = jax.ShapeDtypeStruct((R, C), F32)
    return pl.pallas_call(
        body, name=name, grid=(R // tr,),
        in_specs=[pl.BlockSpec((N_DEV, tr, C), lambda i: (0, i, 0)), blk, blk, blk],
        out_specs=[blk, blk, blk, blk], out_shape=[shp, shp, shp, shp],
        compiler_params=pltpu.CompilerParams(dimension_semantics=("parallel",), vmem_limit_bytes=VMEM_LIMIT),
    )(recv, w, m, v)


SHARDED = (("a_norm", 1), ("a_w_in", 2), ("a_w_out", 1), ("w_dkv", 0), ("w_ukv", 1), ("b_w_in", 2),
           ("b_w_uq", 2), ("b_w_out", 1), ("w_mem_kv", 1))
SMALL = ("kv_norm", "g_ckv", "g_k_nope", "g_k_rope", "b_norm", "b_g_q_lat", "b_g_q_nope", "b_g_q_rope",
         "mem_norm", "g_mem_q", "g_mem_k")
WEIGHTS = ("a_norm", "a_w_in", "a_w_out", "kv_norm", "w_dkv", "g_ckv", "w_ukv", "g_k_nope", "g_k_rope", "b_norm",
           "b_w_in", "b_g_q_lat", "b_w_uq", "b_g_q_nope", "b_g_q_rope", "b_w_out", "mem_norm", "w_mem_kv",
           "g_mem_q", "g_mem_k")
ROW_MULT = 8
ROW_BLOCK = 128


def _rows_of(n, mult):
    rows = -(-n // LANES)
    return -(-rows // mult) * mult


def _to_rows(flat, mult):
    n = flat.shape[-1]
    rows = _rows_of(n, mult)
    pad = [(0, 0)] * (flat.ndim - 1) + [(0, rows * LANES - n)]
    return jnp.pad(flat, pad).reshape(*flat.shape[:-1], rows, LANES)


def _split8(full, axis):
    shp = full.shape
    t = full.reshape(*shp[:axis], N_DEV, shp[axis] // N_DEV, *shp[axis + 1:])
    return jnp.moveaxis(t, axis, 0).reshape(N_DEV, -1)


def _join8(rows, axis, shard_shape):
    t = rows.reshape(N_DEV, *shard_shape)
    t = jnp.moveaxis(t, 0, axis)
    return t.reshape(*shard_shape[:axis], N_DEV * shard_shape[axis], *shard_shape[axis + 1:])


def _stack_rows(parts, block=ROW_BLOCK):
    rows = sum(p.shape[-2] for p in parts)
    if rows % block:
        parts = list(parts) + [jnp.zeros((*parts[0].shape[:-2], -rows % block, LANES), parts[0].dtype)]
    return jnp.concatenate(parts, axis=-2)


def _pack_local(vals, names, mult):
    return _stack_rows([_to_rows(vals[n].reshape(-1), mult) for n in names])


def _unpack_local(slab, names, shapes, mult):
    out, row = {}, 0
    for n in names:
        size = 1
        for d in shapes[n]:
            size *= d
        rows = _rows_of(size, mult)
        out[n] = slab[row:row + rows].reshape(-1)[:size].reshape(shapes[n])
        row += rows
    return out, row


def kernel(x, mem, positions, a_norm, a_w_in, a_w_out, kv_norm, w_dkv, g_ckv, w_ukv, g_k_nope, g_k_rope, b_norm, b_w_in, b_g_q_lat, b_w_uq, b_g_q_nope, b_g_q_rope, b_w_out, mem_norm, w_mem_kv, g_mem_q, g_mem_k, loss_target, m_a_norm, m_a_w_in, m_a_w_out, m_kv_norm, m_w_dkv, m_g_ckv, m_w_ukv, m_g_k_nope, m_g_k_rope, m_b_norm, m_b_w_in, m_b_g_q_lat, m_b_w_uq, m_b_g_q_nope, m_b_g_q_rope, m_b_w_out, m_mem_norm, m_w_mem_kv, m_g_mem_q, m_g_mem_k, v_a_norm, v_a_w_in, v_a_w_out, v_kv_norm, v_w_dkv, v_g_ckv, v_w_ukv, v_g_k_nope, v_g_k_rope, v_b_norm, v_b_w_in, v_b_g_q_lat, v_b_w_uq, v_b_g_q_nope, v_b_g_q_rope, v_b_w_out, v_mem_norm, v_w_mem_kv, v_g_mem_q, v_g_mem_k):
    wts = dict(a_norm=a_norm, a_w_in=a_w_in, a_w_out=a_w_out, kv_norm=kv_norm, w_dkv=w_dkv, g_ckv=g_ckv, w_ukv=w_ukv,
               g_k_nope=g_k_nope, g_k_rope=g_k_rope, b_norm=b_norm, b_w_in=b_w_in, b_g_q_lat=b_g_q_lat, b_w_uq=b_w_uq,
               b_g_q_nope=b_g_q_nope, b_g_q_rope=b_g_q_rope, b_w_out=b_w_out, mem_norm=mem_norm, w_mem_kv=w_mem_kv,
               g_mem_q=g_mem_q, g_mem_k=g_mem_k)
    mom = dict(a_norm=m_a_norm, a_w_in=m_a_w_in, a_w_out=m_a_w_out, kv_norm=m_kv_norm, w_dkv=m_w_dkv, g_ckv=m_g_ckv,
               w_ukv=m_w_ukv, g_k_nope=m_g_k_nope, g_k_rope=m_g_k_rope, b_norm=m_b_norm, b_w_in=m_b_w_in,
               b_g_q_lat=m_b_g_q_lat, b_w_uq=m_b_w_uq, b_g_q_nope=m_b_g_q_nope, b_g_q_rope=m_b_g_q_rope,
               b_w_out=m_b_w_out, mem_norm=m_mem_norm, w_mem_kv=m_w_mem_kv, g_mem_q=m_g_mem_q, g_mem_k=m_g_mem_k)
    var = dict(a_norm=v_a_norm, a_w_in=v_a_w_in, a_w_out=v_a_w_out, kv_norm=v_kv_norm, w_dkv=v_w_dkv, g_ckv=v_g_ckv,
               w_ukv=v_w_ukv, g_k_nope=v_g_k_nope, g_k_rope=v_g_k_rope, b_norm=v_b_norm, b_w_in=v_b_w_in,
               b_g_q_lat=v_b_g_q_lat, b_w_uq=v_b_w_uq, b_g_q_nope=v_b_g_q_nope, b_g_q_rope=v_b_g_q_rope,
               b_w_out=v_b_w_out, mem_norm=v_mem_norm, w_mem_kv=v_w_mem_kv, g_mem_q=v_g_mem_q, g_mem_k=v_g_mem_k)
    shapes = {n: wts[n].shape for n in WEIGHTS}
    S, D = x.shape[1], x.shape[2]
    xs, ms, tgt = x[0], mem[0], loss_target[0]
    sb_w, mem_w, mla_w = H_SB * DH, H_MEM * DH, H_MLA * DH
    q_lora, kv_lora = b_g_q_lat.shape[-1], g_ckv.shape[-1]

    def pieces_of(names):
        return [_to_rows(lax.bitcast_convert_type(a_norm.reshape(-1), BF16).reshape(-1), 16) if n == "a_norm"
                else _to_rows(wts[n].astype(BF16).reshape(-1), 16) for n in names]

    def unpack_gathered(gathered, names):
        full, row = {}, 0
        for n in names:
            size = wts[n].size * (2 if n == "a_norm" else 1)
            rows = _rows_of(size, 16)
            flat = gathered[:, row:row + rows].reshape(N_DEV, -1)[:, :size]
            if n == "a_norm":
                flat = lax.bitcast_convert_type(flat.reshape(N_DEV, -1, 2), F32)
            full[n] = _join8(flat, dict(SHARDED)[n], wts[n].shape)
            row += rows
        return full

    first, second, third = ("a_norm", "a_w_in"), ("a_w_out", "w_dkv", "w_ukv"), ("w_mem_kv", "b_w_in", "b_w_uq", "b_w_out")
    full = unpack_gathered(_all_gather(jnp.concatenate(pieces_of(first), axis=0), name="gather_first"), first)
    g_a, w_a_in = full["a_norm"], full["a_w_in"][0]

    row2 = lambda g: g.reshape(1, -1)
    h0 = _rowwise(_f_norm, [(xs, D, 0)], [g_a], [(D, BF16)], name="a_norm_fwd", tm=512)[0]
    qkv, gathered = _mm(h0, w_a_in[:, :3 * sb_w], "nn", out_dtype=BF16, name="a_in_qkv",
                        ride=("gather", jnp.concatenate(pieces_of(second), axis=0)))
    full = unpack_gathered(gathered, second)
    pa = _mm(h0, w_a_in[:, 3 * sb_w:], "nn", name="a_in_rest")
    sb, gathered = _sb_fwd(qkv, tq=512, tk=256, name="sb_fwd",
                           ride=("gather", jnp.concatenate(pieces_of(third), axis=0)))
    full.update(unpack_gathered(gathered, third))
    w_a_out = full["a_w_out"][0]
    w_dkv_p = jnp.pad(full["w_dkv"], ((0, 0), (0, ROPE)))
    w_ukv_f = full["w_ukv"]
    wb = full["b_w_in"][0]
    w_b_in = jnp.concatenate([wb[:, q_lora:q_lora + mla_w], wb[:, :q_lora], wb[:, q_lora + mla_w:]], axis=1)
    w_uq_p = jnp.pad(full["b_w_uq"][0].reshape(q_lora, H_MLA, MLA_QK),
                     ((0, 0), (0, 0), (0, 2 * DH - MLA_QK))).reshape(q_lora, H_MLA * 2 * DH)
    w_b_out = full["b_w_out"][0]
    w_mem = full["w_mem_kv"]

    pad128 = lambda g: jnp.pad(g.reshape(1, -1), ((0, 0), (0, DH - ROPE)))
    g_kr, g_qr = pad128(g_k_rope), pad128(b_g_q_rope[0])
    g_kv, g_b, g_c, g_kn = row2(kv_norm), row2(b_norm[0]), row2(g_ckv), row2(g_k_nope)
    g_ql, g_qn = row2(b_g_q_lat[0]), row2(b_g_q_nope[0])

    inv_freq = jnp.power(ROPE_THETA, -jnp.arange(0, ROPE, 2, dtype=F32) / ROPE)
    ang = positions[0].astype(F32)[:, None] * inv_freq
    z64 = jnp.zeros((S, DH - ROPE), F32)
    cs = jnp.concatenate([jnp.cos(ang), jnp.cos(ang), z64, -jnp.sin(ang), jnp.sin(ang), z64], axis=1)

    mn, mkv = [], []
    for l in range(2):
        mn.append(_rowwise(_f_norm, [(ms, D, 0)], [row2(mem_norm[l])], [(D, BF16)], name=f"mem_norm{l}")[0])
        mkv.append(_mm(mn[l], w_mem[l], "nn", name=f"mem_kv{l}"))
    g_mq = [row2(g_mem_q[l]) for l in range(2)]
    g_mk = [row2(g_mem_k[l]) for l in range(2)]

    mix_a_rows = [(sb, sb_w, 0), (pa, sb_w, 0), (pa, mem_w, sb_w // mem_w), (pa, mem_w, sb_w // mem_w + 1)]
    mixed_a = _rowwise(_f_mix, mix_a_rows, [mkv[0], g_mq[0], g_mk[0]], [(sb_w + mem_w, BF16)], name="a_mix_fwd")[0]
    x1 = _mm(mixed_a, w_a_out, "nn", add=xs, name="a_out")

    hk, hb = _rowwise(_f_norm2, [(x1, D, 0)], [g_kv, g_b], [(D, BF16), (D, BF16)], name="b_norm_fwd", tm=512)
    ckr = _mm(hk, w_dkv_p, "nn", name="kv_down")
    cn, kr = _rowwise(_f_kv1, [(ckr, kv_lora + DH, 0), (cs, 2 * DH, 0)], [g_c, g_kr],
                      [(kv_lora, BF16), (DH, F32)], name="kv1_fwd", tm=512)
    kvu = _mm(cn, w_ukv_f, "nn", name="kv_up")
    k2, v2 = _rowwise(_f_kv2, [(kvu, H_MLA * 2 * DH, 0), (kr, DH, 0)], [g_kn],
                      [(H_MLA * 2 * DH, BF16), (mla_w, BF16)], name="kv2_fwd")
    pb = _mm(hb, w_b_in, "nn", name="b_in")
    ql = _rowwise(_f_norm, [(pb, q_lora, mla_w // q_lora)], [g_ql], [(q_lora, BF16)], name="q_lat_fwd", tm=512)[0]
    qraw = _mm(ql, w_uq_p, "nn", name="q_up")
    q2 = _rowwise(_f_q2, [(qraw, H_MLA * 2 * DH, 0), (cs, 2 * DH, 0)], [g_qn, g_qr],
                  [(H_MLA * 2 * DH, BF16)], name="q2_fwd")[0]
    att, lse = _mla_fwd(q2, k2, v2, tq=1024, tk=1024, name="mla_fwd")
    cb = (mla_w + q_lora) // mem_w
    mix_b_rows = [(att, mla_w, 0), (pb, mla_w, 0), (pb, mem_w, cb), (pb, mem_w, cb + 1)]
    mixed_b = _rowwise(_f_mix, mix_b_rows, [mkv[1], g_mq[1], g_mk[1]], [(mla_w + mem_w, BF16)], name="b_mix_fwd")[0]
    y = _mm(mixed_b, w_b_out, "nn", add=x1, name="b_out")

    def loss_fn(yb, tb):
        err = yb - tb
        part = 0.5 * jnp.sum(jnp.sum(err * err, axis=-1, keepdims=True) * (1.0 / D))
        return err * (1.0 / D), jnp.full((1, DH), part, F32)

    dy, loss_part = _rowwise(loss_fn, [(y, D, 0), (tgt, D, 0)], [], [(D, F32)], [(1, DH)], name="loss", tm=512)

    gr = {}
    d_mixed_b = _mm(dy, w_b_out, "nt", name="b_out_dx")
    gr["b_w_out"] = _mm(mixed_b, dy, "tn", out_dtype=BF16, name="b_out_dw")[None]
    d_att, d_gmla, d_qm_b, d_gm_b, d_mkv1, d_gq1, d_gk1 = _rowwise_bwd(
        _f_mix, mix_b_rows, [mkv[1], g_mq[1], g_mk[1]], [(d_mixed_b, mla_w + mem_w, 0)],
        [0, 1, 2, 3], [0, 1, 2], out_dtypes=[F32, BF16, BF16, BF16], name="b_mix_bwd")
    dq2, dk2, dv2 = _mla_bwd(q2, k2, v2, att, d_att, lse, tq=512, tk=512, name="mla_bwd")
    d_qraw, d_gqn, d_gqr = _rowwise_bwd(
        _f_q2, [(qraw, H_MLA * 2 * DH, 0), (cs, 2 * DH, 0)], [g_qn, g_qr], [(dq2, H_MLA * 2 * DH, 0)],
        [0], [0, 1], out_dtypes=[BF16], name="q2_bwd")
    d_ql = _mm(d_qraw, w_uq_p, "nt", name="q_up_dx")
    d_wuq = _mm(ql, d_qraw, "tn", out_dtype=BF16, name="q_up_dw")
    gr["b_w_uq"] = d_wuq.reshape(q_lora, H_MLA, 2 * DH)[:, :, :MLA_QK].reshape(1, q_lora, H_MLA * MLA_QK)
    d_qlat, d_gql = _rowwise_bwd(_f_norm, [(pb, q_lora, mla_w // q_lora)], [g_ql], [(d_ql, q_lora, 0)],
                                 [0], [0], out_dtypes=[BF16], name="q_lat_bwd", tm=512)
    d_pb = jnp.concatenate([d_gmla, d_qlat, d_qm_b, d_gm_b], axis=1)
    d_hb = _mm(d_pb, w_b_in, "nt", name="b_in_dx")
    d_wbin = _mm(hb, d_pb, "tn", out_dtype=BF16, name="b_in_dw")
    gr["b_w_in"] = jnp.concatenate([d_wbin[:, mla_w:mla_w + q_lora], d_wbin[:, :mla_w], d_wbin[:, mla_w + q_lora:]],
                                   axis=1)[None]
    d_kvu, d_kr, d_gkn = _rowwise_bwd(
        _f_kv2, [(kvu, H_MLA * 2 * DH, 0), (kr, DH, 0)], [g_kn], [(dk2, H_MLA * 2 * DH, 0), (dv2, mla_w, 0)],
        [0, 1], [0], out_dtypes=[BF16, F32], name="kv2_bwd")
    d_cn = _mm(d_kvu, w_ukv_f, "nt", name="kv_up_dx")
    gr["w_ukv"] = _mm(cn, d_kvu, "tn", out_dtype=BF16, name="kv_up_dw")
    d_ckr, d_gc, d_gkr = _rowwise_bwd(
        _f_kv1, [(ckr, kv_lora + DH, 0), (cs, 2 * DH, 0)], [g_c, g_kr], [(d_cn, kv_lora, 0), (d_kr, DH, 0)],
        [0], [0, 1], out_dtypes=[BF16], name="kv1_bwd", tm=512)
    d_hk = _mm(d_ckr, w_dkv_p, "nt", name="kv_down_dx")
    gr["w_dkv"] = _mm(hk, d_ckr, "tn", out_dtype=BF16, name="kv_down_dw")[:, :kv_lora + ROPE]
    d_x1, d_gkv, d_gb = _rowwise_bwd(_f_norm2, [(x1, D, 0)], [g_kv, g_b], [(d_hk, D, 0), (d_hb, D, 0)],
                                     [0], [0, 1], add=(dy, D, 0), name="b_norm_bwd")
    d_mixed_a = _mm(d_x1, w_a_out, "nt", name="a_out_dx")
    gr["a_w_out"] = _mm(mixed_a, d_x1, "tn", out_dtype=BF16, name="a_out_dw")[None]
    d_sb, d_gsb, d_qm_a, d_gm_a, d_mkv0, d_gq0, d_gk0 = _rowwise_bwd(
        _f_mix, mix_a_rows, [mkv[0], g_mq[0], g_mk[0]], [(d_mixed_a, sb_w + mem_w, 0)],
        [0, 1, 2, 3], [0, 1, 2], out_dtypes=[F32, BF16, BF16, BF16], name="a_mix_bwd")
    d_wmem, d_mnorm = [], []
    for l, d_mkv in enumerate((d_mkv0, d_mkv1)):
        d_mn = _mm(d_mkv, w_mem[l], "nt", name=f"mem_kv_dx{l}")
        d_wmem.append(_mm(mn[l], d_mkv, "tn", out_dtype=BF16, name=f"mem_kv_dw{l}"))
        d_mnorm.append(_rowwise_bwd(_f_norm, [(ms, D, 0)], [row2(mem_norm[l])], [(d_mn, D, 0)], [], [0],
                                    name=f"mem_norm_bwd{l}")[0])
    gr["w_mem_kv"] = jnp.stack(d_wmem)

    mid = tuple(n for n, _ in SHARDED[2:])
    send_mid = _stack_rows([_to_rows(_split8(gr[n], ax), 16) for n, ax in SHARDED[2:]])
    dq, dk, dv, recv_mid = _sb_bwd(qkv, sb, d_sb, tq=512, tk=256, name="sb_bwd", ride=("a2a", send_mid))
    d_pa = jnp.concatenate([dq, dk, dv, d_gsb, d_qm_a, d_gm_a], axis=1)
    send_ain = _mm(h0, d_pa, "tn", out_dtype=BF16, out_split=N_DEV, name="a_in_dw")
    d_h0, recv_ain = _mm(d_pa, w_a_in, "nt", name="a_in_dx", ride=("a2a", send_ain))
    grad_x, d_ga = _rowwise_bwd(_f_norm, [(xs, D, 0)], [g_a], [(d_h0, D, 0)], [0], [0], add=(d_x1, D, 0),
                                name="a_norm_bwd")
    gr["mem_norm"] = jnp.concatenate(d_mnorm, axis=0)
    gr["g_mem_q"] = jnp.concatenate([d_gq0, d_gq1], axis=0)
    gr["g_mem_k"] = jnp.concatenate([d_gk0, d_gk1], axis=0)
    gr["kv_norm"], gr["b_norm"], gr["g_ckv"], gr["g_k_nope"] = d_gkv, d_gb, d_gc, d_gkn
    gr["g_k_rope"], gr["b_g_q_rope"] = d_gkr[:, :ROPE], d_gqr[:, :ROPE]
    gr["b_g_q_lat"], gr["b_g_q_nope"] = d_gql, d_gqn
    last = ("a_norm",) + SMALL
    shared = jnp.concatenate([gr[n].reshape(-1) for n in SMALL] + [loss_part[0, :1]])
    flat = jnp.concatenate([_split8(d_ga, 1), jnp.broadcast_to(shared[None], (N_DEV, shared.size))], axis=1)
    recv_last = _all_to_all(_to_rows(flat, ROW_MULT), name="exchange_small")

    out = [{}, {}, {}, {}]
    slabs = _reduce_adamw(recv_mid, *[_pack_local(t, mid, 16) for t in (wts, mom, var)], name="reduce_adamw_mid")
    for o, slab in zip(out, slabs):
        o.update(_unpack_local(slab, mid, shapes, 16)[0])
    zero = jnp.zeros((1,), F32)
    slabs = _reduce_adamw(recv_last, *[_to_rows(jnp.concatenate([t[n].reshape(-1) for n in last] + [zero]), ROW_MULT)
                                       for t in (wts, mom, var)], name="reduce_adamw_last")
    for o, slab in zip(out, slabs):
        vec, at = slab.reshape(-1), 0
        for n in last:
            o[n] = vec[at:at + wts[n].size].reshape(shapes[n])
            at += wts[n].size
    loss = slabs[0].reshape(-1)[at]
    slabs = _reduce_adamw(recv_ain, *[t["a_w_in"][0] for t in (wts, mom, var)], name="reduce_adamw_ain")
    for o, slab in zip(out, slabs):
        o["a_w_in"] = slab[None]
    return (loss, grad_x[None], *[o[n] for o in out for n in WEIGHTS])
```

```python
import functools

import jax
import jax.numpy as jnp
from jax import lax
from jax.experimental import pallas as pl
from jax.experimental.pallas import tpu as pltpu

F32, BF16 = jnp.float32, jnp.bfloat16

N_DEV = 8
DH = 128
H_SB, H_MEM, H_MLA = 12, 4, 12
ROPE = 64
MLA_QK = DH + ROPE
EPS = 1e-6
ROPE_THETA = 10000.0
ADAM_LR, ADAM_B1, ADAM_B2, ADAM_EPS, ADAM_WD, ADAM_STEP = 0.001, 0.9, 0.999, 1e-08, 0.01, 10

LANES = 1024
VMEM_LIMIT = 48 * 1024 * 1024
VMEM_LIMIT_BIG = 56 * 1024 * 1024

NN = (((1,), (0,)), ((), ()))
NT = (((1,), (1,)), ((), ()))
TN = (((0,), (0,)), ((), ()))
_DIMS = {"nn": NN, "nt": NT, "tn": TN}


def _dot16(a, b, dims):
    return lax.dot_general(a.astype(BF16), b.astype(BF16), _DIMS[dims], preferred_element_type=F32)


@functools.partial(jax.custom_vjp, nondiff_argnums=(2,))
def _bdot(a, b, dims):
    return _dot16(a, b, dims)


def _bdot_fwd(a, b, dims):
    return _dot16(a, b, dims), (a, b)


def _bdot_bwd(dims, res, g):
    a, b = res
    if dims == "nn":
        return _dot16(g, b, "nt"), _dot16(a, g, "tn")
    return _dot16(g, b, "nn"), _dot16(g, a, "tn")


_bdot.defvjp(_bdot_fwd, _bdot_bwd)


def _tile(n, pref):
    if n <= pref:
        return n
    t = (pref // 128) * 128
    while n % t:
        t -= 128
    return t


def _mesh_pos():
    return lax.axis_index("x"), lax.axis_index("y"), lax.axis_index("c")


def _ride_shape(kind, src):
    return jax.ShapeDtypeStruct((N_DEV, *src.shape) if kind == "gather" else src.shape, src.dtype)


def _ride_scratch():
    return [pltpu.SemaphoreType.DMA((N_DEV - 1,)), pltpu.SemaphoreType.DMA((N_DEV - 1,)), pltpu.SemaphoreType.DMA]


def _ride(kind, phase, s_ref, r_ref, send_sems, recv_sems, local_sem):
    x, y, c = _mesh_pos()
    me = 4 * x + 2 * y + c
    src = (lambda lin: s_ref) if kind == "gather" else (lambda lin: s_ref.at[lin])
    local = pltpu.make_async_copy(src(me), r_ref.at[me], local_sem)
    if phase == "start":
        local.start()
    for k in range(1, N_DEV):
        p = (1 - x if k & 4 else x, 1 - y if k & 2 else y, 1 - c if k & 1 else c)
        lin = 4 * p[0] + 2 * p[1] + p[2]
        cp = pltpu.make_async_remote_copy(
            src_ref=src(lin), dst_ref=r_ref.at[me] if phase == "start" else r_ref.at[lin],
            send_sem=send_sems.at[k - 1], recv_sem=recv_sems.at[k - 1],
            device_id=p, device_id_type=pl.DeviceIdType.MESH)
        if phase == "start":
            cp.start()
        else:
            cp.wait_recv()
            cp.wait_send()
    if phase == "wait":
        local.wait()


def _mm(a, b, dims, *, name, out_dtype=F32, add=None, ride=None, out_split=None, tm=1024, tn=1024, tk=2048):
    if dims == "tn":
        (K, M), (_, N) = a.shape, b.shape
    elif dims == "nt":
        (M, K), (N, _) = a.shape, b.shape
    else:
        (M, K), (_, N) = a.shape, b.shape
    tm, tk = _tile(M, tm), _tile(K, tk)
    tn = N // out_split if out_split else _tile(N, tn)
    ni, nj, nk = M // tm, N // tn, K // tk
    n_in = 2 + (add is not None) + (ride is not None)

    def body(*refs):
        a_ref, b_ref, o_ref = refs[0], refs[1], refs[n_in]
        acc_ref = refs[n_in + 1 + (ride is not None)]
        i, j, k = pl.program_id(0), pl.program_id(1), pl.program_id(2)
        if ride is not None:
            ride_refs = (refs[n_in - 1], refs[n_in + 1], *refs[-3:])

            @pl.when((i == 0) & (j == 0) & (k == 0))
            def _():
                _ride(ride[0], "start", *ride_refs)

        if nk == 1:
            r = _dot16(a_ref[...], b_ref[...], dims)
            o_ref[...] = (r if add is None else r + refs[2][...]).astype(o_ref.dtype)
        else:
            @pl.when(k == 0)
            def _():
                acc_ref[...] = jnp.zeros_like(acc_ref)

            acc_ref[...] += _dot16(a_ref[...], b_ref[...], dims)

            @pl.when(k == nk - 1)
            def _():
                r = acc_ref[...]
                if add is not None:
                    r = r + refs[2][...]
                o_ref[...] = r.astype(o_ref.dtype)

        if ride is not None:
            @pl.when((i == ni - 1) & (j == nj - 1) & (k == nk - 1))
            def _():
                _ride(ride[0], "wait", *ride_refs)

    a_spec = pl.BlockSpec((tk, tm), lambda i, j, k: (k, i)) if dims == "tn" else pl.BlockSpec((tm, tk), lambda i, j, k: (i, k))
    b_spec = pl.BlockSpec((tn, tk), lambda i, j, k: (j, k)) if dims == "nt" else pl.BlockSpec((tk, tn), lambda i, j, k: (k, j))
    o_spec = pl.BlockSpec((tm, tn), lambda i, j, k: (i, j))
    in_specs, args = [a_spec, b_spec], [a, b]
    if add is not None:
        in_specs.append(o_spec)
        args.append(add)
    out_specs, out_shape = [o_spec], [jax.ShapeDtypeStruct((M, N), out_dtype)]
    scratch = [pltpu.VMEM((tm, tn) if nk > 1 else (8, 128), F32)]
    if out_split:
        out_specs = [pl.BlockSpec((None, tm, tn), lambda i, j, k: (j, i, 0))]
        out_shape = [jax.ShapeDtypeStruct((out_split, M, tn), out_dtype)]
    if ride is not None:
        in_specs.append(pl.BlockSpec(memory_space=pl.ANY))
        args.append(ride[1])
        out_specs.append(pl.BlockSpec(memory_space=pl.ANY))
        out_shape.append(_ride_shape(*ride))
        scratch += _ride_scratch()
    sem = ("arbitrary",) * 3 if ride is not None else ("parallel", "parallel", "arbitrary")
    res = pl.pallas_call(
        body, name=name, grid=(ni, nj, nk), in_specs=in_specs, out_specs=out_specs, out_shape=out_shape,
        scratch_shapes=scratch,
        compiler_params=pltpu.CompilerParams(dimension_semantics=sem, vmem_limit_bytes=VMEM_LIMIT),
    )(*args)
    return res[0] if ride is None else res


def _rowwise(fn, rows, consts, outs, accs=(), *, name, tm=256):
    S = rows[0][0].shape[0]
    tm = min(tm, S)
    nr, nc, no = len(rows), len(consts), len(outs)

    def body(*refs):
        res = fn(*[r[...] for r in refs[:nr + nc]])
        res = tuple(res) if isinstance(res, (tuple, list)) else (res,)
        orefs, arefs = refs[nr + nc:nr + nc + no], refs[nr + nc + no:]
        for r, v in zip(orefs, res[:no]):
            r[...] = v.astype(r.dtype)
        if arefs:
            @pl.when(pl.program_id(0) == 0)
            def _():
                for r in arefs:
                    r[...] = jnp.zeros_like(r)

            for r, v in zip(arefs, res[no:]):
                r[...] += v

    in_specs = [pl.BlockSpec((tm, w), lambda i, cb=cb: (i, cb)) for (_, w, cb) in rows]
    in_specs += [pl.BlockSpec(c.shape, lambda i: (0, 0)) for c in consts]
    out_specs = [pl.BlockSpec((tm, w), lambda i: (i, 0)) for (w, _) in outs]
    out_specs += [pl.BlockSpec(s, lambda i: (0, 0)) for s in accs]
    out_shape = [jax.ShapeDtypeStruct((S, w), dt) for (w, dt) in outs]
    out_shape += [jax.ShapeDtypeStruct(s, F32) for s in accs]
    res = pl.pallas_call(
        body, name=name, grid=(S // tm,), in_specs=in_specs, out_specs=out_specs, out_shape=out_shape,
        compiler_params=pltpu.CompilerParams(dimension_semantics=("arbitrary",), vmem_limit_bytes=VMEM_LIMIT),
    )(*[r[0] for r in rows], *consts)
    return res


def _rowwise_bwd(f, rows, consts, cts, row_grads, const_grads, *, name, add=None, out_dtypes=None, tm=256):
    nr, nc, nct = len(rows), len(consts), len(cts)
    all_rows = list(rows) + list(cts) + ([add] if add is not None else [])

    def fn(*args):
        nrow = len(all_rows)
        prim = [x.astype(F32) for x in args[:nr]] + [x.astype(F32) for x in args[nrow:]]
        ct = tuple(x.astype(F32) for x in args[nr:nr + nct])
        out, vjp = jax.vjp(f, *prim)
        gs = vjp(ct if isinstance(out, (tuple, list)) else ct[0])
        res = [gs[k] for k in row_grads]
        if add is not None:
            res[0] = res[0] + args[nrow - 1]
        return tuple(res) + tuple(gs[nr + k] for k in const_grads)

    out_dtypes = out_dtypes or [F32] * len(row_grads)
    outs = [(rows[k][1], dt) for k, dt in zip(row_grads, out_dtypes)]
    accs = [consts[k].shape for k in const_grads]
    return _rowwise(fn, all_rows, consts, outs, accs, name=name, tm=tm)


def _rms(x, g, n=None):
    ms = jnp.sum(x * x, axis=-1, keepdims=True) * (1.0 / (n or x.shape[-1]))
    return x * lax.rsqrt(ms + EPS) * g


def _sigmoid(x):
    return 1.0 / (1.0 + jnp.exp(-x))


def _swap_halves_exact(x):
    r = lax.broadcasted_iota(jnp.int32, (DH, DH), 0)
    c = lax.broadcasted_iota(jnp.int32, (DH, DH), 1)
    half = ROPE // 2
    perm = (((r < half) & (c == r + half)) | ((r >= half) & (r < ROPE) & (c == r - half))).astype(BF16)
    hi = x.astype(BF16)
    rest = x - hi.astype(F32)
    mid = rest.astype(BF16)
    lo = (rest - mid.astype(F32)).astype(BF16)
    return sum(lax.dot_general(p, perm, NN, preferred_element_type=F32) for p in (hi, mid, lo))


_swap_halves = jax.custom_vjp(_swap_halves_exact)
_swap_halves.defvjp(lambda x: (_swap_halves_exact(x), None), lambda _, g: (_swap_halves_exact(g),))


def _rope128(x, g128, cs):
    y = _rms(x, g128, n=ROPE)
    return y * cs[:, :DH] + _swap_halves(y) * cs[:, DH:]


def _f_norm(x, g):
    return _rms(x, g)


def _f_norm2(x, g1, g2):
    xn = x * lax.rsqrt(jnp.mean(x * x, axis=-1, keepdims=True) + EPS)
    return xn * g1, xn * g2


def _f_kv1(ckr, cs, g_ckv, g_kr):
    w = g_ckv.shape[-1]
    return _rms(ckr[:, :w], g_ckv), _rope128(ckr[:, w:], g_kr, cs)


def _f_kv2(kv, kr, g_kn):
    ks, vs = [], []
    for h in range(H_MLA):
        ks += [_rms(kv[:, 2 * DH * h:2 * DH * h + DH], g_kn), kr]
        vs.append(kv[:, 2 * DH * h + DH:2 * DH * (h + 1)])
    return jnp.concatenate(ks, axis=1), jnp.concatenate(vs, axis=1)


def _f_q2(q, cs, g_n, g_r):
    out = []
    for h in range(H_MLA):
        out += [_rms(q[:, 2 * DH * h:2 * DH * h + DH], g_n), _rope128(q[:, 2 * DH * h + DH:2 * DH * (h + 1)], g_r, cs)]
    return jnp.concatenate(out, axis=1)


def _f_mix(att, g_att, q_m, g_m, mkv, g_q, g_k):
    mem_w = H_MEM * DH
    heads = []
    for h in range(H_MEM):
        kh = _rms(mkv[:, h * DH:(h + 1) * DH], g_k)
        vh = mkv[:, mem_w + h * DH:mem_w + (h + 1) * DH]
        qh = _rms(q_m[:, h * DH:(h + 1) * DH], g_q)
        s = _bdot(qh, kh, "nt") * (DH ** -0.5)
        p = jnp.exp(s - lax.stop_gradient(jnp.max(s, axis=-1, keepdims=True)))
        p = p / jnp.sum(p, axis=-1, keepdims=True)
        heads.append(_bdot(p, vh, "nn"))
    mo = jnp.concatenate(heads, axis=1)
    return jnp.concatenate([att * (g_att * _sigmoid(g_att)), mo * (g_m * _sigmoid(g_m))], axis=1)


def _split_dot(x, u):
    hi = x.astype(BF16)
    lo = (x - hi.astype(F32)).astype(BF16)
    return (lax.dot_general(hi, u, NN, preferred_element_type=F32)
            + lax.dot_general(lo, u, NN, preferred_element_type=F32))


def _tri(t):
    r = lax.broadcasted_iota(jnp.int32, (t, t), 0)
    c = lax.broadcasted_iota(jnp.int32, (t, t), 1)
    return r, c


def _strict_lower(t):
    r, c = _tri(t)
    return (r > c).astype(BF16)


def _head_blocks_t(x, w, tk, *, name):
    S = x.shape[0]
    H = x.shape[1] // w

    def body(x_ref, o_ref):
        for h in range(H):
            o_ref[h] = x_ref[:, h * w:(h + 1) * w].T

    return pl.pallas_call(
        body, name=name, grid=(S // tk,),
        in_specs=[pl.BlockSpec((tk, H * w), lambda i: (i, 0))],
        out_specs=pl.BlockSpec((H, None, w, tk), lambda i: (0, i, 0, 0)),
        out_shape=jax.ShapeDtypeStruct((H, S // tk, w, tk), x.dtype),
        compiler_params=pltpu.CompilerParams(dimension_semantics=("parallel",), vmem_limit_bytes=VMEM_LIMIT),
    )(x)


def _rows_ahead(tq, tk):
    return lax.broadcasted_iota(jnp.int32, (tq, tk), 0) - lax.broadcasted_iota(jnp.int32, (tq, tk), 1)


EXP_UNDERFLOW = -110.0


def _log_one_minus_beta(zr, scale):
    zs, nz = zr * scale, zr * (-scale)
    return zs, jnp.minimum(nz, 0.0) - jnp.log(1.0 + jnp.exp(jnp.minimum(zs, nz)))


def _sb_fwd(qkv, *, tq, tk, name, ride=None):
    S = qkv.shape[0]
    tq, tk = min(tq, S), min(tk, tq, S)
    nd = tq // tk
    H = H_SB
    scale = DH ** -0.5

    nq = S // tq

    def body(q_ref, k_ref, v_ref, *rest):
        o_ref = rest[1] if ride is not None else rest[0]
        h, i = pl.program_id(0), pl.program_id(1)
        if ride is not None:
            ride_refs = (rest[0], *rest[2:])

            @pl.when((h == 0) & (i == 0))
            def _():
                _ride(ride[0], "start", *ride_refs)

        q = q_ref[...]
        u = _strict_lower(tk)
        ahead = _rows_ahead(tq, tk)

        def block(j, acc, cb, keep):
            off = pl.multiple_of(j * tk, tk)
            k = k_ref[pl.ds(off, tk), :]
            v = v_ref[pl.ds(off, tk), :]
            z, l = _log_one_minus_beta(lax.dot_general(q, k, NT, preferred_element_type=F32), scale)
            if keep is not None:
                l = jnp.where(keep, l, 0.0)
            a = jnp.exp((z + l) + (_split_dot(l, u) + cb))
            if keep is not None:
                a = jnp.where(keep, a, 0.0)
            acc = acc + lax.dot_general(a.astype(BF16), v, NN, preferred_element_type=F32)
            return acc, cb + jnp.sum(l, axis=1, keepdims=True)

        carry = (jnp.zeros((tq, DH), F32), jnp.zeros((tq, 1), F32))
        for t in reversed(range(nd)):
            carry = block(i * nd + t, *carry, ahead > t * tk)
        _, acc, _ = lax.while_loop(
            lambda st: (st[0] < i * nd) & (jnp.max(st[2]) > EXP_UNDERFLOW),
            lambda st: (st[0] + 1, *block(i * nd - 1 - st[0], st[1], st[2], None)), (jnp.int32(0), *carry))
        o_ref[...] = acc

        if ride is not None:
            @pl.when((h == H - 1) & (i == nq - 1))
            def _():
                _ride(ride[0], "wait", *ride_refs)

    in_specs = [pl.BlockSpec((tq, DH), lambda h, i: (i, h)),
                pl.BlockSpec((S, DH), lambda h, i: (0, H + h)),
                pl.BlockSpec((S, DH), lambda h, i: (0, 2 * H + h))]
    out_specs = [pl.BlockSpec((tq, DH), lambda h, i: (i, h))]
    out_shape = [jax.ShapeDtypeStruct((S, H * DH), F32)]
    args, scratch = [qkv, qkv, qkv], []
    if ride is not None:
        in_specs.append(pl.BlockSpec(memory_space=pl.ANY))
        args.append(ride[1])
        out_specs.append(pl.BlockSpec(memory_space=pl.ANY))
        out_shape.append(_ride_shape(*ride))
        scratch = _ride_scratch()
    res = pl.pallas_call(
        body, name=name, grid=(H, nq), in_specs=in_specs, out_specs=out_specs, out_shape=out_shape,
        scratch_shapes=scratch,
        compiler_params=pltpu.CompilerParams(dimension_semantics=("arbitrary", "arbitrary"), vmem_limit_bytes=VMEM_LIMIT),
    )(*args)
    return res[0] if ride is None else res


def _sb_bwd(qkv, o, do, *, tq, tk, name, ride=None):
    S = qkv.shape[0]
    tq, tk = min(tq, S), min(tk, tq, S)
    nd = tq // tk
    H = H_SB
    scale = DH ** -0.5

    nq = S // tq

    def body(q_ref, k_ref, v_ref, o_ref, do_ref, *rest):
        if ride is not None:
            dq_ref, dk_out, dv_out, dk_ref, dv_ref = rest[1], rest[2], rest[3], rest[5], rest[6]
            ride_refs = (rest[0], rest[4], *rest[7:])
        else:
            dq_ref, dk_out, dv_out, dk_ref, dv_ref = rest
        h, i = pl.program_id(0), pl.program_id(1)
        if ride is not None:
            @pl.when((h == 0) & (i == 0))
            def _():
                _ride(ride[0], "start", *ride_refs)

        @pl.when(i == 0)
        def _():
            dk_ref[...] = jnp.zeros_like(dk_ref)
            dv_ref[...] = jnp.zeros_like(dv_ref)

        q = q_ref[...]
        do = do_ref[...]
        do16 = do.astype(BF16)
        dsum = jnp.sum(do16.astype(F32) * o_ref[...], axis=1, keepdims=True)
        u = _strict_lower(tk)
        ahead = _rows_ahead(tq, tk)

        def block(j, dq, cb, ce, keep):
            off = pl.multiple_of(j * tk, tk)
            k = k_ref[pl.ds(off, tk), :]
            v = v_ref[pl.ds(off, tk), :]
            z, l = _log_one_minus_beta(lax.dot_general(q, k, NT, preferred_element_type=F32), scale)
            if keep is not None:
                l = jnp.where(keep, l, 0.0)
            log_beta = z + l
            a = jnp.exp(log_beta + (_split_dot(l, u) + cb))
            if keep is not None:
                a = jnp.where(keep, a, 0.0)
            a16 = a.astype(BF16)
            e = a16.astype(F32) * lax.dot_general(do16, v, NT, preferred_element_type=F32)
            left = dsum - (ce + _split_dot(e, u) + e)
            dz = (e - jnp.exp(log_beta) * (e + left)) * scale
            if keep is not None:
                dz = jnp.where(keep, dz, 0.0)
            dz = dz.astype(BF16)
            dq = dq + lax.dot_general(dz, k, NN, preferred_element_type=F32)
            dk_ref[pl.ds(off, tk), :] += lax.dot_general(dz, q, TN, preferred_element_type=F32)
            dv_ref[pl.ds(off, tk), :] += lax.dot_general(a16, do16, TN, preferred_element_type=F32)
            return dq, cb + jnp.sum(l, axis=1, keepdims=True), ce + jnp.sum(e, axis=1, keepdims=True)

        zero = jnp.zeros((tq, 1), F32)
        carry = (jnp.zeros((tq, DH), F32), zero, zero)
        for t in reversed(range(nd)):
            carry = block(i * nd + t, *carry, ahead > t * tk)
        _, dq, _, _ = lax.while_loop(
            lambda st: (st[0] < i * nd) & (jnp.max(st[2]) > EXP_UNDERFLOW),
            lambda st: (st[0] + 1, *block(i * nd - 1 - st[0], st[1], st[2], st[3], None)), (jnp.int32(0), *carry))
        dq_ref[...] = dq.astype(dq_ref.dtype)

        @pl.when(i == nq - 1)
        def _():
            dk_out[...] = dk_ref[...].astype(dk_out.dtype)
            dv_out[...] = dv_ref[...].astype(dv_out.dtype)

        if ride is not None:
            @pl.when((h == H - 1) & (i == nq - 1))
            def _():
                _ride(ride[0], "wait", *ride_refs)

    blk = pl.BlockSpec((tq, DH), lambda h, i: (i, h))
    whole = pl.BlockSpec((S, DH), lambda h, i: (0, h))
    shp = jax.ShapeDtypeStruct((S, H * DH), BF16)
    in_specs = [blk, pl.BlockSpec((S, DH), lambda h, i: (0, H + h)), pl.BlockSpec((S, DH), lambda h, i: (0, 2 * H + h)),
                blk, blk]
    out_specs, out_shape = [blk, whole, whole], [shp, shp, shp]
    args, scratch = [qkv, qkv, qkv, o, do], [pltpu.VMEM((S, DH), F32), pltpu.VMEM((S, DH), F32)]
    if ride is not None:
        in_specs.append(pl.BlockSpec(memory_space=pl.ANY))
        args.append(ride[1])
        out_specs.append(pl.BlockSpec(memory_space=pl.ANY))
        out_shape.append(_ride_shape(*ride))
        scratch += _ride_scratch()
    return pl.pallas_call(
        body, name=name, grid=(H, nq), in_specs=in_specs, out_specs=out_specs, out_shape=out_shape,
        scratch_shapes=scratch,
        compiler_params=pltpu.CompilerParams(dimension_semantics=("arbitrary", "arbitrary"), vmem_limit_bytes=VMEM_LIMIT),
    )(*args)


def _mla_fwd(q, k, v, *, tq, tk, name):
    S = q.shape[0]
    tq, tk = min(tq, S), min(tk, tq, S)
    nd, nb = tq // tk, S // tk
    H = H_MLA
    scale = MLA_QK ** -0.5
    vt = _head_blocks_t(v, DH, tk, name=name + "_vt")

    def body(q_ref, k_ref, vt_ref, o_ref, lse_ref):
        i = pl.program_id(1)
        qb = q_ref[...]
        behind = lax.broadcasted_iota(jnp.int32, (tk, tq), 1) - lax.broadcasted_iota(jnp.int32, (tk, tq), 0)

        def block(j, m, den, acct, keep):
            off = pl.multiple_of(j * tk, tk)
            st = lax.dot_general(k_ref[pl.ds(off, tk), :], qb, NT, preferred_element_type=F32) * scale
            if keep is not None:
                st = jnp.where(keep, st, -1e30)
            m_new = jnp.maximum(m, jnp.max(st, axis=0, keepdims=True))
            pt = jnp.exp(st - m_new)
            alpha = jnp.exp(m - m_new)
            den = alpha * den + jnp.sum(pt, axis=0, keepdims=True)
            acct = alpha * acct + lax.dot_general(vt_ref[j], pt.astype(BF16), NN, preferred_element_type=F32)
            return m_new, den, acct

        init = (jnp.full((1, tq), -1e30, F32), jnp.zeros((1, tq), F32), jnp.zeros((DH, tq), F32))
        carry = lax.fori_loop(0, i * nd, lambda j, carry: block(j, *carry, None), init)
        for t in range(nd):
            carry = block(i * nd + t, *carry, behind >= t * tk)
        m, den, acct = carry
        o_ref[...] = (acct / den).T
        lse_ref[0] = m + jnp.log(den)

    return pl.pallas_call(
        body, name=name, grid=(H, S // tq),
        in_specs=[pl.BlockSpec((tq, 2 * DH), lambda h, i: (i, h)),
                  pl.BlockSpec((S, 2 * DH), lambda h, i: (0, h)),
                  pl.BlockSpec((None, nb, DH, tk), lambda h, i: (h, 0, 0, 0))],
        out_specs=[pl.BlockSpec((tq, DH), lambda h, i: (i, h)), pl.BlockSpec((1, 1, tq), lambda h, i: (h, 0, i))],
        out_shape=[jax.ShapeDtypeStruct((S, H * DH), F32), jax.ShapeDtypeStruct((H, 1, S), F32)],
        compiler_params=pltpu.CompilerParams(dimension_semantics=("arbitrary", "arbitrary"), vmem_limit_bytes=VMEM_LIMIT),
    )(q, k, vt)


def _mla_bwd(q, k, v, o, do, lse, *, tq, tk, name):
    S = q.shape[0]
    tq, tk = min(tq, S), min(tk, tq, S)
    nd, nb = tq // tk, S // tk
    H = H_MLA
    scale = MLA_QK ** -0.5
    kt = _head_blocks_t(k, 2 * DH, tk, name=name + "_kt")

    def body(q_ref, k_ref, kt_ref, v_ref, o_ref, do_ref, lse_ref, dq_ref, dk_ref, dv_ref):
        i = pl.program_id(1)

        @pl.when(i == 0)
        def _():
            dk_ref[...] = jnp.zeros_like(dk_ref)
            dv_ref[...] = jnp.zeros_like(dv_ref)

        qb = q_ref[...]
        do = do_ref[...]
        do16 = do.astype(BF16)
        dsum = jnp.sum((do * o_ref[...]).T, axis=0, keepdims=True)
        lse = lse_ref[0]
        behind = lax.broadcasted_iota(jnp.int32, (tk, tq), 1) - lax.broadcasted_iota(jnp.int32, (tk, tq), 0)

        def block(j, dqt, keep):
            off = pl.multiple_of(j * tk, tk)
            kb = k_ref[pl.ds(off, tk), :]
            vb = v_ref[pl.ds(off, tk), :]
            st = lax.dot_general(kb, qb, NT, preferred_element_type=F32) * scale
            if keep is not None:
                st = jnp.where(keep, st, -1e30)
            pt = jnp.exp(st - lse)
            dpt = lax.dot_general(vb, do16, NT, preferred_element_type=F32)
            dst = (pt * (dpt - dsum) * scale).astype(BF16)
            dk_ref[pl.ds(off, tk), :] += lax.dot_general(dst, qb, NN, preferred_element_type=F32)
            dv_ref[pl.ds(off, tk), :] += lax.dot_general(pt.astype(BF16), do16, NN, preferred_element_type=F32)
            return dqt + lax.dot_general(kt_ref[j], dst, NN, preferred_element_type=F32)

        dqt = lax.fori_loop(0, i * nd, lambda j, dqt: block(j, dqt, None), jnp.zeros((2 * DH, tq), F32))
        for t in range(nd):
            dqt = block(i * nd + t, dqt, behind >= t * tk)
        dq_ref[...] = dqt.T

    blk = pl.BlockSpec((tq, DH), lambda h, i: (i, h))
    blk2 = pl.BlockSpec((tq, 2 * DH), lambda h, i: (i, h))
    return pl.pallas_call(
        body, name=name, grid=(H, S // tq),
        in_specs=[blk2, pl.BlockSpec((S, 2 * DH), lambda h, i: (0, h)),
                  pl.BlockSpec((None, nb, 2 * DH, tk), lambda h, i: (h, 0, 0, 0)),
                  pl.BlockSpec((S, DH), lambda h, i: (0, h)),
                  blk, blk, pl.BlockSpec((1, 1, tq), lambda h, i: (h, 0, i))],
        out_specs=[blk2, pl.BlockSpec((S, 2 * DH), lambda h, i: (0, h)), pl.BlockSpec((S, DH), lambda h, i: (0, h))],
        out_shape=[jax.ShapeDtypeStruct((S, H * 2 * DH), F32), jax.ShapeDtypeStruct((S, H * 2 * DH), F32),
                   jax.ShapeDtypeStruct((S, H * DH), F32)],
        compiler_params=pltpu.CompilerParams(dimension_semantics=("arbitrary", "arbitrary"), vmem_limit_bytes=VMEM_LIMIT_BIG),
    )(q, k, kt, v, o, do, lse)


def _all_gather(block, *, name):
    R, C = block.shape

    def body(x_ref, out_ref, send_sems, recv_sems, local_sem):
        x, y, c = _mesh_pos()
        me, sibling = (x, y, c), (x, y, 1 - c)
        chips = [(1 - x, y), (x, 1 - y), (1 - x, 1 - y)]

        def slot(px, py, pc):
            return out_ref.at[4 * px + 2 * py + pc]

        def copy(k, blk, to, src=None):
            return pltpu.make_async_remote_copy(
                src_ref=slot(*blk) if src is None else src, dst_ref=slot(*blk),
                send_sem=send_sems.at[k], recv_sem=recv_sems.at[k],
                device_id=to, device_id_type=pl.DeviceIdType.MESH)

        mine = pltpu.make_async_copy(x_ref, slot(*me), local_sem)
        mine.start()
        first = [copy(0, me, sibling, src=x_ref)]
        first += [copy(1 + j, me, (*chip, c), src=x_ref) for j, chip in enumerate(chips)]
        for cp in first:
            cp.start()
        passed = [copy(4 + j, (*chip, c), sibling) for j, chip in enumerate(chips)]
        for j, chip in enumerate(chips):
            copy(1 + j, (*chip, c), me).wait_recv()
            passed[j].start()
        copy(0, sibling, me).wait_recv()
        for j, chip in enumerate(chips):
            copy(4 + j, (*chip, 1 - c), me).wait_recv()
        for cp in first + passed:
            cp.wait_send()
        mine.wait()

    return pl.pallas_call(
        body, name=name,
        out_shape=jax.ShapeDtypeStruct((N_DEV, R, C), block.dtype),
        in_specs=[pl.BlockSpec(memory_space=pl.ANY)], out_specs=pl.BlockSpec(memory_space=pl.ANY),
        scratch_shapes=[pltpu.SemaphoreType.DMA((7,)), pltpu.SemaphoreType.DMA((7,)), pltpu.SemaphoreType.DMA],
    )(block)


def _all_to_all(send, *, name):
    def body(*refs):
        _ride("a2a", "start", *refs)
        _ride("a2a", "wait", *refs)

    return pl.pallas_call(
        body, name=name, out_shape=_ride_shape("a2a", send),
        in_specs=[pl.BlockSpec(memory_space=pl.ANY)], out_specs=pl.BlockSpec(memory_space=pl.ANY),
        scratch_shapes=_ride_scratch(),
    )(send)


def _reduce_adamw(recv, w, m, v, *, name):
    R, C = w.shape
    tr = next(t for t in (128, 64, 32, 16, 8) if R % t == 0)

    def body(g_ref, w_ref, m_ref, v_ref, og_ref, od_ref, om_ref, ov_ref):
        g = g_ref[0].astype(F32)
        for s in range(1, N_DEV):
            g = g + g_ref[s].astype(F32)
        mn = ADAM_B1 * m_ref[...] + (1.0 - ADAM_B1) * g
        vn = ADAM_B2 * v_ref[...] + (1.0 - ADAM_B2) * jnp.square(g)
        m_hat = mn / (1.0 - ADAM_B1 ** ADAM_STEP)
        v_hat = vn / (1.0 - ADAM_B2 ** ADAM_STEP)
        og_ref[...] = g
        od_ref[...] = -ADAM_LR * (m_hat / (jnp.sqrt(v_hat) + ADAM_EPS) + ADAM_WD * w_ref[...])
        om_ref[...] = mn
        ov_ref[...] = vn

    blk = pl.BlockSpec((tr, C), lambda i: (i, 0))
    shp = jax.ShapeDtypeStruct((R, C), F32)
    return pl.pallas_call(
        body, name=name, grid=(R // tr,),
        in_specs=[pl.BlockSpec((N_DEV, tr, C), lambda i: (0, i, 0)), blk, blk, blk],
        out_specs=[blk, blk, blk, blk], out_shape=[shp, shp, shp, shp],
        compiler_params=pltpu.CompilerParams(dimension_semantics=("parallel",), vmem_limit_bytes=VMEM_LIMIT),
    )(recv, w, m, v)


SHARDED = (("a_norm", 1), ("a_w_in", 2), ("a_w_out", 1), ("w_dkv", 0), ("w_ukv", 1), ("b_w_in", 2),
           ("b_w_uq", 2), ("b_w_out", 1), ("w_mem_kv", 1))
SMALL = ("kv_norm", "g_ckv", "g_k_nope", "g_k_rope", "b_norm", "b_g_q_lat", "b_g_q_nope", "b_g_q_rope",
         "mem_norm", "g_mem_q", "g_mem_k")
WEIGHTS = ("a_norm", "a_w_in", "a_w_out", "kv_norm", "w_dkv", "g_ckv", "w_ukv", "g_k_nope", "g_k_rope", "b_norm",
           "b_w_in", "b_g_q_lat", "b_w_uq", "b_g_q_nope", "b_g_q_rope", "b_w_out", "mem_norm", "w_mem_kv",
           "g_mem_q", "g_mem_k")
ROW_MULT = 8
ROW_BLOCK = 128


def _rows_of(n, mult):
    rows = -(-n // LANES)
    return -(-rows // mult) * mult


def _to_rows(flat, mult):
    n = flat.shape[-1]
    rows = _rows_of(n, mult)
    pad = [(0, 0)] * (flat.ndim - 1) + [(0, rows * LANES - n)]
    return jnp.pad(flat, pad).reshape(*flat.shape[:-1], rows, LANES)


def _split8(full, axis):
    shp = full.shape
    t = full.reshape(*shp[:axis], N_DEV, shp[axis] // N_DEV, *shp[axis + 1:])
    return jnp.moveaxis(t, axis, 0).reshape(N_DEV, -1)


def _join8(rows, axis, shard_shape):
    t = rows.reshape(N_DEV, *shard_shape)
    t = jnp.moveaxis(t, 0, axis)
    return t.reshape(*shard_shape[:axis], N_DEV * shard_shape[axis], *shard_shape[axis + 1:])


def _stack_rows(parts, block=ROW_BLOCK):
    rows = sum(p.shape[-2] for p in parts)
    if rows % block:
        parts = list(parts) + [jnp.zeros((*parts[0].shape[:-2], -rows % block, LANES), parts[0].dtype)]
    return jnp.concatenate(parts, axis=-2)


def _pack_local(vals, names, mult):
    return _stack_rows([_to_rows(vals[n].reshape(-1), mult) for n in names])


def _unpack_local(slab, names, shapes, mult):
    out, row = {}, 0
    for n in names:
        size = 1
        for d in shapes[n]:
            size *= d
        rows = _rows_of(size, mult)
        out[n] = slab[row:row + rows].reshape(-1)[:size].reshape(shapes[n])
        row += rows
    return out, row


def kernel(x, mem, positions, a_norm, a_w_in, a_w_out, kv_norm, w_dkv, g_ckv, w_ukv, g_k_nope, g_k_rope, b_norm, b_w_in, b_g_q_lat, b_w_uq, b_g_q_nope, b_g_q_rope, b_w_out, mem_norm, w_mem_kv, g_mem_q, g_mem_k, loss_target, m_a_norm, m_a_w_in, m_a_w_out, m_kv_norm, m_w_dkv, m_g_ckv, m_w_ukv, m_g_k_nope, m_g_k_rope, m_b_norm, m_b_w_in, m_b_g_q_lat, m_b_w_uq, m_b_g_q_nope, m_b_g_q_rope, m_b_w_out, m_mem_norm, m_w_mem_kv, m_g_mem_q, m_g_mem_k, v_a_norm, v_a_w_in, v_a_w_out, v_kv_norm, v_w_dkv, v_g_ckv, v_w_ukv, v_g_k_nope, v_g_k_rope, v_b_norm, v_b_w_in, v_b_g_q_lat, v_b_w_uq, v_b_g_q_nope, v_b_g_q_rope, v_b_w_out, v_mem_norm, v_w_mem_kv, v_g_mem_q, v_g_mem_k):
    wts = dict(a_norm=a_norm, a_w_in=a_w_in, a_w_out=a_w_out, kv_norm=kv_norm, w_dkv=w_dkv, g_ckv=g_ckv, w_ukv=w_ukv,
               g_k_nope=g_k_nope, g_k_rope=g_k_rope, b_norm=b_norm, b_w_in=b_w_in, b_g_q_lat=b_g_q_lat, b_w_uq=b_w_uq,
               b_g_q_nope=b_g_q_nope, b_g_q_rope=b_g_q_rope, b_w_out=b_w_out, mem_norm=mem_norm, w_mem_kv=w_mem_kv,
               g_mem_q=g_mem_q, g_mem_k=g_mem_k)
    mom = dict(a_norm=m_a_norm, a_w_in=m_a_w_in, a_w_out=m_a_w_out, kv_norm=m_kv_norm, w_dkv=m_w_dkv, g_ckv=m_g_ckv,
               w_ukv=m_w_ukv, g_k_nope=m_g_k_nope, g_k_rope=m_g_k_rope, b_norm=m_b_norm, b_w_in=m_b_w_in,
               b_g_q_lat=m_b_g_q_lat, b_w_uq=m_b_w_uq, b_g_q_nope=m_b_g_q_nope, b_g_q_rope=m_b_g_q_rope,
               b_w_out=m_b_w_out, mem_norm=m_mem_norm, w_mem_kv=m_w_mem_kv, g_mem_q=m_g_mem_q, g_mem_k=m_g_mem_k)
    var = dict(a_norm=v_a_norm, a_w_in=v_a_w_in, a_w_out=v_a_w_out, kv_norm=v_kv_norm, w_dkv=v_w_dkv, g_ckv=v_g_ckv,
               w_ukv=v_w_ukv, g_k_nope=v_g_k_nope, g_k_rope=v_g_k_rope, b_norm=v_b_norm, b_w_in=v_b_w_in,
               b_g_q_lat=v_b_g_q_lat, b_w_uq=v_b_w_uq, b_g_q_nope=v_b_g_q_nope, b_g_q_rope=v_b_g_q_rope,
               b_w_out=v_b_w_out, mem_norm=v_mem_norm, w_mem_kv=v_w_mem_kv, g_mem_q=v_g_mem_q, g_mem_k=v_g_mem_k)
    shapes = {n: wts[n].shape for n in WEIGHTS}
    S, D = x.shape[1], x.shape[2]
    xs, ms, tgt = x[0], mem[0], loss_target[0]
    sb_w, mem_w, mla_w = H_SB * DH, H_MEM * DH, H_MLA * DH
    q_lora, kv_lora = b_g_q_lat.shape[-1], g_ckv.shape[-1]

    def pieces_of(names):
        return [_to_rows(lax.bitcast_convert_type(a_norm.reshape(-1), BF16).reshape(-1), 16) if n == "a_norm"
                else _to_rows(wts[n].astype(BF16).reshape(-1), 16) for n in names]

    def unpack_gathered(gathered, names):
        full, row = {}, 0
        for n in names:
            size = wts[n].size * (2 if n == "a_norm" else 1)
            rows = _rows_of(size, 16)
            flat = gathered[:, row:row + rows].reshape(N_DEV, -1)[:, :size]
            if n == "a_norm":
                flat = lax.bitcast_convert_type(flat.reshape(N_DEV, -1, 2), F32)
            full[n] = _join8(flat, dict(SHARDED)[n], wts[n].shape)
            row += rows
        return full

    first, second, third = ("a_norm", "a_w_in"), ("a_w_out", "w_dkv", "w_ukv"), ("w_mem_kv", "b_w_in", "b_w_uq", "b_w_out")
    full = unpack_gathered(_all_gather(jnp.concatenate(pieces_of(first), axis=0), name="gather_first"), first)
    g_a, w_a_in = full["a_norm"], full["a_w_in"][0]

    row2 = lambda g: g.reshape(1, -1)
    h0 = _rowwise(_f_norm, [(xs, D, 0)], [g_a], [(D, BF16)], name="a_norm_fwd", tm=512)[0]
    qkv, gathered = _mm(h0, w_a_in[:, :3 * sb_w], "nn", out_dtype=BF16, name="a_in_qkv",
                        ride=("gather", jnp.concatenate(pieces_of(second), axis=0)))
    full = unpack_gathered(gathered, second)
    pa = _mm(h0, w_a_in[:, 3 * sb_w:], "nn", name="a_in_rest")
    sb, gathered = _sb_fwd(qkv, tq=512, tk=256, name="sb_fwd",
                           ride=("gather", jnp.concatenate(pieces_of(third), axis=0)))
    full.update(unpack_gathered(gathered, third))
    w_a_out = full["a_w_out"][0]
    w_dkv_p = jnp.pad(full["w_dkv"], ((0, 0), (0, ROPE)))
    w_ukv_f = full["w_ukv"]
    wb = full["b_w_in"][0]
    w_b_in = jnp.concatenate([wb[:, q_lora:q_lora + mla_w], wb[:, :q_lora], wb[:, q_lora + mla_w:]], axis=1)
    w_uq_p = jnp.pad(full["b_w_uq"][0].reshape(q_lora, H_MLA, MLA_QK),
                     ((0, 0), (0, 0), (0, 2 * DH - MLA_QK))).reshape(q_lora, H_MLA * 2 * DH)
    w_b_out = full["b_w_out"][0]
    w_mem = full["w_mem_kv"]

    pad128 = lambda g: jnp.pad(g.reshape(1, -1), ((0, 0), (0, DH - ROPE)))
    g_kr, g_qr = pad128(g_k_rope), pad128(b_g_q_rope[0])
    g_kv, g_b, g_c, g_kn = row2(kv_norm), row2(b_norm[0]), row2(g_ckv), row2(g_k_nope)
    g_ql, g_qn = row2(b_g_q_lat[0]), row2(b_g_q_nope[0])

    inv_freq = jnp.power(ROPE_THETA, -jnp.arange(0, ROPE, 2, dtype=F32) / ROPE)
    ang = positions[0].astype(F32)[:, None] * inv_freq
    z64 = jnp.zeros((S, DH - ROPE), F32)
    cs = jnp.concatenate([jnp.cos(ang), jnp.cos(ang), z64, -jnp.sin(ang), jnp.sin(ang), z64], axis=1)

    mn, mkv = [], []
    for l in range(2):
        mn.append(_rowwise(_f_norm, [(ms, D, 0)], [row2(mem_norm[l])], [(D, BF16)], name=f"mem_norm{l}")[0])
        mkv.append(_mm(mn[l], w_mem[l], "nn", name=f"mem_kv{l}"))
    g_mq = [row2(g_mem_q[l]) for l in range(2)]
    g_mk = [row2(g_mem_k[l]) for l in range(2)]

    mix_a_rows = [(sb, sb_w, 0), (pa, sb_w, 0), (pa, mem_w, sb_w // mem_w), (pa, mem_w, sb_w // mem_w + 1)]
    mixed_a = _rowwise(_f_mix, mix_a_rows, [mkv[0], g_mq[0], g_mk[0]], [(sb_w + mem_w, BF16)], name="a_mix_fwd")[0]
    x1 = _mm(mixed_a, w_a_out, "nn", add=xs, name="a_out")

    hk, hb = _rowwise(_f_norm2, [(x1, D, 0)], [g_kv, g_b], [(D, BF16), (D, BF16)], name="b_norm_fwd", tm=512)
    ckr = _mm(hk, w_dkv_p, "nn", name="kv_down")
    cn, kr = _rowwise(_f_kv1, [(ckr, kv_lora + DH, 0), (cs, 2 * DH, 0)], [g_c, g_kr],
                      [(kv_lora, BF16), (DH, F32)], name="kv1_fwd", tm=512)
    kvu = _mm(cn, w_ukv_f, "nn", name="kv_up")
    k2, v2 = _rowwise(_f_kv2, [(kvu, H_MLA * 2 * DH, 0), (kr, DH, 0)], [g_kn],
                      [(H_MLA * 2 * DH, BF16), (mla_w, BF16)], name="kv2_fwd")
    pb = _mm(hb, w_b_in, "nn", name="b_in")
    ql = _rowwise(_f_norm, [(pb, q_lora, mla_w // q_lora)], [g_ql], [(q_lora, BF16)], name="q_lat_fwd", tm=512)[0]
    qraw = _mm(ql, w_uq_p, "nn", name="q_up")
    q2 = _rowwise(_f_q2, [(qraw, H_MLA * 2 * DH, 0), (cs, 2 * DH, 0)], [g_qn, g_qr],
                  [(H_MLA * 2 * DH, BF16)], name="q2_fwd")[0]
    att, lse = _mla_fwd(q2, k2, v2, tq=1024, tk=1024, name="mla_fwd")
    cb = (mla_w + q_lora) // mem_w
    mix_b_rows = [(att, mla_w, 0), (pb, mla_w, 0), (pb, mem_w, cb), (pb, mem_w, cb + 1)]
    mixed_b = _rowwise(_f_mix, mix_b_rows, [mkv[1], g_mq[1], g_mk[1]], [(mla_w + mem_w, BF16)], name="b_mix_fwd")[0]
    y = _mm(mixed_b, w_b_out, "nn", add=x1, name="b_out")

    def loss_fn(yb, tb):
        err = yb - tb
        part = 0.5 * jnp.sum(jnp.sum(err * err, axis=-1, keepdims=True) * (1.0 / D))
        return err * (1.0 / D), jnp.full((1, DH), part, F32)

    dy, loss_part = _rowwise(loss_fn, [(y, D, 0), (tgt, D, 0)], [], [(D, F32)], [(1, DH)], name="loss", tm=512)

    gr = {}
    d_mixed_b = _mm(dy, w_b_out, "nt", name="b_out_dx")
    gr["b_w_out"] = _mm(mixed_b, dy, "tn", out_dtype=BF16, name="b_out_dw")[None]
    d_att, d_gmla, d_qm_b, d_gm_b, d_mkv1, d_gq1, d_gk1 = _rowwise_bwd(
        _f_mix, mix_b_rows, [mkv[1], g_mq[1], g_mk[1]], [(d_mixed_b, mla_w + mem_w, 0)],
        [0, 1, 2, 3], [0, 1, 2], out_dtypes=[F32, BF16, BF16, BF16], name="b_mix_bwd")
    dq2, dk2, dv2 = _mla_bwd(q2, k2, v2, att, d_att, lse, tq=512, tk=512, name="mla_bwd")
    d_qraw, d_gqn, d_gqr = _rowwise_bwd(
        _f_q2, [(qraw, H_MLA * 2 * DH, 0), (cs, 2 * DH, 0)], [g_qn, g_qr], [(dq2, H_MLA * 2 * DH, 0)],
        [0], [0, 1], out_dtypes=[BF16], name="q2_bwd")
    d_ql = _mm(d_qraw, w_uq_p, "nt", name="q_up_dx")
    d_wuq = _mm(ql, d_qraw, "tn", out_dtype=BF16, name="q_up_dw")
    gr["b_w_uq"] = d_wuq.reshape(q_lora, H_MLA, 2 * DH)[:, :, :MLA_QK].reshape(1, q_lora, H_MLA * MLA_QK)
    d_qlat, d_gql = _rowwise_bwd(_f_norm, [(pb, q_lora, mla_w // q_lora)], [g_ql], [(d_ql, q_lora, 0)],
                                 [0], [0], out_dtypes=[BF16], name="q_lat_bwd", tm=512)
    d_pb = jnp.concatenate([d_gmla, d_qlat, d_qm_b, d_gm_b], axis=1)
    d_hb = _mm(d_pb, w_b_in, "nt", name="b_in_dx")
    d_wbin = _mm(hb, d_pb, "tn", out_dtype=BF16, name="b_in_dw")
    gr["b_w_in"] = jnp.concatenate([d_wbin[:, mla_w:mla_w + q_lora], d_wbin[:, :mla_w], d_wbin[:, mla_w + q_lora:]],
                                   axis=1)[None]
    d_kvu, d_kr, d_gkn = _rowwise_bwd(
        _f_kv2, [(kvu, H_MLA * 2 * DH, 0), (kr, DH, 0)], [g_kn], [(dk2, H_MLA * 2 * DH, 0), (dv2, mla_w, 0)],
        [0, 1], [0], out_dtypes=[BF16, F32], name="kv2_bwd")
    d_cn = _mm(d_kvu, w_ukv_f, "nt", name="kv_up_dx")
    gr["w_ukv"] = _mm(cn, d_kvu, "tn", out_dtype=BF16, name="kv_up_dw")
    d_ckr, d_gc, d_gkr = _rowwise_bwd(
        _f_kv1, [(ckr, kv_lora + DH, 0), (cs, 2 * DH, 0)], [g_c, g_kr], [(d_cn, kv_lora, 0), (d_kr, DH, 0)],
        [0], [0, 1], out_dtypes=[BF16], name="kv1_bwd", tm=512)
    d_hk = _mm(d_ckr, w_dkv_p, "nt", name="kv_down_dx")
    gr["w_dkv"] = _mm(hk, d_ckr, "tn", out_dtype=BF16, name="kv_down_dw")[:, :kv_lora + ROPE]
    d_x1, d_gkv, d_gb = _rowwise_bwd(_f_norm2, [(x1, D, 0)], [g_kv, g_b], [(d_hk, D, 0), (d_hb, D, 0)],
                                     [0], [0, 1], add=(dy, D, 0), name="b_norm_bwd")
    d_mixed_a = _mm(d_x1, w_a_out, "nt", name="a_out_dx")
    gr["a_w_out"] = _mm(mixed_a, d_x1, "tn", out_dtype=BF16, name="a_out_dw")[None]
    d_sb, d_gsb, d_qm_a, d_gm_a, d_mkv0, d_gq0, d_gk0 = _rowwise_bwd(
        _f_mix, mix_a_rows, [mkv[0], g_mq[0], g_mk[0]], [(d_mixed_a, sb_w + mem_w, 0)],
        [0, 1, 2, 3], [0, 1, 2], out_dtypes=[F32, BF16, BF16, BF16], name="a_mix_bwd")
    d_wmem, d_mnorm = [], []
    for l, d_mkv in enumerate((d_mkv0, d_mkv1)):
        d_mn = _mm(d_mkv, w_mem[l], "nt", name=f"mem_kv_dx{l}")
        d_wmem.append(_mm(mn[l], d_mkv, "tn", out_dtype=BF16, name=f"mem_kv_dw{l}"))
        d_mnorm.append(_rowwise_bwd(_f_norm, [(ms, D, 0)], [row2(mem_norm[l])], [(d_mn, D, 0)], [], [0],
                                    name=f"mem_norm_bwd{l}")[0])
    gr["w_mem_kv"] = jnp.stack(d_wmem)

    mid = tuple(n for n, _ in SHARDED[2:])
    send_mid = _stack_rows([_to_rows(_split8(gr[n], ax), 16) for n, ax in SHARDED[2:]])
    dq, dk, dv, recv_mid = _sb_bwd(qkv, sb, d_sb, tq=512, tk=256, name="sb_bwd", ride=("a2a", send_mid))
    d_pa = jnp.concatenate([dq, dk, dv, d_gsb, d_qm_a, d_gm_a], axis=1)
    send_ain = _mm(h0, d_pa, "tn", out_dtype=BF16, out_split=N_DEV, name="a_in_dw")
    d_h0, recv_ain = _mm(d_pa, w_a_in, "nt", name="a_in_dx", ride=("a2a", send_ain))
    grad_x, d_ga = _rowwise_bwd(_f_norm, [(xs, D, 0)], [g_a], [(d_h0, D, 0)], [0], [0], add=(d_x1, D, 0),
                                name="a_norm_bwd")
    gr["mem_norm"] = jnp.concatenate(d_mnorm, axis=0)
    gr["g_mem_q"] = jnp.concatenate([d_gq0, d_gq1], axis=0)
    gr["g_mem_k"] = jnp.concatenate([d_gk0, d_gk1], axis=0)
    gr["kv_norm"], gr["b_norm"], gr["g_ckv"], gr["g_k_nope"] = d_gkv, d_gb, d_gc, d_gkn
    gr["g_k_rope"], gr["b_g_q_rope"] = d_gkr[:, :ROPE], d_gqr[:, :ROPE]
    gr["b_g_q_lat"], gr["b_g_q_nope"] = d_gql, d_gqn
    last = ("a_norm",) + SMALL
    shared = jnp.concatenate([gr[n].reshape(-1) for n in SMALL] + [loss_part[0, :1]])
    flat = jnp.concatenate([_split8(d_ga, 1), jnp.broadcast_to(shared[None], (N_DEV, shared.size))], axis=1)
    recv_last = _all_to_all(_to_rows(flat, ROW_MULT), name="exchange_small")

    out = [{}, {}, {}, {}]
    slabs = _reduce_adamw(recv_mid, *[_pack_local(t, mid, 16) for t in (wts, mom, var)], name="reduce_adamw_mid")
    for o, slab in zip(out, slabs):
        o.update(_unpack_local(slab, mid, shapes, 16)[0])
    zero = jnp.zeros((1,), F32)
    slabs = _reduce_adamw(recv_last, *[_to_rows(jnp.concatenate([t[n].reshape(-1) for n in last] + [zero]), ROW_MULT)
                                       for t in (wts, mom, var)], name="reduce_adamw_last")
    for o, slab in zip(out, slabs):
        vec, at = slab.reshape(-1), 0
        for n in last:
            o[n] = vec[at:at + wts[n].size].reshape(shapes[n])
            at += wts[n].size
    loss = slabs[0].reshape(-1)[at]
    slabs = _reduce_adamw(recv_ain, *[t["a_w_in"][0] for t in (wts, mom, var)], name="reduce_adamw_ain")
    for o, slab in zip(out, slabs):
        o["a_w_in"] = slab[None]
    return (loss, grad_x[None], *[o[n] for o in out for n in WEIGHTS])
```

```python
import functools

import jax
import jax.numpy as jnp
from jax import lax
from jax.experimental import pallas as pl
from jax.experimental.pallas import tpu as pltpu

F32, BF16 = jnp.float32, jnp.bfloat16

N_DEV = 8
DH = 128
H_SB, H_MEM, H_MLA = 12, 4, 12
ROPE = 64
MLA_QK = DH + ROPE
EPS = 1e-6
ROPE_THETA = 10000.0
ADAM_LR, ADAM_B1, ADAM_B2, ADAM_EPS, ADAM_WD, ADAM_STEP = 0.001, 0.9, 0.999, 1e-08, 0.01, 10

LANES = 1024
VMEM_LIMIT = 48 * 1024 * 1024
VMEM_LIMIT_BIG = 56 * 1024 * 1024

NN = (((1,), (0,)), ((), ()))
NT = (((1,), (1,)), ((), ()))
TN = (((0,), (0,)), ((), ()))
_DIMS = {"nn": NN, "nt": NT, "tn": TN}


def _dot16(a, b, dims):
    return lax.dot_general(a.astype(BF16), b.astype(BF16), _DIMS[dims], preferred_element_type=F32)


@functools.partial(jax.custom_vjp, nondiff_argnums=(2,))
def _bdot(a, b, dims):
    return _dot16(a, b, dims)


def _bdot_fwd(a, b, dims):
    return _dot16(a, b, dims), (a, b)


def _bdot_bwd(dims, res, g):
    a, b = res
    if dims == "nn":
        return _dot16(g, b, "nt"), _dot16(a, g, "tn")
    return _dot16(g, b, "nn"), _dot16(g, a, "tn")


_bdot.defvjp(_bdot_fwd, _bdot_bwd)


def _tile(n, pref):
    if n <= pref:
        return n
    t = (pref // 128) * 128
    while n % t:
        t -= 128
    return t


def _mesh_pos():
    return lax.axis_index("x"), lax.axis_index("y"), lax.axis_index("c")


def _ride_shape(kind, src):
    return jax.ShapeDtypeStruct((N_DEV, *src.shape) if kind == "gather" else src.shape, src.dtype)


def _ride_scratch():
    return [pltpu.SemaphoreType.DMA((N_DEV - 1,)), pltpu.SemaphoreType.DMA((N_DEV - 1,)), pltpu.SemaphoreType.DMA]


def _ride(kind, phase, s_ref, r_ref, send_sems, recv_sems, local_sem):
    x, y, c = _mesh_pos()
    me = 4 * x + 2 * y + c
    src = (lambda lin: s_ref) if kind == "gather" else (lambda lin: s_ref.at[lin])
    local = pltpu.make_async_copy(src(me), r_ref.at[me], local_sem)
    if phase == "start":
        local.start()
    for k in range(1, N_DEV):
        p = (1 - x if k & 4 else x, 1 - y if k & 2 else y, 1 - c if k & 1 else c)
        lin = 4 * p[0] + 2 * p[1] + p[2]
        cp = pltpu.make_async_remote_copy(
            src_ref=src(lin), dst_ref=r_ref.at[me] if phase == "start" else r_ref.at[lin],
            send_sem=send_sems.at[k - 1], recv_sem=recv_sems.at[k - 1],
            device_id=p, device_id_type=pl.DeviceIdType.MESH)
        if phase == "start":
            cp.start()
        else:
            cp.wait_recv()
            cp.wait_send()
    if phase == "wait":
        local.wait()


def _mm(a, b, dims, *, name, out_dtype=F32, add=None, ride=None, out_split=None, b_rows=None, b_cols=None,
        tm=1024, tn=1024, tk=2048):
    col0 = 0
    if dims == "tn":
        (K, M), (_, N) = a.shape, b.shape
    elif dims == "nt":
        (M, K), N = a.shape, b_rows or b.shape[0]
    else:
        (M, K), N = a.shape, b.shape[1]
        if b_cols is not None:
            col0, N = b_cols
    tm, tk = _tile(M, tm), _tile(K, tk)
    tn = N // out_split if out_split else _tile(N, tn)
    while col0 % tn or N % tn:
        tn -= 128
    jb = col0 // tn
    ni, nj, nk = M // tm, N // tn, K // tk
    n_in = 2 + (add is not None) + (ride is not None)

    def body(*refs):
        a_ref, b_ref, o_ref = refs[0], refs[1], refs[n_in]
        acc_ref = refs[n_in + 1 + (ride is not None)]
        i, j, k = pl.program_id(0), pl.program_id(1), pl.program_id(2)
        if ride is not None:
            ride_refs = (refs[n_in - 1], refs[n_in + 1], *refs[-3:])

            @pl.when((i == 0) & (j == 0) & (k == 0))
            def _():
                _ride(ride[0], "start", *ride_refs)

        if nk == 1:
            r = _dot16(a_ref[...], b_ref[...], dims)
            o_ref[...] = (r if add is None else r + refs[2][...]).astype(o_ref.dtype)
        else:
            @pl.when(k == 0)
            def _():
                acc_ref[...] = jnp.zeros_like(acc_ref)

            acc_ref[...] += _dot16(a_ref[...], b_ref[...], dims)

            @pl.when(k == nk - 1)
            def _():
                r = acc_ref[...]
                if add is not None:
                    r = r + refs[2][...]
                o_ref[...] = r.astype(o_ref.dtype)

        if ride is not None:
            @pl.when((i == ni - 1) & (j == nj - 1) & (k == nk - 1))
            def _():
                _ride(ride[0], "wait", *ride_refs)

    a_spec = pl.BlockSpec((tk, tm), lambda i, j, k: (k, i)) if dims == "tn" else pl.BlockSpec((tm, tk), lambda i, j, k: (i, k))
    b_spec = pl.BlockSpec((tn, tk), lambda i, j, k: (j, k)) if dims == "nt" else pl.BlockSpec((tk, tn), lambda i, j, k: (k, j + jb))
    o_spec = pl.BlockSpec((tm, tn), lambda i, j, k: (i, j))
    in_specs, args = [a_spec, b_spec], [a, b]
    if add is not None:
        in_specs.append(o_spec)
        args.append(add)
    out_specs, out_shape = [o_spec], [jax.ShapeDtypeStruct((M, N), out_dtype)]
    scratch = [pltpu.VMEM((tm, tn) if nk > 1 else (8, 128), F32)]
    if out_split:
        out_specs = [pl.BlockSpec((None, tm, tn), lambda i, j, k: (j, i, 0))]
        out_shape = [jax.ShapeDtypeStruct((out_split, M, tn), out_dtype)]
    if ride is not None:
        in_specs.append(pl.BlockSpec(memory_space=pl.ANY))
        args.append(ride[1])
        out_specs.append(pl.BlockSpec(memory_space=pl.ANY))
        out_shape.append(_ride_shape(*ride))
        scratch += _ride_scratch()
    sem = ("arbitrary",) * 3 if ride is not None else ("parallel", "parallel", "arbitrary")
    res = pl.pallas_call(
        body, name=name, grid=(ni, nj, nk), in_specs=in_specs, out_specs=out_specs, out_shape=out_shape,
        scratch_shapes=scratch,
        compiler_params=pltpu.CompilerParams(dimension_semantics=sem, vmem_limit_bytes=VMEM_LIMIT),
    )(*args)
    return res[0] if ride is None else res


def _rowwise(fn, rows, consts, outs, accs=(), *, name, tm=256):
    S = rows[0][0].shape[0]
    tm = min(tm, S)
    nr, nc, no = len(rows), len(consts), len(outs)

    def body(*refs):
        res = fn(*[r[...] for r in refs[:nr + nc]])
        res = tuple(res) if isinstance(res, (tuple, list)) else (res,)
        orefs, arefs = refs[nr + nc:nr + nc + no], refs[nr + nc + no:]
        for r, v in zip(orefs, res[:no]):
            r[...] = v.astype(r.dtype)
        if arefs:
            @pl.when(pl.program_id(0) == 0)
            def _():
                for r in arefs:
                    r[...] = jnp.zeros_like(r)

            for r, v in zip(arefs, res[no:]):
                r[...] += v

    in_specs = [pl.BlockSpec((tm, w), lambda i, cb=cb: (i, cb)) for (_, w, cb) in rows]
    in_specs += [pl.BlockSpec(c.shape, lambda i: (0, 0)) for c in consts]
    out_specs = [pl.BlockSpec((tm, w), lambda i: (i, 0)) for (w, _) in outs]
    out_specs += [pl.BlockSpec(s, lambda i: (0, 0)) for s in accs]
    out_shape = [jax.ShapeDtypeStruct((S, w), dt) for (w, dt) in outs]
    out_shape += [jax.ShapeDtypeStruct(s, F32) for s in accs]
    res = pl.pallas_call(
        body, name=name, grid=(S // tm,), in_specs=in_specs, out_specs=out_specs, out_shape=out_shape,
        compiler_params=pltpu.CompilerParams(dimension_semantics=("arbitrary",), vmem_limit_bytes=VMEM_LIMIT),
    )(*[r[0] for r in rows], *consts)
    return res


def _rowwise_bwd(f, rows, consts, cts, row_grads, const_grads, *, name, add=None, out_dtypes=None, tm=256):
    nr, nc, nct = len(rows), len(consts), len(cts)
    all_rows = list(rows) + list(cts) + ([add] if add is not None else [])

    def fn(*args):
        nrow = len(all_rows)
        prim = [x.astype(F32) for x in args[:nr]] + [x.astype(F32) for x in args[nrow:]]
        ct = tuple(x.astype(F32) for x in args[nr:nr + nct])
        out, vjp = jax.vjp(f, *prim)
        gs = vjp(ct if isinstance(out, (tuple, list)) else ct[0])
        res = [gs[k] for k in row_grads]
        if add is not None:
            res[0] = res[0] + args[nrow - 1]
        return tuple(res) + tuple(gs[nr + k] for k in const_grads)

    out_dtypes = out_dtypes or [F32] * len(row_grads)
    outs = [(rows[k][1], dt) for k, dt in zip(row_grads, out_dtypes)]
    accs = [consts[k].shape for k in const_grads]
    return _rowwise(fn, all_rows, consts, outs, accs, name=name, tm=tm)


def _rms(x, g, n=None):
    ms = jnp.sum(x * x, axis=-1, keepdims=True) * (1.0 / (n or x.shape[-1]))
    return x * lax.rsqrt(ms + EPS) * g


def _sigmoid(x):
    return 1.0 / (1.0 + jnp.exp(-x))


def _swap_halves_exact(x):
    r = lax.broadcasted_iota(jnp.int32, (DH, DH), 0)
    c = lax.broadcasted_iota(jnp.int32, (DH, DH), 1)
    half = ROPE // 2
    perm = (((r < half) & (c == r + half)) | ((r >= half) & (r < ROPE) & (c == r - half))).astype(BF16)
    hi = x.astype(BF16)
    rest = x - hi.astype(F32)
    mid = rest.astype(BF16)
    lo = (rest - mid.astype(F32)).astype(BF16)
    return sum(lax.dot_general(p, perm, NN, preferred_element_type=F32) for p in (hi, mid, lo))


_swap_halves = jax.custom_vjp(_swap_halves_exact)
_swap_halves.defvjp(lambda x: (_swap_halves_exact(x), None), lambda _, g: (_swap_halves_exact(g),))


def _rope128(x, g128, cs):
    y = _rms(x, g128, n=ROPE)
    return y * cs[:, :DH] + _swap_halves(y) * cs[:, DH:]


def _f_norm(x, g):
    return _rms(x, g)


def _f_norm2(x, g1, g2):
    xn = x * lax.rsqrt(jnp.mean(x * x, axis=-1, keepdims=True) + EPS)
    return xn * g1, xn * g2


def _f_kv1(ckr, cs, g_ckv, g_kr):
    w = g_ckv.shape[-1]
    return _rms(ckr[:, :w], g_ckv), _rope128(ckr[:, w:], g_kr, cs)


def _f_kv2(kv, kr, g_kn):
    ks, vs = [], []
    for h in range(H_MLA):
        ks += [_rms(kv[:, 2 * DH * h:2 * DH * h + DH], g_kn), kr]
        vs.append(kv[:, 2 * DH * h + DH:2 * DH * (h + 1)])
    return jnp.concatenate(ks, axis=1), jnp.concatenate(vs, axis=1)


def _f_q2(q, cs, g_n, g_r):
    out = []
    for h in range(H_MLA):
        out += [_rms(q[:, 2 * DH * h:2 * DH * h + DH], g_n), _rope128(q[:, 2 * DH * h + DH:2 * DH * (h + 1)], g_r, cs)]
    return jnp.concatenate(out, axis=1)


def _f_mix(att, g_att, q_m, g_m, mkv, g_q, g_k):
    mem_w = H_MEM * DH
    heads = []
    for h in range(H_MEM):
        kh = _rms(mkv[:, h * DH:(h + 1) * DH], g_k)
        vh = mkv[:, mem_w + h * DH:mem_w + (h + 1) * DH]
        qh = _rms(q_m[:, h * DH:(h + 1) * DH], g_q)
        s = _bdot(qh, kh, "nt") * (DH ** -0.5)
        p = jnp.exp(s - lax.stop_gradient(jnp.max(s, axis=-1, keepdims=True)))
        p = p / jnp.sum(p, axis=-1, keepdims=True)
        heads.append(_bdot(p, vh, "nn"))
    mo = jnp.concatenate(heads, axis=1)
    return jnp.concatenate([att * (g_att * _sigmoid(g_att)), mo * (g_m * _sigmoid(g_m))], axis=1)


def _split_dot(x, u):
    hi = x.astype(BF16)
    lo = (x - hi.astype(F32)).astype(BF16)
    return (lax.dot_general(hi, u, NN, preferred_element_type=F32)
            + lax.dot_general(lo, u, NN, preferred_element_type=F32))


def _tri(t):
    r = lax.broadcasted_iota(jnp.int32, (t, t), 0)
    c = lax.broadcasted_iota(jnp.int32, (t, t), 1)
    return r, c


def _strict_lower(t):
    r, c = _tri(t)
    return (r > c).astype(BF16)


def _head_blocks_t(x, w, tk, *, name):
    S = x.shape[0]
    H = x.shape[1] // w

    def body(x_ref, o_ref):
        for h in range(H):
            o_ref[h] = x_ref[:, h * w:(h + 1) * w].T

    return pl.pallas_call(
        body, name=name, grid=(S // tk,),
        in_specs=[pl.BlockSpec((tk, H * w), lambda i: (i, 0))],
        out_specs=pl.BlockSpec((H, None, w, tk), lambda i: (0, i, 0, 0)),
        out_shape=jax.ShapeDtypeStruct((H, S // tk, w, tk), x.dtype),
        compiler_params=pltpu.CompilerParams(dimension_semantics=("parallel",), vmem_limit_bytes=VMEM_LIMIT),
    )(x)


def _rows_ahead(tq, tk):
    return lax.broadcasted_iota(jnp.int32, (tq, tk), 0) - lax.broadcasted_iota(jnp.int32, (tq, tk), 1)


EXP_UNDERFLOW = -110.0


def _log_one_minus_beta(zr, scale):
    zs, nz = zr * scale, zr * (-scale)
    return zs, jnp.minimum(nz, 0.0) - jnp.log(1.0 + jnp.exp(jnp.minimum(zs, nz)))


def _sb_fwd(qkv, *, tq, tk, name, ride=None):
    S = qkv.shape[0]
    tq, tk = min(tq, S), min(tk, tq, S)
    nd = tq // tk
    H = H_SB
    scale = DH ** -0.5

    nq = S // tq

    def body(q_ref, k_ref, v_ref, *rest):
        o_ref = rest[1] if ride is not None else rest[0]
        h, i = pl.program_id(0), pl.program_id(1)
        if ride is not None:
            ride_refs = (rest[0], *rest[2:])

            @pl.when((h == 0) & (i == 0))
            def _():
                _ride(ride[0], "start", *ride_refs)

        q = q_ref[...]
        u = _strict_lower(tk)
        ahead = _rows_ahead(tq, tk)

        def block(j, acc, cb, keep):
            off = pl.multiple_of(j * tk, tk)
            k = k_ref[pl.ds(off, tk), :]
            v = v_ref[pl.ds(off, tk), :]
            z, l = _log_one_minus_beta(lax.dot_general(q, k, NT, preferred_element_type=F32), scale)
            if keep is not None:
                l = jnp.where(keep, l, 0.0)
            a = jnp.exp((z + l) + (_split_dot(l, u) + cb))
            if keep is not None:
                a = jnp.where(keep, a, 0.0)
            acc = acc + lax.dot_general(a.astype(BF16), v, NN, preferred_element_type=F32)
            return acc, cb + jnp.sum(l, axis=1, keepdims=True)

        carry = (jnp.zeros((tq, DH), F32), jnp.zeros((tq, 1), F32))
        for t in reversed(range(nd)):
            carry = block(i * nd + t, *carry, ahead > t * tk)
        _, acc, _ = lax.while_loop(
            lambda st: (st[0] < i * nd) & (jnp.max(st[2]) > EXP_UNDERFLOW),
            lambda st: (st[0] + 1, *block(i * nd - 1 - st[0], st[1], st[2], None)), (jnp.int32(0), *carry))
        o_ref[...] = acc

        if ride is not None:
            @pl.when((h == H - 1) & (i == nq - 1))
            def _():
                _ride(ride[0], "wait", *ride_refs)

    in_specs = [pl.BlockSpec((tq, DH), lambda h, i: (i, h)),
                pl.BlockSpec((S, DH), lambda h, i: (0, H + h)),
                pl.BlockSpec((S, DH), lambda h, i: (0, 2 * H + h))]
    out_specs = [pl.BlockSpec((tq, DH), lambda h, i: (i, h))]
    out_shape = [jax.ShapeDtypeStruct((S, H * DH), F32)]
    args, scratch = [qkv, qkv, qkv], []
    if ride is not None:
        in_specs.append(pl.BlockSpec(memory_space=pl.ANY))
        args.append(ride[1])
        out_specs.append(pl.BlockSpec(memory_space=pl.ANY))
        out_shape.append(_ride_shape(*ride))
        scratch = _ride_scratch()
    res = pl.pallas_call(
        body, name=name, grid=(H, nq), in_specs=in_specs, out_specs=out_specs, out_shape=out_shape,
        scratch_shapes=scratch,
        compiler_params=pltpu.CompilerParams(dimension_semantics=("arbitrary", "arbitrary"), vmem_limit_bytes=VMEM_LIMIT),
    )(*args)
    return res[0] if ride is None else res


def _sb_bwd(qkv, o, do, *, tq, tk, name, ride=None):
    S = qkv.shape[0]
    tq, tk = min(tq, S), min(tk, tq, S)
    nd = tq // tk
    H = H_SB
    scale = DH ** -0.5

    nq = S // tq

    def body(q_ref, k_ref, v_ref, o_ref, do_ref, *rest):
        if ride is not None:
            dq_ref, dk_out, dv_out, dk_ref, dv_ref = rest[1], rest[2], rest[3], rest[5], rest[6]
            ride_refs = (rest[0], rest[4], *rest[7:])
        else:
            dq_ref, dk_out, dv_out, dk_ref, dv_ref = rest
        h, i = pl.program_id(0), pl.program_id(1)
        if ride is not None:
            @pl.when((h == 0) & (i == 0))
            def _():
                _ride(ride[0], "start", *ride_refs)

        @pl.when(i == 0)
        def _():
            dk_ref[...] = jnp.zeros_like(dk_ref)
            dv_ref[...] = jnp.zeros_like(dv_ref)

        q = q_ref[...]
        do = do_ref[...]
        do16 = do.astype(BF16)
        dsum = jnp.sum(do16.astype(F32) * o_ref[...], axis=1, keepdims=True)
        u = _strict_lower(tk)
        ahead = _rows_ahead(tq, tk)

        def block(j, dq, cb, ce, keep):
            off = pl.multiple_of(j * tk, tk)
            k = k_ref[pl.ds(off, tk), :]
            v = v_ref[pl.ds(off, tk), :]
            z, l = _log_one_minus_beta(lax.dot_general(q, k, NT, preferred_element_type=F32), scale)
            if keep is not None:
                l = jnp.where(keep, l, 0.0)
            log_beta = z + l
            a = jnp.exp(log_beta + (_split_dot(l, u) + cb))
            if keep is not None:
                a = jnp.where(keep, a, 0.0)
            a16 = a.astype(BF16)
            e = a16.astype(F32) * lax.dot_general(do16, v, NT, preferred_element_type=F32)
            left = dsum - (ce + _split_dot(e, u) + e)
            dz = (e - jnp.exp(log_beta) * (e + left)) * scale
            if keep is not None:
                dz = jnp.where(keep, dz, 0.0)
            dz = dz.astype(BF16)
            dq = dq + lax.dot_general(dz, k, NN, preferred_element_type=F32)
            dk_ref[pl.ds(off, tk), :] += lax.dot_general(dz, q, TN, preferred_element_type=F32)
            dv_ref[pl.ds(off, tk), :] += lax.dot_general(a16, do16, TN, preferred_element_type=F32)
            return dq, cb + jnp.sum(l, axis=1, keepdims=True), ce + jnp.sum(e, axis=1, keepdims=True)

        zero = jnp.zeros((tq, 1), F32)
        carry = (jnp.zeros((tq, DH), F32), zero, zero)
        for t in reversed(range(nd)):
            carry = block(i * nd + t, *carry, ahead > t * tk)
        _, dq, _, _ = lax.while_loop(
            lambda st: (st[0] < i * nd) & (jnp.max(st[2]) > EXP_UNDERFLOW),
            lambda st: (st[0] + 1, *block(i * nd - 1 - st[0], st[1], st[2], st[3], None)), (jnp.int32(0), *carry))
        dq_ref[...] = dq.astype(dq_ref.dtype)

        @pl.when(i == nq - 1)
        def _():
            dk_out[...] = dk_ref[...].astype(dk_out.dtype)
            dv_out[...] = dv_ref[...].astype(dv_out.dtype)

        if ride is not None:
            @pl.when((h == H - 1) & (i == nq - 1))
            def _():
                _ride(ride[0], "wait", *ride_refs)

    blk = pl.BlockSpec((tq, DH), lambda h, i: (i, h))
    whole = pl.BlockSpec((S, DH), lambda h, i: (0, h))
    shp = jax.ShapeDtypeStruct((S, H * DH), BF16)
    in_specs = [blk, pl.BlockSpec((S, DH), lambda h, i: (0, H + h)), pl.BlockSpec((S, DH), lambda h, i: (0, 2 * H + h)),
                blk, blk]
    out_specs, out_shape = [blk, whole, whole], [shp, shp, shp]
    args, scratch = [qkv, qkv, qkv, o, do], [pltpu.VMEM((S, DH), F32), pltpu.VMEM((S, DH), F32)]
    if ride is not None:
        in_specs.append(pl.BlockSpec(memory_space=pl.ANY))
        args.append(ride[1])
        out_specs.append(pl.BlockSpec(memory_space=pl.ANY))
        out_shape.append(_ride_shape(*ride))
        scratch += _ride_scratch()
    return pl.pallas_call(
        body, name=name, grid=(H, nq), in_specs=in_specs, out_specs=out_specs, out_shape=out_shape,
        scratch_shapes=scratch,
        compiler_params=pltpu.CompilerParams(dimension_semantics=("arbitrary", "arbitrary"), vmem_limit_bytes=VMEM_LIMIT),
    )(*args)


def _mla_fwd(q, k, v, *, tq, tk, name):
    S = q.shape[0]
    tq, tk = min(tq, S), min(tk, tq, S)
    nd, nb = tq // tk, S // tk
    H = H_MLA
    scale = MLA_QK ** -0.5
    vt = _head_blocks_t(v, DH, tk, name=name + "_vt")

    def body(q_ref, k_ref, vt_ref, o_ref, lse_ref):
        i = pl.program_id(1)
        qb = q_ref[...]
        behind = lax.broadcasted_iota(jnp.int32, (tk, tq), 1) - lax.broadcasted_iota(jnp.int32, (tk, tq), 0)

        def block(j, m, den, acct, keep):
            off = pl.multiple_of(j * tk, tk)
            st = lax.dot_general(k_ref[pl.ds(off, tk), :], qb, NT, preferred_element_type=F32) * scale
            if keep is not None:
                st = jnp.where(keep, st, -1e30)
            m_new = jnp.maximum(m, jnp.max(st, axis=0, keepdims=True))
            pt = jnp.exp(st - m_new)
            alpha = jnp.exp(m - m_new)
            den = alpha * den + jnp.sum(pt, axis=0, keepdims=True)
            acct = alpha * acct + lax.dot_general(vt_ref[j], pt.astype(BF16), NN, preferred_element_type=F32)
            return m_new, den, acct

        init = (jnp.full((1, tq), -1e30, F32), jnp.zeros((1, tq), F32), jnp.zeros((DH, tq), F32))
        carry = lax.fori_loop(0, i * nd, lambda j, carry: block(j, *carry, None), init)
        for t in range(nd):
            carry = block(i * nd + t, *carry, behind >= t * tk)
        m, den, acct = carry
        o_ref[...] = (acct / den).T
        lse_ref[0] = m + jnp.log(den)

    return pl.pallas_call(
        body, name=name, grid=(H, S // tq),
        in_specs=[pl.BlockSpec((tq, 2 * DH), lambda h, i: (i, h)),
                  pl.BlockSpec((S, 2 * DH), lambda h, i: (0, h)),
                  pl.BlockSpec((None, nb, DH, tk), lambda h, i: (h, 0, 0, 0))],
        out_specs=[pl.BlockSpec((tq, DH), lambda h, i: (i, h)), pl.BlockSpec((1, 1, tq), lambda h, i: (h, 0, i))],
        out_shape=[jax.ShapeDtypeStruct((S, H * DH), F32), jax.ShapeDtypeStruct((H, 1, S), F32)],
        compiler_params=pltpu.CompilerParams(dimension_semantics=("arbitrary", "arbitrary"), vmem_limit_bytes=VMEM_LIMIT),
    )(q, k, vt)


def _mla_bwd(q, k, v, o, do, lse, *, tq, tk, name):
    S = q.shape[0]
    tq, tk = min(tq, S), min(tk, tq, S)
    nd, nb = tq // tk, S // tk
    H = H_MLA
    scale = MLA_QK ** -0.5
    kt = _head_blocks_t(k, 2 * DH, tk, name=name + "_kt")

    def body(q_ref, k_ref, kt_ref, v_ref, o_ref, do_ref, lse_ref, dq_ref, dk_ref, dv_ref):
        i = pl.program_id(1)

        @pl.when(i == 0)
        def _():
            dk_ref[...] = jnp.zeros_like(dk_ref)
            dv_ref[...] = jnp.zeros_like(dv_ref)

        qb = q_ref[...]
        do = do_ref[...]
        do16 = do.astype(BF16)
        dsum = jnp.sum((do * o_ref[...]).T, axis=0, keepdims=True)
        lse = lse_ref[0]
        behind = lax.broadcasted_iota(jnp.int32, (tk, tq), 1) - lax.broadcasted_iota(jnp.int32, (tk, tq), 0)

        def block(j, dqt, keep):
            off = pl.multiple_of(j * tk, tk)
            kb = k_ref[pl.ds(off, tk), :]
            vb = v_ref[pl.ds(off, tk), :]
            st = lax.dot_general(kb, qb, NT, preferred_element_type=F32) * scale
            if keep is not None:
                st = jnp.where(keep, st, -1e30)
            pt = jnp.exp(st - lse)
            dpt = lax.dot_general(vb, do16, NT, preferred_element_type=F32)
            dst = (pt * (dpt - dsum) * scale).astype(BF16)
            dk_ref[pl.ds(off, tk), :] += lax.dot_general(dst, qb, NN, preferred_element_type=F32)
            dv_ref[pl.ds(off, tk), :] += lax.dot_general(pt.astype(BF16), do16, NN, preferred_element_type=F32)
            return dqt + lax.dot_general(kt_ref[j], dst, NN, preferred_element_type=F32)

        dqt = lax.fori_loop(0, i * nd, lambda j, dqt: block(j, dqt, None), jnp.zeros((2 * DH, tq), F32))
        for t in range(nd):
            dqt = block(i * nd + t, dqt, behind >= t * tk)
        dq_ref[...] = dqt.T

    blk = pl.BlockSpec((tq, DH), lambda h, i: (i, h))
    blk2 = pl.BlockSpec((tq, 2 * DH), lambda h, i: (i, h))
    return pl.pallas_call(
        body, name=name, grid=(H, S // tq),
        in_specs=[blk2, pl.BlockSpec((S, 2 * DH), lambda h, i: (0, h)),
                  pl.BlockSpec((None, nb, 2 * DH, tk), lambda h, i: (h, 0, 0, 0)),
                  pl.BlockSpec((S, DH), lambda h, i: (0, h)),
                  blk, blk, pl.BlockSpec((1, 1, tq), lambda h, i: (h, 0, i))],
        out_specs=[blk2, pl.BlockSpec((S, 2 * DH), lambda h, i: (0, h)), pl.BlockSpec((S, DH), lambda h, i: (0, h))],
        out_shape=[jax.ShapeDtypeStruct((S, H * 2 * DH), F32), jax.ShapeDtypeStruct((S, H * 2 * DH), F32),
                   jax.ShapeDtypeStruct((S, H * DH), F32)],
        compiler_params=pltpu.CompilerParams(dimension_semantics=("arbitrary", "arbitrary"), vmem_limit_bytes=VMEM_LIMIT_BIG),
    )(q, k, kt, v, o, do, lse)


def _all_gather_columns(block, *, name):
    R, C = block.shape

    def body(x_ref, out_ref, send_sems, recv_sems, local_sem):
        x, y, c = _mesh_pos()
        me, sibling = (x, y, c), (x, y, 1 - c)
        chips = [(1 - x, y), (x, 1 - y), (1 - x, 1 - y)]

        def slot(px, py, pc):
            return out_ref.at[:, pl.ds(pl.multiple_of((4 * px + 2 * py + pc) * C, 128), C)]

        def copy(k, blk, to, src=None):
            return pltpu.make_async_remote_copy(
                src_ref=slot(*blk) if src is None else src, dst_ref=slot(*blk),
                send_sem=send_sems.at[k], recv_sem=recv_sems.at[k],
                device_id=to, device_id_type=pl.DeviceIdType.MESH)

        mine = pltpu.make_async_copy(x_ref, slot(*me), local_sem)
        mine.start()
        first = [copy(0, me, sibling, src=x_ref)]
        first += [copy(1 + j, me, (*chip, c), src=x_ref) for j, chip in enumerate(chips)]
        for cp in first:
            cp.start()
        passed = [copy(4 + j, (*chip, c), sibling) for j, chip in enumerate(chips)]
        for j, chip in enumerate(chips):
            copy(1 + j, (*chip, c), me).wait_recv()
            passed[j].start()
        copy(0, sibling, me).wait_recv()
        for j, chip in enumerate(chips):
            copy(4 + j, (*chip, 1 - c), me).wait_recv()
        for cp in first + passed:
            cp.wait_send()
        mine.wait()

    return pl.pallas_call(
        body, name=name,
        out_shape=jax.ShapeDtypeStruct((R, N_DEV * C), block.dtype),
        in_specs=[pl.BlockSpec(memory_space=pl.ANY)], out_specs=pl.BlockSpec(memory_space=pl.ANY),
        scratch_shapes=[pltpu.SemaphoreType.DMA((7,)), pltpu.SemaphoreType.DMA((7,)), pltpu.SemaphoreType.DMA],
    )(block)


def _all_to_all(send, *, name):
    def body(*refs):
        _ride("a2a", "start", *refs)
        _ride("a2a", "wait", *refs)

    return pl.pallas_call(
        body, name=name, out_shape=_ride_shape("a2a", send),
        in_specs=[pl.BlockSpec(memory_space=pl.ANY)], out_specs=pl.BlockSpec(memory_space=pl.ANY),
        scratch_shapes=_ride_scratch(),
    )(send)


def _reduce_adamw(recv, w, m, v, *, name):
    R, C = w.shape
    tr = next(t for t in (128, 64, 32, 16, 8) if R % t == 0)

    def body(g_ref, w_ref, m_ref, v_ref, og_ref, od_ref, om_ref, ov_ref):
        g = g_ref[0].astype(F32)
        for s in range(1, N_DEV):
            g = g + g_ref[s].astype(F32)
        mn = ADAM_B1 * m_ref[...] + (1.0 - ADAM_B1) * g
        vn = ADAM_B2 * v_ref[...] + (1.0 - ADAM_B2) * jnp.square(g)
        m_hat = mn / (1.0 - ADAM_B1 ** ADAM_STEP)
        v_hat = vn / (1.0 - ADAM_B2 ** ADAM_STEP)
        og_ref[...] = g
        od_ref[...] = -ADAM_LR * (m_hat / (jnp.sqrt(v_hat) + ADAM_EPS) + ADAM_WD * w_ref[...])
        om_ref[...] = mn
        ov_ref[...] = vn

    blk = pl.BlockSpec((tr, C), lambda i: (i, 0))
    shp = jax.ShapeDtypeStruct((R, C), F32)
    return pl.pallas_call(
        body, name=name, grid=(R // tr,),
        in_specs=[pl.BlockSpec((N_DEV, tr, C), lambda i: (0, i, 0)), blk, blk, blk],
        out_specs=[blk, blk, blk, blk], out_shape=[shp, shp, shp, shp],
        compiler_params=pltpu.CompilerParams(dimension_semantics=("parallel",), vmem_limit_bytes=VMEM_LIMIT),
    )(recv, w, m, v)


SHARDED = (("a_norm", 1), ("a_w_in", 2), ("a_w_out", 1), ("w_dkv", 0), ("w_ukv", 1), ("b_w_in", 2),
           ("b_w_uq", 2), ("b_w_out", 1), ("w_mem_kv", 1))
SMALL = ("kv_norm", "g_ckv", "g_k_nope", "g_k_rope", "b_norm", "b_g_q_lat", "b_g_q_nope", "b_g_q_rope",
         "mem_norm", "g_mem_q", "g_mem_k")
WEIGHTS = ("a_norm", "a_w_in", "a_w_out", "kv_norm", "w_dkv", "g_ckv", "w_ukv", "g_k_nope", "g_k_rope", "b_norm",
           "b_w_in", "b_g_q_lat", "b_w_uq", "b_g_q_nope", "b_g_q_rope", "b_w_out", "mem_norm", "w_mem_kv",
           "g_mem_q", "g_mem_k")
ROW_MULT = 8
ROW_BLOCK = 128


def _rows_of(n, mult):
    rows = -(-n // LANES)
    return -(-rows // mult) * mult


def _to_rows(flat, mult):
    n = flat.shape[-1]
    rows = _rows_of(n, mult)
    pad = [(0, 0)] * (flat.ndim - 1) + [(0, rows * LANES - n)]
    return jnp.pad(flat, pad).reshape(*flat.shape[:-1], rows, LANES)


def _split8(full, axis):
    shp = full.shape
    t = full.reshape(*shp[:axis], N_DEV, shp[axis] // N_DEV, *shp[axis + 1:])
    return jnp.moveaxis(t, axis, 0).reshape(N_DEV, -1)


def _join8(rows, axis, shard_shape):
    t = rows.reshape(N_DEV, *shard_shape)
    t = jnp.moveaxis(t, 0, axis)
    return t.reshape(*shard_shape[:axis], N_DEV * shard_shape[axis], *shard_shape[axis + 1:])


def _stack_rows(parts, block=ROW_BLOCK):
    rows = sum(p.shape[-2] for p in parts)
    if rows % block:
        parts = list(parts) + [jnp.zeros((*parts[0].shape[:-2], -rows % block, LANES), parts[0].dtype)]
    return jnp.concatenate(parts, axis=-2)


def _pack_local(vals, names, mult):
    return _stack_rows([_to_rows(vals[n].reshape(-1), mult) for n in names])


def _unpack_local(slab, names, shapes, mult):
    out, row = {}, 0
    for n in names:
        size = 1
        for d in shapes[n]:
            size *= d
        rows = _rows_of(size, mult)
        out[n] = slab[row:row + rows].reshape(-1)[:size].reshape(shapes[n])
        row += rows
    return out, row


def kernel(x, mem, positions, a_norm, a_w_in, a_w_out, kv_norm, w_dkv, g_ckv, w_ukv, g_k_nope, g_k_rope, b_norm, b_w_in, b_g_q_lat, b_w_uq, b_g_q_nope, b_g_q_rope, b_w_out, mem_norm, w_mem_kv, g_mem_q, g_mem_k, loss_target, m_a_norm, m_a_w_in, m_a_w_out, m_kv_norm, m_w_dkv, m_g_ckv, m_w_ukv, m_g_k_nope, m_g_k_rope, m_b_norm, m_b_w_in, m_b_g_q_lat, m_b_w_uq, m_b_g_q_nope, m_b_g_q_rope, m_b_w_out, m_mem_norm, m_w_mem_kv, m_g_mem_q, m_g_mem_k, v_a_norm, v_a_w_in, v_a_w_out, v_kv_norm, v_w_dkv, v_g_ckv, v_w_ukv, v_g_k_nope, v_g_k_rope, v_b_norm, v_b_w_in, v_b_g_q_lat, v_b_w_uq, v_b_g_q_nope, v_b_g_q_rope, v_b_w_out, v_mem_norm, v_w_mem_kv, v_g_mem_q, v_g_mem_k):
    wts = dict(a_norm=a_norm, a_w_in=a_w_in, a_w_out=a_w_out, kv_norm=kv_norm, w_dkv=w_dkv, g_ckv=g_ckv, w_ukv=w_ukv,
               g_k_nope=g_k_nope, g_k_rope=g_k_rope, b_norm=b_norm, b_w_in=b_w_in, b_g_q_lat=b_g_q_lat, b_w_uq=b_w_uq,
               b_g_q_nope=b_g_q_nope, b_g_q_rope=b_g_q_rope, b_w_out=b_w_out, mem_norm=mem_norm, w_mem_kv=w_mem_kv,
               g_mem_q=g_mem_q, g_mem_k=g_mem_k)
    mom = dict(a_norm=m_a_norm, a_w_in=m_a_w_in, a_w_out=m_a_w_out, kv_norm=m_kv_norm, w_dkv=m_w_dkv, g_ckv=m_g_ckv,
               w_ukv=m_w_ukv, g_k_nope=m_g_k_nope, g_k_rope=m_g_k_rope, b_norm=m_b_norm, b_w_in=m_b_w_in,
               b_g_q_lat=m_b_g_q_lat, b_w_uq=m_b_w_uq, b_g_q_nope=m_b_g_q_nope, b_g_q_rope=m_b_g_q_rope,
               b_w_out=m_b_w_out, mem_norm=m_mem_norm, w_mem_kv=m_w_mem_kv, g_mem_q=m_g_mem_q, g_mem_k=m_g_mem_k)
    var = dict(a_norm=v_a_norm, a_w_in=v_a_w_in, a_w_out=v_a_w_out, kv_norm=v_kv_norm, w_dkv=v_w_dkv, g_ckv=v_g_ckv,
               w_ukv=v_w_ukv, g_k_nope=v_g_k_nope, g_k_rope=v_g_k_rope, b_norm=v_b_norm, b_w_in=v_b_w_in,
               b_g_q_lat=v_b_g_q_lat, b_w_uq=v_b_w_uq, b_g_q_nope=v_b_g_q_nope, b_g_q_rope=v_b_g_q_rope,
               b_w_out=v_b_w_out, mem_norm=v_mem_norm, w_mem_kv=v_w_mem_kv, g_mem_q=v_g_mem_q, g_mem_k=v_g_mem_k)
    shapes = {n: wts[n].shape for n in WEIGHTS}
    S, D = x.shape[1], x.shape[2]
    xs, ms, tgt = x[0], mem[0], loss_target[0]
    sb_w, mem_w, mla_w = H_SB * DH, H_MEM * DH, H_MLA * DH
    q_lora, kv_lora = b_g_q_lat.shape[-1], g_ckv.shape[-1]

    def pieces_of(names):
        return [_to_rows(lax.bitcast_convert_type(a_norm.reshape(-1), BF16).reshape(-1), 16) if n == "a_norm"
                else _to_rows(wts[n].astype(BF16).reshape(-1), 16) for n in names]

    def unpack_gathered(gathered, names):
        full, row = {}, 0
        for n in names:
            size = wts[n].size * (2 if n == "a_norm" else 1)
            rows = _rows_of(size, 16)
            flat = gathered[:, row:row + rows].reshape(N_DEV, -1)[:, :size]
            if n == "a_norm":
                flat = lax.bitcast_convert_type(flat.reshape(N_DEV, -1, 2), F32)
            full[n] = _join8(flat, dict(SHARDED)[n], wts[n].shape)
            row += rows
        return full

    second, third = ("a_w_out", "w_dkv", "w_ukv"), ("w_mem_kv", "b_w_in", "b_w_uq", "b_w_out")
    a_in_w = a_w_in.shape[-1]
    bits = jnp.pad(lax.bitcast_convert_type(a_norm.reshape(-1), BF16).reshape(1, -1), ((0, 15), (0, a_in_w - 2 * a_norm.size)))
    w_a_in = _all_gather_columns(jnp.concatenate([a_w_in[0].astype(BF16), bits], axis=0), name="gather_first")
    g_a = lax.bitcast_convert_type(w_a_in[D].reshape(N_DEV, a_in_w)[:, :2 * a_norm.size].reshape(N_DEV, -1, 2),
                                   F32).reshape(1, D)

    row2 = lambda g: g.reshape(1, -1)
    h0 = _rowwise(_f_norm, [(xs, D, 0)], [g_a], [(D, BF16)], name="a_norm_fwd", tm=512)[0]
    qkv, gathered = _mm(h0, w_a_in, "nn", b_cols=(0, 3 * sb_w), out_dtype=BF16, name="a_in_qkv",
                        ride=("gather", jnp.concatenate(pieces_of(second), axis=0)))
    full = unpack_gathered(gathered, second)
    pa = _mm(h0, w_a_in, "nn", b_cols=(3 * sb_w, N_DEV * a_in_w - 3 * sb_w), name="a_in_rest")
    sb, gathered = _sb_fwd(qkv, tq=512, tk=256, name="sb_fwd",
                           ride=("gather", jnp.concatenate(pieces_of(third), axis=0)))
    full.update(unpack_gathered(gathered, third))
    w_a_out = full["a_w_out"][0]
    w_dkv_p = jnp.pad(full["w_dkv"], ((0, 0), (0, ROPE)))
    w_ukv_f = full["w_ukv"]
    wb = full["b_w_in"][0]
    w_b_in = jnp.concatenate([wb[:, q_lora:q_lora + mla_w], wb[:, :q_lora], wb[:, q_lora + mla_w:]], axis=1)
    w_uq_p = jnp.pad(full["b_w_uq"][0].reshape(q_lora, H_MLA, MLA_QK),
                     ((0, 0), (0, 0), (0, 2 * DH - MLA_QK))).reshape(q_lora, H_MLA * 2 * DH)
    w_b_out = full["b_w_out"][0]
    w_mem = full["w_mem_kv"]

    pad128 = lambda g: jnp.pad(g.reshape(1, -1), ((0, 0), (0, DH - ROPE)))
    g_kr, g_qr = pad128(g_k_rope), pad128(b_g_q_rope[0])
    g_kv, g_b, g_c, g_kn = row2(kv_norm), row2(b_norm[0]), row2(g_ckv), row2(g_k_nope)
    g_ql, g_qn = row2(b_g_q_lat[0]), row2(b_g_q_nope[0])

    inv_freq = jnp.power(ROPE_THETA, -jnp.arange(0, ROPE, 2, dtype=F32) / ROPE)
    ang = positions[0].astype(F32)[:, None] * inv_freq
    z64 = jnp.zeros((S, DH - ROPE), F32)
    cs = jnp.concatenate([jnp.cos(ang), jnp.cos(ang), z64, -jnp.sin(ang), jnp.sin(ang), z64], axis=1)

    mn, mkv = [], []
    for l in range(2):
        mn.append(_rowwise(_f_norm, [(ms, D, 0)], [row2(mem_norm[l])], [(D, BF16)], name=f"mem_norm{l}")[0])
        mkv.append(_mm(mn[l], w_mem[l], "nn", name=f"mem_kv{l}"))
    g_mq = [row2(g_mem_q[l]) for l in range(2)]
    g_mk = [row2(g_mem_k[l]) for l in range(2)]

    mix_a_rows = [(sb, sb_w, 0), (pa, sb_w, 0), (pa, mem_w, sb_w // mem_w), (pa, mem_w, sb_w // mem_w + 1)]
    mixed_a = _rowwise(_f_mix, mix_a_rows, [mkv[0], g_mq[0], g_mk[0]], [(sb_w + mem_w, BF16)], name="a_mix_fwd")[0]
    x1 = _mm(mixed_a, w_a_out, "nn", add=xs, name="a_out")

    hk, hb = _rowwise(_f_norm2, [(x1, D, 0)], [g_kv, g_b], [(D, BF16), (D, BF16)], name="b_norm_fwd", tm=512)
    ckr = _mm(hk, w_dkv_p, "nn", name="kv_down")
    cn, kr = _rowwise(_f_kv1, [(ckr, kv_lora + DH, 0), (cs, 2 * DH, 0)], [g_c, g_kr],
                      [(kv_lora, BF16), (DH, F32)], name="kv1_fwd", tm=512)
    kvu = _mm(cn, w_ukv_f, "nn", name="kv_up")
    k2, v2 = _rowwise(_f_kv2, [(kvu, H_MLA * 2 * DH, 0), (kr, DH, 0)], [g_kn],
                      [(H_MLA * 2 * DH, BF16), (mla_w, BF16)], name="kv2_fwd")
    pb = _mm(hb, w_b_in, "nn", name="b_in")
    ql = _rowwise(_f_norm, [(pb, q_lora, mla_w // q_lora)], [g_ql], [(q_lora, BF16)], name="q_lat_fwd", tm=512)[0]
    qraw = _mm(ql, w_uq_p, "nn", name="q_up")
    q2 = _rowwise(_f_q2, [(qraw, H_MLA * 2 * DH, 0), (cs, 2 * DH, 0)], [g_qn, g_qr],
                  [(H_MLA * 2 * DH, BF16)], name="q2_fwd")[0]
    att, lse = _mla_fwd(q2, k2, v2, tq=1024, tk=1024, name="mla_fwd")
    cb = (mla_w + q_lora) // mem_w
    mix_b_rows = [(att, mla_w, 0), (pb, mla_w, 0), (pb, mem_w, cb), (pb, mem_w, cb + 1)]
    mixed_b = _rowwise(_f_mix, mix_b_rows, [mkv[1], g_mq[1], g_mk[1]], [(mla_w + mem_w, BF16)], name="b_mix_fwd")[0]
    y = _mm(mixed_b, w_b_out, "nn", add=x1, name="b_out")

    def loss_fn(yb, tb):
        err = yb - tb
        part = 0.5 * jnp.sum(jnp.sum(err * err, axis=-1, keepdims=True) * (1.0 / D))
        return err * (1.0 / D), jnp.full((1, DH), part, F32)

    dy, loss_part = _rowwise(loss_fn, [(y, D, 0), (tgt, D, 0)], [], [(D, F32)], [(1, DH)], name="loss", tm=512)

    gr = {}
    d_mixed_b = _mm(dy, w_b_out, "nt", name="b_out_dx")
    gr["b_w_out"] = _mm(mixed_b, dy, "tn", out_dtype=BF16, name="b_out_dw")[None]
    d_att, d_gmla, d_qm_b, d_gm_b, d_mkv1, d_gq1, d_gk1 = _rowwise_bwd(
        _f_mix, mix_b_rows, [mkv[1], g_mq[1], g_mk[1]], [(d_mixed_b, mla_w + mem_w, 0)],
        [0, 1, 2, 3], [0, 1, 2], out_dtypes=[F32, BF16, BF16, BF16], name="b_mix_bwd")
    dq2, dk2, dv2 = _mla_bwd(q2, k2, v2, att, d_att, lse, tq=512, tk=512, name="mla_bwd")
    d_qraw, d_gqn, d_gqr = _rowwise_bwd(
        _f_q2, [(qraw, H_MLA * 2 * DH, 0), (cs, 2 * DH, 0)], [g_qn, g_qr], [(dq2, H_MLA * 2 * DH, 0)],
        [0], [0, 1], out_dtypes=[BF16], name="q2_bwd")
    d_ql = _mm(d_qraw, w_uq_p, "nt", name="q_up_dx")
    d_wuq = _mm(ql, d_qraw, "tn", out_dtype=BF16, name="q_up_dw")
    gr["b_w_uq"] = d_wuq.reshape(q_lora, H_MLA, 2 * DH)[:, :, :MLA_QK].reshape(1, q_lora, H_MLA * MLA_QK)
    d_qlat, d_gql = _rowwise_bwd(_f_norm, [(pb, q_lora, mla_w // q_lora)], [g_ql], [(d_ql, q_lora, 0)],
                                 [0], [0], out_dtypes=[BF16], name="q_lat_bwd", tm=512)
    d_pb = jnp.concatenate([d_gmla, d_qlat, d_qm_b, d_gm_b], axis=1)
    d_hb = _mm(d_pb, w_b_in, "nt", name="b_in_dx")
    d_wbin = _mm(hb, d_pb, "tn", out_dtype=BF16, name="b_in_dw")
    gr["b_w_in"] = jnp.concatenate([d_wbin[:, mla_w:mla_w + q_lora], d_wbin[:, :mla_w], d_wbin[:, mla_w + q_lora:]],
                                   axis=1)[None]
    d_kvu, d_kr, d_gkn = _rowwise_bwd(
        _f_kv2, [(kvu, H_MLA * 2 * DH, 0), (kr, DH, 0)], [g_kn], [(dk2, H_MLA * 2 * DH, 0), (dv2, mla_w, 0)],
        [0, 1], [0], out_dtypes=[BF16, F32], name="kv2_bwd")
    d_cn = _mm(d_kvu, w_ukv_f, "nt", name="kv_up_dx")
    gr["w_ukv"] = _mm(cn, d_kvu, "tn", out_dtype=BF16, name="kv_up_dw")
    d_ckr, d_gc, d_gkr = _rowwise_bwd(
        _f_kv1, [(ckr, kv_lora + DH, 0), (cs, 2 * DH, 0)], [g_c, g_kr], [(d_cn, kv_lora, 0), (d_kr, DH, 0)],
        [0], [0, 1], out_dtypes=[BF16], name="kv1_bwd", tm=512)
    d_hk = _mm(d_ckr, w_dkv_p, "nt", name="kv_down_dx")
    gr["w_dkv"] = _mm(hk, d_ckr, "tn", out_dtype=BF16, name="kv_down_dw")[:, :kv_lora + ROPE]
    d_x1, d_gkv, d_gb = _rowwise_bwd(_f_norm2, [(x1, D, 0)], [g_kv, g_b], [(d_hk, D, 0), (d_hb, D, 0)],
                                     [0], [0, 1], add=(dy, D, 0), name="b_norm_bwd")
    d_mixed_a = _mm(d_x1, w_a_out, "nt", name="a_out_dx")
    gr["a_w_out"] = _mm(mixed_a, d_x1, "tn", out_dtype=BF16, name="a_out_dw")[None]
    d_sb, d_gsb, d_qm_a, d_gm_a, d_mkv0, d_gq0, d_gk0 = _rowwise_bwd(
        _f_mix, mix_a_rows, [mkv[0], g_mq[0], g_mk[0]], [(d_mixed_a, sb_w + mem_w, 0)],
        [0, 1, 2, 3], [0, 1, 2], out_dtypes=[F32, BF16, BF16, BF16], name="a_mix_bwd")
    d_wmem, d_mnorm = [], []
    for l, d_mkv in enumerate((d_mkv0, d_mkv1)):
        d_mn = _mm(d_mkv, w_mem[l], "nt", name=f"mem_kv_dx{l}")
        d_wmem.append(_mm(mn[l], d_mkv, "tn", out_dtype=BF16, name=f"mem_kv_dw{l}"))
        d_mnorm.append(_rowwise_bwd(_f_norm, [(ms, D, 0)], [row2(mem_norm[l])], [(d_mn, D, 0)], [], [0],
                                    name=f"mem_norm_bwd{l}")[0])
    gr["w_mem_kv"] = jnp.stack(d_wmem)

    mid = tuple(n for n, _ in SHARDED[2:])
    send_mid = _stack_rows([_to_rows(_split8(gr[n], ax), 16) for n, ax in SHARDED[2:]])
    dq, dk, dv, recv_mid = _sb_bwd(qkv, sb, d_sb, tq=512, tk=256, name="sb_bwd", ride=("a2a", send_mid))
    d_pa = jnp.concatenate([dq, dk, dv, d_gsb, d_qm_a, d_gm_a], axis=1)
    send_ain = _mm(h0, d_pa, "tn", out_dtype=BF16, out_split=N_DEV, name="a_in_dw")
    d_h0, recv_ain = _mm(d_pa, w_a_in, "nt", b_rows=D, name="a_in_dx", ride=("a2a", send_ain))
    grad_x, d_ga = _rowwise_bwd(_f_norm, [(xs, D, 0)], [g_a], [(d_h0, D, 0)], [0], [0], add=(d_x1, D, 0),
                                name="a_norm_bwd")
    gr["mem_norm"] = jnp.concatenate(d_mnorm, axis=0)
    gr["g_mem_q"] = jnp.concatenate([d_gq0, d_gq1], axis=0)
    gr["g_mem_k"] = jnp.concatenate([d_gk0, d_gk1], axis=0)
    gr["kv_norm"], gr["b_norm"], gr["g_ckv"], gr["g_k_nope"] = d_gkv, d_gb, d_gc, d_gkn
    gr["g_k_rope"], gr["b_g_q_rope"] = d_gkr[:, :ROPE], d_gqr[:, :ROPE]
    gr["b_g_q_lat"], gr["b_g_q_nope"] = d_gql, d_gqn
    last = ("a_norm",) + SMALL
    shared = jnp.concatenate([gr[n].reshape(-1) for n in SMALL] + [loss_part[0, :1]])
    flat = jnp.concatenate([_split8(d_ga, 1), jnp.broadcast_to(shared[None], (N_DEV, shared.size))], axis=1)
    recv_last = _all_to_all(_to_rows(flat, ROW_MULT), name="exchange_small")

    out = [{}, {}, {}, {}]
    slabs = _reduce_adamw(recv_mid, *[_pack_local(t, mid, 16) for t in (wts, mom, var)], name="reduce_adamw_mid")
    for o, slab in zip(out, slabs):
        o.update(_unpack_local(slab, mid, shapes, 16)[0])
    zero = jnp.zeros((1,), F32)
    slabs = _reduce_adamw(recv_last, *[_to_rows(jnp.concatenate([t[n].reshape(-1) for n in last] + [zero]), ROW_MULT)
                                       for t in (wts, mom, var)], name="reduce_adamw_last")
    for o, slab in zip(out, slabs):
        vec, at = slab.reshape(-1), 0
        for n in last:
            o[n] = vec[at:at + wts[n].size].reshape(shapes[n])
            at += wts[n].size
    loss = slabs[0].reshape(-1)[at]
    slabs = _reduce_adamw(recv_ain, *[t["a_w_in"][0] for t in (wts, mom, var)], name="reduce_adamw_ain")
    for o, slab in zip(out, slabs):
        o["a_w_in"] = slab[None]
    return (loss, grad_x[None], *[o[n] for o in out for n in WEIGHTS])
```

```python
import functools

import jax
import jax.numpy as jnp
from jax import lax
from jax.experimental import pallas as pl
from jax.experimental.pallas import tpu as pltpu

F32, BF16 = jnp.float32, jnp.bfloat16

N_DEV = 8
DH = 128
H_SB, H_MEM, H_MLA = 12, 4, 12
ROPE = 64
MLA_QK = DH + ROPE
EPS = 1e-6
ROPE_THETA = 10000.0
ADAM_LR, ADAM_B1, ADAM_B2, ADAM_EPS, ADAM_WD, ADAM_STEP = 0.001, 0.9, 0.999, 1e-08, 0.01, 10

LANES = 1024
VMEM_LIMIT = 48 * 1024 * 1024
VMEM_LIMIT_BIG = 56 * 1024 * 1024

NN = (((1,), (0,)), ((), ()))
NT = (((1,), (1,)), ((), ()))
TN = (((0,), (0,)), ((), ()))
_DIMS = {"nn": NN, "nt": NT, "tn": TN}


def _dot16(a, b, dims):
    return lax.dot_general(a.astype(BF16), b.astype(BF16), _DIMS[dims], preferred_element_type=F32)


@functools.partial(jax.custom_vjp, nondiff_argnums=(2,))
def _bdot(a, b, dims):
    return _dot16(a, b, dims)


def _bdot_fwd(a, b, dims):
    return _dot16(a, b, dims), (a, b)


def _bdot_bwd(dims, res, g):
    a, b = res
    if dims == "nn":
        return _dot16(g, b, "nt"), _dot16(a, g, "tn")
    return _dot16(g, b, "nn"), _dot16(g, a, "tn")


_bdot.defvjp(_bdot_fwd, _bdot_bwd)


def _tile(n, pref):
    if n <= pref:
        return n
    t = (pref // 128) * 128
    while n % t:
        t -= 128
    return t


def _mesh_pos():
    return lax.axis_index("x"), lax.axis_index("y"), lax.axis_index("c")


def _ride_shape(kind, src):
    return jax.ShapeDtypeStruct((N_DEV, *src.shape) if kind == "gather" else src.shape, src.dtype)


def _ride_scratch():
    return [pltpu.SemaphoreType.DMA((N_DEV - 1,)), pltpu.SemaphoreType.DMA((N_DEV - 1,)), pltpu.SemaphoreType.DMA]


def _ride(kind, phase, s_ref, r_ref, send_sems, recv_sems, local_sem):
    x, y, c = _mesh_pos()
    me = 4 * x + 2 * y + c
    src = (lambda lin: s_ref) if kind == "gather" else (lambda lin: s_ref.at[lin])
    local = pltpu.make_async_copy(src(me), r_ref.at[me], local_sem)
    if phase == "start":
        local.start()
    for k in range(1, N_DEV):
        p = (1 - x if k & 4 else x, 1 - y if k & 2 else y, 1 - c if k & 1 else c)
        lin = 4 * p[0] + 2 * p[1] + p[2]
        cp = pltpu.make_async_remote_copy(
            src_ref=src(lin), dst_ref=r_ref.at[me] if phase == "start" else r_ref.at[lin],
            send_sem=send_sems.at[k - 1], recv_sem=recv_sems.at[k - 1],
            device_id=p, device_id_type=pl.DeviceIdType.MESH)
        if phase == "start":
            cp.start()
        else:
            cp.wait_recv()
            cp.wait_send()
    if phase == "wait":
        local.wait()


def _mm(a, b, dims, *, name, out_dtype=F32, add=None, ride=None, out_split=None, b_rows=None, b_cols=None,
        tm=1024, tn=1024, tk=2048):
    col0 = 0
    if dims == "tn":
        (K, M), (_, N) = a.shape, b.shape
    elif dims == "nt":
        (M, K), N = a.shape, b_rows or b.shape[0]
    else:
        (M, K), N = a.shape, b.shape[1]
        if b_cols is not None:
            col0, N = b_cols
    tm, tk = _tile(M, tm), _tile(K, tk)
    tn = N // out_split if out_split else _tile(N, tn)
    while col0 % tn or N % tn:
        tn -= 128
    jb = col0 // tn
    ni, nj, nk = M // tm, N // tn, K // tk
    n_in = 2 + (add is not None) + (ride is not None)

    def body(*refs):
        a_ref, b_ref, o_ref = refs[0], refs[1], refs[n_in]
        acc_ref = refs[n_in + 1 + (ride is not None)]
        i, j, k = pl.program_id(0), pl.program_id(1), pl.program_id(2)
        if ride is not None:
            ride_refs = (refs[n_in - 1], refs[n_in + 1], *refs[-3:])

            @pl.when((i == 0) & (j == 0) & (k == 0))
            def _():
                _ride(ride[0], "start", *ride_refs)

        if nk == 1:
            r = _dot16(a_ref[...], b_ref[...], dims)
            o_ref[...] = (r if add is None else r + refs[2][...]).astype(o_ref.dtype)
        else:
            @pl.when(k == 0)
            def _():
                acc_ref[...] = jnp.zeros_like(acc_ref)

            acc_ref[...] += _dot16(a_ref[...], b_ref[...], dims)

            @pl.when(k == nk - 1)
            def _():
                r = acc_ref[...]
                if add is not None:
                    r = r + refs[2][...]
                o_ref[...] = r.astype(o_ref.dtype)

        if ride is not None:
            @pl.when((i == ni - 1) & (j == nj - 1) & (k == nk - 1))
            def _():
                _ride(ride[0], "wait", *ride_refs)

    a_spec = pl.BlockSpec((tk, tm), lambda i, j, k: (k, i)) if dims == "tn" else pl.BlockSpec((tm, tk), lambda i, j, k: (i, k))
    b_spec = pl.BlockSpec((tn, tk), lambda i, j, k: (j, k)) if dims == "nt" else pl.BlockSpec((tk, tn), lambda i, j, k: (k, j + jb))
    o_spec = pl.BlockSpec((tm, tn), lambda i, j, k: (i, j))
    in_specs, args = [a_spec, b_spec], [a, b]
    if add is not None:
        in_specs.append(o_spec)
        args.append(add)
    out_specs, out_shape = [o_spec], [jax.ShapeDtypeStruct((M, N), out_dtype)]
    scratch = [pltpu.VMEM((tm, tn) if nk > 1 else (8, 128), F32)]
    if out_split:
        out_specs = [pl.BlockSpec((None, tm, tn), lambda i, j, k: (j, i, 0))]
        out_shape = [jax.ShapeDtypeStruct((out_split, M, tn), out_dtype)]
    if ride is not None:
        in_specs.append(pl.BlockSpec(memory_space=pl.ANY))
        args.append(ride[1])
        out_specs.append(pl.BlockSpec(memory_space=pl.ANY))
        out_shape.append(_ride_shape(*ride))
        scratch += _ride_scratch()
    sem = ("arbitrary",) * 3 if ride is not None else ("parallel", "parallel", "arbitrary")
    res = pl.pallas_call(
        body, name=name, grid=(ni, nj, nk), in_specs=in_specs, out_specs=out_specs, out_shape=out_shape,
        scratch_shapes=scratch,
        compiler_params=pltpu.CompilerParams(dimension_semantics=sem, vmem_limit_bytes=VMEM_LIMIT),
    )(*args)
    return res[0] if ride is None else res


def _mm_loss(a, b, res, tgt, *, name, tm=1024, tn=1024):
    (M, K), N = a.shape, b.shape[1]
    tm, tn = _tile(M, tm), _tile(N, tn)

    def body(a_ref, b_ref, r_ref, t_ref, dy_ref, loss_ref):
        @pl.when((pl.program_id(0) == 0) & (pl.program_id(1) == 0))
        def _():
            loss_ref[...] = jnp.zeros_like(loss_ref)

        err = _dot16(a_ref[...], b_ref[...], "nn") + r_ref[...] - t_ref[...]
        dy_ref[...] = err * (1.0 / N)
        loss_ref[...] += 0.5 * jnp.sum(err * err) * (1.0 / N)

    tile = pl.BlockSpec((tm, tn), lambda i, j: (i, j))
    return pl.pallas_call(
        body, name=name, grid=(M // tm, N // tn),
        in_specs=[pl.BlockSpec((tm, K), lambda i, j: (i, 0)), pl.BlockSpec((K, tn), lambda i, j: (0, j)), tile, tile],
        out_specs=[tile, pl.BlockSpec((1, DH), lambda i, j: (0, 0))],
        out_shape=[jax.ShapeDtypeStruct((M, N), F32), jax.ShapeDtypeStruct((1, DH), F32)],
        compiler_params=pltpu.CompilerParams(dimension_semantics=("arbitrary", "arbitrary"), vmem_limit_bytes=VMEM_LIMIT),
    )(a, b, res, tgt)


def _rowwise(fn, rows, consts, outs, accs=(), *, name, tm=256):
    S = rows[0][0].shape[0]
    tm = min(tm, S)
    nr, nc, no = len(rows), len(consts), len(outs)

    def body(*refs):
        res = fn(*[r[...] for r in refs[:nr + nc]])
        res = tuple(res) if isinstance(res, (tuple, list)) else (res,)
        orefs, arefs = refs[nr + nc:nr + nc + no], refs[nr + nc + no:]
        for r, v in zip(orefs, res[:no]):
            r[...] = v.astype(r.dtype)
        if arefs:
            @pl.when(pl.program_id(0) == 0)
            def _():
                for r in arefs:
                    r[...] = jnp.zeros_like(r)

            for r, v in zip(arefs, res[no:]):
                r[...] += v

    in_specs = [pl.BlockSpec((tm, w), lambda i, cb=cb: (i, cb)) for (_, w, cb) in rows]
    in_specs += [pl.BlockSpec(c.shape, lambda i: (0, 0)) for c in consts]
    out_specs = [pl.BlockSpec((tm, w), lambda i: (i, 0)) for (w, _) in outs]
    out_specs += [pl.BlockSpec(s, lambda i: (0, 0)) for s in accs]
    out_shape = [jax.ShapeDtypeStruct((S, w), dt) for (w, dt) in outs]
    out_shape += [jax.ShapeDtypeStruct(s, F32) for s in accs]
    res = pl.pallas_call(
        body, name=name, grid=(S // tm,), in_specs=in_specs, out_specs=out_specs, out_shape=out_shape,
        compiler_params=pltpu.CompilerParams(dimension_semantics=("arbitrary",), vmem_limit_bytes=VMEM_LIMIT),
    )(*[r[0] for r in rows], *consts)
    return res


def _rowwise_bwd(f, rows, consts, cts, row_grads, const_grads, *, name, add=None, out_dtypes=None, tm=256):
    nr, nc, nct = len(rows), len(consts), len(cts)
    all_rows = list(rows) + list(cts) + ([add] if add is not None else [])

    def fn(*args):
        nrow = len(all_rows)
        prim = [x.astype(F32) for x in args[:nr]] + [x.astype(F32) for x in args[nrow:]]
        ct = tuple(x.astype(F32) for x in args[nr:nr + nct])
        out, vjp = jax.vjp(f, *prim)
        gs = vjp(ct if isinstance(out, (tuple, list)) else ct[0])
        res = [gs[k] for k in row_grads]
        if add is not None:
            res[0] = res[0] + args[nrow - 1]
        return tuple(res) + tuple(gs[nr + k] for k in const_grads)

    out_dtypes = out_dtypes or [F32] * len(row_grads)
    outs = [(rows[k][1], dt) for k, dt in zip(row_grads, out_dtypes)]
    accs = [consts[k].shape for k in const_grads]
    return _rowwise(fn, all_rows, consts, outs, accs, name=name, tm=tm)


def _rms(x, g, n=None):
    ms = jnp.sum(x * x, axis=-1, keepdims=True) * (1.0 / (n or x.shape[-1]))
    return x * lax.rsqrt(ms + EPS) * g


def _sigmoid(x):
    return 1.0 / (1.0 + jnp.exp(-x))


def _swap_halves_exact(x):
    r = lax.broadcasted_iota(jnp.int32, (DH, DH), 0)
    c = lax.broadcasted_iota(jnp.int32, (DH, DH), 1)
    half = ROPE // 2
    perm = (((r < half) & (c == r + half)) | ((r >= half) & (r < ROPE) & (c == r - half))).astype(BF16)
    hi = x.astype(BF16)
    rest = x - hi.astype(F32)
    mid = rest.astype(BF16)
    lo = (rest - mid.astype(F32)).astype(BF16)
    return sum(lax.dot_general(p, perm, NN, preferred_element_type=F32) for p in (hi, mid, lo))


_swap_halves = jax.custom_vjp(_swap_halves_exact)
_swap_halves.defvjp(lambda x: (_swap_halves_exact(x), None), lambda _, g: (_swap_halves_exact(g),))


def _rope128(x, g128, cs):
    y = _rms(x, g128, n=ROPE)
    return y * cs[:, :DH] + _swap_halves(y) * cs[:, DH:]


def _f_norm(x, g):
    return _rms(x, g)


def _f_norm2(x, g1, g2):
    xn = x * lax.rsqrt(jnp.mean(x * x, axis=-1, keepdims=True) + EPS)
    return xn * g1, xn * g2


def _f_kv1(ckr, cs, g_ckv, g_kr):
    w = g_ckv.shape[-1]
    return _rms(ckr[:, :w], g_ckv), _rope128(ckr[:, w:], g_kr, cs)


def _f_kv2(kv, kr, g_kn):
    ks, vs = [], []
    for h in range(H_MLA):
        ks += [_rms(kv[:, 2 * DH * h:2 * DH * h + DH], g_kn), kr]
        vs.append(kv[:, 2 * DH * h + DH:2 * DH * (h + 1)])
    return jnp.concatenate(ks, axis=1), jnp.concatenate(vs, axis=1)


def _f_q2(q, cs, g_n, g_r):
    out = []
    for h in range(H_MLA):
        out += [_rms(q[:, 2 * DH * h:2 * DH * h + DH], g_n), _rope128(q[:, 2 * DH * h + DH:2 * DH * (h + 1)], g_r, cs)]
    return jnp.concatenate(out, axis=1)


def _f_mix(att, g_att, q_m, g_m, mkv, g_q, g_k):
    mem_w = H_MEM * DH
    heads = []
    for h in range(H_MEM):
        kh = _rms(mkv[:, h * DH:(h + 1) * DH], g_k)
        vh = mkv[:, mem_w + h * DH:mem_w + (h + 1) * DH]
        qh = _rms(q_m[:, h * DH:(h + 1) * DH], g_q)
        s = _bdot(qh, kh, "nt") * (DH ** -0.5)
        p = jnp.exp(s - lax.stop_gradient(jnp.max(s, axis=-1, keepdims=True)))
        p = p / jnp.sum(p, axis=-1, keepdims=True)
        heads.append(_bdot(p, vh, "nn"))
    mo = jnp.concatenate(heads, axis=1)
    return jnp.concatenate([att * (g_att * _sigmoid(g_att)), mo * (g_m * _sigmoid(g_m))], axis=1)


def _split_dot(x, u):
    hi = x.astype(BF16)
    lo = (x - hi.astype(F32)).astype(BF16)
    return (lax.dot_general(hi, u, NN, preferred_element_type=F32)
            + lax.dot_general(lo, u, NN, preferred_element_type=F32))


def _tri(t):
    r = lax.broadcasted_iota(jnp.int32, (t, t), 0)
    c = lax.broadcasted_iota(jnp.int32, (t, t), 1)
    return r, c


def _strict_lower(t):
    r, c = _tri(t)
    return (r > c).astype(BF16)


def _head_blocks_t(x, w, tk, *, name):
    S = x.shape[0]
    H = x.shape[1] // w

    def body(x_ref, o_ref):
        for h in range(H):
            o_ref[h] = x_ref[:, h * w:(h + 1) * w].T

    return pl.pallas_call(
        body, name=name, grid=(S // tk,),
        in_specs=[pl.BlockSpec((tk, H * w), lambda i: (i, 0))],
        out_specs=pl.BlockSpec((H, None, w, tk), lambda i: (0, i, 0, 0)),
        out_shape=jax.ShapeDtypeStruct((H, S // tk, w, tk), x.dtype),
        compiler_params=pltpu.CompilerParams(dimension_semantics=("parallel",), vmem_limit_bytes=VMEM_LIMIT),
    )(x)


def _rows_ahead(tq, tk):
    return lax.broadcasted_iota(jnp.int32, (tq, tk), 0) - lax.broadcasted_iota(jnp.int32, (tq, tk), 1)


EXP_UNDERFLOW = -110.0


def _log_one_minus_beta(zr, scale):
    zs, nz = zr * scale, zr * (-scale)
    return zs, jnp.minimum(nz, 0.0) - jnp.log(1.0 + jnp.exp(jnp.minimum(zs, nz)))


def _sb_fwd(qkv, *, tq, tk, name, ride=None):
    S = qkv.shape[0]
    tq, tk = min(tq, S), min(tk, tq, S)
    nd = tq // tk
    H = H_SB
    scale = DH ** -0.5

    nq = S // tq

    def body(q_ref, k_ref, v_ref, *rest):
        o_ref = rest[1] if ride is not None else rest[0]
        h, i = pl.program_id(0), pl.program_id(1)
        if ride is not None:
            ride_refs = (rest[0], *rest[2:])

            @pl.when((h == 0) & (i == 0))
            def _():
                _ride(ride[0], "start", *ride_refs)

        q = q_ref[...]
        u = _strict_lower(tk)
        ahead = _rows_ahead(tq, tk)

        def block(j, acc, cb, keep):
            off = pl.multiple_of(j * tk, tk)
            k = k_ref[pl.ds(off, tk), :]
            v = v_ref[pl.ds(off, tk), :]
            z, l = _log_one_minus_beta(lax.dot_general(q, k, NT, preferred_element_type=F32), scale)
            if keep is not None:
                l = jnp.where(keep, l, 0.0)
            a = jnp.exp((z + l) + (_split_dot(l, u) + cb))
            if keep is not None:
                a = jnp.where(keep, a, 0.0)
            acc = acc + lax.dot_general(a.astype(BF16), v, NN, preferred_element_type=F32)
            return acc, cb + jnp.sum(l, axis=1, keepdims=True)

        carry = (jnp.zeros((tq, DH), F32), jnp.zeros((tq, 1), F32))
        for t in reversed(range(nd)):
            carry = block(i * nd + t, *carry, ahead > t * tk)
        _, acc, _ = lax.while_loop(
            lambda st: (st[0] < i * nd) & (jnp.max(st[2]) > EXP_UNDERFLOW),
            lambda st: (st[0] + 1, *block(i * nd - 1 - st[0], st[1], st[2], None)), (jnp.int32(0), *carry))
        o_ref[...] = acc

        if ride is not None:
            @pl.when((h == H - 1) & (i == nq - 1))
            def _():
                _ride(ride[0], "wait", *ride_refs)

    in_specs = [pl.BlockSpec((tq, DH), lambda h, i: (i, h)),
                pl.BlockSpec((S, DH), lambda h, i: (0, H + h)),
                pl.BlockSpec((S, DH), lambda h, i: (0, 2 * H + h))]
    out_specs = [pl.BlockSpec((tq, DH), lambda h, i: (i, h))]
    out_shape = [jax.ShapeDtypeStruct((S, H * DH), F32)]
    args, scratch = [qkv, qkv, qkv], []
    if ride is not None:
        in_specs.append(pl.BlockSpec(memory_space=pl.ANY))
        args.append(ride[1])
        out_specs.append(pl.BlockSpec(memory_space=pl.ANY))
        out_shape.append(_ride_shape(*ride))
        scratch = _ride_scratch()
    res = pl.pallas_call(
        body, name=name, grid=(H, nq), in_specs=in_specs, out_specs=out_specs, out_shape=out_shape,
        scratch_shapes=scratch,
        compiler_params=pltpu.CompilerParams(dimension_semantics=("arbitrary", "arbitrary"), vmem_limit_bytes=VMEM_LIMIT),
    )(*args)
    return res[0] if ride is None else res


def _sb_bwd(qkv, o, do, *, tq, tk, name, ride=None):
    S = qkv.shape[0]
    tq, tk = min(tq, S), min(tk, tq, S)
    nd = tq // tk
    H = H_SB
    scale = DH ** -0.5

    nq = S // tq

    def body(q_ref, k_ref, v_ref, o_ref, do_ref, *rest):
        if ride is not None:
            dq_ref, dk_out, dv_out, dk_ref, dv_ref = rest[1], rest[2], rest[3], rest[5], rest[6]
            ride_refs = (rest[0], rest[4], *rest[7:])
        else:
            dq_ref, dk_out, dv_out, dk_ref, dv_ref = rest
        h, i = pl.program_id(0), pl.program_id(1)
        if ride is not None:
            @pl.when((h == 0) & (i == 0))
            def _():
                _ride(ride[0], "start", *ride_refs)

        @pl.when(i == 0)
        def _():
            dk_ref[...] = jnp.zeros_like(dk_ref)
            dv_ref[...] = jnp.zeros_like(dv_ref)

        q = q_ref[...]
        do = do_ref[...]
        do16 = do.astype(BF16)
        dsum = jnp.sum(do16.astype(F32) * o_ref[...], axis=1, keepdims=True)
        u = _strict_lower(tk)
        ahead = _rows_ahead(tq, tk)

        def block(j, dq, cb, ce, keep):
            off = pl.multiple_of(j * tk, tk)
            k = k_ref[pl.ds(off, tk), :]
            v = v_ref[pl.ds(off, tk), :]
            z, l = _log_one_minus_beta(lax.dot_general(q, k, NT, preferred_element_type=F32), scale)
            if keep is not None:
                l = jnp.where(keep, l, 0.0)
            log_beta = z + l
            a = jnp.exp(log_beta + (_split_dot(l, u) + cb))
            if keep is not None:
                a = jnp.where(keep, a, 0.0)
            a16 = a.astype(BF16)
            e = a16.astype(F32) * lax.dot_general(do16, v, NT, preferred_element_type=F32)
            left = dsum - (ce + _split_dot(e, u) + e)
            dz = (e - jnp.exp(log_beta) * (e + left)) * scale
            if keep is not None:
                dz = jnp.where(keep, dz, 0.0)
            dz = dz.astype(BF16)
            dq = dq + lax.dot_general(dz, k, NN, preferred_element_type=F32)
            dk_ref[pl.ds(off, tk), :] += lax.dot_general(dz, q, TN, preferred_element_type=F32)
            dv_ref[pl.ds(off, tk), :] += lax.dot_general(a16, do16, TN, preferred_element_type=F32)
            return dq, cb + jnp.sum(l, axis=1, keepdims=True), ce + jnp.sum(e, axis=1, keepdims=True)

        zero = jnp.zeros((tq, 1), F32)
        carry = (jnp.zeros((tq, DH), F32), zero, zero)
        for t in reversed(range(nd)):
            carry = block(i * nd + t, *carry, ahead > t * tk)
        _, dq, _, _ = lax.while_loop(
            lambda st: (st[0] < i * nd) & (jnp.max(st[2]) > EXP_UNDERFLOW),
            lambda st: (st[0] + 1, *block(i * nd - 1 - st[0], st[1], st[2], st[3], None)), (jnp.int32(0), *carry))
        dq_ref[...] = dq.astype(dq_ref.dtype)

        @pl.when(i == nq - 1)
        def _():
            dk_out[...] = dk_ref[...].astype(dk_out.dtype)
            dv_out[...] = dv_ref[...].astype(dv_out.dtype)

        if ride is not None:
            @pl.when((h == H - 1) & (i == nq - 1))
            def _():
                _ride(ride[0], "wait", *ride_refs)

    blk = pl.BlockSpec((tq, DH), lambda h, i: (i, h))
    whole = pl.BlockSpec((S, DH), lambda h, i: (0, h))
    shp = jax.ShapeDtypeStruct((S, H * DH), BF16)
    in_specs = [blk, pl.BlockSpec((S, DH), lambda h, i: (0, H + h)), pl.BlockSpec((S, DH), lambda h, i: (0, 2 * H + h)),
                blk, blk]
    out_specs, out_shape = [blk, whole, whole], [shp, shp, shp]
    args, scratch = [qkv, qkv, qkv, o, do], [pltpu.VMEM((S, DH), F32), pltpu.VMEM((S, DH), F32)]
    if ride is not None:
        in_specs.append(pl.BlockSpec(memory_space=pl.ANY))
        args.append(ride[1])
        out_specs.append(pl.BlockSpec(memory_space=pl.ANY))
        out_shape.append(_ride_shape(*ride))
        scratch += _ride_scratch()
    return pl.pallas_call(
        body, name=name, grid=(H, nq), in_specs=in_specs, out_specs=out_specs, out_shape=out_shape,
        scratch_shapes=scratch,
        compiler_params=pltpu.CompilerParams(dimension_semantics=("arbitrary", "arbitrary"), vmem_limit_bytes=VMEM_LIMIT),
    )(*args)


def _mla_fwd(q, k, v, *, tq, tk, name):
    S = q.shape[0]
    tq, tk = min(tq, S), min(tk, tq, S)
    nd, nb = tq // tk, S // tk
    H = H_MLA
    scale = MLA_QK ** -0.5
    vt = _head_blocks_t(v, DH, tk, name=name + "_vt")

    def body(q_ref, k_ref, vt_ref, o_ref, lse_ref):
        i = pl.program_id(1)
        qb = q_ref[...]
        behind = lax.broadcasted_iota(jnp.int32, (tk, tq), 1) - lax.broadcasted_iota(jnp.int32, (tk, tq), 0)

        def block(j, m, den, acct, keep):
            off = pl.multiple_of(j * tk, tk)
            st = lax.dot_general(k_ref[pl.ds(off, tk), :], qb, NT, preferred_element_type=F32) * scale
            if keep is not None:
                st = jnp.where(keep, st, -1e30)
            m_new = jnp.maximum(m, jnp.max(st, axis=0, keepdims=True))
            pt = jnp.exp(st - m_new)
            alpha = jnp.exp(m - m_new)
            den = alpha * den + jnp.sum(pt, axis=0, keepdims=True)
            acct = alpha * acct + lax.dot_general(vt_ref[j], pt.astype(BF16), NN, preferred_element_type=F32)
            return m_new, den, acct

        init = (jnp.full((1, tq), -1e30, F32), jnp.zeros((1, tq), F32), jnp.zeros((DH, tq), F32))
        carry = lax.fori_loop(0, i * nd, lambda j, carry: block(j, *carry, None), init)
        for t in range(nd):
            carry = block(i * nd + t, *carry, behind >= t * tk)
        m, den, acct = carry
        o_ref[...] = (acct / den).T
        lse_ref[0] = m + jnp.log(den)

    return pl.pallas_call(
        body, name=name, grid=(H, S // tq),
        in_specs=[pl.BlockSpec((tq, 2 * DH), lambda h, i: (i, h)),
                  pl.BlockSpec((S, 2 * DH), lambda h, i: (0, h)),
                  pl.BlockSpec((None, nb, DH, tk), lambda h, i: (h, 0, 0, 0))],
        out_specs=[pl.BlockSpec((tq, DH), lambda h, i: (i, h)), pl.BlockSpec((1, 1, tq), lambda h, i: (h, 0, i))],
        out_shape=[jax.ShapeDtypeStruct((S, H * DH), F32), jax.ShapeDtypeStruct((H, 1, S), F32)],
        compiler_params=pltpu.CompilerParams(dimension_semantics=("arbitrary", "arbitrary"), vmem_limit_bytes=VMEM_LIMIT),
    )(q, k, vt)


def _mla_bwd(q, k, v, o, do, lse, *, tq, tk, name):
    S = q.shape[0]
    tq, tk = min(tq, S), min(tk, tq, S)
    nd, nb = tq // tk, S // tk
    H = H_MLA
    scale = MLA_QK ** -0.5
    kt = _head_blocks_t(k, 2 * DH, tk, name=name + "_kt")

    def body(q_ref, k_ref, kt_ref, v_ref, o_ref, do_ref, lse_ref, dq_ref, dk_ref, dv_ref):
        i = pl.program_id(1)

        @pl.when(i == 0)
        def _():
            dk_ref[...] = jnp.zeros_like(dk_ref)
            dv_ref[...] = jnp.zeros_like(dv_ref)

        qb = q_ref[...]
        do = do_ref[...]
        do16 = do.astype(BF16)
        dsum = jnp.sum((do * o_ref[...]).T, axis=0, keepdims=True)
        lse = lse_ref[0]
        behind = lax.broadcasted_iota(jnp.int32, (tk, tq), 1) - lax.broadcasted_iota(jnp.int32, (tk, tq), 0)

        def block(j, dqt, keep):
            off = pl.multiple_of(j * tk, tk)
            kb = k_ref[pl.ds(off, tk), :]
            vb = v_ref[pl.ds(off, tk), :]
            st = lax.dot_general(kb, qb, NT, preferred_element_type=F32) * scale
            if keep is not None:
                st = jnp.where(keep, st, -1e30)
            pt = jnp.exp(st - lse)
            dpt = lax.dot_general(vb, do16, NT, preferred_element_type=F32)
            dst = (pt * (dpt - dsum) * scale).astype(BF16)
            dk_ref[pl.ds(off, tk), :] += lax.dot_general(dst, qb, NN, preferred_element_type=F32)
            dv_ref[pl.ds(off, tk), :] += lax.dot_general(pt.astype(BF16), do16, NN, preferred_element_type=F32)
            return dqt + lax.dot_general(kt_ref[j], dst, NN, preferred_element_type=F32)

        dqt = lax.fori_loop(0, i * nd, lambda j, dqt: block(j, dqt, None), jnp.zeros((2 * DH, tq), F32))
        for t in range(nd):
            dqt = block(i * nd + t, dqt, behind >= t * tk)
        dq_ref[...] = dqt.T

    blk = pl.BlockSpec((tq, DH), lambda h, i: (i, h))
    blk2 = pl.BlockSpec((tq, 2 * DH), lambda h, i: (i, h))
    return pl.pallas_call(
        body, name=name, grid=(H, S // tq),
        in_specs=[blk2, pl.BlockSpec((S, 2 * DH), lambda h, i: (0, h)),
                  pl.BlockSpec((None, nb, 2 * DH, tk), lambda h, i: (h, 0, 0, 0)),
                  pl.BlockSpec((S, DH), lambda h, i: (0, h)),
                  blk, blk, pl.BlockSpec((1, 1, tq), lambda h, i: (h, 0, i))],
        out_specs=[blk2, pl.BlockSpec((S, 2 * DH), lambda h, i: (0, h)), pl.BlockSpec((S, DH), lambda h, i: (0, h))],
        out_shape=[jax.ShapeDtypeStruct((S, H * 2 * DH), F32), jax.ShapeDtypeStruct((S, H * 2 * DH), F32),
                   jax.ShapeDtypeStruct((S, H * DH), F32)],
        compiler_params=pltpu.CompilerParams(dimension_semantics=("arbitrary", "arbitrary"), vmem_limit_bytes=VMEM_LIMIT_BIG),
    )(q, k, kt, v, o, do, lse)


def _all_gather_columns(block, *, name):
    R, C = block.shape

    def body(x_ref, out_ref, send_sems, recv_sems, local_sem):
        x, y, c = _mesh_pos()
        me, sibling = (x, y, c), (x, y, 1 - c)
        chips = [(1 - x, y), (x, 1 - y), (1 - x, 1 - y)]

        def slot(px, py, pc):
            return out_ref.at[:, pl.ds(pl.multiple_of((4 * px + 2 * py + pc) * C, 128), C)]

        def copy(k, blk, to, src=None):
            return pltpu.make_async_remote_copy(
                src_ref=slot(*blk) if src is None else src, dst_ref=slot(*blk),
                send_sem=send_sems.at[k], recv_sem=recv_sems.at[k],
                device_id=to, device_id_type=pl.DeviceIdType.MESH)

        mine = pltpu.make_async_copy(x_ref, slot(*me), local_sem)
        mine.start()
        first = [copy(0, me, sibling, src=x_ref)]
        first += [copy(1 + j, me, (*chip, c), src=x_ref) for j, chip in enumerate(chips)]
        for cp in first:
            cp.start()
        passed = [copy(4 + j, (*chip, c), sibling) for j, chip in enumerate(chips)]
        for j, chip in enumerate(chips):
            copy(1 + j, (*chip, c), me).wait_recv()
            passed[j].start()
        copy(0, sibling, me).wait_recv()
        for j, chip in enumerate(chips):
            copy(4 + j, (*chip, 1 - c), me).wait_recv()
        for cp in first + passed:
            cp.wait_send()
        mine.wait()

    return pl.pallas_call(
        body, name=name,
        out_shape=jax.ShapeDtypeStruct((R, N_DEV * C), block.dtype),
        in_specs=[pl.BlockSpec(memory_space=pl.ANY)], out_specs=pl.BlockSpec(memory_space=pl.ANY),
        scratch_shapes=[pltpu.SemaphoreType.DMA((7,)), pltpu.SemaphoreType.DMA((7,)), pltpu.SemaphoreType.DMA],
    )(block)


def _all_to_all(send, *, name):
    def body(*refs):
        _ride("a2a", "start", *refs)
        _ride("a2a", "wait", *refs)

    return pl.pallas_call(
        body, name=name, out_shape=_ride_shape("a2a", send),
        in_specs=[pl.BlockSpec(memory_space=pl.ANY)], out_specs=pl.BlockSpec(memory_space=pl.ANY),
        scratch_shapes=_ride_scratch(),
    )(send)


def _reduce_adamw(recv, w, m, v, *, name):
    R, C = w.shape
    tr = next(t for t in (128, 64, 32, 16, 8) if R % t == 0)

    def body(g_ref, w_ref, m_ref, v_ref, og_ref, od_ref, om_ref, ov_ref):
        g = g_ref[0].astype(F32)
        for s in range(1, N_DEV):
            g = g + g_ref[s].astype(F32)
        mn = ADAM_B1 * m_ref[...] + (1.0 - ADAM_B1) * g
        vn = ADAM_B2 * v_ref[...] + (1.0 - ADAM_B2) * jnp.square(g)
        m_hat = mn / (1.0 - ADAM_B1 ** ADAM_STEP)
        v_hat = vn / (1.0 - ADAM_B2 ** ADAM_STEP)
        og_ref[...] = g
        od_ref[...] = -ADAM_LR * (m_hat / (jnp.sqrt(v_hat) + ADAM_EPS) + ADAM_WD * w_ref[...])
        om_ref[...] = mn
        ov_ref[...] = vn

    blk = pl.BlockSpec((tr, C), lambda i: (i, 0))
    shp = jax.ShapeDtypeStruct((R, C), F32)
    return pl.pallas_call(
        body, name=name, grid=(R // tr,),
        in_specs=[pl.BlockSpec((N_DEV, tr, C), lambda i: (0, i, 0)), blk, blk, blk],
        out_specs=[blk, blk, blk, blk], out_shape=[shp, shp, shp, shp],
        compiler_params=pltpu.CompilerParams(dimension_semantics=("parallel",), vmem_limit_bytes=VMEM_LIMIT),
    )(recv, w, m, v)


SHARDED = (("a_norm", 1), ("a_w_in", 2), ("a_w_out", 1), ("w_dkv", 0), ("w_ukv", 1), ("b_w_in", 2),
           ("b_w_uq", 2), ("b_w_out", 1), ("w_mem_kv", 1))
SMALL = ("kv_norm", "g_ckv", "g_k_nope", "g_k_rope", "b_norm", "b_g_q_lat", "b_g_q_nope", "b_g_q_rope",
         "mem_norm", "g_mem_q", "g_mem_k")
WEIGHTS = ("a_norm", "a_w_in", "a_w_out", "kv_norm", "w_dkv", "g_ckv", "w_ukv", "g_k_nope", "g_k_rope", "b_norm",
           "b_w_in", "b_g_q_lat", "b_w_uq", "b_g_q_nope", "b_g_q_rope", "b_w_out", "mem_norm", "w_mem_kv",
           "g_mem_q", "g_mem_k")
ROW_MULT = 8
ROW_BLOCK = 128


def _rows_of(n, mult):
    rows = -(-n // LANES)
    return -(-rows // mult) * mult


def _to_rows(flat, mult):
    n = flat.shape[-1]
    rows = _rows_of(n, mult)
    pad = [(0, 0)] * (flat.ndim - 1) + [(0, rows * LANES - n)]
    return jnp.pad(flat, pad).reshape(*flat.shape[:-1], rows, LANES)


def _split8(full, axis):
    shp = full.shape
    t = full.reshape(*shp[:axis], N_DEV, shp[axis] // N_DEV, *shp[axis + 1:])
    return jnp.moveaxis(t, axis, 0).reshape(N_DEV, -1)


def _join8(rows, axis, shard_shape):
    t = rows.reshape(N_DEV, *shard_shape)
    t = jnp.moveaxis(t, 0, axis)
    return t.reshape(*shard_shape[:axis], N_DEV * shard_shape[axis], *shard_shape[axis + 1:])


def _stack_rows(parts, block=ROW_BLOCK):
    rows = sum(p.shape[-2] for p in parts)
    if rows % block:
        parts = list(parts) + [jnp.zeros((*parts[0].shape[:-2], -rows % block, LANES), parts[0].dtype)]
    return jnp.concatenate(parts, axis=-2)


def _pack_local(vals, names, mult):
    return _stack_rows([_to_rows(vals[n].reshape(-1), mult) for n in names])


def _unpack_local(slab, names, shapes, mult):
    out, row = {}, 0
    for n in names:
        size = 1
        for d in shapes[n]:
            size *= d
        rows = _rows_of(size, mult)
        out[n] = slab[row:row + rows].reshape(-1)[:size].reshape(shapes[n])
        row += rows
    return out, row


def kernel(x, mem, positions, a_norm, a_w_in, a_w_out, kv_norm, w_dkv, g_ckv, w_ukv, g_k_nope, g_k_rope, b_norm, b_w_in, b_g_q_lat, b_w_uq, b_g_q_nope, b_g_q_rope, b_w_out, mem_norm, w_mem_kv, g_mem_q, g_mem_k, loss_target, m_a_norm, m_a_w_in, m_a_w_out, m_kv_norm, m_w_dkv, m_g_ckv, m_w_ukv, m_g_k_nope, m_g_k_rope, m_b_norm, m_b_w_in, m_b_g_q_lat, m_b_w_uq, m_b_g_q_nope, m_b_g_q_rope, m_b_w_out, m_mem_norm, m_w_mem_kv, m_g_mem_q, m_g_mem_k, v_a_norm, v_a_w_in, v_a_w_out, v_kv_norm, v_w_dkv, v_g_ckv, v_w_ukv, v_g_k_nope, v_g_k_rope, v_b_norm, v_b_w_in, v_b_g_q_lat, v_b_w_uq, v_b_g_q_nope, v_b_g_q_rope, v_b_w_out, v_mem_norm, v_w_mem_kv, v_g_mem_q, v_g_mem_k):
    wts = dict(a_norm=a_norm, a_w_in=a_w_in, a_w_out=a_w_out, kv_norm=kv_norm, w_dkv=w_dkv, g_ckv=g_ckv, w_ukv=w_ukv,
               g_k_nope=g_k_nope, g_k_rope=g_k_rope, b_norm=b_norm, b_w_in=b_w_in, b_g_q_lat=b_g_q_lat, b_w_uq=b_w_uq,
               b_g_q_nope=b_g_q_nope, b_g_q_rope=b_g_q_rope, b_w_out=b_w_out, mem_norm=mem_norm, w_mem_kv=w_mem_kv,
               g_mem_q=g_mem_q, g_mem_k=g_mem_k)
    mom = dict(a_norm=m_a_norm, a_w_in=m_a_w_in, a_w_out=m_a_w_out, kv_norm=m_kv_norm, w_dkv=m_w_dkv, g_ckv=m_g_ckv,
               w_ukv=m_w_ukv, g_k_nope=m_g_k_nope, g_k_rope=m_g_k_rope, b_norm=m_b_norm, b_w_in=m_b_w_in,
               b_g_q_lat=m_b_g_q_lat, b_w_uq=m_b_w_uq, b_g_q_nope=m_b_g_q_nope, b_g_q_rope=m_b_g_q_rope,
               b_w_out=m_b_w_out, mem_norm=m_mem_norm, w_mem_kv=m_w_mem_kv, g_mem_q=m_g_mem_q, g_mem_k=m_g_mem_k)
    var = dict(a_norm=v_a_norm, a_w_in=v_a_w_in, a_w_out=v_a_w_out, kv_norm=v_kv_norm, w_dkv=v_w_dkv, g_ckv=v_g_ckv,
               w_ukv=v_w_ukv, g_k_nope=v_g_k_nope, g_k_rope=v_g_k_rope, b_norm=v_b_norm, b_w_in=v_b_w_in,
               b_g_q_lat=v_b_g_q_lat, b_w_uq=v_b_w_uq, b_g_q_nope=v_b_g_q_nope, b_g_q_rope=v_b_g_q_rope,
               b_w_out=v_b_w_out, mem_norm=v_mem_norm, w_mem_kv=v_w_mem_kv, g_mem_q=v_g_mem_q, g_mem_k=v_g_mem_k)
    shapes = {n: wts[n].shape for n in WEIGHTS}
    S, D = x.shape[1], x.shape[2]
    xs, ms, tgt = x[0], mem[0], loss_target[0]
    sb_w, mem_w, mla_w = H_SB * DH, H_MEM * DH, H_MLA * DH
    q_lora, kv_lora = b_g_q_lat.shape[-1], g_ckv.shape[-1]

    def pieces_of(names):
        return [_to_rows(lax.bitcast_convert_type(a_norm.reshape(-1), BF16).reshape(-1), 16) if n == "a_norm"
                else _to_rows(wts[n].astype(BF16).reshape(-1), 16) for n in names]

    def unpack_gathered(gathered, names):
        full, row = {}, 0
        for n in names:
            size = wts[n].size * (2 if n == "a_norm" else 1)
            rows = _rows_of(size, 16)
            flat = gathered[:, row:row + rows].reshape(N_DEV, -1)[:, :size]
            if n == "a_norm":
                flat = lax.bitcast_convert_type(flat.reshape(N_DEV, -1, 2), F32)
            full[n] = _join8(flat, dict(SHARDED)[n], wts[n].shape)
            row += rows
        return full

    second, third = ("a_w_out", "w_dkv", "w_ukv"), ("w_mem_kv", "b_w_in", "b_w_uq", "b_w_out")
    a_in_w = a_w_in.shape[-1]
    bits = jnp.pad(lax.bitcast_convert_type(a_norm.reshape(-1), BF16).reshape(1, -1), ((0, 15), (0, a_in_w - 2 * a_norm.size)))
    w_a_in = _all_gather_columns(jnp.concatenate([a_w_in[0].astype(BF16), bits], axis=0), name="gather_first")
    g_a = lax.bitcast_convert_type(w_a_in[D].reshape(N_DEV, a_in_w)[:, :2 * a_norm.size].reshape(N_DEV, -1, 2),
                                   F32).reshape(1, D)

    row2 = lambda g: g.reshape(1, -1)
    h0 = _rowwise(_f_norm, [(xs, D, 0)], [g_a], [(D, BF16)], name="a_norm_fwd", tm=512)[0]
    qkv, gathered = _mm(h0, w_a_in, "nn", b_cols=(0, 3 * sb_w), out_dtype=BF16, name="a_in_qkv",
                        ride=("gather", jnp.concatenate(pieces_of(second), axis=0)))
    full = unpack_gathered(gathered, second)
    pa = _mm(h0, w_a_in, "nn", b_cols=(3 * sb_w, N_DEV * a_in_w - 3 * sb_w), name="a_in_rest")
    sb, gathered = _sb_fwd(qkv, tq=512, tk=256, name="sb_fwd",
                           ride=("gather", jnp.concatenate(pieces_of(third), axis=0)))
    full.update(unpack_gathered(gathered, third))
    w_a_out = full["a_w_out"][0]
    w_dkv_p = jnp.pad(full["w_dkv"], ((0, 0), (0, ROPE)))
    w_ukv_f = full["w_ukv"]
    wb = full["b_w_in"][0]
    w_b_in = jnp.concatenate([wb[:, q_lora:q_lora + mla_w], wb[:, :q_lora], wb[:, q_lora + mla_w:]], axis=1)
    w_uq_p = jnp.pad(full["b_w_uq"][0].reshape(q_lora, H_MLA, MLA_QK),
                     ((0, 0), (0, 0), (0, 2 * DH - MLA_QK))).reshape(q_lora, H_MLA * 2 * DH)
    w_b_out = full["b_w_out"][0]
    w_mem = full["w_mem_kv"]

    pad128 = lambda g: jnp.pad(g.reshape(1, -1), ((0, 0), (0, DH - ROPE)))
    g_kr, g_qr = pad128(g_k_rope), pad128(b_g_q_rope[0])
    g_kv, g_b, g_c, g_kn = row2(kv_norm), row2(b_norm[0]), row2(g_ckv), row2(g_k_nope)
    g_ql, g_qn = row2(b_g_q_lat[0]), row2(b_g_q_nope[0])

    inv_freq = jnp.power(ROPE_THETA, -jnp.arange(0, ROPE, 2, dtype=F32) / ROPE)
    ang = positions[0].astype(F32)[:, None] * inv_freq
    z64 = jnp.zeros((S, DH - ROPE), F32)
    cs = jnp.concatenate([jnp.cos(ang), jnp.cos(ang), z64, -jnp.sin(ang), jnp.sin(ang), z64], axis=1)

    mn, mkv = [], []
    for l in range(2):
        mn.append(_rowwise(_f_norm, [(ms, D, 0)], [row2(mem_norm[l])], [(D, BF16)], name=f"mem_norm{l}")[0])
        mkv.append(_mm(mn[l], w_mem[l], "nn", name=f"mem_kv{l}"))
    g_mq = [row2(g_mem_q[l]) for l in range(2)]
    g_mk = [row2(g_mem_k[l]) for l in range(2)]

    mix_a_rows = [(sb, sb_w, 0), (pa, sb_w, 0), (pa, mem_w, sb_w // mem_w), (pa, mem_w, sb_w // mem_w + 1)]
    mixed_a = _rowwise(_f_mix, mix_a_rows, [mkv[0], g_mq[0], g_mk[0]], [(sb_w + mem_w, BF16)], name="a_mix_fwd")[0]
    x1 = _mm(mixed_a, w_a_out, "nn", add=xs, name="a_out")

    hk, hb = _rowwise(_f_norm2, [(x1, D, 0)], [g_kv, g_b], [(D, BF16), (D, BF16)], name="b_norm_fwd", tm=512)
    ckr = _mm(hk, w_dkv_p, "nn", name="kv_down")
    cn, kr = _rowwise(_f_kv1, [(ckr, kv_lora + DH, 0), (cs, 2 * DH, 0)], [g_c, g_kr],
                      [(kv_lora, BF16), (DH, F32)], name="kv1_fwd", tm=512)
    kvu = _mm(cn, w_ukv_f, "nn", name="kv_up")
    k2, v2 = _rowwise(_f_kv2, [(kvu, H_MLA * 2 * DH, 0), (kr, DH, 0)], [g_kn],
                      [(H_MLA * 2 * DH, BF16), (mla_w, BF16)], name="kv2_fwd")
    pb = _mm(hb, w_b_in, "nn", name="b_in")
    ql = _rowwise(_f_norm, [(pb, q_lora, mla_w // q_lora)], [g_ql], [(q_lora, BF16)], name="q_lat_fwd", tm=512)[0]
    qraw = _mm(ql, w_uq_p, "nn", name="q_up")
    q2 = _rowwise(_f_q2, [(qraw, H_MLA * 2 * DH, 0), (cs, 2 * DH, 0)], [g_qn, g_qr],
                  [(H_MLA * 2 * DH, BF16)], name="q2_fwd")[0]
    att, lse = _mla_fwd(q2, k2, v2, tq=1024, tk=1024, name="mla_fwd")
    cb = (mla_w + q_lora) // mem_w
    mix_b_rows = [(att, mla_w, 0), (pb, mla_w, 0), (pb, mem_w, cb), (pb, mem_w, cb + 1)]
    mixed_b = _rowwise(_f_mix, mix_b_rows, [mkv[1], g_mq[1], g_mk[1]], [(mla_w + mem_w, BF16)], name="b_mix_fwd")[0]
    dy, loss_part = _mm_loss(mixed_b, w_b_out, x1, tgt, name="b_out_loss")

    gr = {}
    d_mixed_b = _mm(dy, w_b_out, "nt", name="b_out_dx")
    gr["b_w_out"] = _mm(mixed_b, dy, "tn", out_dtype=BF16, name="b_out_dw")[None]
    d_att, d_gmla, d_qm_b, d_gm_b, d_mkv1, d_gq1, d_gk1 = _rowwise_bwd(
        _f_mix, mix_b_rows, [mkv[1], g_mq[1], g_mk[1]], [(d_mixed_b, mla_w + mem_w, 0)],
        [0, 1, 2, 3], [0, 1, 2], out_dtypes=[F32, BF16, BF16, BF16], name="b_mix_bwd")
    dq2, dk2, dv2 = _mla_bwd(q2, k2, v2, att, d_att, lse, tq=512, tk=512, name="mla_bwd")
    d_qraw, d_gqn, d_gqr = _rowwise_bwd(
        _f_q2, [(qraw, H_MLA * 2 * DH, 0), (cs, 2 * DH, 0)], [g_qn, g_qr], [(dq2, H_MLA * 2 * DH, 0)],
        [0], [0, 1], out_dtypes=[BF16], name="q2_bwd")
    d_ql = _mm(d_qraw, w_uq_p, "nt", name="q_up_dx")
    d_wuq = _mm(ql, d_qraw, "tn", out_dtype=BF16, name="q_up_dw")
    gr["b_w_uq"] = d_wuq.reshape(q_lora, H_MLA, 2 * DH)[:, :, :MLA_QK].reshape(1, q_lora, H_MLA * MLA_QK)
    d_qlat, d_gql = _rowwise_bwd(_f_norm, [(pb, q_lora, mla_w // q_lora)], [g_ql], [(d_ql, q_lora, 0)],
                                 [0], [0], out_dtypes=[BF16], name="q_lat_bwd", tm=512)
    d_pb = jnp.concatenate([d_gmla, d_qlat, d_qm_b, d_gm_b], axis=1)
    d_hb = _mm(d_pb, w_b_in, "nt", name="b_in_dx")
    d_wbin = _mm(hb, d_pb, "tn", out_dtype=BF16, name="b_in_dw")
    gr["b_w_in"] = jnp.concatenate([d_wbin[:, mla_w:mla_w + q_lora], d_wbin[:, :mla_w], d_wbin[:, mla_w + q_lora:]],
                                   axis=1)[None]
    d_kvu, d_kr, d_gkn = _rowwise_bwd(
        _f_kv2, [(kvu, H_MLA * 2 * DH, 0), (kr, DH, 0)], [g_kn], [(dk2, H_MLA * 2 * DH, 0), (dv2, mla_w, 0)],
        [0, 1], [0], out_dtypes=[BF16, F32], name="kv2_bwd")
    d_cn = _mm(d_kvu, w_ukv_f, "nt", name="kv_up_dx")
    gr["w_ukv"] = _mm(cn, d_kvu, "tn", out_dtype=BF16, name="kv_up_dw")
    d_ckr, d_gc, d_gkr = _rowwise_bwd(
        _f_kv1, [(ckr, kv_lora + DH, 0), (cs, 2 * DH, 0)], [g_c, g_kr], [(d_cn, kv_lora, 0), (d_kr, DH, 0)],
        [0], [0, 1], out_dtypes=[BF16], name="kv1_bwd", tm=512)
    d_hk = _mm(d_ckr, w_dkv_p, "nt", name="kv_down_dx")
    gr["w_dkv"] = _mm(hk, d_ckr, "tn", out_dtype=BF16, name="kv_down_dw")[:, :kv_lora + ROPE]
    d_x1, d_gkv, d_gb = _rowwise_bwd(_f_norm2, [(x1, D, 0)], [g_kv, g_b], [(d_hk, D, 0), (d_hb, D, 0)],
                                     [0], [0, 1], add=(dy, D, 0), name="b_norm_bwd")
    d_mixed_a = _mm(d_x1, w_a_out, "nt", name="a_out_dx")
    gr["a_w_out"] = _mm(mixed_a, d_x1, "tn", out_dtype=BF16, name="a_out_dw")[None]
    d_sb, d_gsb, d_qm_a, d_gm_a, d_mkv0, d_gq0, d_gk0 = _rowwise_bwd(
        _f_mix, mix_a_rows, [mkv[0], g_mq[0], g_mk[0]], [(d_mixed_a, sb_w + mem_w, 0)],
        [0, 1, 2, 3], [0, 1, 2], out_dtypes=[F32, BF16, BF16, BF16], name="a_mix_bwd")
    d_wmem, d_mnorm = [], []
    for l, d_mkv in enumerate((d_mkv0, d_mkv1)):
        d_mn = _mm(d_mkv, w_mem[l], "nt", name=f"mem_kv_dx{l}")
        d_wmem.append(_mm(mn[l], d_mkv, "tn", out_dtype=BF16, name=f"mem_kv_dw{l}"))
        d_mnorm.append(_rowwise_bwd(_f_norm, [(ms, D, 0)], [row2(mem_norm[l])], [(d_mn, D, 0)], [], [0],
                                    name=f"mem_norm_bwd{l}")[0])
    gr["w_mem_kv"] = jnp.stack(d_wmem)

    mid = tuple(n for n, _ in SHARDED[2:])
    send_mid = _stack_rows([_to_rows(_split8(gr[n], ax), 16) for n, ax in SHARDED[2:]])
    dq, dk, dv, recv_mid = _sb_bwd(qkv, sb, d_sb, tq=512, tk=256, name="sb_bwd", ride=("a2a", send_mid))
    d_pa = jnp.concatenate([dq, dk, dv, d_gsb, d_qm_a, d_gm_a], axis=1)
    send_ain = _mm(h0, d_pa, "tn", out_dtype=BF16, out_split=N_DEV, name="a_in_dw")
    d_h0, recv_ain = _mm(d_pa, w_a_in, "nt", b_rows=D, name="a_in_dx", ride=("a2a", send_ain))
    grad_x, d_ga = _rowwise_bwd(_f_norm, [(xs, D, 0)], [g_a], [(d_h0, D, 0)], [0], [0], add=(d_x1, D, 0),
                                name="a_norm_bwd")
    gr["mem_norm"] = jnp.concatenate(d_mnorm, axis=0)
    gr["g_mem_q"] = jnp.concatenate([d_gq0, d_gq1], axis=0)
    gr["g_mem_k"] = jnp.concatenate([d_gk0, d_gk1], axis=0)
    gr["kv_norm"], gr["b_norm"], gr["g_ckv"], gr["g_k_nope"] = d_gkv, d_gb, d_gc, d_gkn
    gr["g_k_rope"], gr["b_g_q_rope"] = d_gkr[:, :ROPE], d_gqr[:, :ROPE]
    gr["b_g_q_lat"], gr["b_g_q_nope"] = d_gql, d_gqn
    last = ("a_norm",) + SMALL
    shared = jnp.concatenate([gr[n].reshape(-1) for n in SMALL] + [loss_part[0, :1]])
    flat = jnp.concatenate([_split8(d_ga, 1), jnp.broadcast_to(shared[None], (N_DEV, shared.size))], axis=1)
    recv_last = _all_to_all(_to_rows(flat, ROW_MULT), name="exchange_small")

    out = [{}, {}, {}, {}]
    slabs = _reduce_adamw(recv_mid, *[_pack_local(t, mid, 16) for t in (wts, mom, var)], name="reduce_adamw_mid")
    for o, slab in zip(out, slabs):
        o.update(_unpack_local(slab, mid, shapes, 16)[0])
    zero = jnp.zeros((1,), F32)
    slabs = _reduce_adamw(recv_last, *[_to_rows(jnp.concatenate([t[n].reshape(-1) for n in last] + [zero]), ROW_MULT)
                                       for t in (wts, mom, var)], name="reduce_adamw_last")
    for o, slab in zip(out, slabs):
        vec, at = slab.reshape(-1), 0
        for n in last:
            o[n] = vec[at:at + wts[n].size].reshape(shapes[n])
            at += wts[n].size
    loss = slabs[0].reshape(-1)[at]
    slabs = _reduce_adamw(recv_ain, *[t["a_w_in"][0] for t in (wts, mom, var)], name="reduce_adamw_ain")
    for o, slab in zip(out, slabs):
        o["a_w_in"] = slab[None]
    return (loss, grad_x[None], *[o[n] for o in out for n in WEIGHTS])
```

```python
import functools

import jax
import jax.numpy as jnp
from jax import lax
from jax.experimental import pallas as pl
from jax.experimental.pallas import tpu as pltpu

F32, BF16 = jnp.float32, jnp.bfloat16

N_DEV = 8
DH = 128
H_SB, H_MEM, H_MLA = 12, 4, 12
ROPE = 64
MLA_QK = DH + ROPE
EPS = 1e-6
ROPE_THETA = 10000.0
ADAM_LR, ADAM_B1, ADAM_B2, ADAM_EPS, ADAM_WD, ADAM_STEP = 0.001, 0.9, 0.999, 1e-08, 0.01, 10

LANES = 1024
VMEM_LIMIT = 48 * 1024 * 1024
VMEM_LIMIT_BIG = 56 * 1024 * 1024

NN = (((1,), (0,)), ((), ()))
NT = (((1,), (1,)), ((), ()))
TN = (((0,), (0,)), ((), ()))
_DIMS = {"nn": NN, "nt": NT, "tn": TN}


def _dot16(a, b, dims):
    return lax.dot_general(a.astype(BF16), b.astype(BF16), _DIMS[dims], preferred_element_type=F32)


@functools.partial(jax.custom_vjp, nondiff_argnums=(2,))
def _bdot(a, b, dims):
    return _dot16(a, b, dims)


def _bdot_fwd(a, b, dims):
    return _dot16(a, b, dims), (a, b)


def _bdot_bwd(dims, res, g):
    a, b = res
    if dims == "nn":
        return _dot16(g, b, "nt"), _dot16(a, g, "tn")
    return _dot16(g, b, "nn"), _dot16(g, a, "tn")


_bdot.defvjp(_bdot_fwd, _bdot_bwd)


def _tile(n, pref):
    if n <= pref:
        return n
    t = (pref // 128) * 128
    while n % t:
        t -= 128
    return t


def _mesh_pos():
    return lax.axis_index("x"), lax.axis_index("y"), lax.axis_index("c")


def _ride_shape(kind, src):
    return jax.ShapeDtypeStruct((N_DEV, *src.shape) if kind == "gather" else src.shape, src.dtype)


def _ride_scratch():
    return [pltpu.SemaphoreType.DMA((N_DEV - 1,)), pltpu.SemaphoreType.DMA((N_DEV - 1,)), pltpu.SemaphoreType.DMA]


def _ride(kind, phase, s_ref, r_ref, send_sems, recv_sems, local_sem):
    x, y, c = _mesh_pos()
    me = 4 * x + 2 * y + c
    src = (lambda lin: s_ref) if kind == "gather" else (lambda lin: s_ref.at[lin])
    local = pltpu.make_async_copy(src(me), r_ref.at[me], local_sem)
    if phase == "start":
        local.start()
    for k in range(1, N_DEV):
        p = (1 - x if k & 4 else x, 1 - y if k & 2 else y, 1 - c if k & 1 else c)
        lin = 4 * p[0] + 2 * p[1] + p[2]
        cp = pltpu.make_async_remote_copy(
            src_ref=src(lin), dst_ref=r_ref.at[me] if phase == "start" else r_ref.at[lin],
            send_sem=send_sems.at[k - 1], recv_sem=recv_sems.at[k - 1],
            device_id=p, device_id_type=pl.DeviceIdType.MESH)
        if phase == "start":
            cp.start()
        else:
            cp.wait_recv()
            cp.wait_send()
    if phase == "wait":
        local.wait()


def _mm(a, b, dims, *, name, out_dtype=F32, add=None, ride=None, out_split=None, b_rows=None, b_cols=None,
        tm=1024, tn=1024, tk=2048):
    col0 = 0
    if dims == "tn":
        (K, M), (_, N) = a.shape, b.shape
    elif dims == "nt":
        (M, K), N = a.shape, b_rows or b.shape[0]
    else:
        (M, K), N = a.shape, b.shape[1]
        if b_cols is not None:
            col0, N = b_cols
    tm, tk = _tile(M, tm), _tile(K, tk)
    tn = N // out_split if out_split else _tile(N, tn)
    while col0 % tn or N % tn:
        tn -= 128
    jb = col0 // tn
    ni, nj, nk = M // tm, N // tn, K // tk
    n_in = 2 + (add is not None) + (ride is not None)

    def body(*refs):
        a_ref, b_ref, o_ref = refs[0], refs[1], refs[n_in]
        acc_ref = refs[n_in + 1 + (ride is not None)]
        i, j, k = pl.program_id(0), pl.program_id(1), pl.program_id(2)
        if ride is not None:
            ride_refs = (refs[n_in - 1], refs[n_in + 1], *refs[-3:])

            @pl.when((i == 0) & (j == 0) & (k == 0))
            def _():
                _ride(ride[0], "start", *ride_refs)

        if nk == 1:
            r = _dot16(a_ref[...], b_ref[...], dims)
            o_ref[...] = (r if add is None else r + refs[2][...]).astype(o_ref.dtype)
        else:
            @pl.when(k == 0)
            def _():
                acc_ref[...] = jnp.zeros_like(acc_ref)

            acc_ref[...] += _dot16(a_ref[...], b_ref[...], dims)

            @pl.when(k == nk - 1)
            def _():
                r = acc_ref[...]
                if add is not None:
                    r = r + refs[2][...]
                o_ref[...] = r.astype(o_ref.dtype)

        if ride is not None:
            @pl.when((i == ni - 1) & (j == nj - 1) & (k == nk - 1))
            def _():
                _ride(ride[0], "wait", *ride_refs)

    a_spec = pl.BlockSpec((tk, tm), lambda i, j, k: (k, i)) if dims == "tn" else pl.BlockSpec((tm, tk), lambda i, j, k: (i, k))
    b_spec = pl.BlockSpec((tn, tk), lambda i, j, k: (j, k)) if dims == "nt" else pl.BlockSpec((tk, tn), lambda i, j, k: (k, j + jb))
    o_spec = pl.BlockSpec((tm, tn), lambda i, j, k: (i, j))
    in_specs, args = [a_spec, b_spec], [a, b]
    if add is not None:
        in_specs.append(o_spec)
        args.append(add)
    out_specs, out_shape = [o_spec], [jax.ShapeDtypeStruct((M, N), out_dtype)]
    scratch = [pltpu.VMEM((tm, tn) if nk > 1 else (8, 128), F32)]
    if out_split:
        out_specs = [pl.BlockSpec((None, tm, tn), lambda i, j, k: (j, i, 0))]
        out_shape = [jax.ShapeDtypeStruct((out_split, M, tn), out_dtype)]
    if ride is not None:
        in_specs.append(pl.BlockSpec(memory_space=pl.ANY))
        args.append(ride[1])
        out_specs.append(pl.BlockSpec(memory_space=pl.ANY))
        out_shape.append(_ride_shape(*ride))
        scratch += _ride_scratch()
    sem = ("arbitrary",) * 3 if ride is not None else ("parallel", "parallel", "arbitrary")
    res = pl.pallas_call(
        body, name=name, grid=(ni, nj, nk), in_specs=in_specs, out_specs=out_specs, out_shape=out_shape,
        scratch_shapes=scratch,
        compiler_params=pltpu.CompilerParams(dimension_semantics=sem, vmem_limit_bytes=VMEM_LIMIT),
    )(*args)
    return res[0] if ride is None else res


def _mm_loss(a, b, res, tgt, *, name, tm=1024, tn=1024):
    (M, K), N = a.shape, b.shape[1]
    tm, tn = _tile(M, tm), _tile(N, tn)

    def body(a_ref, b_ref, r_ref, t_ref, dy_ref, loss_ref):
        @pl.when((pl.program_id(0) == 0) & (pl.program_id(1) == 0))
        def _():
            loss_ref[...] = jnp.zeros_like(loss_ref)

        err = _dot16(a_ref[...], b_ref[...], "nn") + r_ref[...] - t_ref[...]
        dy_ref[...] = err * (1.0 / N)
        loss_ref[...] += 0.5 * jnp.sum(err * err) * (1.0 / N)

    tile = pl.BlockSpec((tm, tn), lambda i, j: (i, j))
    return pl.pallas_call(
        body, name=name, grid=(M // tm, N // tn),
        in_specs=[pl.BlockSpec((tm, K), lambda i, j: (i, 0)), pl.BlockSpec((K, tn), lambda i, j: (0, j)), tile, tile],
        out_specs=[tile, pl.BlockSpec((1, DH), lambda i, j: (0, 0))],
        out_shape=[jax.ShapeDtypeStruct((M, N), F32), jax.ShapeDtypeStruct((1, DH), F32)],
        compiler_params=pltpu.CompilerParams(dimension_semantics=("arbitrary", "arbitrary"), vmem_limit_bytes=VMEM_LIMIT),
    )(a, b, res, tgt)


def _rowwise(fn, rows, consts, outs, accs=(), *, name, tm=256):
    S = rows[0][0].shape[0]
    tm = min(tm, S)
    nr, nc, no = len(rows), len(consts), len(outs)

    def body(*refs):
        res = fn(*[r[...] for r in refs[:nr + nc]])
        res = tuple(res) if isinstance(res, (tuple, list)) else (res,)
        orefs, arefs = refs[nr + nc:nr + nc + no], refs[nr + nc + no:]
        for r, v in zip(orefs, res[:no]):
            r[...] = v.astype(r.dtype)
        if arefs:
            @pl.when(pl.program_id(0) == 0)
            def _():
                for r in arefs:
                    r[...] = jnp.zeros_like(r)

            for r, v in zip(arefs, res[no:]):
                r[...] += v

    in_specs = [pl.BlockSpec((tm, w), lambda i, cb=cb: (i, cb)) for (_, w, cb) in rows]
    in_specs += [pl.BlockSpec(c.shape, lambda i: (0, 0)) for c in consts]
    out_specs = [pl.BlockSpec((tm, w), lambda i: (i, 0)) for (w, _) in outs]
    out_specs += [pl.BlockSpec(s, lambda i: (0, 0)) for s in accs]
    out_shape = [jax.ShapeDtypeStruct((S, w), dt) for (w, dt) in outs]
    out_shape += [jax.ShapeDtypeStruct(s, F32) for s in accs]
    res = pl.pallas_call(
        body, name=name, grid=(S // tm,), in_specs=in_specs, out_specs=out_specs, out_shape=out_shape,
        compiler_params=pltpu.CompilerParams(dimension_semantics=("arbitrary",), vmem_limit_bytes=VMEM_LIMIT),
    )(*[r[0] for r in rows], *consts)
    return res


def _rowwise_bwd(f, rows, consts, cts, row_grads, const_grads, *, name, add=None, out_dtypes=None, tm=256):
    nr, nc, nct = len(rows), len(consts), len(cts)
    all_rows = list(rows) + list(cts) + ([add] if add is not None else [])

    def fn(*args):
        nrow = len(all_rows)
        prim = [x.astype(F32) for x in args[:nr]] + [x.astype(F32) for x in args[nrow:]]
        ct = tuple(x.astype(F32) for x in args[nr:nr + nct])
        out, vjp = jax.vjp(f, *prim)
        gs = vjp(ct if isinstance(out, (tuple, list)) else ct[0])
        res = [gs[k] for k in row_grads]
        if add is not None:
            res[0] = res[0] + args[nrow - 1]
        return tuple(res) + tuple(gs[nr + k] for k in const_grads)

    out_dtypes = out_dtypes or [F32] * len(row_grads)
    outs = [(rows[k][1], dt) for k, dt in zip(row_grads, out_dtypes)]
    accs = [consts[k].shape for k in const_grads]
    return _rowwise(fn, all_rows, consts, outs, accs, name=name, tm=tm)


def _rms(x, g, n=None):
    ms = jnp.sum(x * x, axis=-1, keepdims=True) * (1.0 / (n or x.shape[-1]))
    return x * lax.rsqrt(ms + EPS) * g


def _sigmoid(x):
    return 1.0 / (1.0 + jnp.exp(-x))


def _swap_halves_exact(x):
    r = lax.broadcasted_iota(jnp.int32, (DH, DH), 0)
    c = lax.broadcasted_iota(jnp.int32, (DH, DH), 1)
    half = ROPE // 2
    perm = (((r < half) & (c == r + half)) | ((r >= half) & (r < ROPE) & (c == r - half))).astype(BF16)
    hi = x.astype(BF16)
    rest = x - hi.astype(F32)
    mid = rest.astype(BF16)
    lo = (rest - mid.astype(F32)).astype(BF16)
    return sum(lax.dot_general(p, perm, NN, preferred_element_type=F32) for p in (hi, mid, lo))


_swap_halves = jax.custom_vjp(_swap_halves_exact)
_swap_halves.defvjp(lambda x: (_swap_halves_exact(x), None), lambda _, g: (_swap_halves_exact(g),))


def _rope128(x, g128, cs):
    y = _rms(x, g128, n=ROPE)
    return y * cs[:, :DH] + _swap_halves(y) * cs[:, DH:]


def _f_norm(x, g):
    return _rms(x, g)


def _f_norm2(x, g1, g2):
    xn = x * lax.rsqrt(jnp.mean(x * x, axis=-1, keepdims=True) + EPS)
    return xn * g1, xn * g2


def _f_kv1(ckr, cs, g_ckv, g_kr):
    w = g_ckv.shape[-1]
    return _rms(ckr[:, :w], g_ckv), _rope128(ckr[:, w:], g_kr, cs)


def _f_kv2(kv, kr, g_kn):
    ks, vs = [], []
    for h in range(H_MLA):
        ks += [_rms(kv[:, 2 * DH * h:2 * DH * h + DH], g_kn), kr]
        vs.append(kv[:, 2 * DH * h + DH:2 * DH * (h + 1)])
    return jnp.concatenate(ks, axis=1), jnp.concatenate(vs, axis=1)


def _f_q2(q, cs, g_n, g_r):
    out = []
    for h in range(H_MLA):
        out += [_rms(q[:, 2 * DH * h:2 * DH * h + DH], g_n), _rope128(q[:, 2 * DH * h + DH:2 * DH * (h + 1)], g_r, cs)]
    return jnp.concatenate(out, axis=1)


def _f_mix(att, g_att, q_m, g_m, mkv, g_q, g_k):
    mem_w = H_MEM * DH
    heads = []
    for h in range(H_MEM):
        kh = _rms(mkv[:, h * DH:(h + 1) * DH], g_k)
        vh = mkv[:, mem_w + h * DH:mem_w + (h + 1) * DH]
        qh = _rms(q_m[:, h * DH:(h + 1) * DH], g_q)
        s = _bdot(qh, kh, "nt") * (DH ** -0.5)
        p = jnp.exp(s - lax.stop_gradient(jnp.max(s, axis=-1, keepdims=True)))
        p = p / jnp.sum(p, axis=-1, keepdims=True)
        heads.append(_bdot(p, vh, "nn"))
    mo = jnp.concatenate(heads, axis=1)
    return jnp.concatenate([att * (g_att * _sigmoid(g_att)), mo * (g_m * _sigmoid(g_m))], axis=1)


def _split_dot(x, u):
    hi = x.astype(BF16)
    lo = (x - hi.astype(F32)).astype(BF16)
    return (lax.dot_general(hi, u, NN, preferred_element_type=F32)
            + lax.dot_general(lo, u, NN, preferred_element_type=F32))


def _tri(t):
    r = lax.broadcasted_iota(jnp.int32, (t, t), 0)
    c = lax.broadcasted_iota(jnp.int32, (t, t), 1)
    return r, c


def _strict_lower(t):
    r, c = _tri(t)
    return (r > c).astype(BF16)


def _head_blocks_t(x, w, tk, *, name):
    S = x.shape[0]
    H = x.shape[1] // w

    def body(x_ref, o_ref):
        for h in range(H):
            o_ref[h] = x_ref[:, h * w:(h + 1) * w].T

    return pl.pallas_call(
        body, name=name, grid=(S // tk,),
        in_specs=[pl.BlockSpec((tk, H * w), lambda i: (i, 0))],
        out_specs=pl.BlockSpec((H, None, w, tk), lambda i: (0, i, 0, 0)),
        out_shape=jax.ShapeDtypeStruct((H, S // tk, w, tk), x.dtype),
        compiler_params=pltpu.CompilerParams(dimension_semantics=("parallel",), vmem_limit_bytes=VMEM_LIMIT),
    )(x)


def _rows_ahead(tq, tk):
    return lax.broadcasted_iota(jnp.int32, (tq, tk), 0) - lax.broadcasted_iota(jnp.int32, (tq, tk), 1)


EXP_UNDERFLOW = -110.0


def _log_one_minus_beta(zr, scale):
    zs, nz = zr * scale, zr * (-scale)
    return zs, jnp.minimum(nz, 0.0) - jnp.log(1.0 + jnp.exp(jnp.minimum(zs, nz)))


def _sb_fwd(qkv, *, tq, tk, name, ride=None):
    S = qkv.shape[0]
    tq, tk = min(tq, S), min(tk, tq, S)
    nd = tq // tk
    H = H_SB
    scale = DH ** -0.5

    nq = S // tq

    def body(q_ref, k_ref, v_ref, *rest):
        o_ref = rest[1] if ride is not None else rest[0]
        h, i = pl.program_id(0), pl.program_id(1)
        if ride is not None:
            ride_refs = (rest[0], *rest[2:])

            @pl.when((h == 0) & (i == 0))
            def _():
                _ride(ride[0], "start", *ride_refs)

        q = q_ref[...]
        u = _strict_lower(tk)
        ahead = _rows_ahead(tq, tk)

        def block(j, acc, cb, keep):
            off = pl.multiple_of(j * tk, tk)
            k = k_ref[pl.ds(off, tk), :]
            v = v_ref[pl.ds(off, tk), :]
            z, l = _log_one_minus_beta(lax.dot_general(q, k, NT, preferred_element_type=F32), scale)
            if keep is not None:
                l = jnp.where(keep, l, 0.0)
            a = jnp.exp((z + l) + (_split_dot(l, u) + cb))
            if keep is not None:
                a = jnp.where(keep, a, 0.0)
            acc = acc + lax.dot_general(a.astype(BF16), v, NN, preferred_element_type=F32)
            return acc, cb + jnp.sum(l, axis=1, keepdims=True)

        carry = (jnp.zeros((tq, DH), F32), jnp.zeros((tq, 1), F32))
        for t in reversed(range(nd)):
            carry = block(i * nd + t, *carry, ahead > t * tk)
        _, acc, _ = lax.while_loop(
            lambda st: (st[0] < i * nd) & (jnp.max(st[2]) > EXP_UNDERFLOW),
            lambda st: (st[0] + 1, *block(i * nd - 1 - st[0], st[1], st[2], None)), (jnp.int32(0), *carry))
        o_ref[...] = acc

        if ride is not None:
            @pl.when((h == H - 1) & (i == nq - 1))
            def _():
                _ride(ride[0], "wait", *ride_refs)

    in_specs = [pl.BlockSpec((tq, DH), lambda h, i: (i, h)),
                pl.BlockSpec((S, DH), lambda h, i: (0, H + h)),
                pl.BlockSpec((S, DH), lambda h, i: (0, 2 * H + h))]
    out_specs = [pl.BlockSpec((tq, DH), lambda h, i: (i, h))]
    out_shape = [jax.ShapeDtypeStruct((S, H * DH), F32)]
    args, scratch = [qkv, qkv, qkv], []
    if ride is not None:
        in_specs.append(pl.BlockSpec(memory_space=pl.ANY))
        args.append(ride[1])
        out_specs.append(pl.BlockSpec(memory_space=pl.ANY))
        out_shape.append(_ride_shape(*ride))
        scratch = _ride_scratch()
    res = pl.pallas_call(
        body, name=name, grid=(H, nq), in_specs=in_specs, out_specs=out_specs, out_shape=out_shape,
        scratch_shapes=scratch,
        compiler_params=pltpu.CompilerParams(dimension_semantics=("arbitrary", "arbitrary"), vmem_limit_bytes=VMEM_LIMIT),
    )(*args)
    return res[0] if ride is None else res


def _sb_bwd(qkv, o, do, *, tq, tk, name, ride=None):
    S = qkv.shape[0]
    tq, tk = min(tq, S), min(tk, tq, S)
    nd = tq // tk
    H = H_SB
    scale = DH ** -0.5

    nq = S // tq

    def body(q_ref, k_ref, v_ref, o_ref, do_ref, *rest):
        if ride is not None:
            dq_ref, dk_out, dv_out, dk_ref, dv_ref = rest[1], rest[2], rest[3], rest[5], rest[6]
            ride_refs = (rest[0], rest[4], *rest[7:])
        else:
            dq_ref, dk_out, dv_out, dk_ref, dv_ref = rest
        h, i = pl.program_id(0), pl.program_id(1)
        if ride is not None:
            @pl.when((h == 0) & (i == 0))
            def _():
                _ride(ride[0], "start", *ride_refs)

        @pl.when(i == 0)
        def _():
            dk_ref[...] = jnp.zeros_like(dk_ref)
            dv_ref[...] = jnp.zeros_like(dv_ref)

        q = q_ref[...]
        do = do_ref[...]
        do16 = do.astype(BF16)
        dsum = jnp.sum(do16.astype(F32) * o_ref[...], axis=1, keepdims=True)
        u = _strict_lower(tk)
        ahead = _rows_ahead(tq, tk)

        def block(j, dq, cb, ce, keep):
            off = pl.multiple_of(j * tk, tk)
            k = k_ref[pl.ds(off, tk), :]
            v = v_ref[pl.ds(off, tk), :]
            z, l = _log_one_minus_beta(lax.dot_general(q, k, NT, preferred_element_type=F32), scale)
            if keep is not None:
                l = jnp.where(keep, l, 0.0)
            log_beta = z + l
            a = jnp.exp(log_beta + (_split_dot(l, u) + cb))
            if keep is not None:
                a = jnp.where(keep, a, 0.0)
            a16 = a.astype(BF16)
            e = a16.astype(F32) * lax.dot_general(do16, v, NT, preferred_element_type=F32)
            left = dsum - (ce + _split_dot(e, u) + e)
            dz = (e - jnp.exp(log_beta) * (e + left)) * scale
            if keep is not None:
                dz = jnp.where(keep, dz, 0.0)
            dz = dz.astype(BF16)
            dq = dq + lax.dot_general(dz, k, NN, preferred_element_type=F32)
            dk_ref[pl.ds(off, tk), :] += lax.dot_general(dz, q, TN, preferred_element_type=F32)
            dv_ref[pl.ds(off, tk), :] += lax.dot_general(a16, do16, TN, preferred_element_type=F32)
            return dq, cb + jnp.sum(l, axis=1, keepdims=True), ce + jnp.sum(e, axis=1, keepdims=True)

        zero = jnp.zeros((tq, 1), F32)
        carry = (jnp.zeros((tq, DH), F32), zero, zero)
        for t in reversed(range(nd)):
            carry = block(i * nd + t, *carry, ahead > t * tk)
        _, dq, _, _ = lax.while_loop(
            lambda st: (st[0] < i * nd) & (jnp.max(st[2]) > EXP_UNDERFLOW),
            lambda st: (st[0] + 1, *block(i * nd - 1 - st[0], st[1], st[2], st[3], None)), (jnp.int32(0), *carry))
        dq_ref[...] = dq.astype(dq_ref.dtype)

        @pl.when(i == nq - 1)
        def _():
            dk_out[...] = dk_ref[...].astype(dk_out.dtype)
            dv_out[...] = dv_ref[...].astype(dv_out.dtype)

        if ride is not None:
            @pl.when((h == H - 1) & (i == nq - 1))
            def _():
                _ride(ride[0], "wait", *ride_refs)

    blk = pl.BlockSpec((tq, DH), lambda h, i: (i, h))
    whole = pl.BlockSpec((S, DH), lambda h, i: (0, h))
    shp = jax.ShapeDtypeStruct((S, H * DH), BF16)
    in_specs = [blk, pl.BlockSpec((S, DH), lambda h, i: (0, H + h)), pl.BlockSpec((S, DH), lambda h, i: (0, 2 * H + h)),
                blk, blk]
    out_specs, out_shape = [blk, whole, whole], [shp, shp, shp]
    args, scratch = [qkv, qkv, qkv, o, do], [pltpu.VMEM((S, DH), F32), pltpu.VMEM((S, DH), F32)]
    if ride is not None:
        in_specs.append(pl.BlockSpec(memory_space=pl.ANY))
        args.append(ride[1])
        out_specs.append(pl.BlockSpec(memory_space=pl.ANY))
        out_shape.append(_ride_shape(*ride))
        scratch += _ride_scratch()
    return pl.pallas_call(
        body, name=name, grid=(H, nq), in_specs=in_specs, out_specs=out_specs, out_shape=out_shape,
        scratch_shapes=scratch,
        compiler_params=pltpu.CompilerParams(dimension_semantics=("arbitrary", "arbitrary"), vmem_limit_bytes=VMEM_LIMIT),
    )(*args)


def _mla_fwd(q, k, v, *, tq, tk, name):
    S = q.shape[0]
    tq, tk = min(tq, S), min(tk, tq, S)
    nd, nb = tq // tk, S // tk
    H = H_MLA
    scale = MLA_QK ** -0.5
    vt = _head_blocks_t(v, DH, tk, name=name + "_vt")

    def body(q_ref, k_ref, vt_ref, o_ref, lse_ref):
        i = pl.program_id(1)
        qb = q_ref[...]
        behind = lax.broadcasted_iota(jnp.int32, (tk, tq), 1) - lax.broadcasted_iota(jnp.int32, (tk, tq), 0)

        def block(j, m, den, acct, keep):
            off = pl.multiple_of(j * tk, tk)
            st = lax.dot_general(k_ref[pl.ds(off, tk), :], qb, NT, preferred_element_type=F32) * scale
            if keep is not None:
                st = jnp.where(keep, st, -1e30)
            m_new = jnp.maximum(m, jnp.max(st, axis=0, keepdims=True))
            pt = jnp.exp(st - m_new)
            alpha = jnp.exp(m - m_new)
            den = alpha * den + jnp.sum(pt, axis=0, keepdims=True)
            acct = alpha * acct + lax.dot_general(vt_ref[j], pt.astype(BF16), NN, preferred_element_type=F32)
            return m_new, den, acct

        init = (jnp.full((1, tq), -1e30, F32), jnp.zeros((1, tq), F32), jnp.zeros((DH, tq), F32))
        carry = lax.fori_loop(0, i * nd, lambda j, carry: block(j, *carry, None), init)
        for t in range(nd):
            carry = block(i * nd + t, *carry, behind >= t * tk)
        m, den, acct = carry
        o_ref[...] = (acct / den).T
        lse_ref[0] = m + jnp.log(den)

    return pl.pallas_call(
        body, name=name, grid=(H, S // tq),
        in_specs=[pl.BlockSpec((tq, 2 * DH), lambda h, i: (i, h)),
                  pl.BlockSpec((S, 2 * DH), lambda h, i: (0, h)),
                  pl.BlockSpec((None, nb, DH, tk), lambda h, i: (h, 0, 0, 0))],
        out_specs=[pl.BlockSpec((tq, DH), lambda h, i: (i, h)), pl.BlockSpec((1, 1, tq), lambda h, i: (h, 0, i))],
        out_shape=[jax.ShapeDtypeStruct((S, H * DH), F32), jax.ShapeDtypeStruct((H, 1, S), F32)],
        compiler_params=pltpu.CompilerParams(dimension_semantics=("arbitrary", "arbitrary"), vmem_limit_bytes=VMEM_LIMIT),
    )(q, k, vt)


def _mla_bwd(q, k, v, o, do, lse, *, tq, tk, name):
    S = q.shape[0]
    tq, tk = min(tq, S), min(tk, tq, S)
    nd, nb = tq // tk, S // tk
    H = H_MLA
    scale = MLA_QK ** -0.5
    kt = _head_blocks_t(k, 2 * DH, tk, name=name + "_kt")

    def body(q_ref, k_ref, kt_ref, v_ref, o_ref, do_ref, lse_ref, dq_ref, dk_ref, dv_ref):
        i = pl.program_id(1)

        @pl.when(i == 0)
        def _():
            dk_ref[...] = jnp.zeros_like(dk_ref)
            dv_ref[...] = jnp.zeros_like(dv_ref)

        qb = q_ref[...]
        do = do_ref[...]
        do16 = do.astype(BF16)
        dsum = jnp.sum((do * o_ref[...]).T, axis=0, keepdims=True)
        lse = lse_ref[0]
        behind = lax.broadcasted_iota(jnp.int32, (tk, tq), 1) - lax.broadcasted_iota(jnp.int32, (tk, tq), 0)

        def block(j, dqt, keep):
            off = pl.multiple_of(j * tk, tk)
            kb = k_ref[pl.ds(off, tk), :]
            vb = v_ref[pl.ds(off, tk), :]
            st = lax.dot_general(kb, qb, NT, preferred_element_type=F32) * scale
            if keep is not None:
                st = jnp.where(keep, st, -1e30)
            pt = jnp.exp(st - lse)
            dpt = lax.dot_general(vb, do16, NT, preferred_element_type=F32)
            dst = (pt * (dpt - dsum) * scale).astype(BF16)
            dk_ref[pl.ds(off, tk), :] += lax.dot_general(dst, qb, NN, preferred_element_type=F32)
            dv_ref[pl.ds(off, tk), :] += lax.dot_general(pt.astype(BF16), do16, NN, preferred_element_type=F32)
            return dqt + lax.dot_general(kt_ref[j], dst, NN, preferred_element_type=F32)

        dqt = lax.fori_loop(0, i * nd, lambda j, dqt: block(j, dqt, None), jnp.zeros((2 * DH, tq), F32))
        for t in range(nd):
            dqt = block(i * nd + t, dqt, behind >= t * tk)
        dq_ref[...] = dqt.T

    blk = pl.BlockSpec((tq, DH), lambda h, i: (i, h))
    blk2 = pl.BlockSpec((tq, 2 * DH), lambda h, i: (i, h))
    return pl.pallas_call(
        body, name=name, grid=(H, S // tq),
        in_specs=[blk2, pl.BlockSpec((S, 2 * DH), lambda h, i: (0, h)),
                  pl.BlockSpec((None, nb, 2 * DH, tk), lambda h, i: (h, 0, 0, 0)),
                  pl.BlockSpec((S, DH), lambda h, i: (0, h)),
                  blk, blk, pl.BlockSpec((1, 1, tq), lambda h, i: (h, 0, i))],
        out_specs=[blk2, pl.BlockSpec((S, 2 * DH), lambda h, i: (0, h)), pl.BlockSpec((S, DH), lambda h, i: (0, h))],
        out_shape=[jax.ShapeDtypeStruct((S, H * 2 * DH), F32), jax.ShapeDtypeStruct((S, H * 2 * DH), F32),
                   jax.ShapeDtypeStruct((S, H * DH), F32)],
        compiler_params=pltpu.CompilerParams(dimension_semantics=("arbitrary", "arbitrary"), vmem_limit_bytes=VMEM_LIMIT_BIG),
    )(q, k, kt, v, o, do, lse)


def _all_gather_columns(block, *, name):
    R, C = block.shape

    def body(x_ref, out_ref, send_sems, recv_sems, local_sem):
        x, y, c = _mesh_pos()
        me, sibling = (x, y, c), (x, y, 1 - c)
        chips = [(1 - x, y), (x, 1 - y), (1 - x, 1 - y)]

        def slot(px, py, pc):
            return out_ref.at[:, pl.ds(pl.multiple_of((4 * px + 2 * py + pc) * C, 128), C)]

        def copy(k, blk, to, src=None):
            return pltpu.make_async_remote_copy(
                src_ref=slot(*blk) if src is None else src, dst_ref=slot(*blk),
                send_sem=send_sems.at[k], recv_sem=recv_sems.at[k],
                device_id=to, device_id_type=pl.DeviceIdType.MESH)

        mine = pltpu.make_async_copy(x_ref, slot(*me), local_sem)
        mine.start()
        first = [copy(0, me, sibling, src=x_ref)]
        first += [copy(1 + j, me, (*chip, c), src=x_ref) for j, chip in enumerate(chips)]
        for cp in first:
            cp.start()
        passed = [copy(4 + j, (*chip, c), sibling) for j, chip in enumerate(chips)]
        for j, chip in enumerate(chips):
            copy(1 + j, (*chip, c), me).wait_recv()
            passed[j].start()
        copy(0, sibling, me).wait_recv()
        for j, chip in enumerate(chips):
            copy(4 + j, (*chip, 1 - c), me).wait_recv()
        for cp in first + passed:
            cp.wait_send()
        mine.wait()

    return pl.pallas_call(
        body, name=name,
        out_shape=jax.ShapeDtypeStruct((R, N_DEV * C), block.dtype),
        in_specs=[pl.BlockSpec(memory_space=pl.ANY)], out_specs=pl.BlockSpec(memory_space=pl.ANY),
        scratch_shapes=[pltpu.SemaphoreType.DMA((7,)), pltpu.SemaphoreType.DMA((7,)), pltpu.SemaphoreType.DMA],
    )(block)


def _all_to_all(send, *, name):
    def body(*refs):
        _ride("a2a", "start", *refs)
        _ride("a2a", "wait", *refs)

    return pl.pallas_call(
        body, name=name, out_shape=_ride_shape("a2a", send),
        in_specs=[pl.BlockSpec(memory_space=pl.ANY)], out_specs=pl.BlockSpec(memory_space=pl.ANY),
        scratch_shapes=_ride_scratch(),
    )(send)


def _reduce_adamw(recv, w, m, v, *, name):
    R, C = w.shape
    tr = next(t for t in (128, 64, 32, 16, 8) if R % t == 0)

    def body(g_ref, w_ref, m_ref, v_ref, og_ref, od_ref, om_ref, ov_ref):
        g = g_ref[0].astype(F32)
        for s in range(1, N_DEV):
            g = g + g_ref[s].astype(F32)
        mn = ADAM_B1 * m_ref[...] + (1.0 - ADAM_B1) * g
        vn = ADAM_B2 * v_ref[...] + (1.0 - ADAM_B2) * jnp.square(g)
        m_hat = mn / (1.0 - ADAM_B1 ** ADAM_STEP)
        v_hat = vn / (1.0 - ADAM_B2 ** ADAM_STEP)
        og_ref[...] = g
        od_ref[...] = -ADAM_LR * (m_hat / (jnp.sqrt(v_hat) + ADAM_EPS) + ADAM_WD * w_ref[...])
        om_ref[...] = mn
        ov_ref[...] = vn

    blk = pl.BlockSpec((tr, C), lambda i: (i, 0))
    shp = jax.ShapeDtypeStruct((R, C), F32)
    return pl.pallas_call(
        body, name=name, grid=(R // tr,),
        in_specs=[pl.BlockSpec((N_DEV, tr, C), lambda i: (0, i, 0)), blk, blk, blk],
        out_specs=[blk, blk, blk, blk], out_shape=[shp, shp, shp, shp],
        compiler_params=pltpu.CompilerParams(dimension_semantics=("parallel",), vmem_limit_bytes=VMEM_LIMIT),
    )(recv, w, m, v)


SHARDED = (("a_norm", 1), ("a_w_in", 2), ("a_w_out", 1), ("w_dkv", 0), ("w_ukv", 1), ("b_w_in", 2),
           ("b_w_uq", 2), ("b_w_out", 1), ("w_mem_kv", 1))
SMALL = ("kv_norm", "g_ckv", "g_k_nope", "g_k_rope", "b_norm", "b_g_q_lat", "b_g_q_nope", "b_g_q_rope",
         "mem_norm", "g_mem_q", "g_mem_k")
WEIGHTS = ("a_norm", "a_w_in", "a_w_out", "kv_norm", "w_dkv", "g_ckv", "w_ukv", "g_k_nope", "g_k_rope", "b_norm",
           "b_w_in", "b_g_q_lat", "b_w_uq", "b_g_q_nope", "b_g_q_rope", "b_w_out", "mem_norm", "w_mem_kv",
           "g_mem_q", "g_mem_k")
ROW_MULT = 8
ROW_BLOCK = 128


def _rows_of(n, mult):
    rows = -(-n // LANES)
    return -(-rows // mult) * mult


def _to_rows(flat, mult):
    n = flat.shape[-1]
    rows = _rows_of(n, mult)
    pad = [(0, 0)] * (flat.ndim - 1) + [(0, rows * LANES - n)]
    return jnp.pad(flat, pad).reshape(*flat.shape[:-1], rows, LANES)


def _split8(full, axis):
    shp = full.shape
    t = full.reshape(*shp[:axis], N_DEV, shp[axis] // N_DEV, *shp[axis + 1:])
    return jnp.moveaxis(t, axis, 0).reshape(N_DEV, -1)


def _join8(rows, axis, shard_shape):
    t = rows.reshape(N_DEV, *shard_shape)
    t = jnp.moveaxis(t, 0, axis)
    return t.reshape(*shard_shape[:axis], N_DEV * shard_shape[axis], *shard_shape[axis + 1:])


def _stack_rows(parts, block=ROW_BLOCK):
    rows = sum(p.shape[-2] for p in parts)
    if rows % block:
        parts = list(parts) + [jnp.zeros((*parts[0].shape[:-2], -rows % block, LANES), parts[0].dtype)]
    return jnp.concatenate(parts, axis=-2)


def _pack_local(vals, names, mult):
    return _stack_rows([_to_rows(vals[n].reshape(-1), mult) for n in names])


def _unpack_local(slab, names, shapes, mult):
    out, row = {}, 0
    for n in names:
        size = 1
        for d in shapes[n]:
            size *= d
        rows = _rows_of(size, mult)
        out[n] = slab[row:row + rows].reshape(-1)[:size].reshape(shapes[n])
        row += rows
    return out, row


def kernel(x, mem, positions, a_norm, a_w_in, a_w_out, kv_norm, w_dkv, g_ckv, w_ukv, g_k_nope, g_k_rope, b_norm, b_w_in, b_g_q_lat, b_w_uq, b_g_q_nope, b_g_q_rope, b_w_out, mem_norm, w_mem_kv, g_mem_q, g_mem_k, loss_target, m_a_norm, m_a_w_in, m_a_w_out, m_kv_norm, m_w_dkv, m_g_ckv, m_w_ukv, m_g_k_nope, m_g_k_rope, m_b_norm, m_b_w_in, m_b_g_q_lat, m_b_w_uq, m_b_g_q_nope, m_b_g_q_rope, m_b_w_out, m_mem_norm, m_w_mem_kv, m_g_mem_q, m_g_mem_k, v_a_norm, v_a_w_in, v_a_w_out, v_kv_norm, v_w_dkv, v_g_ckv, v_w_ukv, v_g_k_nope, v_g_k_rope, v_b_norm, v_b_w_in, v_b_g_q_lat, v_b_w_uq, v_b_g_q_nope, v_b_g_q_rope, v_b_w_out, v_mem_norm, v_w_mem_kv, v_g_mem_q, v_g_mem_k):
    wts = dict(a_norm=a_norm, a_w_in=a_w_in, a_w_out=a_w_out, kv_norm=kv_norm, w_dkv=w_dkv, g_ckv=g_ckv, w_ukv=w_ukv,
               g_k_nope=g_k_nope, g_k_rope=g_k_rope, b_norm=b_norm, b_w_in=b_w_in, b_g_q_lat=b_g_q_lat, b_w_uq=b_w_uq,
               b_g_q_nope=b_g_q_nope, b_g_q_rope=b_g_q_rope, b_w_out=b_w_out, mem_norm=mem_norm, w_mem_kv=w_mem_kv,
               g_mem_q=g_mem_q, g_mem_k=g_mem_k)
    mom = dict(a_norm=m_a_norm, a_w_in=m_a_w_in, a_w_out=m_a_w_out, kv_norm=m_kv_norm, w_dkv=m_w_dkv, g_ckv=m_g_ckv,
               w_ukv=m_w_ukv, g_k_nope=m_g_k_nope, g_k_rope=m_g_k_rope, b_norm=m_b_norm, b_w_in=m_b_w_in,
               b_g_q_lat=m_b_g_q_lat, b_w_uq=m_b_w_uq, b_g_q_nope=m_b_g_q_nope, b_g_q_rope=m_b_g_q_rope,
               b_w_out=m_b_w_out, mem_norm=m_mem_norm, w_mem_kv=m_w_mem_kv, g_mem_q=m_g_mem_q, g_mem_k=m_g_mem_k)
    var = dict(a_norm=v_a_norm, a_w_in=v_a_w_in, a_w_out=v_a_w_out, kv_norm=v_kv_norm, w_dkv=v_w_dkv, g_ckv=v_g_ckv,
               w_ukv=v_w_ukv, g_k_nope=v_g_k_nope, g_k_rope=v_g_k_rope, b_norm=v_b_norm, b_w_in=v_b_w_in,
               b_g_q_lat=v_b_g_q_lat, b_w_uq=v_b_w_uq, b_g_q_nope=v_b_g_q_nope, b_g_q_rope=v_b_g_q_rope,
               b_w_out=v_b_w_out, mem_norm=v_mem_norm, w_mem_kv=v_w_mem_kv, g_mem_q=v_g_mem_q, g_mem_k=v_g_mem_k)
    shapes = {n: wts[n].shape for n in WEIGHTS}
    S, D = x.shape[1], x.shape[2]
    xs, ms, tgt = x[0], mem[0], loss_target[0]
    sb_w, mem_w, mla_w = H_SB * DH, H_MEM * DH, H_MLA * DH
    q_lora, kv_lora = b_g_q_lat.shape[-1], g_ckv.shape[-1]

    def pieces_of(names):
        return [_to_rows(wts[n].astype(BF16).reshape(-1), 16) for n in names]

    def unpack_gathered(gathered, names):
        full, row = {}, 0
        for n in names:
            rows = _rows_of(wts[n].size, 16)
            flat = gathered[:, row:row + rows].reshape(N_DEV, -1)[:, :wts[n].size]
            full[n] = _join8(flat, dict(SHARDED)[n], wts[n].shape)
            row += rows
        return full

    second, third = ("a_w_out", "w_dkv", "w_ukv"), ("w_mem_kv", "b_w_in", "b_w_uq", "b_w_out")
    a_in_w = a_w_in.shape[-1]
    bits = jnp.pad(lax.bitcast_convert_type(a_norm.reshape(-1), BF16).reshape(1, -1), ((0, 15), (0, a_in_w - 2 * a_norm.size)))
    w_a_in = _all_gather_columns(jnp.concatenate([a_w_in[0].astype(BF16), bits], axis=0), name="gather_first")
    g_a = lax.bitcast_convert_type(w_a_in[D].reshape(N_DEV, a_in_w)[:, :2 * a_norm.size].reshape(N_DEV, -1, 2),
                                   F32).reshape(1, D)

    row2 = lambda g: g.reshape(1, -1)
    h0 = _rowwise(_f_norm, [(xs, D, 0)], [g_a], [(D, BF16)], name="a_norm_fwd", tm=512)[0]
    qkv, gathered = _mm(h0, w_a_in, "nn", b_cols=(0, 3 * sb_w), out_dtype=BF16, name="a_in_qkv",
                        ride=("gather", jnp.concatenate(pieces_of(second), axis=0)))
    full = unpack_gathered(gathered, second)
    pa = _mm(h0, w_a_in, "nn", b_cols=(3 * sb_w, N_DEV * a_in_w - 3 * sb_w), name="a_in_rest")
    sb, gathered = _sb_fwd(qkv, tq=512, tk=256, name="sb_fwd",
                           ride=("gather", jnp.concatenate(pieces_of(third), axis=0)))
    full.update(unpack_gathered(gathered, third))
    w_a_out = full["a_w_out"][0]
    w_dkv_p = jnp.pad(full["w_dkv"], ((0, 0), (0, ROPE)))
    w_ukv_f = full["w_ukv"]
    wb = full["b_w_in"][0]
    w_b_in = jnp.concatenate([wb[:, q_lora:q_lora + mla_w], wb[:, :q_lora], wb[:, q_lora + mla_w:]], axis=1)
    w_uq_p = jnp.pad(full["b_w_uq"][0].reshape(q_lora, H_MLA, MLA_QK),
                     ((0, 0), (0, 0), (0, 2 * DH - MLA_QK))).reshape(q_lora, H_MLA * 2 * DH)
    w_b_out = full["b_w_out"][0]
    w_mem = full["w_mem_kv"]

    pad128 = lambda g: jnp.pad(g.reshape(1, -1), ((0, 0), (0, DH - ROPE)))
    g_kr, g_qr = pad128(g_k_rope), pad128(b_g_q_rope[0])
    g_kv, g_b, g_c, g_kn = row2(kv_norm), row2(b_norm[0]), row2(g_ckv), row2(g_k_nope)
    g_ql, g_qn = row2(b_g_q_lat[0]), row2(b_g_q_nope[0])

    inv_freq = jnp.power(ROPE_THETA, -jnp.arange(0, ROPE, 2, dtype=F32) / ROPE)
    ang = positions[0].astype(F32)[:, None] * inv_freq
    z64 = jnp.zeros((S, DH - ROPE), F32)
    cs = jnp.concatenate([jnp.cos(ang), jnp.cos(ang), z64, -jnp.sin(ang), jnp.sin(ang), z64], axis=1)

    mn, mkv = [], []
    for l in range(2):
        mn.append(_rowwise(_f_norm, [(ms, D, 0)], [row2(mem_norm[l])], [(D, BF16)], name=f"mem_norm{l}")[0])
        mkv.append(_mm(mn[l], w_mem[l], "nn", name=f"mem_kv{l}"))
    g_mq = [row2(g_mem_q[l]) for l in range(2)]
    g_mk = [row2(g_mem_k[l]) for l in range(2)]

    mix_a_rows = [(sb, sb_w, 0), (pa, sb_w, 0), (pa, mem_w, sb_w // mem_w), (pa, mem_w, sb_w // mem_w + 1)]
    mixed_a = _rowwise(_f_mix, mix_a_rows, [mkv[0], g_mq[0], g_mk[0]], [(sb_w + mem_w, BF16)], name="a_mix_fwd", tm=512)[0]
    x1 = _mm(mixed_a, w_a_out, "nn", add=xs, name="a_out")

    hk, hb = _rowwise(_f_norm2, [(x1, D, 0)], [g_kv, g_b], [(D, BF16), (D, BF16)], name="b_norm_fwd", tm=512)
    ckr = _mm(hk, w_dkv_p, "nn", name="kv_down")
    cn, kr = _rowwise(_f_kv1, [(ckr, kv_lora + DH, 0), (cs, 2 * DH, 0)], [g_c, g_kr],
                      [(kv_lora, BF16), (DH, F32)], name="kv1_fwd", tm=512)
    kvu = _mm(cn, w_ukv_f, "nn", name="kv_up")
    k2, v2 = _rowwise(_f_kv2, [(kvu, H_MLA * 2 * DH, 0), (kr, DH, 0)], [g_kn],
                      [(H_MLA * 2 * DH, BF16), (mla_w, BF16)], name="kv2_fwd")
    pb = _mm(hb, w_b_in, "nn", name="b_in")
    ql = _rowwise(_f_norm, [(pb, q_lora, mla_w // q_lora)], [g_ql], [(q_lora, BF16)], name="q_lat_fwd", tm=512)[0]
    qraw = _mm(ql, w_uq_p, "nn", name="q_up")
    q2 = _rowwise(_f_q2, [(qraw, H_MLA * 2 * DH, 0), (cs, 2 * DH, 0)], [g_qn, g_qr],
                  [(H_MLA * 2 * DH, BF16)], name="q2_fwd")[0]
    att, lse = _mla_fwd(q2, k2, v2, tq=1024, tk=1024, name="mla_fwd")
    cb = (mla_w + q_lora) // mem_w
    mix_b_rows = [(att, mla_w, 0), (pb, mla_w, 0), (pb, mem_w, cb), (pb, mem_w, cb + 1)]
    mixed_b = _rowwise(_f_mix, mix_b_rows, [mkv[1], g_mq[1], g_mk[1]], [(mla_w + mem_w, BF16)], name="b_mix_fwd", tm=512)[0]
    dy, loss_part = _mm_loss(mixed_b, w_b_out, x1, tgt, name="b_out_loss")

    gr = {}
    d_mixed_b = _mm(dy, w_b_out, "nt", name="b_out_dx")
    gr["b_w_out"] = _mm(mixed_b, dy, "tn", out_dtype=BF16, name="b_out_dw")[None]
    d_att, d_gmla, d_qm_b, d_gm_b, d_mkv1, d_gq1, d_gk1 = _rowwise_bwd(
        _f_mix, mix_b_rows, [mkv[1], g_mq[1], g_mk[1]], [(d_mixed_b, mla_w + mem_w, 0)],
        [0, 1, 2, 3], [0, 1, 2], out_dtypes=[F32, BF16, BF16, BF16], name="b_mix_bwd")
    dq2, dk2, dv2 = _mla_bwd(q2, k2, v2, att, d_att, lse, tq=512, tk=512, name="mla_bwd")
    d_qraw, d_gqn, d_gqr = _rowwise_bwd(
        _f_q2, [(qraw, H_MLA * 2 * DH, 0), (cs, 2 * DH, 0)], [g_qn, g_qr], [(dq2, H_MLA * 2 * DH, 0)],
        [0], [0, 1], out_dtypes=[BF16], name="q2_bwd")
    d_ql = _mm(d_qraw, w_uq_p, "nt", name="q_up_dx")
    d_wuq = _mm(ql, d_qraw, "tn", out_dtype=BF16, name="q_up_dw")
    gr["b_w_uq"] = d_wuq.reshape(q_lora, H_MLA, 2 * DH)[:, :, :MLA_QK].reshape(1, q_lora, H_MLA * MLA_QK)
    d_qlat, d_gql = _rowwise_bwd(_f_norm, [(pb, q_lora, mla_w // q_lora)], [g_ql], [(d_ql, q_lora, 0)],
                                 [0], [0], out_dtypes=[BF16], name="q_lat_bwd", tm=512)
    d_pb = jnp.concatenate([d_gmla, d_qlat, d_qm_b, d_gm_b], axis=1)
    d_hb = _mm(d_pb, w_b_in, "nt", name="b_in_dx")
    d_wbin = _mm(hb, d_pb, "tn", out_dtype=BF16, name="b_in_dw")
    gr["b_w_in"] = jnp.concatenate([d_wbin[:, mla_w:mla_w + q_lora], d_wbin[:, :mla_w], d_wbin[:, mla_w + q_lora:]],
                                   axis=1)[None]
    d_kvu, d_kr, d_gkn = _rowwise_bwd(
        _f_kv2, [(kvu, H_MLA * 2 * DH, 0), (kr, DH, 0)], [g_kn], [(dk2, H_MLA * 2 * DH, 0), (dv2, mla_w, 0)],
        [0, 1], [0], out_dtypes=[BF16, F32], name="kv2_bwd")
    d_cn = _mm(d_kvu, w_ukv_f, "nt", name="kv_up_dx")
    gr["w_ukv"] = _mm(cn, d_kvu, "tn", out_dtype=BF16, name="kv_up_dw")
    d_ckr, d_gc, d_gkr = _rowwise_bwd(
        _f_kv1, [(ckr, kv_lora + DH, 0), (cs, 2 * DH, 0)], [g_c, g_kr], [(d_cn, kv_lora, 0), (d_kr, DH, 0)],
        [0], [0, 1], out_dtypes=[BF16], name="kv1_bwd", tm=512)
    d_hk = _mm(d_ckr, w_dkv_p, "nt", name="kv_down_dx")
    gr["w_dkv"] = _mm(hk, d_ckr, "tn", out_dtype=BF16, name="kv_down_dw")[:, :kv_lora + ROPE]
    d_x1, d_gkv, d_gb = _rowwise_bwd(_f_norm2, [(x1, D, 0)], [g_kv, g_b], [(d_hk, D, 0), (d_hb, D, 0)],
                                     [0], [0, 1], add=(dy, D, 0), name="b_norm_bwd")
    d_mixed_a = _mm(d_x1, w_a_out, "nt", name="a_out_dx")
    gr["a_w_out"] = _mm(mixed_a, d_x1, "tn", out_dtype=BF16, name="a_out_dw")[None]
    d_sb, d_gsb, d_qm_a, d_gm_a, d_mkv0, d_gq0, d_gk0 = _rowwise_bwd(
        _f_mix, mix_a_rows, [mkv[0], g_mq[0], g_mk[0]], [(d_mixed_a, sb_w + mem_w, 0)],
        [0, 1, 2, 3], [0, 1, 2], out_dtypes=[F32, BF16, BF16, BF16], name="a_mix_bwd")
    d_wmem, d_mnorm = [], []
    for l, d_mkv in enumerate((d_mkv0, d_mkv1)):
        d_mn = _mm(d_mkv, w_mem[l], "nt", name=f"mem_kv_dx{l}")
        d_wmem.append(_mm(mn[l], d_mkv, "tn", out_dtype=BF16, name=f"mem_kv_dw{l}"))
        d_mnorm.append(_rowwise_bwd(_f_norm, [(ms, D, 0)], [row2(mem_norm[l])], [(d_mn, D, 0)], [], [0],
                                    name=f"mem_norm_bwd{l}")[0])
    gr["w_mem_kv"] = jnp.stack(d_wmem)

    mid = tuple(n for n, _ in SHARDED[2:])
    send_mid = _stack_rows([_to_rows(_split8(gr[n], ax), 16) for n, ax in SHARDED[2:]])
    dq, dk, dv, recv_mid = _sb_bwd(qkv, sb, d_sb, tq=512, tk=256, name="sb_bwd", ride=("a2a", send_mid))
    d_pa = jnp.concatenate([dq, dk, dv, d_gsb, d_qm_a, d_gm_a], axis=1)
    send_ain = _mm(h0, d_pa, "tn", out_dtype=BF16, out_split=N_DEV, name="a_in_dw")
    d_h0, recv_ain = _mm(d_pa, w_a_in, "nt", b_rows=D, name="a_in_dx", ride=("a2a", send_ain))
    grad_x, d_ga = _rowwise_bwd(_f_norm, [(xs, D, 0)], [g_a], [(d_h0, D, 0)], [0], [0], add=(d_x1, D, 0),
                                name="a_norm_bwd")
    gr["mem_norm"] = jnp.concatenate(d_mnorm, axis=0)
    gr["g_mem_q"] = jnp.concatenate([d_gq0, d_gq1], axis=0)
    gr["g_mem_k"] = jnp.concatenate([d_gk0, d_gk1], axis=0)
    gr["kv_norm"], gr["b_norm"], gr["g_ckv"], gr["g_k_nope"] = d_gkv, d_gb, d_gc, d_gkn
    gr["g_k_rope"], gr["b_g_q_rope"] = d_gkr[:, :ROPE], d_gqr[:, :ROPE]
    gr["b_g_q_lat"], gr["b_g_q_nope"] = d_gql, d_gqn
    last = ("a_norm",) + SMALL
    shared = jnp.concatenate([gr[n].reshape(-1) for n in SMALL] + [loss_part[0, :1]])
    flat = jnp.concatenate([_split8(d_ga, 1), jnp.broadcast_to(shared[None], (N_DEV, shared.size))], axis=1)
    recv_last = _all_to_all(_to_rows(flat, ROW_MULT), name="exchange_small")

    out = [{}, {}, {}, {}]
    slabs = _reduce_adamw(recv_mid, *[_pack_local(t, mid, 16) for t in (wts, mom, var)], name="reduce_adamw_mid")
    for o, slab in zip(out, slabs):
        o.update(_unpack_local(slab, mid, shapes, 16)[0])
    zero = jnp.zeros((1,), F32)
    slabs = _reduce_adamw(recv_last, *[_to_rows(jnp.concatenate([t[n].reshape(-1) for n in last] + [zero]), ROW_MULT)
                                       for t in (wts, mom, var)], name="reduce_adamw_last")
    for o, slab in zip(out, slabs):
        vec, at = slab.reshape(-1), 0
        for n in last:
            o[n] = vec[at:at + wts[n].size].reshape(shapes[n])
            at += wts[n].size
    loss = slabs[0].reshape(-1)[at]
    slabs = _reduce_adamw(recv_ain, *[t["a_w_in"][0] for t in (wts, mom, var)], name="reduce_adamw_ain")
    for o, slab in zip(out, slabs):
        o["a_w_in"] = slab[None]
    return (loss, grad_x[None], *[o[n] for o in out for n in WEIGHTS])
```

```python
import functools

import jax
import jax.numpy as jnp
from jax import lax
from jax.experimental import pallas as pl
from jax.experimental.pallas import tpu as pltpu

F32, BF16 = jnp.float32, jnp.bfloat16

N_DEV = 8
DH = 128
H_SB, H_MEM, H_MLA = 12, 4, 12
ROPE = 64
MLA_QK = DH + ROPE
EPS = 1e-6
ROPE_THETA = 10000.0
ADAM_LR, ADAM_B1, ADAM_B2, ADAM_EPS, ADAM_WD, ADAM_STEP = 0.001, 0.9, 0.999, 1e-08, 0.01, 10

LANES = 1024
VMEM_LIMIT = 48 * 1024 * 1024
VMEM_LIMIT_BIG = 56 * 1024 * 1024

NN = (((1,), (0,)), ((), ()))
NT = (((1,), (1,)), ((), ()))
TN = (((0,), (0,)), ((), ()))
_DIMS = {"nn": NN, "nt": NT, "tn": TN}


def _dot16(a, b, dims):
    return lax.dot_general(a.astype(BF16), b.astype(BF16), _DIMS[dims], preferred_element_type=F32)


@functools.partial(jax.custom_vjp, nondiff_argnums=(2,))
def _bdot(a, b, dims):
    return _dot16(a, b, dims)


def _bdot_fwd(a, b, dims):
    return _dot16(a, b, dims), (a, b)


def _bdot_bwd(dims, res, g):
    a, b = res
    if dims == "nn":
        return _dot16(g, b, "nt"), _dot16(a, g, "tn")
    return _dot16(g, b, "nn"), _dot16(g, a, "tn")


_bdot.defvjp(_bdot_fwd, _bdot_bwd)


def _tile(n, pref):
    if n <= pref:
        return n
    t = (pref // 128) * 128
    while n % t:
        t -= 128
    return t


def _mesh_pos():
    return lax.axis_index("x"), lax.axis_index("y"), lax.axis_index("c")


def _ride_shape(kind, src):
    return jax.ShapeDtypeStruct((N_DEV, *src.shape) if kind == "gather" else src.shape, src.dtype)


def _ride_scratch():
    return [pltpu.SemaphoreType.DMA((N_DEV - 1,)), pltpu.SemaphoreType.DMA((N_DEV - 1,)), pltpu.SemaphoreType.DMA]


def _ride(kind, phase, s_ref, r_ref, send_sems, recv_sems, local_sem):
    x, y, c = _mesh_pos()
    me = 4 * x + 2 * y + c
    src = (lambda lin: s_ref) if kind == "gather" else (lambda lin: s_ref.at[lin])
    local = pltpu.make_async_copy(src(me), r_ref.at[me], local_sem)
    if phase == "start":
        local.start()
    for k in range(1, N_DEV):
        p = (1 - x if k & 4 else x, 1 - y if k & 2 else y, 1 - c if k & 1 else c)
        lin = 4 * p[0] + 2 * p[1] + p[2]
        cp = pltpu.make_async_remote_copy(
            src_ref=src(lin), dst_ref=r_ref.at[me] if phase == "start" else r_ref.at[lin],
            send_sem=send_sems.at[k - 1], recv_sem=recv_sems.at[k - 1],
            device_id=p, device_id_type=pl.DeviceIdType.MESH)
        if phase == "start":
            cp.start()
        else:
            cp.wait_recv()
            cp.wait_send()
    if phase == "wait":
        local.wait()


def _mm(a, b, dims, *, name, out_dtype=F32, add=None, ride=None, out_split=None, b_rows=None, b_cols=None,
        tm=1024, tn=1024, tk=2048):
    col0 = 0
    if dims == "tn":
        (K, M), (_, N) = a.shape, b.shape
    elif dims == "nt":
        (M, K), N = a.shape, b_rows or b.shape[0]
    else:
        (M, K), N = a.shape, b.shape[1]
        if b_cols is not None:
            col0, N = b_cols
    tm, tk = _tile(M, tm), _tile(K, tk)
    tn = N // out_split if out_split else _tile(N, tn)
    while col0 % tn or N % tn:
        tn -= 128
    jb = col0 // tn
    ni, nj, nk = M // tm, N // tn, K // tk
    n_in = 2 + (add is not None) + (ride is not None)

    def body(*refs):
        a_ref, b_ref, o_ref = refs[0], refs[1], refs[n_in]
        acc_ref = refs[n_in + 1 + (ride is not None)]
        i, j, k = pl.program_id(0), pl.program_id(1), pl.program_id(2)
        if ride is not None:
            ride_refs = (refs[n_in - 1], refs[n_in + 1], *refs[-3:])

            @pl.when((i == 0) & (j == 0) & (k == 0))
            def _():
                _ride(ride[0], "start", *ride_refs)

        if nk == 1:
            r = _dot16(a_ref[...], b_ref[...], dims)
            o_ref[...] = (r if add is None else r + refs[2][...]).astype(o_ref.dtype)
        else:
            @pl.when(k == 0)
            def _():
                acc_ref[...] = jnp.zeros_like(acc_ref)

            acc_ref[...] += _dot16(a_ref[...], b_ref[...], dims)

            @pl.when(k == nk - 1)
            def _():
                r = acc_ref[...]
                if add is not None:
                    r = r + refs[2][...]
                o_ref[...] = r.astype(o_ref.dtype)

        if ride is not None:
            @pl.when((i == ni - 1) & (j == nj - 1) & (k == nk - 1))
            def _():
                _ride(ride[0], "wait", *ride_refs)

    a_spec = pl.BlockSpec((tk, tm), lambda i, j, k: (k, i)) if dims == "tn" else pl.BlockSpec((tm, tk), lambda i, j, k: (i, k))
    b_spec = pl.BlockSpec((tn, tk), lambda i, j, k: (j, k)) if dims == "nt" else pl.BlockSpec((tk, tn), lambda i, j, k: (k, j + jb))
    o_spec = pl.BlockSpec((tm, tn), lambda i, j, k: (i, j))
    in_specs, args = [a_spec, b_spec], [a, b]
    if add is not None:
        in_specs.append(o_spec)
        args.append(add)
    out_specs, out_shape = [o_spec], [jax.ShapeDtypeStruct((M, N), out_dtype)]
    scratch = [pltpu.VMEM((tm, tn) if nk > 1 else (8, 128), F32)]
    if out_split:
        out_specs = [pl.BlockSpec((None, tm, tn), lambda i, j, k: (j, i, 0))]
        out_shape = [jax.ShapeDtypeStruct((out_split, M, tn), out_dtype)]
    if ride is not None:
        in_specs.append(pl.BlockSpec(memory_space=pl.ANY))
        args.append(ride[1])
        out_specs.append(pl.BlockSpec(memory_space=pl.ANY))
        out_shape.append(_ride_shape(*ride))
        scratch += _ride_scratch()
    sem = ("arbitrary",) * 3 if ride is not None else ("parallel", "parallel", "arbitrary")
    res = pl.pallas_call(
        body, name=name, grid=(ni, nj, nk), in_specs=in_specs, out_specs=out_specs, out_shape=out_shape,
        scratch_shapes=scratch,
        compiler_params=pltpu.CompilerParams(dimension_semantics=sem, vmem_limit_bytes=VMEM_LIMIT),
    )(*args)
    return res[0] if ride is None else res


def _mm_loss(a, b, res, tgt, *, name, tm=1024, tn=1024):
    (M, K), N = a.shape, b.shape[1]
    tm, tn = _tile(M, tm), _tile(N, tn)

    def body(a_ref, b_ref, r_ref, t_ref, dy_ref, loss_ref):
        @pl.when((pl.program_id(0) == 0) & (pl.program_id(1) == 0))
        def _():
            loss_ref[...] = jnp.zeros_like(loss_ref)

        err = _dot16(a_ref[...], b_ref[...], "nn") + r_ref[...] - t_ref[...]
        dy_ref[...] = err * (1.0 / N)
        loss_ref[...] += 0.5 * jnp.sum(err * err) * (1.0 / N)

    tile = pl.BlockSpec((tm, tn), lambda i, j: (i, j))
    return pl.pallas_call(
        body, name=name, grid=(M // tm, N // tn),
        in_specs=[pl.BlockSpec((tm, K), lambda i, j: (i, 0)), pl.BlockSpec((K, tn), lambda i, j: (0, j)), tile, tile],
        out_specs=[tile, pl.BlockSpec((1, DH), lambda i, j: (0, 0))],
        out_shape=[jax.ShapeDtypeStruct((M, N), F32), jax.ShapeDtypeStruct((1, DH), F32)],
        compiler_params=pltpu.CompilerParams(dimension_semantics=("arbitrary", "arbitrary"), vmem_limit_bytes=VMEM_LIMIT),
    )(a, b, res, tgt)


def _rowwise(fn, rows, consts, outs, accs=(), *, name, tm=256):
    S = rows[0][0].shape[0]
    tm = min(tm, S)
    nr, nc, no = len(rows), len(consts), len(outs)

    def body(*refs):
        res = fn(*[r[...] for r in refs[:nr + nc]])
        res = tuple(res) if isinstance(res, (tuple, list)) else (res,)
        orefs, arefs = refs[nr + nc:nr + nc + no], refs[nr + nc + no:]
        for r, v in zip(orefs, res[:no]):
            r[...] = v.astype(r.dtype)
        if arefs:
            @pl.when(pl.program_id(0) == 0)
            def _():
                for r in arefs:
                    r[...] = jnp.zeros_like(r)

            for r, v in zip(arefs, res[no:]):
                r[...] += v

    in_specs = [pl.BlockSpec((tm, w), lambda i, cb=cb: (i, cb)) for (_, w, cb) in rows]
    in_specs += [pl.BlockSpec(c.shape, lambda i: (0, 0)) for c in consts]
    out_specs = [pl.BlockSpec((tm, w), lambda i: (i, 0)) for (w, _) in outs]
    out_specs += [pl.BlockSpec(s, lambda i: (0, 0)) for s in accs]
    out_shape = [jax.ShapeDtypeStruct((S, w), dt) for (w, dt) in outs]
    out_shape += [jax.ShapeDtypeStruct(s, F32) for s in accs]
    res = pl.pallas_call(
        body, name=name, grid=(S // tm,), in_specs=in_specs, out_specs=out_specs, out_shape=out_shape,
        compiler_params=pltpu.CompilerParams(dimension_semantics=("arbitrary",), vmem_limit_bytes=VMEM_LIMIT),
    )(*[r[0] for r in rows], *consts)
    return res


def _rowwise_bwd(f, rows, consts, cts, row_grads, const_grads, *, name, add=None, out_dtypes=None, tm=256):
    nr, nc, nct = len(rows), len(consts), len(cts)
    all_rows = list(rows) + list(cts) + ([add] if add is not None else [])

    def fn(*args):
        nrow = len(all_rows)
        prim = [x.astype(F32) for x in args[:nr]] + [x.astype(F32) for x in args[nrow:]]
        ct = tuple(x.astype(F32) for x in args[nr:nr + nct])
        out, vjp = jax.vjp(f, *prim)
        gs = vjp(ct if isinstance(out, (tuple, list)) else ct[0])
        res = [gs[k] for k in row_grads]
        if add is not None:
            res[0] = res[0] + args[nrow - 1]
        return tuple(res) + tuple(gs[nr + k] for k in const_grads)

    out_dtypes = out_dtypes or [F32] * len(row_grads)
    outs = [(rows[k][1], dt) for k, dt in zip(row_grads, out_dtypes)]
    accs = [consts[k].shape for k in const_grads]
    return _rowwise(fn, all_rows, consts, outs, accs, name=name, tm=tm)


def _rms(x, g, n=None):
    ms = jnp.sum(x * x, axis=-1, keepdims=True) * (1.0 / (n or x.shape[-1]))
    return x * lax.rsqrt(ms + EPS) * g


def _sigmoid(x):
    return 1.0 / (1.0 + jnp.exp(-x))


def _swap_halves_exact(x):
    r = lax.broadcasted_iota(jnp.int32, (DH, DH), 0)
    c = lax.broadcasted_iota(jnp.int32, (DH, DH), 1)
    half = ROPE // 2
    perm = (((r < half) & (c == r + half)) | ((r >= half) & (r < ROPE) & (c == r - half))).astype(BF16)
    hi = x.astype(BF16)
    rest = x - hi.astype(F32)
    mid = rest.astype(BF16)
    lo = (rest - mid.astype(F32)).astype(BF16)
    return sum(lax.dot_general(p, perm, NN, preferred_element_type=F32) for p in (hi, mid, lo))


_swap_halves = jax.custom_vjp(_swap_halves_exact)
_swap_halves.defvjp(lambda x: (_swap_halves_exact(x), None), lambda _, g: (_swap_halves_exact(g),))


def _rope128(x, g128, cs):
    y = _rms(x, g128, n=ROPE)
    return y * cs[:, :DH] + _swap_halves(y) * cs[:, DH:]


def _f_norm(x, g):
    return _rms(x, g)


def _f_norm2(x, g1, g2):
    xn = x * lax.rsqrt(jnp.mean(x * x, axis=-1, keepdims=True) + EPS)
    return xn * g1, xn * g2


def _f_kv1(ckr, cs, g_ckv, g_kr):
    w = g_ckv.shape[-1]
    return _rms(ckr[:, :w], g_ckv), _rope128(ckr[:, w:], g_kr, cs)


def _f_kv2(kv, kr, g_kn):
    ks, vs = [], []
    for h in range(H_MLA):
        ks += [_rms(kv[:, 2 * DH * h:2 * DH * h + DH], g_kn), kr]
        vs.append(kv[:, 2 * DH * h + DH:2 * DH * (h + 1)])
    return jnp.concatenate(ks, axis=1), jnp.concatenate(vs, axis=1)


def _f_q2(q, cs, g_n, g_r):
    out = []
    for h in range(H_MLA):
        out += [_rms(q[:, 2 * DH * h:2 * DH * h + DH], g_n), _rope128(q[:, 2 * DH * h + DH:2 * DH * (h + 1)], g_r, cs)]
    return jnp.concatenate(out, axis=1)


def _f_memkv(mkv, g_k):
    mem_w = H_MEM * DH
    return jnp.concatenate([_rms(mkv[:, h * DH:(h + 1) * DH], g_k) for h in range(H_MEM)] + [mkv[:, mem_w:]], axis=1)


def _f_mix(att, g_att, q_m, g_m, mkn, g_q):
    mem_w = H_MEM * DH
    heads = []
    for h in range(H_MEM):
        kh = mkn[:, h * DH:(h + 1) * DH]
        vh = mkn[:, mem_w + h * DH:mem_w + (h + 1) * DH]
        qh = _rms(q_m[:, h * DH:(h + 1) * DH], g_q)
        s = _bdot(qh, kh, "nt") * (DH ** -0.5)
        p = jnp.exp(s - lax.stop_gradient(jnp.max(s, axis=-1, keepdims=True)))
        p = p / jnp.sum(p, axis=-1, keepdims=True)
        heads.append(_bdot(p, vh, "nn"))
    mo = jnp.concatenate(heads, axis=1)
    return jnp.concatenate([att * (g_att * _sigmoid(g_att)), mo * (g_m * _sigmoid(g_m))], axis=1)


def _split_dot(x, u):
    hi = x.astype(BF16)
    lo = (x - hi.astype(F32)).astype(BF16)
    return (lax.dot_general(hi, u, NN, preferred_element_type=F32)
            + lax.dot_general(lo, u, NN, preferred_element_type=F32))


def _tri(t):
    r = lax.broadcasted_iota(jnp.int32, (t, t), 0)
    c = lax.broadcasted_iota(jnp.int32, (t, t), 1)
    return r, c


def _strict_lower(t):
    r, c = _tri(t)
    return (r > c).astype(BF16)


def _head_blocks_t(x, w, tk, *, name):
    S = x.shape[0]
    H = x.shape[1] // w

    def body(x_ref, o_ref):
        for h in range(H):
            o_ref[h] = x_ref[:, h * w:(h + 1) * w].T

    return pl.pallas_call(
        body, name=name, grid=(S // tk,),
        in_specs=[pl.BlockSpec((tk, H * w), lambda i: (i, 0))],
        out_specs=pl.BlockSpec((H, None, w, tk), lambda i: (0, i, 0, 0)),
        out_shape=jax.ShapeDtypeStruct((H, S // tk, w, tk), x.dtype),
        compiler_params=pltpu.CompilerParams(dimension_semantics=("parallel",), vmem_limit_bytes=VMEM_LIMIT),
    )(x)


def _rows_ahead(tq, tk):
    return lax.broadcasted_iota(jnp.int32, (tq, tk), 0) - lax.broadcasted_iota(jnp.int32, (tq, tk), 1)


EXP_UNDERFLOW = -110.0


def _log_one_minus_beta(zr, scale):
    zs, nz = zr * scale, zr * (-scale)
    return zs, jnp.minimum(nz, 0.0) - jnp.log(1.0 + jnp.exp(jnp.minimum(zs, nz)))


def _sb_fwd(qkv, *, tq, tk, name, ride=None):
    S = qkv.shape[0]
    tq, tk = min(tq, S), min(tk, tq, S)
    nd = tq // tk
    H = H_SB
    scale = DH ** -0.5

    nq = S // tq

    def body(q_ref, k_ref, v_ref, *rest):
        o_ref = rest[1] if ride is not None else rest[0]
        h, i = pl.program_id(0), pl.program_id(1)
        if ride is not None:
            ride_refs = (rest[0], *rest[2:])

            @pl.when((h == 0) & (i == 0))
            def _():
                _ride(ride[0], "start", *ride_refs)

        q = q_ref[...]
        u = _strict_lower(tk)
        ahead = _rows_ahead(tq, tk)

        def block(j, acc, cb, keep):
            off = pl.multiple_of(j * tk, tk)
            k = k_ref[pl.ds(off, tk), :]
            v = v_ref[pl.ds(off, tk), :]
            z, l = _log_one_minus_beta(lax.dot_general(q, k, NT, preferred_element_type=F32), scale)
            if keep is not None:
                l = jnp.where(keep, l, 0.0)
            a = jnp.exp((z + l) + (_split_dot(l, u) + cb))
            if keep is not None:
                a = jnp.where(keep, a, 0.0)
            acc = acc + lax.dot_general(a.astype(BF16), v, NN, preferred_element_type=F32)
            return acc, cb + jnp.sum(l, axis=1, keepdims=True)

        carry = (jnp.zeros((tq, DH), F32), jnp.zeros((tq, 1), F32))
        for t in reversed(range(nd)):
            carry = block(i * nd + t, *carry, ahead > t * tk)
        _, acc, _ = lax.while_loop(
            lambda st: (st[0] < i * nd) & (jnp.max(st[2]) > EXP_UNDERFLOW),
            lambda st: (st[0] + 1, *block(i * nd - 1 - st[0], st[1], st[2], None)), (jnp.int32(0), *carry))
        o_ref[...] = acc

        if ride is not None:
            @pl.when((h == H - 1) & (i == nq - 1))
            def _():
                _ride(ride[0], "wait", *ride_refs)

    in_specs = [pl.BlockSpec((tq, DH), lambda h, i: (i, h)),
                pl.BlockSpec((S, DH), lambda h, i: (0, H + h)),
                pl.BlockSpec((S, DH), lambda h, i: (0, 2 * H + h))]
    out_specs = [pl.BlockSpec((tq, DH), lambda h, i: (i, h))]
    out_shape = [jax.ShapeDtypeStruct((S, H * DH), F32)]
    args, scratch = [qkv, qkv, qkv], []
    if ride is not None:
        in_specs.append(pl.BlockSpec(memory_space=pl.ANY))
        args.append(ride[1])
        out_specs.append(pl.BlockSpec(memory_space=pl.ANY))
        out_shape.append(_ride_shape(*ride))
        scratch = _ride_scratch()
    res = pl.pallas_call(
        body, name=name, grid=(H, nq), in_specs=in_specs, out_specs=out_specs, out_shape=out_shape,
        scratch_shapes=scratch,
        compiler_params=pltpu.CompilerParams(dimension_semantics=("arbitrary", "arbitrary"), vmem_limit_bytes=VMEM_LIMIT),
    )(*args)
    return res[0] if ride is None else res


def _sb_bwd(qkv, o, do, *, tq, tk, name, ride=None):
    S = qkv.shape[0]
    tq, tk = min(tq, S), min(tk, tq, S)
    nd = tq // tk
    H = H_SB
    scale = DH ** -0.5

    nq = S // tq

    def body(q_ref, k_ref, v_ref, o_ref, do_ref, *rest):
        if ride is not None:
            dq_ref, dk_out, dv_out, dk_ref, dv_ref = rest[1], rest[2], rest[3], rest[5], rest[6]
            ride_refs = (rest[0], rest[4], *rest[7:])
        else:
            dq_ref, dk_out, dv_out, dk_ref, dv_ref = rest
        h, i = pl.program_id(0), pl.program_id(1)
        if ride is not None:
            @pl.when((h == 0) & (i == 0))
            def _():
                _ride(ride[0], "start", *ride_refs)

        @pl.when(i == 0)
        def _():
            dk_ref[...] = jnp.zeros_like(dk_ref)
            dv_ref[...] = jnp.zeros_like(dv_ref)

        q = q_ref[...]
        do = do_ref[...]
        do16 = do.astype(BF16)
        dsum = jnp.sum(do16.astype(F32) * o_ref[...], axis=1, keepdims=True)
        u = _strict_lower(tk)
        ahead = _rows_ahead(tq, tk)

        def block(j, dq, cb, ce, keep):
            off = pl.multiple_of(j * tk, tk)
            k = k_ref[pl.ds(off, tk), :]
            v = v_ref[pl.ds(off, tk), :]
            z, l = _log_one_minus_beta(lax.dot_general(q, k, NT, preferred_element_type=F32), scale)
            if keep is not None:
                l = jnp.where(keep, l, 0.0)
            log_beta = z + l
            a = jnp.exp(log_beta + (_split_dot(l, u) + cb))
            if keep is not None:
                a = jnp.where(keep, a, 0.0)
            a16 = a.astype(BF16)
            e = a16.astype(F32) * lax.dot_general(do16, v, NT, preferred_element_type=F32)
            left = dsum - (ce + _split_dot(e, u) + e)
            dz = (e - jnp.exp(log_beta) * (e + left)) * scale
            if keep is not None:
                dz = jnp.where(keep, dz, 0.0)
            dz = dz.astype(BF16)
            dq = dq + lax.dot_general(dz, k, NN, preferred_element_type=F32)
            dk_ref[pl.ds(off, tk), :] += lax.dot_general(dz, q, TN, preferred_element_type=F32)
            dv_ref[pl.ds(off, tk), :] += lax.dot_general(a16, do16, TN, preferred_element_type=F32)
            return dq, cb + jnp.sum(l, axis=1, keepdims=True), ce + jnp.sum(e, axis=1, keepdims=True)

        zero = jnp.zeros((tq, 1), F32)
        carry = (jnp.zeros((tq, DH), F32), zero, zero)
        for t in reversed(range(nd)):
            carry = block(i * nd + t, *carry, ahead > t * tk)
        _, dq, _, _ = lax.while_loop(
            lambda st: (st[0] < i * nd) & (jnp.max(st[2]) > EXP_UNDERFLOW),
            lambda st: (st[0] + 1, *block(i * nd - 1 - st[0], st[1], st[2], st[3], None)), (jnp.int32(0), *carry))
        dq_ref[...] = dq.astype(dq_ref.dtype)

        @pl.when(i == nq - 1)
        def _():
            dk_out[...] = dk_ref[...].astype(dk_out.dtype)
            dv_out[...] = dv_ref[...].astype(dv_out.dtype)

        if ride is not None:
            @pl.when((h == H - 1) & (i == nq - 1))
            def _():
                _ride(ride[0], "wait", *ride_refs)

    blk = pl.BlockSpec((tq, DH), lambda h, i: (i, h))
    whole = pl.BlockSpec((S, DH), lambda h, i: (0, h))
    shp = jax.ShapeDtypeStruct((S, H * DH), BF16)
    in_specs = [blk, pl.BlockSpec((S, DH), lambda h, i: (0, H + h)), pl.BlockSpec((S, DH), lambda h, i: (0, 2 * H + h)),
                blk, blk]
    out_specs, out_shape = [blk, whole, whole], [shp, shp, shp]
    args, scratch = [qkv, qkv, qkv, o, do], [pltpu.VMEM((S, DH), F32), pltpu.VMEM((S, DH), F32)]
    if ride is not None:
        in_specs.append(pl.BlockSpec(memory_space=pl.ANY))
        args.append(ride[1])
        out_specs.append(pl.BlockSpec(memory_space=pl.ANY))
        out_shape.append(_ride_shape(*ride))
        scratch += _ride_scratch()
    return pl.pallas_call(
        body, name=name, grid=(H, nq), in_specs=in_specs, out_specs=out_specs, out_shape=out_shape,
        scratch_shapes=scratch,
        compiler_params=pltpu.CompilerParams(dimension_semantics=("arbitrary", "arbitrary"), vmem_limit_bytes=VMEM_LIMIT),
    )(*args)


def _mla_fwd(q, k, v, *, tq, tk, name):
    S = q.shape[0]
    tq, tk = min(tq, S), min(tk, tq, S)
    nd, nb = tq // tk, S // tk
    H = H_MLA
    scale = MLA_QK ** -0.5
    vt = _head_blocks_t(v, DH, tk, name=name + "_vt")

    def body(q_ref, k_ref, vt_ref, o_ref, lse_ref):
        i = pl.program_id(1)
        qb = q_ref[...]
        behind = lax.broadcasted_iota(jnp.int32, (tk, tq), 1) - lax.broadcasted_iota(jnp.int32, (tk, tq), 0)

        def block(j, m, den, acct, keep):
            off = pl.multiple_of(j * tk, tk)
            st = lax.dot_general(k_ref[pl.ds(off, tk), :], qb, NT, preferred_element_type=F32) * scale
            if keep is not None:
                st = jnp.where(keep, st, -1e30)
            m_new = jnp.maximum(m, jnp.max(st, axis=0, keepdims=True))
            pt = jnp.exp(st - m_new)
            alpha = jnp.exp(m - m_new)
            den = alpha * den + jnp.sum(pt, axis=0, keepdims=True)
            acct = alpha * acct + lax.dot_general(vt_ref[j], pt.astype(BF16), NN, preferred_element_type=F32)
            return m_new, den, acct

        init = (jnp.full((1, tq), -1e30, F32), jnp.zeros((1, tq), F32), jnp.zeros((DH, tq), F32))
        carry = lax.fori_loop(0, i * nd, lambda j, carry: block(j, *carry, None), init)
        for t in range(nd):
            carry = block(i * nd + t, *carry, behind >= t * tk)
        m, den, acct = carry
        o_ref[...] = (acct / den).T
        lse_ref[0] = m + jnp.log(den)

    return pl.pallas_call(
        body, name=name, grid=(H, S // tq),
        in_specs=[pl.BlockSpec((tq, 2 * DH), lambda h, i: (i, h)),
                  pl.BlockSpec((S, 2 * DH), lambda h, i: (0, h)),
                  pl.BlockSpec((None, nb, DH, tk), lambda h, i: (h, 0, 0, 0))],
        out_specs=[pl.BlockSpec((tq, DH), lambda h, i: (i, h)), pl.BlockSpec((1, 1, tq), lambda h, i: (h, 0, i))],
        out_shape=[jax.ShapeDtypeStruct((S, H * DH), F32), jax.ShapeDtypeStruct((H, 1, S), F32)],
        compiler_params=pltpu.CompilerParams(dimension_semantics=("arbitrary", "arbitrary"), vmem_limit_bytes=VMEM_LIMIT),
    )(q, k, vt)


def _mla_bwd(q, k, v, o, do, lse, *, tq, tk, name):
    S = q.shape[0]
    tq, tk = min(tq, S), min(tk, tq, S)
    nd, nb = tq // tk, S // tk
    H = H_MLA
    scale = MLA_QK ** -0.5
    kt = _head_blocks_t(k, 2 * DH, tk, name=name + "_kt")

    def body(q_ref, k_ref, kt_ref, v_ref, o_ref, do_ref, lse_ref, dq_ref, dk_ref, dv_ref):
        i = pl.program_id(1)

        @pl.when(i == 0)
        def _():
            dk_ref[...] = jnp.zeros_like(dk_ref)
            dv_ref[...] = jnp.zeros_like(dv_ref)

        qb = q_ref[...]
        do = do_ref[...]
        do16 = do.astype(BF16)
        dsum = jnp.sum((do * o_ref[...]).T, axis=0, keepdims=True)
        lse = lse_ref[0]
        behind = lax.broadcasted_iota(jnp.int32, (tk, tq), 1) - lax.broadcasted_iota(jnp.int32, (tk, tq), 0)

        def block(j, dqt, keep):
            off = pl.multiple_of(j * tk, tk)
            kb = k_ref[pl.ds(off, tk), :]
            vb = v_ref[pl.ds(off, tk), :]
            st = lax.dot_general(kb, qb, NT, preferred_element_type=F32) * scale
            if keep is not None:
                st = jnp.where(keep, st, -1e30)
            pt = jnp.exp(st - lse)
            dpt = lax.dot_general(vb, do16, NT, preferred_element_type=F32)
            dst = (pt * (dpt - dsum) * scale).astype(BF16)
            dk_ref[pl.ds(off, tk), :] += lax.dot_general(dst, qb, NN, preferred_element_type=F32)
            dv_ref[pl.ds(off, tk), :] += lax.dot_general(pt.astype(BF16), do16, NN, preferred_element_type=F32)
            return dqt + lax.dot_general(kt_ref[j], dst, NN, preferred_element_type=F32)

        dqt = lax.fori_loop(0, i * nd, lambda j, dqt: block(j, dqt, None), jnp.zeros((2 * DH, tq), F32))
        for t in range(nd):
            dqt = block(i * nd + t, dqt, behind >= t * tk)
        dq_ref[...] = dqt.T

    blk = pl.BlockSpec((tq, DH), lambda h, i: (i, h))
    blk2 = pl.BlockSpec((tq, 2 * DH), lambda h, i: (i, h))
    return pl.pallas_call(
        body, name=name, grid=(H, S // tq),
        in_specs=[blk2, pl.BlockSpec((S, 2 * DH), lambda h, i: (0, h)),
                  pl.BlockSpec((None, nb, 2 * DH, tk), lambda h, i: (h, 0, 0, 0)),
                  pl.BlockSpec((S, DH), lambda h, i: (0, h)),
                  blk, blk, pl.BlockSpec((1, 1, tq), lambda h, i: (h, 0, i))],
        out_specs=[blk2, pl.BlockSpec((S, 2 * DH), lambda h, i: (0, h)), pl.BlockSpec((S, DH), lambda h, i: (0, h))],
        out_shape=[jax.ShapeDtypeStruct((S, H * 2 * DH), F32), jax.ShapeDtypeStruct((S, H * 2 * DH), F32),
                   jax.ShapeDtypeStruct((S, H * DH), F32)],
        compiler_params=pltpu.CompilerParams(dimension_semantics=("arbitrary", "arbitrary"), vmem_limit_bytes=VMEM_LIMIT_BIG),
    )(q, k, kt, v, o, do, lse)


def _all_gather_columns(block, *, name):
    R, C = block.shape

    def body(x_ref, out_ref, send_sems, recv_sems, local_sem):
        x, y, c = _mesh_pos()
        me, sibling = (x, y, c), (x, y, 1 - c)
        chips = [(1 - x, y), (x, 1 - y), (1 - x, 1 - y)]

        def slot(px, py, pc):
            return out_ref.at[:, pl.ds(pl.multiple_of((4 * px + 2 * py + pc) * C, 128), C)]

        def copy(k, blk, to, src=None):
            return pltpu.make_async_remote_copy(
                src_ref=slot(*blk) if src is None else src, dst_ref=slot(*blk),
                send_sem=send_sems.at[k], recv_sem=recv_sems.at[k],
                device_id=to, device_id_type=pl.DeviceIdType.MESH)

        mine = pltpu.make_async_copy(x_ref, slot(*me), local_sem)
        mine.start()
        first = [copy(0, me, sibling, src=x_ref)]
        first += [copy(1 + j, me, (*chip, c), src=x_ref) for j, chip in enumerate(chips)]
        for cp in first:
            cp.start()
        passed = [copy(4 + j, (*chip, c), sibling) for j, chip in enumerate(chips)]
        for j, chip in enumerate(chips):
            copy(1 + j, (*chip, c), me).wait_recv()
            passed[j].start()
        copy(0, sibling, me).wait_recv()
        for j, chip in enumerate(chips):
            copy(4 + j, (*chip, 1 - c), me).wait_recv()
        for cp in first + passed:
            cp.wait_send()
        mine.wait()

    return pl.pallas_call(
        body, name=name,
        out_shape=jax.ShapeDtypeStruct((R, N_DEV * C), block.dtype),
        in_specs=[pl.BlockSpec(memory_space=pl.ANY)], out_specs=pl.BlockSpec(memory_space=pl.ANY),
        scratch_shapes=[pltpu.SemaphoreType.DMA((7,)), pltpu.SemaphoreType.DMA((7,)), pltpu.SemaphoreType.DMA],
    )(block)


def _all_to_all(send, *, name):
    def body(*refs):
        _ride("a2a", "start", *refs)
        _ride("a2a", "wait", *refs)

    return pl.pallas_call(
        body, name=name, out_shape=_ride_shape("a2a", send),
        in_specs=[pl.BlockSpec(memory_space=pl.ANY)], out_specs=pl.BlockSpec(memory_space=pl.ANY),
        scratch_shapes=_ride_scratch(),
    )(send)


def _reduce_adamw(recv, w, m, v, *, name):
    R, C = w.shape
    tr = next(t for t in (128, 64, 32, 16, 8) if R % t == 0)

    def body(g_ref, w_ref, m_ref, v_ref, og_ref, od_ref, om_ref, ov_ref):
        g = g_ref[0].astype(F32)
        for s in range(1, N_DEV):
            g = g + g_ref[s].astype(F32)
        mn = ADAM_B1 * m_ref[...] + (1.0 - ADAM_B1) * g
        vn = ADAM_B2 * v_ref[...] + (1.0 - ADAM_B2) * jnp.square(g)
        m_hat = mn / (1.0 - ADAM_B1 ** ADAM_STEP)
        v_hat = vn / (1.0 - ADAM_B2 ** ADAM_STEP)
        og_ref[...] = g
        od_ref[...] = -ADAM_LR * (m_hat / (jnp.sqrt(v_hat) + ADAM_EPS) + ADAM_WD * w_ref[...])
        om_ref[...] = mn
        ov_ref[...] = vn

    blk = pl.BlockSpec((tr, C), lambda i: (i, 0))
    shp = jax.ShapeDtypeStruct((R, C), F32)
    return pl.pallas_call(
        body, name=name, grid=(R // tr,),
        in_specs=[pl.BlockSpec((N_DEV, tr, C), lambda i: (0, i, 0)), blk, blk, blk],
        out_specs=[blk, blk, blk, blk], out_shape=[shp, shp, shp, shp],
        compiler_params=pltpu.CompilerParams(dimension_semantics=("parallel",), vmem_limit_bytes=VMEM_LIMIT),
    )(recv, w, m, v)


SHARDED = (("a_norm", 1), ("a_w_in", 2), ("a_w_out", 1), ("w_dkv", 0), ("w_ukv", 1), ("b_w_in", 2),
           ("b_w_uq", 2), ("b_w_out", 1), ("w_mem_kv", 1))
SMALL = ("kv_norm", "g_ckv", "g_k_nope", "g_k_rope", "b_norm", "b_g_q_lat", "b_g_q_nope", "b_g_q_rope",
         "mem_norm", "g_mem_q", "g_mem_k")
WEIGHTS = ("a_norm", "a_w_in", "a_w_out", "kv_norm", "w_dkv", "g_ckv", "w_ukv", "g_k_nope", "g_k_rope", "b_norm",
           "b_w_in", "b_g_q_lat", "b_w_uq", "b_g_q_nope", "b_g_q_rope", "b_w_out", "mem_norm", "w_mem_kv",
           "g_mem_q", "g_mem_k")
ROW_MULT = 8
ROW_BLOCK = 128


def _rows_of(n, mult):
    rows = -(-n // LANES)
    return -(-rows // mult) * mult


def _to_rows(flat, mult):
    n = flat.shape[-1]
    rows = _rows_of(n, mult)
    pad = [(0, 0)] * (flat.ndim - 1) + [(0, rows * LANES - n)]
    return jnp.pad(flat, pad).reshape(*flat.shape[:-1], rows, LANES)


def _split8(full, axis):
    shp = full.shape
    t = full.reshape(*shp[:axis], N_DEV, shp[axis] // N_DEV, *shp[axis + 1:])
    return jnp.moveaxis(t, axis, 0).reshape(N_DEV, -1)


def _join8(rows, axis, shard_shape):
    t = rows.reshape(N_DEV, *shard_shape)
    t = jnp.moveaxis(t, 0, axis)
    return t.reshape(*shard_shape[:axis], N_DEV * shard_shape[axis], *shard_shape[axis + 1:])


def _stack_rows(parts, block=ROW_BLOCK):
    rows = sum(p.shape[-2] for p in parts)
    if rows % block:
        parts = list(parts) + [jnp.zeros((*parts[0].shape[:-2], -rows % block, LANES), parts[0].dtype)]
    return jnp.concatenate(parts, axis=-2)


def _pack_local(vals, names, mult):
    return _stack_rows([_to_rows(vals[n].reshape(-1), mult) for n in names])


def _unpack_local(slab, names, shapes, mult):
    out, row = {}, 0
    for n in names:
        size = 1
        for d in shapes[n]:
            size *= d
        rows = _rows_of(size, mult)
        out[n] = slab[row:row + rows].reshape(-1)[:size].reshape(shapes[n])
        row += rows
    return out, row


def kernel(x, mem, positions, a_norm, a_w_in, a_w_out, kv_norm, w_dkv, g_ckv, w_ukv, g_k_nope, g_k_rope, b_norm, b_w_in, b_g_q_lat, b_w_uq, b_g_q_nope, b_g_q_rope, b_w_out, mem_norm, w_mem_kv, g_mem_q, g_mem_k, loss_target, m_a_norm, m_a_w_in, m_a_w_out, m_kv_norm, m_w_dkv, m_g_ckv, m_w_ukv, m_g_k_nope, m_g_k_rope, m_b_norm, m_b_w_in, m_b_g_q_lat, m_b_w_uq, m_b_g_q_nope, m_b_g_q_rope, m_b_w_out, m_mem_norm, m_w_mem_kv, m_g_mem_q, m_g_mem_k, v_a_norm, v_a_w_in, v_a_w_out, v_kv_norm, v_w_dkv, v_g_ckv, v_w_ukv, v_g_k_nope, v_g_k_rope, v_b_norm, v_b_w_in, v_b_g_q_lat, v_b_w_uq, v_b_g_q_nope, v_b_g_q_rope, v_b_w_out, v_mem_norm, v_w_mem_kv, v_g_mem_q, v_g_mem_k):
    wts = dict(a_norm=a_norm, a_w_in=a_w_in, a_w_out=a_w_out, kv_norm=kv_norm, w_dkv=w_dkv, g_ckv=g_ckv, w_ukv=w_ukv,
               g_k_nope=g_k_nope, g_k_rope=g_k_rope, b_norm=b_norm, b_w_in=b_w_in, b_g_q_lat=b_g_q_lat, b_w_uq=b_w_uq,
               b_g_q_nope=b_g_q_nope, b_g_q_rope=b_g_q_rope, b_w_out=b_w_out, mem_norm=mem_norm, w_mem_kv=w_mem_kv,
               g_mem_q=g_mem_q, g_mem_k=g_mem_k)
    mom = dict(a_norm=m_a_norm, a_w_in=m_a_w_in, a_w_out=m_a_w_out, kv_norm=m_kv_norm, w_dkv=m_w_dkv, g_ckv=m_g_ckv,
               w_ukv=m_w_ukv, g_k_nope=m_g_k_nope, g_k_rope=m_g_k_rope, b_norm=m_b_norm, b_w_in=m_b_w_in,
               b_g_q_lat=m_b_g_q_lat, b_w_uq=m_b_w_uq, b_g_q_nope=m_b_g_q_nope, b_g_q_rope=m_b_g_q_rope,
               b_w_out=m_b_w_out, mem_norm=m_mem_norm, w_mem_kv=m_w_mem_kv, g_mem_q=m_g_mem_q, g_mem_k=m_g_mem_k)
    var = dict(a_norm=v_a_norm, a_w_in=v_a_w_in, a_w_out=v_a_w_out, kv_norm=v_kv_norm, w_dkv=v_w_dkv, g_ckv=v_g_ckv,
               w_ukv=v_w_ukv, g_k_nope=v_g_k_nope, g_k_rope=v_g_k_rope, b_norm=v_b_norm, b_w_in=v_b_w_in,
               b_g_q_lat=v_b_g_q_lat, b_w_uq=v_b_w_uq, b_g_q_nope=v_b_g_q_nope, b_g_q_rope=v_b_g_q_rope,
               b_w_out=v_b_w_out, mem_norm=v_mem_norm, w_mem_kv=v_w_mem_kv, g_mem_q=v_g_mem_q, g_mem_k=v_g_mem_k)
    shapes = {n: wts[n].shape for n in WEIGHTS}
    S, D = x.shape[1], x.shape[2]
    xs, ms, tgt = x[0], mem[0], loss_target[0]
    sb_w, mem_w, mla_w = H_SB * DH, H_MEM * DH, H_MLA * DH
    q_lora, kv_lora = b_g_q_lat.shape[-1], g_ckv.shape[-1]

    def pieces_of(names):
        return [_to_rows(wts[n].astype(BF16).reshape(-1), 16) for n in names]

    def unpack_gathered(gathered, names):
        full, row = {}, 0
        for n in names:
            rows = _rows_of(wts[n].size, 16)
            flat = gathered[:, row:row + rows].reshape(N_DEV, -1)[:, :wts[n].size]
            full[n] = _join8(flat, dict(SHARDED)[n], wts[n].shape)
            row += rows
        return full

    second, third = ("a_w_out", "w_dkv", "w_ukv"), ("w_mem_kv", "b_w_in", "b_w_uq", "b_w_out")
    a_in_w = a_w_in.shape[-1]
    bits = jnp.pad(lax.bitcast_convert_type(a_norm.reshape(-1), BF16).reshape(1, -1), ((0, 15), (0, a_in_w - 2 * a_norm.size)))
    w_a_in = _all_gather_columns(jnp.concatenate([a_w_in[0].astype(BF16), bits], axis=0), name="gather_first")
    g_a = lax.bitcast_convert_type(w_a_in[D].reshape(N_DEV, a_in_w)[:, :2 * a_norm.size].reshape(N_DEV, -1, 2),
                                   F32).reshape(1, D)

    row2 = lambda g: g.reshape(1, -1)
    h0 = _rowwise(_f_norm, [(xs, D, 0)], [g_a], [(D, BF16)], name="a_norm_fwd", tm=512)[0]
    qkv, gathered = _mm(h0, w_a_in, "nn", b_cols=(0, 3 * sb_w), out_dtype=BF16, name="a_in_qkv",
                        ride=("gather", jnp.concatenate(pieces_of(second), axis=0)))
    full = unpack_gathered(gathered, second)
    pa = _mm(h0, w_a_in, "nn", b_cols=(3 * sb_w, N_DEV * a_in_w - 3 * sb_w), name="a_in_rest")
    sb, gathered = _sb_fwd(qkv, tq=512, tk=256, name="sb_fwd",
                           ride=("gather", jnp.concatenate(pieces_of(third), axis=0)))
    full.update(unpack_gathered(gathered, third))
    w_a_out = full["a_w_out"][0]
    w_dkv_p = jnp.pad(full["w_dkv"], ((0, 0), (0, ROPE)))
    w_ukv_f = full["w_ukv"]
    wb = full["b_w_in"][0]
    w_b_in = jnp.concatenate([wb[:, q_lora:q_lora + mla_w], wb[:, :q_lora], wb[:, q_lora + mla_w:]], axis=1)
    w_uq_p = jnp.pad(full["b_w_uq"][0].reshape(q_lora, H_MLA, MLA_QK),
                     ((0, 0), (0, 0), (0, 2 * DH - MLA_QK))).reshape(q_lora, H_MLA * 2 * DH)
    w_b_out = full["b_w_out"][0]
    w_mem = full["w_mem_kv"]

    pad128 = lambda g: jnp.pad(g.reshape(1, -1), ((0, 0), (0, DH - ROPE)))
    g_kr, g_qr = pad128(g_k_rope), pad128(b_g_q_rope[0])
    g_kv, g_b, g_c, g_kn = row2(kv_norm), row2(b_norm[0]), row2(g_ckv), row2(g_k_nope)
    g_ql, g_qn = row2(b_g_q_lat[0]), row2(b_g_q_nope[0])

    inv_freq = jnp.power(ROPE_THETA, -jnp.arange(0, ROPE, 2, dtype=F32) / ROPE)
    ang = positions[0].astype(F32)[:, None] * inv_freq
    z64 = jnp.zeros((S, DH - ROPE), F32)
    cs = jnp.concatenate([jnp.cos(ang), jnp.cos(ang), z64, -jnp.sin(ang), jnp.sin(ang), z64], axis=1)

    mn, mkv = [], []
    for l in range(2):
        mn.append(_rowwise(_f_norm, [(ms, D, 0)], [row2(mem_norm[l])], [(D, BF16)], name=f"mem_norm{l}")[0])
        mkv.append(_mm(mn[l], w_mem[l], "nn", name=f"mem_kv{l}"))
    g_mq = [row2(g_mem_q[l]) for l in range(2)]
    g_mk = [row2(g_mem_k[l]) for l in range(2)]
    mkn = [_rowwise(_f_memkv, [(mkv[l], 2 * mem_w, 0)], [g_mk[l]], [(2 * mem_w, F32)], name=f"mem_kn{l}")[0]
           for l in range(2)]

    def mix_bwd(rows, l, d_mixed, name):
        *d_rows, d_mkn, d_gq = _rowwise_bwd(_f_mix, rows, [mkn[l], g_mq[l]], [(d_mixed, d_mixed.shape[1], 0)],
                                            [0, 1, 2, 3], [0, 1], out_dtypes=[F32, BF16, BF16, BF16], name=name)
        d_mkv, d_gk = _rowwise_bwd(_f_memkv, [(mkv[l], 2 * mem_w, 0)], [g_mk[l]], [(d_mkn, 2 * mem_w, 0)], [0], [0],
                                   name=f"mem_kn_bwd{l}")
        return (*d_rows, d_mkv, d_gq, d_gk)

    mix_a_rows = [(sb, sb_w, 0), (pa, sb_w, 0), (pa, mem_w, sb_w // mem_w), (pa, mem_w, sb_w // mem_w + 1)]
    mixed_a = _rowwise(_f_mix, mix_a_rows, [mkn[0], g_mq[0]], [(sb_w + mem_w, BF16)], name="a_mix_fwd", tm=512)[0]
    x1 = _mm(mixed_a, w_a_out, "nn", add=xs, name="a_out")

    hk, hb = _rowwise(_f_norm2, [(x1, D, 0)], [g_kv, g_b], [(D, BF16), (D, BF16)], name="b_norm_fwd", tm=512)
    ckr = _mm(hk, w_dkv_p, "nn", name="kv_down")
    cn, kr = _rowwise(_f_kv1, [(ckr, kv_lora + DH, 0), (cs, 2 * DH, 0)], [g_c, g_kr],
                      [(kv_lora, BF16), (DH, F32)], name="kv1_fwd", tm=512)
    kvu = _mm(cn, w_ukv_f, "nn", name="kv_up")
    k2, v2 = _rowwise(_f_kv2, [(kvu, H_MLA * 2 * DH, 0), (kr, DH, 0)], [g_kn],
                      [(H_MLA * 2 * DH, BF16), (mla_w, BF16)], name="kv2_fwd")
    pb = _mm(hb, w_b_in, "nn", name="b_in")
    ql = _rowwise(_f_norm, [(pb, q_lora, mla_w // q_lora)], [g_ql], [(q_lora, BF16)], name="q_lat_fwd", tm=512)[0]
    qraw = _mm(ql, w_uq_p, "nn", name="q_up")
    q2 = _rowwise(_f_q2, [(qraw, H_MLA * 2 * DH, 0), (cs, 2 * DH, 0)], [g_qn, g_qr],
                  [(H_MLA * 2 * DH, BF16)], name="q2_fwd")[0]
    att, lse = _mla_fwd(q2, k2, v2, tq=1024, tk=1024, name="mla_fwd")
    cb = (mla_w + q_lora) // mem_w
    mix_b_rows = [(att, mla_w, 0), (pb, mla_w, 0), (pb, mem_w, cb), (pb, mem_w, cb + 1)]
    mixed_b = _rowwise(_f_mix, mix_b_rows, [mkn[1], g_mq[1]], [(mla_w + mem_w, BF16)], name="b_mix_fwd", tm=512)[0]
    dy, loss_part = _mm_loss(mixed_b, w_b_out, x1, tgt, name="b_out_loss")

    gr = {}
    d_mixed_b = _mm(dy, w_b_out, "nt", name="b_out_dx")
    gr["b_w_out"] = _mm(mixed_b, dy, "tn", out_dtype=BF16, name="b_out_dw")[None]
    d_att, d_gmla, d_qm_b, d_gm_b, d_mkv1, d_gq1, d_gk1 = mix_bwd(mix_b_rows, 1, d_mixed_b, "b_mix_bwd")
    dq2, dk2, dv2 = _mla_bwd(q2, k2, v2, att, d_att, lse, tq=512, tk=512, name="mla_bwd")
    d_qraw, d_gqn, d_gqr = _rowwise_bwd(
        _f_q2, [(qraw, H_MLA * 2 * DH, 0), (cs, 2 * DH, 0)], [g_qn, g_qr], [(dq2, H_MLA * 2 * DH, 0)],
        [0], [0, 1], out_dtypes=[BF16], name="q2_bwd")
    d_ql = _mm(d_qraw, w_uq_p, "nt", name="q_up_dx")
    d_wuq = _mm(ql, d_qraw, "tn", out_dtype=BF16, name="q_up_dw")
    gr["b_w_uq"] = d_wuq.reshape(q_lora, H_MLA, 2 * DH)[:, :, :MLA_QK].reshape(1, q_lora, H_MLA * MLA_QK)
    d_qlat, d_gql = _rowwise_bwd(_f_norm, [(pb, q_lora, mla_w // q_lora)], [g_ql], [(d_ql, q_lora, 0)],
                                 [0], [0], out_dtypes=[BF16], name="q_lat_bwd", tm=512)
    d_pb = jnp.concatenate([d_gmla, d_qlat, d_qm_b, d_gm_b], axis=1)
    d_hb = _mm(d_pb, w_b_in, "nt", name="b_in_dx")
    d_wbin = _mm(hb, d_pb, "tn", out_dtype=BF16, name="b_in_dw")
    gr["b_w_in"] = jnp.concatenate([d_wbin[:, mla_w:mla_w + q_lora], d_wbin[:, :mla_w], d_wbin[:, mla_w + q_lora:]],
                                   axis=1)[None]
    d_kvu, d_kr, d_gkn = _rowwise_bwd(
        _f_kv2, [(kvu, H_MLA * 2 * DH, 0), (kr, DH, 0)], [g_kn], [(dk2, H_MLA * 2 * DH, 0), (dv2, mla_w, 0)],
        [0, 1], [0], out_dtypes=[BF16, F32], name="kv2_bwd")
    d_cn = _mm(d_kvu, w_ukv_f, "nt", name="kv_up_dx")
    gr["w_ukv"] = _mm(cn, d_kvu, "tn", out_dtype=BF16, name="kv_up_dw")
    d_ckr, d_gc, d_gkr = _rowwise_bwd(
        _f_kv1, [(ckr, kv_lora + DH, 0), (cs, 2 * DH, 0)], [g_c, g_kr], [(d_cn, kv_lora, 0), (d_kr, DH, 0)],
        [0], [0, 1], out_dtypes=[BF16], name="kv1_bwd", tm=512)
    d_hk = _mm(d_ckr, w_dkv_p, "nt", name="kv_down_dx")
    gr["w_dkv"] = _mm(hk, d_ckr, "tn", out_dtype=BF16, name="kv_down_dw")[:, :kv_lora + ROPE]
    d_x1, d_gkv, d_gb = _rowwise_bwd(_f_norm2, [(x1, D, 0)], [g_kv, g_b], [(d_hk, D, 0), (d_hb, D, 0)],
                                     [0], [0, 1], add=(dy, D, 0), name="b_norm_bwd")
    d_mixed_a = _mm(d_x1, w_a_out, "nt", name="a_out_dx")
    gr["a_w_out"] = _mm(mixed_a, d_x1, "tn", out_dtype=BF16, name="a_out_dw")[None]
    d_sb, d_gsb, d_qm_a, d_gm_a, d_mkv0, d_gq0, d_gk0 = mix_bwd(mix_a_rows, 0, d_mixed_a, "a_mix_bwd")
    d_wmem, d_mnorm = [], []
    for l, d_mkv in enumerate((d_mkv0, d_mkv1)):
        d_mn = _mm(d_mkv, w_mem[l], "nt", name=f"mem_kv_dx{l}")
        d_wmem.append(_mm(mn[l], d_mkv, "tn", out_dtype=BF16, name=f"mem_kv_dw{l}"))
        d_mnorm.append(_rowwise_bwd(_f_norm, [(ms, D, 0)], [row2(mem_norm[l])], [(d_mn, D, 0)], [], [0],
                                    name=f"mem_norm_bwd{l}")[0])
    gr["w_mem_kv"] = jnp.stack(d_wmem)

    mid = tuple(n for n, _ in SHARDED[2:])
    send_mid = _stack_rows([_to_rows(_split8(gr[n], ax), 16) for n, ax in SHARDED[2:]])
    dq, dk, dv, recv_mid = _sb_bwd(qkv, sb, d_sb, tq=512, tk=256, name="sb_bwd", ride=("a2a", send_mid))
    d_pa = jnp.concatenate([dq, dk, dv, d_gsb, d_qm_a, d_gm_a], axis=1)
    send_ain = _mm(h0, d_pa, "tn", out_dtype=BF16, out_split=N_DEV, name="a_in_dw")
    d_h0, recv_ain = _mm(d_pa, w_a_in, "nt", b_rows=D, name="a_in_dx", ride=("a2a", send_ain))
    grad_x, d_ga = _rowwise_bwd(_f_norm, [(xs, D, 0)], [g_a], [(d_h0, D, 0)], [0], [0], add=(d_x1, D, 0),
                                name="a_norm_bwd")
    gr["mem_norm"] = jnp.concatenate(d_mnorm, axis=0)
    gr["g_mem_q"] = jnp.concatenate([d_gq0, d_gq1], axis=0)
    gr["g_mem_k"] = jnp.concatenate([d_gk0, d_gk1], axis=0)
    gr["kv_norm"], gr["b_norm"], gr["g_ckv"], gr["g_k_nope"] = d_gkv, d_gb, d_gc, d_gkn
    gr["g_k_rope"], gr["b_g_q_rope"] = d_gkr[:, :ROPE], d_gqr[:, :ROPE]
    gr["b_g_q_lat"], gr["b_g_q_nope"] = d_gql, d_gqn
    last = ("a_norm",) + SMALL
    shared = jnp.concatenate([gr[n].reshape(-1) for n in SMALL] + [loss_part[0, :1]])
    flat = jnp.concatenate([_split8(d_ga, 1), jnp.broadcast_to(shared[None], (N_DEV, shared.size))], axis=1)
    recv_last = _all_to_all(_to_rows(flat, ROW_MULT), name="exchange_small")

    out = [{}, {}, {}, {}]
    slabs = _reduce_adamw(recv_mid, *[_pack_local(t, mid, 16) for t in (wts, mom, var)], name="reduce_adamw_mid")
    for o, slab in zip(out, slabs):
        o.update(_unpack_local(slab, mid, shapes, 16)[0])
    zero = jnp.zeros((1,), F32)
    slabs = _reduce_adamw(recv_last, *[_to_rows(jnp.concatenate([t[n].reshape(-1) for n in last] + [zero]), ROW_MULT)
                                       for t in (wts, mom, var)], name="reduce_adamw_last")
    for o, slab in zip(out, slabs):
        vec, at = slab.reshape(-1), 0
        for n in last:
            o[n] = vec[at:at + wts[n].size].reshape(shapes[n])
            at += wts[n].size
    loss = slabs[0].reshape(-1)[at]
    slabs = _reduce_adamw(recv_ain, *[t["a_w_in"][0] for t in (wts, mom, var)], name="reduce_adamw_ain")
    for o, slab in zip(out, slabs):
        o["a_w_in"] = slab[None]
    return (loss, grad_x[None], *[o[n] for o in out for n in WEIGHTS])
```

```python
import functools

import jax
import jax.numpy as jnp
from jax import lax
from jax.experimental import pallas as pl
from jax.experimental.pallas import tpu as pltpu

F32, BF16 = jnp.float32, jnp.bfloat16

N_DEV = 8
DH = 128
H_SB, H_MEM, H_MLA = 12, 4, 12
ROPE = 64
MLA_QK = DH + ROPE
EPS = 1e-6
ROPE_THETA = 10000.0
ADAM_LR, ADAM_B1, ADAM_B2, ADAM_EPS, ADAM_WD, ADAM_STEP = 0.001, 0.9, 0.999, 1e-08, 0.01, 10

LANES = 1024
VMEM_LIMIT = 48 * 1024 * 1024
VMEM_LIMIT_BIG = 56 * 1024 * 1024

NN = (((1,), (0,)), ((), ()))
NT = (((1,), (1,)), ((), ()))
TN = (((0,), (0,)), ((), ()))
_DIMS = {"nn": NN, "nt": NT, "tn": TN}


def _dot16(a, b, dims):
    return lax.dot_general(a.astype(BF16), b.astype(BF16), _DIMS[dims], preferred_element_type=F32)


@functools.partial(jax.custom_vjp, nondiff_argnums=(2,))
def _bdot(a, b, dims):
    return _dot16(a, b, dims)


def _bdot_fwd(a, b, dims):
    return _dot16(a, b, dims), (a, b)


def _bdot_bwd(dims, res, g):
    a, b = res
    if dims == "nn":
        return _dot16(g, b, "nt"), _dot16(a, g, "tn")
    return _dot16(g, b, "nn"), _dot16(g, a, "tn")


_bdot.defvjp(_bdot_fwd, _bdot_bwd)


def _tile(n, pref):
    if n <= pref:
        return n
    t = (pref // 128) * 128
    while n % t:
        t -= 128
    return t


def _mesh_pos():
    return lax.axis_index("x"), lax.axis_index("y"), lax.axis_index("c")


def _ride_shape(kind, src):
    return jax.ShapeDtypeStruct((N_DEV, *src.shape) if kind == "gather" else src.shape, src.dtype)


def _ride_scratch():
    return [pltpu.SemaphoreType.DMA((N_DEV - 1,)), pltpu.SemaphoreType.DMA((N_DEV - 1,)), pltpu.SemaphoreType.DMA]


def _ride(kind, phase, s_ref, r_ref, send_sems, recv_sems, local_sem):
    x, y, c = _mesh_pos()
    me = 4 * x + 2 * y + c
    src = (lambda lin: s_ref) if kind == "gather" else (lambda lin: s_ref.at[lin])
    local = pltpu.make_async_copy(src(me), r_ref.at[me], local_sem)
    if phase == "start":
        local.start()
    for k in range(1, N_DEV):
        p = (1 - x if k & 4 else x, 1 - y if k & 2 else y, 1 - c if k & 1 else c)
        lin = 4 * p[0] + 2 * p[1] + p[2]
        cp = pltpu.make_async_remote_copy(
            src_ref=src(lin), dst_ref=r_ref.at[me] if phase == "start" else r_ref.at[lin],
            send_sem=send_sems.at[k - 1], recv_sem=recv_sems.at[k - 1],
            device_id=p, device_id_type=pl.DeviceIdType.MESH)
        if phase == "start":
            cp.start()
        else:
            cp.wait_recv()
            cp.wait_send()
    if phase == "wait":
        local.wait()


def _mm(a, b, dims, *, name, out_dtype=F32, add=None, ride=None, out_split=None, b_rows=None, b_cols=None,
        tm=1024, tn=1024, tk=2048):
    col0 = 0
    if dims == "tn":
        (K, M), (_, N) = a.shape, b.shape
    elif dims == "nt":
        (M, K), N = a.shape, b_rows or b.shape[0]
    else:
        (M, K), N = a.shape, b.shape[1]
        if b_cols is not None:
            col0, N = b_cols
    tm, tk = _tile(M, tm), _tile(K, tk)
    tn = N // out_split if out_split else _tile(N, tn)
    while col0 % tn or N % tn:
        tn -= 128
    jb = col0 // tn
    ni, nj, nk = M // tm, N // tn, K // tk
    n_in = 2 + (add is not None) + (ride is not None)

    def body(*refs):
        a_ref, b_ref, o_ref = refs[0], refs[1], refs[n_in]
        acc_ref = refs[n_in + 1 + (ride is not None)]
        i, j, k = pl.program_id(0), pl.program_id(1), pl.program_id(2)
        if ride is not None:
            ride_refs = (refs[n_in - 1], refs[n_in + 1], *refs[-3:])

            @pl.when((i == 0) & (j == 0) & (k == 0))
            def _():
                _ride(ride[0], "start", *ride_refs)

        if nk == 1:
            r = _dot16(a_ref[...], b_ref[...], dims)
            o_ref[...] = (r if add is None else r + refs[2][...]).astype(o_ref.dtype)
        else:
            @pl.when(k == 0)
            def _():
                acc_ref[...] = jnp.zeros_like(acc_ref)

            acc_ref[...] += _dot16(a_ref[...], b_ref[...], dims)

            @pl.when(k == nk - 1)
            def _():
                r = acc_ref[...]
                if add is not None:
                    r = r + refs[2][...]
                o_ref[...] = r.astype(o_ref.dtype)

        if ride is not None:
            @pl.when((i == ni - 1) & (j == nj - 1) & (k == nk - 1))
            def _():
                _ride(ride[0], "wait", *ride_refs)

    a_spec = pl.BlockSpec((tk, tm), lambda i, j, k: (k, i)) if dims == "tn" else pl.BlockSpec((tm, tk), lambda i, j, k: (i, k))
    b_spec = pl.BlockSpec((tn, tk), lambda i, j, k: (j, k)) if dims == "nt" else pl.BlockSpec((tk, tn), lambda i, j, k: (k, j + jb))
    o_spec = pl.BlockSpec((tm, tn), lambda i, j, k: (i, j))
    in_specs, args = [a_spec, b_spec], [a, b]
    if add is not None:
        in_specs.append(o_spec)
        args.append(add)
    out_specs, out_shape = [o_spec], [jax.ShapeDtypeStruct((M, N), out_dtype)]
    scratch = [pltpu.VMEM((tm, tn) if nk > 1 else (8, 128), F32)]
    if out_split:
        out_specs = [pl.BlockSpec((None, tm, tn), lambda i, j, k: (j, i, 0))]
        out_shape = [jax.ShapeDtypeStruct((out_split, M, tn), out_dtype)]
    if ride is not None:
        in_specs.append(pl.BlockSpec(memory_space=pl.ANY))
        args.append(ride[1])
        out_specs.append(pl.BlockSpec(memory_space=pl.ANY))
        out_shape.append(_ride_shape(*ride))
        scratch += _ride_scratch()
    sem = ("arbitrary",) * 3 if ride is not None else ("parallel", "parallel", "arbitrary")
    res = pl.pallas_call(
        body, name=name, grid=(ni, nj, nk), in_specs=in_specs, out_specs=out_specs, out_shape=out_shape,
        scratch_shapes=scratch,
        compiler_params=pltpu.CompilerParams(dimension_semantics=sem, vmem_limit_bytes=VMEM_LIMIT),
    )(*args)
    return res[0] if ride is None else res


def _mm_loss(a, b, res, tgt, *, name, tm=1024, tn=1024):
    (M, K), N = a.shape, b.shape[1]
    tm, tn = _tile(M, tm), _tile(N, tn)

    def body(a_ref, b_ref, r_ref, t_ref, dy_ref, loss_ref):
        @pl.when((pl.program_id(0) == 0) & (pl.program_id(1) == 0))
        def _():
            loss_ref[...] = jnp.zeros_like(loss_ref)

        err = _dot16(a_ref[...], b_ref[...], "nn") + r_ref[...] - t_ref[...]
        dy_ref[...] = err * (1.0 / N)
        loss_ref[...] += 0.5 * jnp.sum(err * err) * (1.0 / N)

    tile = pl.BlockSpec((tm, tn), lambda i, j: (i, j))
    return pl.pallas_call(
        body, name=name, grid=(M // tm, N // tn),
        in_specs=[pl.BlockSpec((tm, K), lambda i, j: (i, 0)), pl.BlockSpec((K, tn), lambda i, j: (0, j)), tile, tile],
        out_specs=[tile, pl.BlockSpec((1, DH), lambda i, j: (0, 0))],
        out_shape=[jax.ShapeDtypeStruct((M, N), F32), jax.ShapeDtypeStruct((1, DH), F32)],
        compiler_params=pltpu.CompilerParams(dimension_semantics=("arbitrary", "arbitrary"), vmem_limit_bytes=VMEM_LIMIT),
    )(a, b, res, tgt)


def _rowwise(fn, rows, consts, outs, accs=(), *, name, tm=256):
    S = rows[0][0].shape[0]
    tm = min(tm, S)
    nr, nc, no = len(rows), len(consts), len(outs)

    def body(*refs):
        res = fn(*[r[...] for r in refs[:nr + nc]])
        res = tuple(res) if isinstance(res, (tuple, list)) else (res,)
        orefs, arefs = refs[nr + nc:nr + nc + no], refs[nr + nc + no:]
        for r, v in zip(orefs, res[:no]):
            r[...] = v.astype(r.dtype)
        if arefs:
            @pl.when(pl.program_id(0) == 0)
            def _():
                for r in arefs:
                    r[...] = jnp.zeros_like(r)

            for r, v in zip(arefs, res[no:]):
                r[...] += v

    in_specs = [pl.BlockSpec((tm, w), lambda i, cb=cb: (i, cb)) for (_, w, cb) in rows]
    in_specs += [pl.BlockSpec(c.shape, lambda i: (0, 0)) for c in consts]
    out_specs = [pl.BlockSpec((tm, w), lambda i: (i, 0)) for (w, _) in outs]
    out_specs += [pl.BlockSpec(s, lambda i: (0, 0)) for s in accs]
    out_shape = [jax.ShapeDtypeStruct((S, w), dt) for (w, dt) in outs]
    out_shape += [jax.ShapeDtypeStruct(s, F32) for s in accs]
    res = pl.pallas_call(
        body, name=name, grid=(S // tm,), in_specs=in_specs, out_specs=out_specs, out_shape=out_shape,
        compiler_params=pltpu.CompilerParams(dimension_semantics=("arbitrary",), vmem_limit_bytes=VMEM_LIMIT),
    )(*[r[0] for r in rows], *consts)
    return res


def _rowwise_bwd(f, rows, consts, cts, row_grads, const_grads, *, name, add=None, out_dtypes=None, tm=256):
    nr, nc, nct = len(rows), len(consts), len(cts)
    all_rows = list(rows) + list(cts) + ([add] if add is not None else [])

    def fn(*args):
        nrow = len(all_rows)
        prim = [x.astype(F32) for x in args[:nr]] + [x.astype(F32) for x in args[nrow:]]
        ct = tuple(x.astype(F32) for x in args[nr:nr + nct])
        out, vjp = jax.vjp(f, *prim)
        gs = vjp(ct if isinstance(out, (tuple, list)) else ct[0])
        res = [gs[k] for k in row_grads]
        if add is not None:
            res[0] = res[0] + args[nrow - 1]
        return tuple(res) + tuple(gs[nr + k] for k in const_grads)

    out_dtypes = out_dtypes or [F32] * len(row_grads)
    outs = [(rows[k][1], dt) for k, dt in zip(row_grads, out_dtypes)]
    accs = [consts[k].shape for k in const_grads]
    return _rowwise(fn, all_rows, consts, outs, accs, name=name, tm=tm)


def _rms(x, g, n=None):
    ms = jnp.sum(x * x, axis=-1, keepdims=True) * (1.0 / (n or x.shape[-1]))
    return x * lax.rsqrt(ms + EPS) * g


def _sigmoid(x):
    return 1.0 / (1.0 + jnp.exp(-x))


def _swap_halves_exact(x):
    r = lax.broadcasted_iota(jnp.int32, (DH, DH), 0)
    c = lax.broadcasted_iota(jnp.int32, (DH, DH), 1)
    half = ROPE // 2
    perm = (((r < half) & (c == r + half)) | ((r >= half) & (r < ROPE) & (c == r - half))).astype(BF16)
    hi = x.astype(BF16)
    rest = x - hi.astype(F32)
    mid = rest.astype(BF16)
    lo = (rest - mid.astype(F32)).astype(BF16)
    return sum(lax.dot_general(p, perm, NN, preferred_element_type=F32) for p in (hi, mid, lo))


_swap_halves = jax.custom_vjp(_swap_halves_exact)
_swap_halves.defvjp(lambda x: (_swap_halves_exact(x), None), lambda _, g: (_swap_halves_exact(g),))


def _rope128(x, g128, cs):
    y = _rms(x, g128, n=ROPE)
    return y * cs[:, :DH] + _swap_halves(y) * cs[:, DH:]


def _f_norm(x, g):
    return _rms(x, g)


def _f_norm2(x, g1, g2):
    xn = x * lax.rsqrt(jnp.mean(x * x, axis=-1, keepdims=True) + EPS)
    return xn * g1, xn * g2


def _f_kv1(ckr, cs, g_ckv, g_kr):
    w = g_ckv.shape[-1]
    return _rms(ckr[:, :w], g_ckv), _rope128(ckr[:, w:], g_kr, cs)


def _f_kv2(kv, kr, g_kn):
    ks, vs = [], []
    for h in range(H_MLA):
        ks += [_rms(kv[:, 2 * DH * h:2 * DH * h + DH], g_kn), kr]
        vs.append(kv[:, 2 * DH * h + DH:2 * DH * (h + 1)])
    return jnp.concatenate(ks, axis=1), jnp.concatenate(vs, axis=1)


def _f_q2(q, cs, g_n, g_r):
    out = []
    for h in range(H_MLA):
        out += [_rms(q[:, 2 * DH * h:2 * DH * h + DH], g_n), _rope128(q[:, 2 * DH * h + DH:2 * DH * (h + 1)], g_r, cs)]
    return jnp.concatenate(out, axis=1)


def _f_mix(att, g_att, q_m, g_m, mkv, g_q, g_k):
    mem_w = H_MEM * DH
    heads = []
    for h in range(H_MEM):
        kh = _rms(mkv[:, h * DH:(h + 1) * DH], g_k)
        vh = mkv[:, mem_w + h * DH:mem_w + (h + 1) * DH]
        qh = _rms(q_m[:, h * DH:(h + 1) * DH], g_q)
        s = _bdot(qh, kh, "nt") * (DH ** -0.5)
        p = jnp.exp(s - lax.stop_gradient(jnp.max(s, axis=-1, keepdims=True)))
        p = p / jnp.sum(p, axis=-1, keepdims=True)
        heads.append(_bdot(p, vh, "nn"))
    mo = jnp.concatenate(heads, axis=1)
    return jnp.concatenate([att * (g_att * _sigmoid(g_att)), mo * (g_m * _sigmoid(g_m))], axis=1)


def _split_dot(x, u):
    hi = x.astype(BF16)
    lo = (x - hi.astype(F32)).astype(BF16)
    return (lax.dot_general(hi, u, NN, preferred_element_type=F32)
            + lax.dot_general(lo, u, NN, preferred_element_type=F32))


def _tri(t):
    r = lax.broadcasted_iota(jnp.int32, (t, t), 0)
    c = lax.broadcasted_iota(jnp.int32, (t, t), 1)
    return r, c


def _strict_lower(t):
    r, c = _tri(t)
    return (r > c).astype(BF16)


def _head_blocks_t(x, w, tk, *, name):
    S = x.shape[0]
    H = x.shape[1] // w

    def body(x_ref, o_ref):
        for h in range(H):
            o_ref[h] = x_ref[:, h * w:(h + 1) * w].T

    return pl.pallas_call(
        body, name=name, grid=(S // tk,),
        in_specs=[pl.BlockSpec((tk, H * w), lambda i: (i, 0))],
        out_specs=pl.BlockSpec((H, None, w, tk), lambda i: (0, i, 0, 0)),
        out_shape=jax.ShapeDtypeStruct((H, S // tk, w, tk), x.dtype),
        compiler_params=pltpu.CompilerParams(dimension_semantics=("parallel",), vmem_limit_bytes=VMEM_LIMIT),
    )(x)


def _rows_ahead(tq, tk):
    return lax.broadcasted_iota(jnp.int32, (tq, tk), 0) - lax.broadcasted_iota(jnp.int32, (tq, tk), 1)


EXP_UNDERFLOW = -110.0


def _log_one_minus_beta(zr, scale):
    zs, nz = zr * scale, zr * (-scale)
    return zs, jnp.minimum(nz, 0.0) - jnp.log(1.0 + jnp.exp(jnp.minimum(zs, nz)))


def _sb_fwd(qkv, *, tq, tk, name, ride=None):
    S = qkv.shape[0]
    tq, tk = min(tq, S), min(tk, tq, S)
    nd = tq // tk
    H = H_SB
    scale = DH ** -0.5

    nq = S // tq

    def body(q_ref, k_ref, v_ref, *rest):
        o_ref = rest[1] if ride is not None else rest[0]
        h, i = pl.program_id(0), pl.program_id(1)
        if ride is not None:
            ride_refs = (rest[0], *rest[2:])

            @pl.when((h == 0) & (i == 0))
            def _():
                _ride(ride[0], "start", *ride_refs)

        q = q_ref[...]
        u = _strict_lower(tk)
        ahead = _rows_ahead(tq, tk)

        def block(j, acc, cb, keep):
            off = pl.multiple_of(j * tk, tk)
            k = k_ref[pl.ds(off, tk), :]
            v = v_ref[pl.ds(off, tk), :]
            z, l = _log_one_minus_beta(lax.dot_general(q, k, NT, preferred_element_type=F32), scale)
            if keep is not None:
                l = jnp.where(keep, l, 0.0)
            a = jnp.exp((z + l) + (_split_dot(l, u) + cb))
            if keep is not None:
                a = jnp.where(keep, a, 0.0)
            acc = acc + lax.dot_general(a.astype(BF16), v, NN, preferred_element_type=F32)
            return acc, cb + jnp.sum(l, axis=1, keepdims=True)

        carry = (jnp.zeros((tq, DH), F32), jnp.zeros((tq, 1), F32))
        for t in reversed(range(nd)):
            carry = block(i * nd + t, *carry, ahead > t * tk)
        _, acc, _ = lax.while_loop(
            lambda st: (st[0] < i * nd) & (jnp.max(st[2]) > EXP_UNDERFLOW),
            lambda st: (st[0] + 1, *block(i * nd - 1 - st[0], st[1], st[2], None)), (jnp.int32(0), *carry))
        o_ref[...] = acc

        if ride is not None:
            @pl.when((h == H - 1) & (i == nq - 1))
            def _():
                _ride(ride[0], "wait", *ride_refs)

    in_specs = [pl.BlockSpec((tq, DH), lambda h, i: (i, h)),
                pl.BlockSpec((S, DH), lambda h, i: (0, H + h)),
                pl.BlockSpec((S, DH), lambda h, i: (0, 2 * H + h))]
    out_specs = [pl.BlockSpec((tq, DH), lambda h, i: (i, h))]
    out_shape = [jax.ShapeDtypeStruct((S, H * DH), F32)]
    args, scratch = [qkv, qkv, qkv], []
    if ride is not None:
        in_specs.append(pl.BlockSpec(memory_space=pl.ANY))
        args.append(ride[1])
        out_specs.append(pl.BlockSpec(memory_space=pl.ANY))
        out_shape.append(_ride_shape(*ride))
        scratch = _ride_scratch()
    res = pl.pallas_call(
        body, name=name, grid=(H, nq), in_specs=in_specs, out_specs=out_specs, out_shape=out_shape,
        scratch_shapes=scratch,
        compiler_params=pltpu.CompilerParams(dimension_semantics=("arbitrary", "arbitrary"), vmem_limit_bytes=VMEM_LIMIT),
    )(*args)
    return res[0] if ride is None else res


def _sb_bwd(qkv, o, do, *, tq, tk, name, ride=None):
    S = qkv.shape[0]
    tq, tk = min(tq, S), min(tk, tq, S)
    nd = tq // tk
    H = H_SB
    scale = DH ** -0.5

    nq = S // tq

    def body(q_ref, k_ref, v_ref, o_ref, do_ref, *rest):
        if ride is not None:
            dq_ref, dk_out, dv_out, dk_ref, dv_ref = rest[1], rest[2], rest[3], rest[5], rest[6]
            ride_refs = (rest[0], rest[4], *rest[7:])
        else:
            dq_ref, dk_out, dv_out, dk_ref, dv_ref = rest
        h, i = pl.program_id(0), pl.program_id(1)
        if ride is not None:
            @pl.when((h == 0) & (i == 0))
            def _():
                _ride(ride[0], "start", *ride_refs)

        @pl.when(i == 0)
        def _():
            dk_ref[...] = jnp.zeros_like(dk_ref)
            dv_ref[...] = jnp.zeros_like(dv_ref)

        q = q_ref[...]
        do = do_ref[...]
        do16 = do.astype(BF16)
        dsum = jnp.sum(do16.astype(F32) * o_ref[...], axis=1, keepdims=True)
        u = _strict_lower(tk)
        ahead = _rows_ahead(tq, tk)

        def block(j, dq, cb, ce, keep):
            off = pl.multiple_of(j * tk, tk)
            k = k_ref[pl.ds(off, tk), :]
            v = v_ref[pl.ds(off, tk), :]
            z, l = _log_one_minus_beta(lax.dot_general(q, k, NT, preferred_element_type=F32), scale)
            if keep is not None:
                l = jnp.where(keep, l, 0.0)
            log_beta = z + l
            a = jnp.exp(log_beta + (_split_dot(l, u) + cb))
            if keep is not None:
                a = jnp.where(keep, a, 0.0)
            a16 = a.astype(BF16)
            e = a16.astype(F32) * lax.dot_general(do16, v, NT, preferred_element_type=F32)
            left = dsum - (ce + _split_dot(e, u) + e)
            dz = (e - jnp.exp(log_beta) * (e + left)) * scale
            if keep is not None:
                dz = jnp.where(keep, dz, 0.0)
            dz = dz.astype(BF16)
            dq = dq + lax.dot_general(dz, k, NN, preferred_element_type=F32)
            dk_ref[pl.ds(off, tk), :] += lax.dot_general(dz, q, TN, preferred_element_type=F32)
            dv_ref[pl.ds(off, tk), :] += lax.dot_general(a16, do16, TN, preferred_element_type=F32)
            return dq, cb + jnp.sum(l, axis=1, keepdims=True), ce + jnp.sum(e, axis=1, keepdims=True)

        zero = jnp.zeros((tq, 1), F32)
        carry = (jnp.zeros((tq, DH), F32), zero, zero)
        for t in reversed(range(nd)):
            carry = block(i * nd + t, *carry, ahead > t * tk)
        _, dq, _, _ = lax.while_loop(
            lambda st: (st[0] < i * nd) & (jnp.max(st[2]) > EXP_UNDERFLOW),
            lambda st: (st[0] + 1, *block(i * nd - 1 - st[0], st[1], st[2], st[3], None)), (jnp.int32(0), *carry))
        dq_ref[...] = dq.astype(dq_ref.dtype)

        @pl.when(i == nq - 1)
        def _():
            dk_out[...] = dk_ref[...].astype(dk_out.dtype)
            dv_out[...] = dv_ref[...].astype(dv_out.dtype)

        if ride is not None:
            @pl.when((h == H - 1) & (i == nq - 1))
            def _():
                _ride(ride[0], "wait", *ride_refs)

    blk = pl.BlockSpec((tq, DH), lambda h, i: (i, h))
    whole = pl.BlockSpec((S, DH), lambda h, i: (0, h))
    shp = jax.ShapeDtypeStruct((S, H * DH), BF16)
    in_specs = [blk, pl.BlockSpec((S, DH), lambda h, i: (0, H + h)), pl.BlockSpec((S, DH), lambda h, i: (0, 2 * H + h)),
                blk, blk]
    out_specs, out_shape = [blk, whole, whole], [shp, shp, shp]
    args, scratch = [qkv, qkv, qkv, o, do], [pltpu.VMEM((S, DH), F32), pltpu.VMEM((S, DH), F32)]
    if ride is not None:
        in_specs.append(pl.BlockSpec(memory_space=pl.ANY))
        args.append(ride[1])
        out_specs.append(pl.BlockSpec(memory_space=pl.ANY))
        out_shape.append(_ride_shape(*ride))
        scratch += _ride_scratch()
    return pl.pallas_call(
        body, name=name, grid=(H, nq), in_specs=in_specs, out_specs=out_specs, out_shape=out_shape,
        scratch_shapes=scratch,
        compiler_params=pltpu.CompilerParams(dimension_semantics=("arbitrary", "arbitrary"), vmem_limit_bytes=VMEM_LIMIT),
    )(*args)


def _mla_fwd(q, k, v, *, tq, tk, name):
    S = q.shape[0]
    tq, tk = min(tq, S), min(tk, tq, S)
    nd, nb = tq // tk, S // tk
    H = H_MLA
    scale = MLA_QK ** -0.5
    vt = _head_blocks_t(v, DH, tk, name=name + "_vt")

    def body(q_ref, k_ref, vt_ref, o_ref, lse_ref):
        i = pl.program_id(1)
        qb = q_ref[...]
        behind = lax.broadcasted_iota(jnp.int32, (tk, tq), 1) - lax.broadcasted_iota(jnp.int32, (tk, tq), 0)

        def block(j, m, den, acct, keep):
            off = pl.multiple_of(j * tk, tk)
            st = lax.dot_general(k_ref[pl.ds(off, tk), :], qb, NT, preferred_element_type=F32) * scale
            if keep is not None:
                st = jnp.where(keep, st, -1e30)
            m_new = jnp.maximum(m, jnp.max(st, axis=0, keepdims=True))
            pt = jnp.exp(st - m_new)
            alpha = jnp.exp(m - m_new)
            den = alpha * den + jnp.sum(pt, axis=0, keepdims=True)
            acct = alpha * acct + lax.dot_general(vt_ref[j], pt.astype(BF16), NN, preferred_element_type=F32)
            return m_new, den, acct

        init = (jnp.full((1, tq), -1e30, F32), jnp.zeros((1, tq), F32), jnp.zeros((DH, tq), F32))
        carry = lax.fori_loop(0, i * nd, lambda j, carry: block(j, *carry, None), init)
        for t in range(nd):
            carry = block(i * nd + t, *carry, behind >= t * tk)
        m, den, acct = carry
        o_ref[...] = (acct / den).T
        lse_ref[0] = m + jnp.log(den)

    return pl.pallas_call(
        body, name=name, grid=(H, S // tq),
        in_specs=[pl.BlockSpec((tq, 2 * DH), lambda h, i: (i, h)),
                  pl.BlockSpec((S, 2 * DH), lambda h, i: (0, h)),
                  pl.BlockSpec((None, nb, DH, tk), lambda h, i: (h, 0, 0, 0))],
        out_specs=[pl.BlockSpec((tq, DH), lambda h, i: (i, h)), pl.BlockSpec((1, 1, tq), lambda h, i: (h, 0, i))],
        out_shape=[jax.ShapeDtypeStruct((S, H * DH), F32), jax.ShapeDtypeStruct((H, 1, S), F32)],
        compiler_params=pltpu.CompilerParams(dimension_semantics=("arbitrary", "arbitrary"), vmem_limit_bytes=VMEM_LIMIT),
    )(q, k, vt)


def _mla_bwd(q, k, v, o, do, lse, *, tq, tk, name):
    S = q.shape[0]
    tq, tk = min(tq, S), min(tk, tq, S)
    nd, nb = tq // tk, S // tk
    H = H_MLA
    scale = MLA_QK ** -0.5
    kt = _head_blocks_t(k, 2 * DH, tk, name=name + "_kt")

    def body(q_ref, k_ref, kt_ref, v_ref, o_ref, do_ref, lse_ref, dq_ref, dk_ref, dv_ref):
        i = pl.program_id(1)

        @pl.when(i == 0)
        def _():
            dk_ref[...] = jnp.zeros_like(dk_ref)
            dv_ref[...] = jnp.zeros_like(dv_ref)

        qb = q_ref[...]
        do = do_ref[...]
        do16 = do.astype(BF16)
        dsum = jnp.sum((do * o_ref[...]).T, axis=0, keepdims=True)
        lse = lse_ref[0]
        behind = lax.broadcasted_iota(jnp.int32, (tk, tq), 1) - lax.broadcasted_iota(jnp.int32, (tk, tq), 0)

        def block(j, dqt, keep):
            off = pl.multiple_of(j * tk, tk)
            kb = k_ref[pl.ds(off, tk), :]
            vb = v_ref[pl.ds(off, tk), :]
            st = lax.dot_general(kb, qb, NT, preferred_element_type=F32) * scale
            if keep is not None:
                st = jnp.where(keep, st, -1e30)
            pt = jnp.exp(st - lse)
            dpt = lax.dot_general(vb, do16, NT, preferred_element_type=F32)
            dst = (pt * (dpt - dsum) * scale).astype(BF16)
            dk_ref[pl.ds(off, tk), :] += lax.dot_general(dst, qb, NN, preferred_element_type=F32)
            dv_ref[pl.ds(off, tk), :] += lax.dot_general(pt.astype(BF16), do16, NN, preferred_element_type=F32)
            return dqt + lax.dot_general(kt_ref[j], dst, NN, preferred_element_type=F32)

        dqt = lax.fori_loop(0, i * nd, lambda j, dqt: block(j, dqt, None), jnp.zeros((2 * DH, tq), F32))
        for t in range(nd):
            dqt = block(i * nd + t, dqt, behind >= t * tk)
        dq_ref[...] = dqt.T

    blk = pl.BlockSpec((tq, DH), lambda h, i: (i, h))
    blk2 = pl.BlockSpec((tq, 2 * DH), lambda h, i: (i, h))
    return pl.pallas_call(
        body, name=name, grid=(H, S // tq),
        in_specs=[blk2, pl.BlockSpec((S, 2 * DH), lambda h, i: (0, h)),
                  pl.BlockSpec((None, nb, 2 * DH, tk), lambda h, i: (h, 0, 0, 0)),
                  pl.BlockSpec((S, DH), lambda h, i: (0, h)),
                  blk, blk, pl.BlockSpec((1, 1, tq), lambda h, i: (h, 0, i))],
        out_specs=[blk2, pl.BlockSpec((S, 2 * DH), lambda h, i: (0, h)), pl.BlockSpec((S, DH), lambda h, i: (0, h))],
        out_shape=[jax.ShapeDtypeStruct((S, H * 2 * DH), F32), jax.ShapeDtypeStruct((S, H * 2 * DH), F32),
                   jax.ShapeDtypeStruct((S, H * DH), F32)],
        compiler_params=pltpu.CompilerParams(dimension_semantics=("arbitrary", "arbitrary"), vmem_limit_bytes=VMEM_LIMIT_BIG),
    )(q, k, kt, v, o, do, lse)


def _all_gather_columns(block, *, name):
    R, C = block.shape

    def body(x_ref, out_ref, send_sems, recv_sems, local_sem):
        x, y, c = _mesh_pos()
        me, sibling = (x, y, c), (x, y, 1 - c)
        chips = [(1 - x, y), (x, 1 - y), (1 - x, 1 - y)]

        def slot(px, py, pc):
            return out_ref.at[:, pl.ds(pl.multiple_of((4 * px + 2 * py + pc) * C, 128), C)]

        def copy(k, blk, to, src=None):
            return pltpu.make_async_remote_copy(
                src_ref=slot(*blk) if src is None else src, dst_ref=slot(*blk),
                send_sem=send_sems.at[k], recv_sem=recv_sems.at[k],
                device_id=to, device_id_type=pl.DeviceIdType.MESH)

        mine = pltpu.make_async_copy(x_ref, slot(*me), local_sem)
        mine.start()
        first = [copy(0, me, sibling, src=x_ref)]
        first += [copy(1 + j, me, (*chip, c), src=x_ref) for j, chip in enumerate(chips)]
        for cp in first:
            cp.start()
        passed = [copy(4 + j, (*chip, c), sibling) for j, chip in enumerate(chips)]
        for j, chip in enumerate(chips):
            copy(1 + j, (*chip, c), me).wait_recv()
            passed[j].start()
        copy(0, sibling, me).wait_recv()
        for j, chip in enumerate(chips):
            copy(4 + j, (*chip, 1 - c), me).wait_recv()
        for cp in first + passed:
            cp.wait_send()
        mine.wait()

    return pl.pallas_call(
        body, name=name,
        out_shape=jax.ShapeDtypeStruct((R, N_DEV * C), block.dtype),
        in_specs=[pl.BlockSpec(memory_space=pl.ANY)], out_specs=pl.BlockSpec(memory_space=pl.ANY),
        scratch_shapes=[pltpu.SemaphoreType.DMA((7,)), pltpu.SemaphoreType.DMA((7,)), pltpu.SemaphoreType.DMA],
    )(block)


def _all_to_all(send, *, name):
    def body(*refs):
        _ride("a2a", "start", *refs)
        _ride("a2a", "wait", *refs)

    return pl.pallas_call(
        body, name=name, out_shape=_ride_shape("a2a", send),
        in_specs=[pl.BlockSpec(memory_space=pl.ANY)], out_specs=pl.BlockSpec(memory_space=pl.ANY),
        scratch_shapes=_ride_scratch(),
    )(send)


def _reduce_adamw(recv, w, m, v, *, name, ride=None):
    R, C = w.shape
    tr = next(t for t in (128, 64, 32, 16, 8) if R % t == 0)

    def body(g_ref, w_ref, m_ref, v_ref, og_ref, od_ref, om_ref, ov_ref):
        g = g_ref[0].astype(F32)
        for s in range(1, N_DEV):
            g = g + g_ref[s].astype(F32)
        mn = ADAM_B1 * m_ref[...] + (1.0 - ADAM_B1) * g
        vn = ADAM_B2 * v_ref[...] + (1.0 - ADAM_B2) * jnp.square(g)
        m_hat = mn / (1.0 - ADAM_B1 ** ADAM_STEP)
        v_hat = vn / (1.0 - ADAM_B2 ** ADAM_STEP)
        og_ref[...] = g
        od_ref[...] = -ADAM_LR * (m_hat / (jnp.sqrt(v_hat) + ADAM_EPS) + ADAM_WD * w_ref[...])
        om_ref[...] = mn
        ov_ref[...] = vn

    blk = pl.BlockSpec((tr, C), lambda i: (i, 0))
    shp = jax.ShapeDtypeStruct((R, C), F32)
    in_specs = [pl.BlockSpec((N_DEV, tr, C), lambda i: (0, i, 0)), blk, blk, blk]
    if ride is None:
        return pl.pallas_call(
            body, name=name, grid=(R // tr,), in_specs=in_specs,
            out_specs=[blk, blk, blk, blk], out_shape=[shp, shp, shp, shp],
            compiler_params=pltpu.CompilerParams(dimension_semantics=("parallel",), vmem_limit_bytes=VMEM_LIMIT),
        )(recv, w, m, v)

    def riding(g_ref, w_ref, m_ref, v_ref, s_ref, og_ref, od_ref, om_ref, ov_ref, r_ref, *sems):
        @pl.when(pl.program_id(0) == 0)
        def _():
            _ride(ride[0], "start", s_ref, r_ref, *sems)

        body(g_ref, w_ref, m_ref, v_ref, og_ref, od_ref, om_ref, ov_ref)

        @pl.when(pl.program_id(0) == R // tr - 1)
        def _():
            _ride(ride[0], "wait", s_ref, r_ref, *sems)

    hbm = pl.BlockSpec(memory_space=pl.ANY)
    return pl.pallas_call(
        riding, name=name, grid=(R // tr,), in_specs=in_specs + [hbm],
        out_specs=[blk, blk, blk, blk, hbm], out_shape=[shp, shp, shp, shp, _ride_shape(*ride)],
        scratch_shapes=_ride_scratch(),
        compiler_params=pltpu.CompilerParams(dimension_semantics=("arbitrary",), vmem_limit_bytes=VMEM_LIMIT),
    )(recv, w, m, v, ride[1])


SHARDED = (("a_norm", 1), ("a_w_in", 2), ("a_w_out", 1), ("w_dkv", 0), ("w_ukv", 1), ("b_w_in", 2),
           ("b_w_uq", 2), ("b_w_out", 1), ("w_mem_kv", 1))
SMALL = ("kv_norm", "g_ckv", "g_k_nope", "g_k_rope", "b_norm", "b_g_q_lat", "b_g_q_nope", "b_g_q_rope",
         "mem_norm", "g_mem_q", "g_mem_k")
WEIGHTS = ("a_norm", "a_w_in", "a_w_out", "kv_norm", "w_dkv", "g_ckv", "w_ukv", "g_k_nope", "g_k_rope", "b_norm",
           "b_w_in", "b_g_q_lat", "b_w_uq", "b_g_q_nope", "b_g_q_rope", "b_w_out", "mem_norm", "w_mem_kv",
           "g_mem_q", "g_mem_k")
ROW_MULT = 8
ROW_BLOCK = 128


def _rows_of(n, mult):
    rows = -(-n // LANES)
    return -(-rows // mult) * mult


def _to_rows(flat, mult):
    n = flat.shape[-1]
    rows = _rows_of(n, mult)
    pad = [(0, 0)] * (flat.ndim - 1) + [(0, rows * LANES - n)]
    return jnp.pad(flat, pad).reshape(*flat.shape[:-1], rows, LANES)


def _split8(full, axis):
    shp = full.shape
    t = full.reshape(*shp[:axis], N_DEV, shp[axis] // N_DEV, *shp[axis + 1:])
    return jnp.moveaxis(t, axis, 0).reshape(N_DEV, -1)


def _join8(rows, axis, shard_shape):
    t = rows.reshape(N_DEV, *shard_shape)
    t = jnp.moveaxis(t, 0, axis)
    return t.reshape(*shard_shape[:axis], N_DEV * shard_shape[axis], *shard_shape[axis + 1:])


def _stack_rows(parts, block=ROW_BLOCK):
    rows = sum(p.shape[-2] for p in parts)
    if rows % block:
        parts = list(parts) + [jnp.zeros((*parts[0].shape[:-2], -rows % block, LANES), parts[0].dtype)]
    return jnp.concatenate(parts, axis=-2)


def _pack_local(vals, names, mult):
    return _stack_rows([_to_rows(vals[n].reshape(-1), mult) for n in names])


def _unpack_local(slab, names, shapes, mult):
    out, row = {}, 0
    for n in names:
        size = 1
        for d in shapes[n]:
            size *= d
        rows = _rows_of(size, mult)
        out[n] = slab[row:row + rows].reshape(-1)[:size].reshape(shapes[n])
        row += rows
    return out, row


def kernel(x, mem, positions, a_norm, a_w_in, a_w_out, kv_norm, w_dkv, g_ckv, w_ukv, g_k_nope, g_k_rope, b_norm, b_w_in, b_g_q_lat, b_w_uq, b_g_q_nope, b_g_q_rope, b_w_out, mem_norm, w_mem_kv, g_mem_q, g_mem_k, loss_target, m_a_norm, m_a_w_in, m_a_w_out, m_kv_norm, m_w_dkv, m_g_ckv, m_w_ukv, m_g_k_nope, m_g_k_rope, m_b_norm, m_b_w_in, m_b_g_q_lat, m_b_w_uq, m_b_g_q_nope, m_b_g_q_rope, m_b_w_out, m_mem_norm, m_w_mem_kv, m_g_mem_q, m_g_mem_k, v_a_norm, v_a_w_in, v_a_w_out, v_kv_norm, v_w_dkv, v_g_ckv, v_w_ukv, v_g_k_nope, v_g_k_rope, v_b_norm, v_b_w_in, v_b_g_q_lat, v_b_w_uq, v_b_g_q_nope, v_b_g_q_rope, v_b_w_out, v_mem_norm, v_w_mem_kv, v_g_mem_q, v_g_mem_k):
    wts = dict(a_norm=a_norm, a_w_in=a_w_in, a_w_out=a_w_out, kv_norm=kv_norm, w_dkv=w_dkv, g_ckv=g_ckv, w_ukv=w_ukv,
               g_k_nope=g_k_nope, g_k_rope=g_k_rope, b_norm=b_norm, b_w_in=b_w_in, b_g_q_lat=b_g_q_lat, b_w_uq=b_w_uq,
               b_g_q_nope=b_g_q_nope, b_g_q_rope=b_g_q_rope, b_w_out=b_w_out, mem_norm=mem_norm, w_mem_kv=w_mem_kv,
               g_mem_q=g_mem_q, g_mem_k=g_mem_k)
    mom = dict(a_norm=m_a_norm, a_w_in=m_a_w_in, a_w_out=m_a_w_out, kv_norm=m_kv_norm, w_dkv=m_w_dkv, g_ckv=m_g_ckv,
               w_ukv=m_w_ukv, g_k_nope=m_g_k_nope, g_k_rope=m_g_k_rope, b_norm=m_b_norm, b_w_in=m_b_w_in,
               b_g_q_lat=m_b_g_q_lat, b_w_uq=m_b_w_uq, b_g_q_nope=m_b_g_q_nope, b_g_q_rope=m_b_g_q_rope,
               b_w_out=m_b_w_out, mem_norm=m_mem_norm, w_mem_kv=m_w_mem_kv, g_mem_q=m_g_mem_q, g_mem_k=m_g_mem_k)
    var = dict(a_norm=v_a_norm, a_w_in=v_a_w_in, a_w_out=v_a_w_out, kv_norm=v_kv_norm, w_dkv=v_w_dkv, g_ckv=v_g_ckv,
               w_ukv=v_w_ukv, g_k_nope=v_g_k_nope, g_k_rope=v_g_k_rope, b_norm=v_b_norm, b_w_in=v_b_w_in,
               b_g_q_lat=v_b_g_q_lat, b_w_uq=v_b_w_uq, b_g_q_nope=v_b_g_q_nope, b_g_q_rope=v_b_g_q_rope,
               b_w_out=v_b_w_out, mem_norm=v_mem_norm, w_mem_kv=v_w_mem_kv, g_mem_q=v_g_mem_q, g_mem_k=v_g_mem_k)
    shapes = {n: wts[n].shape for n in WEIGHTS}
    S, D = x.shape[1], x.shape[2]
    xs, ms, tgt = x[0], mem[0], loss_target[0]
    sb_w, mem_w, mla_w = H_SB * DH, H_MEM * DH, H_MLA * DH
    q_lora, kv_lora = b_g_q_lat.shape[-1], g_ckv.shape[-1]

    def pieces_of(names):
        return [_to_rows(wts[n].astype(BF16).reshape(-1), 16) for n in names]

    def unpack_gathered(gathered, names):
        full, row = {}, 0
        for n in names:
            rows = _rows_of(wts[n].size, 16)
            flat = gathered[:, row:row + rows].reshape(N_DEV, -1)[:, :wts[n].size]
            full[n] = _join8(flat, dict(SHARDED)[n], wts[n].shape)
            row += rows
        return full

    second, third = ("a_w_out", "w_dkv", "w_ukv"), ("w_mem_kv", "b_w_in", "b_w_uq", "b_w_out")
    a_in_w = a_w_in.shape[-1]
    bits = jnp.pad(lax.bitcast_convert_type(a_norm.reshape(-1), BF16).reshape(1, -1), ((0, 15), (0, a_in_w - 2 * a_norm.size)))
    w_a_in = _all_gather_columns(jnp.concatenate([a_w_in[0].astype(BF16), bits], axis=0), name="gather_first")
    g_a = lax.bitcast_convert_type(w_a_in[D].reshape(N_DEV, a_in_w)[:, :2 * a_norm.size].reshape(N_DEV, -1, 2),
                                   F32).reshape(1, D)

    row2 = lambda g: g.reshape(1, -1)
    h0 = _rowwise(_f_norm, [(xs, D, 0)], [g_a], [(D, BF16)], name="a_norm_fwd", tm=512)[0]
    qkv, gathered = _mm(h0, w_a_in, "nn", b_cols=(0, 3 * sb_w), out_dtype=BF16, name="a_in_qkv",
                        ride=("gather", jnp.concatenate(pieces_of(second), axis=0)))
    full = unpack_gathered(gathered, second)
    pa = _mm(h0, w_a_in, "nn", b_cols=(3 * sb_w, N_DEV * a_in_w - 3 * sb_w), name="a_in_rest")
    sb, gathered = _sb_fwd(qkv, tq=512, tk=256, name="sb_fwd",
                           ride=("gather", jnp.concatenate(pieces_of(third), axis=0)))
    full.update(unpack_gathered(gathered, third))
    w_a_out = full["a_w_out"][0]
    w_dkv_p = jnp.pad(full["w_dkv"], ((0, 0), (0, ROPE)))
    w_ukv_f = full["w_ukv"]
    wb = full["b_w_in"][0]
    w_b_in = jnp.concatenate([wb[:, q_lora:q_lora + mla_w], wb[:, :q_lora], wb[:, q_lora + mla_w:]], axis=1)
    w_uq_p = jnp.pad(full["b_w_uq"][0].reshape(q_lora, H_MLA, MLA_QK),
                     ((0, 0), (0, 0), (0, 2 * DH - MLA_QK))).reshape(q_lora, H_MLA * 2 * DH)
    w_b_out = full["b_w_out"][0]
    w_mem = full["w_mem_kv"]

    pad128 = lambda g: jnp.pad(g.reshape(1, -1), ((0, 0), (0, DH - ROPE)))
    g_kr, g_qr = pad128(g_k_rope), pad128(b_g_q_rope[0])
    g_kv, g_b, g_c, g_kn = row2(kv_norm), row2(b_norm[0]), row2(g_ckv), row2(g_k_nope)
    g_ql, g_qn = row2(b_g_q_lat[0]), row2(b_g_q_nope[0])

    inv_freq = jnp.power(ROPE_THETA, -jnp.arange(0, ROPE, 2, dtype=F32) / ROPE)
    ang = positions[0].astype(F32)[:, None] * inv_freq
    z64 = jnp.zeros((S, DH - ROPE), F32)
    cs = jnp.concatenate([jnp.cos(ang), jnp.cos(ang), z64, -jnp.sin(ang), jnp.sin(ang), z64], axis=1)

    mn, mkv = [], []
    for l in range(2):
        mn.append(_rowwise(_f_norm, [(ms, D, 0)], [row2(mem_norm[l])], [(D, BF16)], name=f"mem_norm{l}")[0])
        mkv.append(_mm(mn[l], w_mem[l], "nn", name=f"mem_kv{l}"))
    g_mq = [row2(g_mem_q[l]) for l in range(2)]
    g_mk = [row2(g_mem_k[l]) for l in range(2)]

    mix_a_rows = [(sb, sb_w, 0), (pa, sb_w, 0), (pa, mem_w, sb_w // mem_w), (pa, mem_w, sb_w // mem_w + 1)]
    mixed_a = _rowwise(_f_mix, mix_a_rows, [mkv[0], g_mq[0], g_mk[0]], [(sb_w + mem_w, BF16)], name="a_mix_fwd", tm=512)[0]
    x1 = _mm(mixed_a, w_a_out, "nn", add=xs, name="a_out")

    hk, hb = _rowwise(_f_norm2, [(x1, D, 0)], [g_kv, g_b], [(D, BF16), (D, BF16)], name="b_norm_fwd", tm=512)
    ckr = _mm(hk, w_dkv_p, "nn", name="kv_down")
    cn, kr = _rowwise(_f_kv1, [(ckr, kv_lora + DH, 0), (cs, 2 * DH, 0)], [g_c, g_kr],
                      [(kv_lora, BF16), (DH, F32)], name="kv1_fwd", tm=512)
    kvu = _mm(cn, w_ukv_f, "nn", name="kv_up")
    k2, v2 = _rowwise(_f_kv2, [(kvu, H_MLA * 2 * DH, 0), (kr, DH, 0)], [g_kn],
                      [(H_MLA * 2 * DH, BF16), (mla_w, BF16)], name="kv2_fwd")
    pb = _mm(hb, w_b_in, "nn", name="b_in")
    ql = _rowwise(_f_norm, [(pb, q_lora, mla_w // q_lora)], [g_ql], [(q_lora, BF16)], name="q_lat_fwd", tm=512)[0]
    qraw = _mm(ql, w_uq_p, "nn", name="q_up")
    q2 = _rowwise(_f_q2, [(qraw, H_MLA * 2 * DH, 0), (cs, 2 * DH, 0)], [g_qn, g_qr],
                  [(H_MLA * 2 * DH, BF16)], name="q2_fwd")[0]
    att, lse = _mla_fwd(q2, k2, v2, tq=1024, tk=1024, name="mla_fwd")
    cb = (mla_w + q_lora) // mem_w
    mix_b_rows = [(att, mla_w, 0), (pb, mla_w, 0), (pb, mem_w, cb), (pb, mem_w, cb + 1)]
    mixed_b = _rowwise(_f_mix, mix_b_rows, [mkv[1], g_mq[1], g_mk[1]], [(mla_w + mem_w, BF16)], name="b_mix_fwd", tm=512)[0]
    dy, loss_part = _mm_loss(mixed_b, w_b_out, x1, tgt, name="b_out_loss")

    gr = {}
    d_mixed_b = _mm(dy, w_b_out, "nt", name="b_out_dx")
    gr["b_w_out"] = _mm(mixed_b, dy, "tn", out_dtype=BF16, name="b_out_dw")[None]
    d_att, d_gmla, d_qm_b, d_gm_b, d_mkv1, d_gq1, d_gk1 = _rowwise_bwd(
        _f_mix, mix_b_rows, [mkv[1], g_mq[1], g_mk[1]], [(d_mixed_b, mla_w + mem_w, 0)],
        [0, 1, 2, 3], [0, 1, 2], out_dtypes=[F32, BF16, BF16, BF16], name="b_mix_bwd")
    dq2, dk2, dv2 = _mla_bwd(q2, k2, v2, att, d_att, lse, tq=512, tk=512, name="mla_bwd")
    d_qraw, d_gqn, d_gqr = _rowwise_bwd(
        _f_q2, [(qraw, H_MLA * 2 * DH, 0), (cs, 2 * DH, 0)], [g_qn, g_qr], [(dq2, H_MLA * 2 * DH, 0)],
        [0], [0, 1], out_dtypes=[BF16], name="q2_bwd")
    d_ql = _mm(d_qraw, w_uq_p, "nt", name="q_up_dx")
    d_wuq = _mm(ql, d_qraw, "tn", out_dtype=BF16, name="q_up_dw")
    gr["b_w_uq"] = d_wuq.reshape(q_lora, H_MLA, 2 * DH)[:, :, :MLA_QK].reshape(1, q_lora, H_MLA * MLA_QK)
    d_qlat, d_gql = _rowwise_bwd(_f_norm, [(pb, q_lora, mla_w // q_lora)], [g_ql], [(d_ql, q_lora, 0)],
                                 [0], [0], out_dtypes=[BF16], name="q_lat_bwd", tm=512)
    d_pb = jnp.concatenate([d_gmla, d_qlat, d_qm_b, d_gm_b], axis=1)
    d_hb = _mm(d_pb, w_b_in, "nt", name="b_in_dx")
    d_wbin = _mm(hb, d_pb, "tn", out_dtype=BF16, name="b_in_dw")
    gr["b_w_in"] = jnp.concatenate([d_wbin[:, mla_w:mla_w + q_lora], d_wbin[:, :mla_w], d_wbin[:, mla_w + q_lora:]],
                                   axis=1)[None]
    d_kvu, d_kr, d_gkn = _rowwise_bwd(
        _f_kv2, [(kvu, H_MLA * 2 * DH, 0), (kr, DH, 0)], [g_kn], [(dk2, H_MLA * 2 * DH, 0), (dv2, mla_w, 0)],
        [0, 1], [0], out_dtypes=[BF16, F32], name="kv2_bwd")
    d_cn = _mm(d_kvu, w_ukv_f, "nt", name="kv_up_dx")
    gr["w_ukv"] = _mm(cn, d_kvu, "tn", out_dtype=BF16, name="kv_up_dw")
    d_ckr, d_gc, d_gkr = _rowwise_bwd(
        _f_kv1, [(ckr, kv_lora + DH, 0), (cs, 2 * DH, 0)], [g_c, g_kr], [(d_cn, kv_lora, 0), (d_kr, DH, 0)],
        [0], [0, 1], out_dtypes=[BF16], name="kv1_bwd", tm=512)
    d_hk = _mm(d_ckr, w_dkv_p, "nt", name="kv_down_dx")
    gr["w_dkv"] = _mm(hk, d_ckr, "tn", out_dtype=BF16, name="kv_down_dw")[:, :kv_lora + ROPE]
    d_x1, d_gkv, d_gb = _rowwise_bwd(_f_norm2, [(x1, D, 0)], [g_kv, g_b], [(d_hk, D, 0), (d_hb, D, 0)],
                                     [0], [0, 1], add=(dy, D, 0), name="b_norm_bwd")
    d_mixed_a = _mm(d_x1, w_a_out, "nt", name="a_out_dx")
    gr["a_w_out"] = _mm(mixed_a, d_x1, "tn", out_dtype=BF16, name="a_out_dw")[None]
    d_sb, d_gsb, d_qm_a, d_gm_a, d_mkv0, d_gq0, d_gk0 = _rowwise_bwd(
        _f_mix, mix_a_rows, [mkv[0], g_mq[0], g_mk[0]], [(d_mixed_a, sb_w + mem_w, 0)],
        [0, 1, 2, 3], [0, 1, 2], out_dtypes=[F32, BF16, BF16, BF16], name="a_mix_bwd")
    d_wmem, d_mnorm = [], []
    for l, d_mkv in enumerate((d_mkv0, d_mkv1)):
        d_mn = _mm(d_mkv, w_mem[l], "nt", name=f"mem_kv_dx{l}")
        d_wmem.append(_mm(mn[l], d_mkv, "tn", out_dtype=BF16, name=f"mem_kv_dw{l}"))
        d_mnorm.append(_rowwise_bwd(_f_norm, [(ms, D, 0)], [row2(mem_norm[l])], [(d_mn, D, 0)], [], [0],
                                    name=f"mem_norm_bwd{l}")[0])
    gr["w_mem_kv"] = jnp.stack(d_wmem)

    mid = tuple(n for n, _ in SHARDED[2:])
    send_mid = _stack_rows([_to_rows(_split8(gr[n], ax), 16) for n, ax in SHARDED[2:]])
    dq, dk, dv, recv_mid = _sb_bwd(qkv, sb, d_sb, tq=512, tk=256, name="sb_bwd", ride=("a2a", send_mid))
    d_pa = jnp.concatenate([dq, dk, dv, d_gsb, d_qm_a, d_gm_a], axis=1)
    send_ain = _mm(h0, d_pa, "tn", out_dtype=BF16, out_split=N_DEV, name="a_in_dw")
    d_h0, recv_ain = _mm(d_pa, w_a_in, "nt", b_rows=D, name="a_in_dx", ride=("a2a", send_ain))
    grad_x, d_ga = _rowwise_bwd(_f_norm, [(xs, D, 0)], [g_a], [(d_h0, D, 0)], [0], [0], add=(d_x1, D, 0),
                                name="a_norm_bwd")
    gr["mem_norm"] = jnp.concatenate(d_mnorm, axis=0)
    gr["g_mem_q"] = jnp.concatenate([d_gq0, d_gq1], axis=0)
    gr["g_mem_k"] = jnp.concatenate([d_gk0, d_gk1], axis=0)
    gr["kv_norm"], gr["b_norm"], gr["g_ckv"], gr["g_k_nope"] = d_gkv, d_gb, d_gc, d_gkn
    gr["g_k_rope"], gr["b_g_q_rope"] = d_gkr[:, :ROPE], d_gqr[:, :ROPE]
    gr["b_g_q_lat"], gr["b_g_q_nope"] = d_gql, d_gqn
    last = ("a_norm",) + SMALL
    shared = jnp.concatenate([gr[n].reshape(-1) for n in SMALL] + [loss_part[0, :1]])
    flat = jnp.concatenate([_split8(d_ga, 1), jnp.broadcast_to(shared[None], (N_DEV, shared.size))], axis=1)
    out = [{}, {}, {}, {}]
    *slabs, recv_last = _reduce_adamw(recv_mid, *[_pack_local(t, mid, 16) for t in (wts, mom, var)],
                                      name="reduce_adamw_mid", ride=("a2a", _to_rows(flat, ROW_MULT)))
    for o, slab in zip(out, slabs):
        o.update(_unpack_local(slab, mid, shapes, 16)[0])
    zero = jnp.zeros((1,), F32)
    slabs = _reduce_adamw(recv_last, *[_to_rows(jnp.concatenate([t[n].reshape(-1) for n in last] + [zero]), ROW_MULT)
                                       for t in (wts, mom, var)], name="reduce_adamw_last")
    for o, slab in zip(out, slabs):
        vec, at = slab.reshape(-1), 0
        for n in last:
            o[n] = vec[at:at + wts[n].size].reshape(shapes[n])
            at += wts[n].size
    loss = slabs[0].reshape(-1)[at]
    slabs = _reduce_adamw(recv_ain, *[t["a_w_in"][0] for t in (wts, mom, var)], name="reduce_adamw_ain")
    for o, slab in zip(out, slabs):
        o["a_w_in"] = slab[None]
    return (loss, grad_x[None], *[o[n] for o in out for n in WEIGHTS])
```
